```python
import math
import jax, jax.numpy as jnp
from jax import lax
import numpy as np

D_MODEL = 2048
BATCH = 4
SEQ = 2048
DEPTH = 1

RET_HEAD_DIM = 128
RET_HEADS = (D_MODEL // 2) // RET_HEAD_DIM
RET_WIDTH = RET_HEADS * RET_HEAD_DIM
LRU_WIDTH = D_MODEL - RET_WIDTH
LRU_BLOCKS = 8
LRU_BLOCK_DIM = LRU_WIDTH // LRU_BLOCKS
IN_COLS = 4 * RET_WIDTH + 2 * LRU_WIDTH
CONV_WIDTH = 4
LRU_C = 8.0
CHUNK = 128
ROPE_BASE = 10000.0
N_EXPERTS = 32
TOP_K = 4
D_EXPERT = D_MODEL
SWIGLU_LIMIT = 7.0
SWIGLU_ALPHA = 1.702
PLE_DIM = 256
EXPERT_BLOCK = 128
LN_EPS = 1e-5
DN_ALPHA = (2.0 * DEPTH) ** 0.25
DN_BETA = (8.0 * DEPTH) ** -0.25

kernel_name = 'hymba_retnet_rglru_moe_deepnorm_ple'


def layer_norm(x, w, b):
    x32 = x.astype(jnp.float32)
    mu = jnp.mean(x32, axis=-1, keepdims=True)
    var = jnp.mean(jnp.square(x32 - mu), axis=-1, keepdims=True)
    return (x32 - mu) * lax.rsqrt(var + LN_EPS) * w.astype(jnp.float32) + b.astype(jnp.float32)


def rms_norm(x, w):
    x32 = x.astype(jnp.float32)
    return x32 * lax.rsqrt(jnp.mean(jnp.square(x32), axis=-1, keepdims=True) + LN_EPS) * w.astype(jnp.float32)


def rope(x, pos):
    d = x.shape[-1]
    inv = ROPE_BASE ** (-jnp.arange(0, d, 2, dtype=jnp.float32) / d)
    ang = pos[:, None] * inv[None, :]
    cos = jnp.cos(ang)[:, None, :]
    sin = jnp.sin(ang)[:, None, :]
    x1, x2 = x[..., : d // 2], x[..., d // 2:]
    return jnp.concatenate([x1 * cos - x2 * sin, x1 * sin + x2 * cos], axis=-1)


def retention(q, k, v):
    b, s, h, d = q.shape
    n = s // CHUNK
    log_gamma = jnp.log1p(-jnp.exp2(-5.0 - jnp.arange(h, dtype=jnp.float32)))
    idx = jnp.arange(CHUNK, dtype=jnp.float32)
    diff = idx[:, None] - idx[None, :]
    decay_intra = jnp.where((diff >= 0)[None],
                            jnp.exp(jnp.maximum(diff, 0.0)[None] * log_gamma[:, None, None]), 0.0)
    q_dec = jnp.exp((idx[:, None] + 1.0) * log_gamma[None, :])
    k_dec = jnp.exp((CHUNK - 1.0 - idx)[:, None] * log_gamma[None, :])
    chunk_dec = jnp.exp(CHUNK * log_gamma)
    qc = q.reshape(b, n, CHUNK, h, d)
    kc = k.reshape(b, n, CHUNK, h, d)
    vc = v.reshape(b, n, CHUNK, h, d)
    scores = jnp.einsum('bnihd,bnjhd->bnhij', qc, kc) * decay_intra
    intra = jnp.einsum('bnhij,bnjhd->bnihd', scores, vc)
    kv = jnp.einsum('bnjhd,bnjhe->nbhde', kc * k_dec[:, :, None], vc)

    def step(state, kv_n):
        return chunk_dec[None, :, None, None] * state + kv_n, state

    _, prev = lax.scan(step, jnp.zeros((b, h, d, d), jnp.float32), kv)
    cross = jnp.einsum('bnihd,nbhde->bnihe', qc * q_dec[:, :, None], prev)
    return (intra + cross).reshape(b, s, h, d)


def causal_depthwise_conv(x, w, bias):
    s = x.shape[1]
    xp = jnp.pad(x, ((0, 0), (CONV_WIDTH - 1, 0), (0, 0)))
    out = bias.astype(jnp.float32)
    for j in range(CONV_WIDTH):
        out = out + xp[:, j:j + s] * w[j]
    return out


def rg_lru(x, w_a, b_a, w_x, b_x, lam):
    bsz, s, c = x.shape
    xb = x.reshape(bsz, s, LRU_BLOCKS, LRU_BLOCK_DIM)
    r = jax.nn.sigmoid(jnp.einsum('bshi,hij->bshj', xb, w_a) + b_a).reshape(bsz, s, c)
    gi = jax.nn.sigmoid(jnp.einsum('bshi,hij->bshj', xb, w_x) + b_x).reshape(bsz, s, c)
    log_a = LRU_C * r * jax.nn.log_sigmoid(lam.astype(jnp.float32))
    a = jnp.exp(log_a)
    bx = jnp.sqrt(-jnp.expm1(2.0 * log_a)) * (gi * x)

    def combine(left, right):
        a1, b1 = left
        a2, b2 = right
        return a1 * a2, a2 * b1 + b2

    _, h = lax.associative_scan(combine, (a, bx), axis=1)
    return h


def hybrid_mixer(x, w_in, ret_gn_w, conv_w, conv_b, lru_wa, lru_ba, lru_wx, lru_bx, lru_lam, w_out):
    bsz, s, _ = x.shape
    proj = jnp.einsum('bsd,de->bse', x, w_in).astype(jnp.float32)
    R = RET_WIDTH
    q, k, v, g, xr, yg = jnp.split(proj, [R, 2 * R, 3 * R, 4 * R, 4 * R + LRU_WIDTH], axis=-1)
    pos = jnp.arange(s, dtype=jnp.float32)
    q = rope(q.reshape(bsz, s, RET_HEADS, RET_HEAD_DIM), pos)
    k = rope(k.reshape(bsz, s, RET_HEADS, RET_HEAD_DIM), pos) * (RET_HEAD_DIM ** -0.5)
    ret = retention(q, k, v.reshape(bsz, s, RET_HEADS, RET_HEAD_DIM))
    mu = jnp.mean(ret, axis=-1, keepdims=True)
    var = jnp.mean(jnp.square(ret - mu), axis=-1, keepdims=True)
    ret = ((ret - mu) * lax.rsqrt(var + LN_EPS)).reshape(bsz, s, R) * ret_gn_w.astype(jnp.float32)
    ret_out = jax.nn.silu(g) * ret
    xr = causal_depthwise_conv(xr, conv_w.astype(jnp.float32), conv_b)
    h = rg_lru(xr, lru_wa.astype(jnp.float32), lru_ba.astype(jnp.float32),
               lru_wx.astype(jnp.float32), lru_bx.astype(jnp.float32), lru_lam)
    lru_out = jax.nn.gelu(yg) * h
    mixed = jnp.concatenate([ret_out, lru_out], axis=-1)
    return jnp.einsum('bse,ed->bsd', mixed, w_out.astype(jnp.float32))


def moe(x, w_router, b_router, w_gate, b_gate, w_up, b_up, w_down, b_down):
    bsz, s, d = x.shape
    t = bsz * s
    xf = x.reshape(t, d).astype(jnp.float32)
    logits = xf @ w_router.astype(jnp.float32) + b_router.astype(jnp.float32)
    top_logit, top_e = lax.top_k(logits, TOP_K)
    gates = jax.nn.softmax(top_logit, axis=-1)
    n_assign = t * TOP_K
    flat_e = top_e.reshape(-1).astype(jnp.int32)
    flat_tok = jnp.arange(n_assign, dtype=jnp.int32) // TOP_K
    flat_gate = gates.reshape(-1)
    order = jnp.argsort(flat_e)
    se, stok, sgate = flat_e[order], flat_tok[order], flat_gate[order]
    counts = jnp.zeros((N_EXPERTS,), jnp.int32).at[flat_e].add(1)
    starts = jnp.cumsum(counts) - counts
    padded = (counts + EXPERT_BLOCK - 1) // EXPERT_BLOCK * EXPERT_BLOCK
    pad_ends = jnp.cumsum(padded)
    pad_starts = pad_ends - padded
    dest = pad_starts[se] + (jnp.arange(n_assign, dtype=jnp.int32) - starts[se])
    n_pad = n_assign + N_EXPERTS * EXPERT_BLOCK
    n_blocks = n_pad // EXPERT_BLOCK
    buf_tok = jnp.full((n_pad,), t, jnp.int32).at[dest].set(stok)
    buf_gate = jnp.zeros((n_pad,), jnp.float32).at[dest].set(sgate)
    block_start = jnp.arange(n_blocks, dtype=jnp.int32) * EXPERT_BLOCK
    block_e = jnp.minimum(jnp.searchsorted(pad_ends, block_start, side='right'), N_EXPERTS - 1)
    xpad = jnp.concatenate([xf, jnp.zeros((1, d), jnp.float32)], axis=0)
    xb = xpad[buf_tok].reshape(n_blocks, EXPERT_BLOCK, d)

    def expert_block(args):
        xblk, e = args
        gt = xblk @ w_gate[e].astype(jnp.float32) + b_gate[e].astype(jnp.float32)
        up = xblk @ w_up[e].astype(jnp.float32) + b_up[e].astype(jnp.float32)
        gt = jnp.minimum(gt, SWIGLU_LIMIT)
        up = jnp.clip(up, -SWIGLU_LIMIT, SWIGLU_LIMIT)
        hid = (up + 1.0) * gt * jax.nn.sigmoid(SWIGLU_ALPHA * gt)
        return hid @ w_down[e].astype(jnp.float32) + b_down[e].astype(jnp.float32)

    yb = lax.map(expert_block, (xb, block_e)).reshape(n_pad, d)
    out = jnp.zeros((t + 1, d), jnp.float32).at[buf_tok].add(yb * buf_gate[:, None])
    return out[:t].reshape(bsz, s, d)


def setup_inputs(seed: int = 0) -> dict:
    key = jax.random.key(seed)
    ks = jax.random.split(key, 32)
    f32 = jnp.float32
    nrm = lambda k, shape, scale: jax.random.normal(k, shape, f32) * scale
    col_scale = jnp.concatenate([jnp.ones((2 * RET_WIDTH,), f32),
                                 jnp.full((RET_WIDTH,), DN_BETA, f32),
                                 jnp.ones((RET_WIDTH + 2 * LRU_WIDTH,), f32)])
    lam_u = jax.random.uniform(ks[9], (DEPTH, LRU_WIDTH), f32, 0.9, 0.999)
    lam_a = lam_u ** (1.0 / LRU_C)
    return {
        'x': nrm(ks[0], (BATCH, SEQ, D_MODEL), 1.0),
        'p': nrm(ks[1], (DEPTH, BATCH, SEQ, PLE_DIM), 1.0),
        'w_in': nrm(ks[2], (DEPTH, D_MODEL, IN_COLS), D_MODEL ** -0.5) * col_scale,
        'ret_gn_w': 1.0 + nrm(ks[3], (DEPTH, RET_WIDTH), 0.02),
        'conv_w': nrm(ks[4], (DEPTH, CONV_WIDTH, LRU_WIDTH), CONV_WIDTH ** -0.5),
        'conv_b': nrm(ks[5], (DEPTH, LRU_WIDTH), 0.01),
        'lru_wa': nrm(ks[6], (DEPTH, LRU_BLOCKS, LRU_BLOCK_DIM, LRU_BLOCK_DIM), LRU_BLOCK_DIM ** -0.5),
        'lru_ba': nrm(ks[7], (DEPTH, LRU_BLOCKS, LRU_BLOCK_DIM), 0.01),
        'lru_wx': nrm(ks[8], (DEPTH, LRU_BLOCKS, LRU_BLOCK_DIM, LRU_BLOCK_DIM), LRU_BLOCK_DIM ** -0.5),
        'lru_bx': nrm(ks[10], (DEPTH, LRU_BLOCKS, LRU_BLOCK_DIM), 0.01),
        'lru_lam': jnp.log(lam_a) - jnp.log1p(-lam_a),
        'w_out': nrm(ks[11], (DEPTH, D_MODEL, D_MODEL), D_MODEL ** -0.5 * DN_BETA),
        'ln1_w': 1.0 + nrm(ks[12], (DEPTH, D_MODEL), 0.02),
        'ln1_b': nrm(ks[13], (DEPTH, D_MODEL), 0.01),
        'w_router': nrm(ks[14], (DEPTH, D_MODEL, N_EXPERTS), D_MODEL ** -0.5),
        'b_router': nrm(ks[15], (DEPTH, N_EXPERTS), 0.01),
        'w_gate': nrm(ks[16], (DEPTH, N_EXPERTS, D_MODEL, D_EXPERT), D_MODEL ** -0.5),
        'b_gate': nrm(ks[17], (DEPTH, N_EXPERTS, D_EXPERT), 0.01),
        'w_up': nrm(ks[18], (DEPTH, N_EXPERTS, D_MODEL, D_EXPERT), D_MODEL ** -0.5),
        'b_up': nrm(ks[19], (DEPTH, N_EXPERTS, D_EXPERT), 0.01),
        'w_down': nrm(ks[20], (DEPTH, N_EXPERTS, D_EXPERT, D_MODEL), D_EXPERT ** -0.5 * DN_BETA),
        'b_down': nrm(ks[21], (DEPTH, N_EXPERTS, D_MODEL), 0.01),
        'ln2_w': 1.0 + nrm(ks[22], (DEPTH, D_MODEL), 0.02),
        'ln2_b': nrm(ks[23], (DEPTH, D_MODEL), 0.01),
        'w_ple_proj': nrm(ks[24], (DEPTH, PLE_DIM, D_MODEL), PLE_DIM ** -0.5 * DN_BETA),
        'ple_norm_w': 1.0 + nrm(ks[25], (DEPTH, D_MODEL), 0.02),
        'w_ple_gate': nrm(ks[26], (DEPTH, D_MODEL, D_MODEL), D_MODEL ** -0.5),
    }


def reference(x, p, w_in, ret_gn_w, conv_w, conv_b, lru_wa, lru_ba, lru_wx, lru_bx, lru_lam, w_out,
              ln1_w, ln1_b, w_router, b_router, w_gate, b_gate, w_up, b_up, w_down, b_down,
              ln2_w, ln2_b, w_ple_proj, ple_norm_w, w_ple_gate):
    h = x.astype(jnp.float32)
    for i in range(DEPTH):
        m = hybrid_mixer(h, w_in[i], ret_gn_w[i], conv_w[i], conv_b[i], lru_wa[i], lru_ba[i],
                         lru_wx[i], lru_bx[i], lru_lam[i], w_out[i])
        h = layer_norm(DN_ALPHA * h + m, ln1_w[i], ln1_b[i])
        f = moe(h, w_router[i], b_router[i], w_gate[i], b_gate[i], w_up[i], b_up[i], w_down[i], b_down[i])
        h = layer_norm(DN_ALPHA * h + f, ln2_w[i], ln2_b[i])
        e = rms_norm(jnp.einsum('bsp,pd->bsd', p[i].astype(jnp.float32), w_ple_proj[i].astype(jnp.float32)), ple_norm_w[i])
        gate = jax.nn.sigmoid(jnp.einsum('bsd,de->bse', h, w_ple_gate[i].astype(jnp.float32)))
        h = h + gate * e
    return h.astype(x.dtype)
```

```python
import functools
import math

import jax
import jax.numpy as jnp
from jax import lax
from jax.experimental import pallas as pl
from jax.experimental.pallas import tpu as pltpu

D_MODEL = 2048
RET_HEAD_DIM = 128
RET_HEADS = 8
RET_WIDTH = RET_HEADS * RET_HEAD_DIM
LRU_WIDTH = D_MODEL - RET_WIDTH
LRU_BLOCKS = 8
LRU_BLOCK_DIM = LRU_WIDTH // LRU_BLOCKS
IN_COLS = 4 * RET_WIDTH + 2 * LRU_WIDTH
CONV_WIDTH = 4
LRU_C = 8.0
CHUNK = 128
ROPE_BASE = 10000.0
N_EXPERTS = 32
TOP_K = 4
SWIGLU_LIMIT = 7.0
SWIGLU_ALPHA = 1.702
PLE_DIM = 256
LN_EPS = 1e-5
DEPTH = 1
DN_ALPHA = (2.0 * DEPTH) ** 0.25

LANES = 128
SUBLANES = 8
VMEM_LIMIT = 60 * 1024 * 1024

ROW_CHUNK = 256
UNIT_ROWS = 2048
F_TILE = 256

f32 = jnp.float32
bf16 = jnp.bfloat16


def _cparams(sem):
    return pltpu.CompilerParams(dimension_semantics=sem, vmem_limit_bytes=VMEM_LIMIT)


def _in_proj_kernel(x_ref, w_ref, o_ref, xb_ref):
    @pl.when(pl.program_id(1) == 0)
    def _():
        xb_ref[...] = x_ref[...].astype(bf16)

    o_ref[...] = jnp.dot(xb_ref[...], w_ref[...], preferred_element_type=f32)


def _in_proj(x2d, w_bf):
    t, d = x2d.shape
    n = w_bf.shape[1]
    tm, tn = 1024, 1024
    return pl.pallas_call(
        _in_proj_kernel,
        grid=(t // tm, n // tn),
        in_specs=[pl.BlockSpec((tm, d), lambda i, j: (i, 0)),
                  pl.BlockSpec((d, tn), lambda i, j: (0, j))],
        out_specs=pl.BlockSpec((tm, tn), lambda i, j: (i, j)),
        out_shape=jax.ShapeDtypeStruct((t, n), f32),
        scratch_shapes=[pltpu.VMEM((tm, d), bf16)],
        compiler_params=_cparams(("parallel", "arbitrary")),
        name="in_proj",
    )(x2d, w_bf)


def _retention_kernel(q_ref, k_ref, v_ref, g_ref, cos_ref, sin_ref, dec_ref, qd_ref, kd_ref, cd_ref, gnw_ref,
                      o_ref):
    s = q_ref.shape[0]
    n_chunks = s // CHUNK
    decay = dec_ref[...]
    q_dec = qd_ref[...]
    k_dec = kd_ref[...]
    c_dec = cd_ref[0:1, :]
    gnw = gnw_ref[...]
    k_scale = RET_HEAD_DIM ** -0.5

    def rope(xv, cos, sin):
        return xv * cos + pltpu.roll(xv, RET_HEAD_DIM // 2, axis=1) * sin

    def body(n, state):
        sl = pl.ds(pl.multiple_of(n * CHUNK, CHUNK), CHUNK)
        cos = cos_ref[sl, :]
        sin = sin_ref[sl, :]
        q = rope(q_ref[sl, :], cos, sin)
        k = rope(k_ref[sl, :], cos, sin) * k_scale
        vb = v_ref[sl, :].astype(bf16)
        scores = lax.dot_general(q.astype(bf16), k.astype(bf16), (((1,), (1,)), ((), ())),
                                 preferred_element_type=f32) * decay
        intra = jnp.dot(scores.astype(bf16), vb, preferred_element_type=f32)
        cross = jnp.dot((q * q_dec).astype(bf16), state.astype(bf16), preferred_element_type=f32)
        kv = lax.dot_general((k * k_dec).astype(bf16), vb, (((0,), (0,)), ((), ())),
                             preferred_element_type=f32)
        ret = intra + cross
        mu = jnp.mean(ret, axis=-1, keepdims=True)
        cen = ret - mu
        var = jnp.mean(cen * cen, axis=-1, keepdims=True)
        ret = cen * lax.rsqrt(var + LN_EPS) * gnw
        g = g_ref[sl, :]
        o_ref[sl, :] = (g * jax.nn.sigmoid(g) * ret).astype(o_ref.dtype)
        return c_dec * state + kv

    lax.fori_loop(0, n_chunks, body, jnp.zeros((RET_HEAD_DIM, RET_HEAD_DIM), f32), unroll=2)


def _retention_tables(s):
    h, d = RET_HEADS, RET_HEAD_DIM
    inv = ROPE_BASE ** (-jnp.arange(0, d, 2, dtype=f32) / d)
    ang = jnp.arange(s, dtype=f32)[:, None] * inv[None, :]
    cos = jnp.cos(ang)
    sin = jnp.sin(ang)
    cos_t = jnp.concatenate([cos, cos], axis=-1)
    sin_t = jnp.concatenate([-sin, sin], axis=-1)
    log_gamma = jnp.log1p(-jnp.exp2(-5.0 - jnp.arange(h, dtype=f32)))
    idx = jnp.arange(CHUNK, dtype=f32)
    diff = idx[:, None] - idx[None, :]
    decay = jnp.where((diff >= 0)[None], jnp.exp(jnp.maximum(diff, 0.0)[None] * log_gamma[:, None, None]), 0.0)
    q_dec = jnp.exp((idx[None, :] + 1.0) * log_gamma[:, None])
    k_dec = jnp.exp((CHUNK - 1.0 - idx)[None, :] * log_gamma[:, None])
    c_dec = jnp.exp(CHUNK * log_gamma)
    q_dec = jnp.broadcast_to(q_dec[:, :, None], (h, CHUNK, d))
    k_dec = jnp.broadcast_to(k_dec[:, :, None], (h, CHUNK, d))
    c_dec = jnp.broadcast_to(c_dec[:, None, None], (h, SUBLANES, d))
    return cos_t, sin_t, decay, q_dec, k_dec, c_dec


def _retention(proj, ret_gn_w):
    b, s, _ = proj.shape
    d = RET_HEAD_DIM
    cos_t, sin_t, decay, q_dec, k_dec, c_dec = _retention_tables(s)
    col = lambda off: pl.BlockSpec((None, s, d), lambda bi, hi: (bi, 0, off + hi))
    per_head = lambda r: pl.BlockSpec((None, r, d), lambda bi, hi: (hi, 0, 0))
    full = lambda shp: pl.BlockSpec(shp, lambda bi, hi: (0,) * len(shp))
    return pl.pallas_call(
        _retention_kernel,
        grid=(b, RET_HEADS),
        in_specs=[col(0), col(RET_HEADS), col(2 * RET_HEADS), col(3 * RET_HEADS),
                  full((s, d)), full((s, d)),
                  per_head(CHUNK), per_head(CHUNK), per_head(CHUNK), per_head(SUBLANES),
                  pl.BlockSpec((1, d), lambda bi, hi: (0, hi))],
        out_specs=pl.BlockSpec((None, s, d), lambda bi, hi: (bi, 0, hi)),
        out_shape=jax.ShapeDtypeStruct((b, s, RET_WIDTH), bf16),
        compiler_params=_cparams(("parallel", "parallel")),
        name="retention",
    )(proj, proj, proj, proj, cos_t, sin_t, decay, q_dec, k_dec, c_dec, ret_gn_w.reshape(1, RET_WIDTH))


def _gelu_tanh(x):
    return 0.5 * x * (1.0 + jnp.tanh(math.sqrt(2.0 / math.pi) * (x + 0.044715 * (x * x * x))))


def _lru_kernel(xr_ref, yg_ref, cw_ref, cb_ref, wa_ref, ba_ref, wx_ref, bx_ref, lam_ref, o_ref, a_ref, b_ref):
    s = xr_ref.shape[0]
    x = xr_ref[...]
    rows = lax.broadcasted_iota(jnp.int32, x.shape, 0)
    xc = cb_ref[...] + cw_ref[CONV_WIDTH - 1:CONV_WIDTH, :] * x
    for back in range(1, CONV_WIDTH):
        shifted = jnp.where(rows >= back, pltpu.roll(x, back, axis=0), 0.0)
        xc = xc + cw_ref[CONV_WIDTH - 1 - back:CONV_WIDTH - back, :] * shifted
    xcb = xc.astype(bf16)
    r = jax.nn.sigmoid(jnp.dot(xcb, wa_ref[...].astype(bf16), preferred_element_type=f32) + ba_ref[...])
    gi = jax.nn.sigmoid(jnp.dot(xcb, wx_ref[...].astype(bf16), preferred_element_type=f32) + bx_ref[...])
    lam = lam_ref[...]
    log_sig = jnp.minimum(lam, 0.0) - jnp.log1p(jnp.exp(-jnp.abs(lam)))
    log_a = LRU_C * r * log_sig
    a = jnp.exp(log_a)
    a_ref[...] = a
    b_ref[...] = jnp.sqrt(-jnp.tanh(log_a) * (a * a + 1.0)) * (gi * xc)

    row8 = lax.broadcasted_iota(jnp.int32, (SUBLANES, LANES), 0)

    def body(i, h_prev):
        sl = pl.ds(pl.multiple_of(i * SUBLANES, SUBLANES), SUBLANES)
        a8 = a_ref[sl, :]
        b8 = b_ref[sl, :]
        for sh in (1, 2, 4):
            a_sh = jnp.where(row8 >= sh, pltpu.roll(a8, sh, axis=0), 1.0)
            b_sh = jnp.where(row8 >= sh, pltpu.roll(b8, sh, axis=0), 0.0)
            b8 = a8 * b_sh + b8
            a8 = a8 * a_sh
        h8 = a8 * h_prev + b8
        o_ref[sl, :] = (_gelu_tanh(yg_ref[sl, :]) * h8).astype(o_ref.dtype)
        return h8[SUBLANES - 1:SUBLANES, :]

    lax.fori_loop(0, s // SUBLANES, body, jnp.zeros((1, LANES), f32), unroll=8)


def _lru(proj, conv_w, conv_b, wa, ba, wx, bx, lam):
    b, s, _ = proj.shape
    d = LRU_BLOCK_DIM
    xr_off = 4 * RET_WIDTH // d
    yg_off = xr_off + LRU_BLOCKS
    col = lambda off: pl.BlockSpec((None, s, d), lambda bi, ji: (bi, 0, off + ji))
    vec = lambda r: pl.BlockSpec((r, d), lambda bi, ji: (0, ji))
    blk = lambda r: pl.BlockSpec((None, r, d), lambda bi, ji: (ji, 0, 0))
    return pl.pallas_call(
        _lru_kernel,
        grid=(b, LRU_BLOCKS),
        in_specs=[col(xr_off), col(yg_off), vec(CONV_WIDTH), vec(1), blk(d), blk(1), blk(d), blk(1), vec(1)],
        out_specs=pl.BlockSpec((None, s, d), lambda bi, ji: (bi, 0, ji)),
        out_shape=jax.ShapeDtypeStruct((b, s, LRU_WIDTH), bf16),
        scratch_shapes=[pltpu.VMEM((s, d), f32), pltpu.VMEM((s, d), f32)],
        compiler_params=_cparams(("parallel", "parallel")),
        name="rg_lru",
    )(proj, proj, conv_w, conv_b.reshape(1, LRU_WIDTH), wa, ba.reshape(LRU_BLOCKS, 1, d), wx,
      bx.reshape(LRU_BLOCKS, 1, d), lam.reshape(1, LRU_WIDTH))


def _layer_norm(y, w, b):
    mu = jnp.mean(y, axis=-1, keepdims=True)
    cen = y - mu
    var = jnp.mean(cen * cen, axis=-1, keepdims=True)
    return cen * lax.rsqrt(var + LN_EPS) * w + b


def _split_bf16(v):
    hi = v.astype(bf16)
    lo = (v - hi.astype(f32)).astype(bf16)
    return hi, lo


def _out_router_kernel(ret_ref, lru_ref, wo_ref, x_ref, lnw_ref, lnb_ref, wr_ref, br_ref,
                       x1_ref, tope_ref, gate_ref):
    m = jnp.dot(ret_ref[...], wo_ref[0:RET_WIDTH, :], preferred_element_type=f32)
    m = m + jnp.dot(lru_ref[...], wo_ref[RET_WIDTH:D_MODEL, :], preferred_element_type=f32)
    x1 = _layer_norm(DN_ALPHA * x_ref[...] + m, lnw_ref[...], lnb_ref[...])
    x1_ref[...] = x1
    xh, xl = _split_bf16(x1)
    wh, wl = _split_bf16(wr_ref[...])
    logits = (jnp.dot(xh, wh, preferred_element_type=f32) + jnp.dot(xl, wh, preferred_element_type=f32)
              + jnp.dot(xh, wl, preferred_element_type=f32)) + br_ref[...]
    tm = logits.shape[0]
    lane = lax.broadcasted_iota(jnp.int32, logits.shape, 1)
    lane_k = lax.broadcasted_iota(jnp.int32, (tm, TOP_K), 1)
    top_e = jnp.zeros((tm, TOP_K), jnp.int32)
    top_v = jnp.zeros((tm, TOP_K), f32)
    cur = logits
    for kk in range(TOP_K):
        mx = jnp.max(cur, axis=-1, keepdims=True)
        idx = jnp.min(jnp.where(cur == mx, lane, N_EXPERTS), axis=-1, keepdims=True)
        top_e = jnp.where(lane_k == kk, idx, top_e)
        top_v = jnp.where(lane_k == kk, mx, top_v)
        cur = jnp.where(lane == idx, -jnp.inf, cur)
    ex = jnp.exp(top_v - top_v[:, 0:1])
    gate_ref[...] = ex / jnp.sum(ex, axis=-1, keepdims=True)
    tope_ref[...] = top_e


def _out_router(ret_out, lru_out, wo_bf, x2d, ln_w, ln_b, w_router, b_router):
    t, d = x2d.shape
    tm = 256
    row = lambda c: pl.BlockSpec((tm, c), lambda i: (i, 0))
    full = lambda r, c: pl.BlockSpec((r, c), lambda i: (0, 0))
    return pl.pallas_call(
        _out_router_kernel,
        grid=(t // tm,),
        in_specs=[row(RET_WIDTH), row(LRU_WIDTH), full(d, d), row(d), full(1, d), full(1, d),
                  full(d, N_EXPERTS), full(1, N_EXPERTS)],
        out_specs=[row(d), row(TOP_K), row(TOP_K)],
        out_shape=[jax.ShapeDtypeStruct((t, d), f32), jax.ShapeDtypeStruct((t, TOP_K), jnp.int32),
                   jax.ShapeDtypeStruct((t, TOP_K), f32)],
        compiler_params=_cparams(("parallel",)),
        name="out_proj_ln1_router",
    )(ret_out, lru_out, wo_bf, x2d, ln_w.reshape(1, d), ln_b.reshape(1, d), w_router,
      b_router.reshape(1, N_EXPERTS))


def _routing_tables(top_e, t):
    n_pad = t * TOP_K + N_EXPERTS * ROW_CHUNK
    max_units = N_EXPERTS + (t * TOP_K) // UNIT_ROWS
    sel = (top_e[:, :, None] == jnp.arange(N_EXPERTS, dtype=jnp.int32)[None, None, :]).any(axis=1)
    sel = sel.astype(jnp.int32)
    counts = jnp.sum(sel, axis=0)
    rank = jnp.cumsum(sel, axis=0) - sel
    padded = (counts + ROW_CHUNK - 1) // ROW_CHUNK * ROW_CHUNK
    pad_ends = jnp.cumsum(padded)
    pad_starts = pad_ends - padded
    dest_dense = pad_starts[None, :] + rank
    dest = jnp.take_along_axis(dest_dense, top_e, axis=1)
    tok = jnp.broadcast_to(jnp.arange(t, dtype=jnp.int32)[:, None], (t, TOP_K))
    slot_tok = jnp.zeros((n_pad,), jnp.int32).at[dest.reshape(-1)].set(tok.reshape(-1))
    units_per_e = (padded + UNIT_ROWS - 1) // UNIT_ROWS
    unit_ends = jnp.cumsum(units_per_e)
    unit_starts = unit_ends - units_per_e
    n_units = unit_ends[-1]
    u = jnp.arange(max_units, dtype=jnp.int32)
    u_clamped = jnp.minimum(u, n_units - 1)
    ue = jnp.searchsorted(unit_ends, u_clamped, side='right').astype(jnp.int32)
    ue = jnp.minimum(ue, N_EXPERTS - 1)
    j = u_clamped - unit_starts[ue]
    u_row = pad_starts[ue] + j * UNIT_ROWS
    u_rows = jnp.minimum(UNIT_ROWS, padded[ue] - j * UNIT_ROWS)
    u_chunks = jnp.where(u < n_units, u_rows // ROW_CHUNK, 0).astype(jnp.int32)
    n_used_rows = pad_ends[-1].astype(jnp.int32)
    return dest.astype(jnp.int32), slot_tok, ue, u_row.astype(jnp.int32), u_chunks, n_used_rows, n_pad, max_units


GATHER_ROWS = 512


def _gather_kernel(used_ref, tok_ref, x_hbm, o_ref, sem):
    i = pl.program_id(0)

    def row_copy(r):
        return pltpu.make_async_copy(x_hbm.at[pl.ds(tok_ref[r], 1), :], o_ref.at[pl.ds(r, 1), :], sem)

    @pl.when(i * GATHER_ROWS < used_ref[0])
    def _():
        def issue(r, c):
            row_copy(r).start()
            return c

        lax.fori_loop(0, GATHER_ROWS, issue, 0)
        pltpu.make_async_copy(x_hbm.at[pl.ds(0, GATHER_ROWS), :], o_ref, sem).wait()

    @pl.when(i * GATHER_ROWS >= used_ref[0])
    def _():
        o_ref[...] = jnp.zeros_like(o_ref)


def _gather_rows(x1, slot_tok, n_used_rows, n_pad):
    d = x1.shape[1]
    return pl.pallas_call(
        _gather_kernel,
        grid_spec=pltpu.PrefetchScalarGridSpec(
            num_scalar_prefetch=1,
            grid=(n_pad // GATHER_ROWS,),
            in_specs=[pl.BlockSpec((GATHER_ROWS,), lambda i, used: (i,), memory_space=pltpu.SMEM),
                      pl.BlockSpec(memory_space=pl.ANY)],
            out_specs=pl.BlockSpec((GATHER_ROWS, d), lambda i, used: (i, 0)),
            scratch_shapes=[pltpu.SemaphoreType.DMA(())],
        ),
        out_shape=jax.ShapeDtypeStruct((n_pad, d), f32),
        compiler_params=_cparams(("arbitrary",)),
        name="moe_gather",
    )(n_used_rows.reshape(1), slot_tok, x1)


def _moe_kernel(ue_ref, urow_ref, uchunks_ref, used_ref, xs_hbm, wg_ref, bg_ref, wu_ref, bu_ref, wd_ref, bd_ref,
                y_hbm, xbuf, yacc, wgb, wub, wdb, stage, in_sem, out_sem):
    u = pl.program_id(0)
    f = pl.program_id(1)
    n_u = pl.num_programs(0)
    n_f = pl.num_programs(1)
    n_chunks = uchunks_ref[u]
    row0 = urow_ref[u]

    def rows(c):
        return pl.ds(pl.multiple_of(c * ROW_CHUNK, ROW_CHUNK), ROW_CHUNK)

    def in_copy(c, slot):
        src = xs_hbm.at[pl.ds(pl.multiple_of(row0 + c * ROW_CHUNK, ROW_CHUNK), ROW_CHUNK), :]
        return pltpu.make_async_copy(src, stage.at[slot], in_sem.at[slot])

    def out_copy(c, slot):
        dst = y_hbm.at[pl.ds(pl.multiple_of(row0 + c * ROW_CHUNK, ROW_CHUNK), ROW_CHUNK), :]
        return pltpu.make_async_copy(stage.at[slot], dst, out_sem.at[slot])

    @pl.when(jnp.logical_and(f == 0, n_chunks > 0))
    def _():
        in_copy(0, 0).start()

        def load(c, carry):
            slot = c % 2

            @pl.when(c + 1 < n_chunks)
            def _():
                in_copy(c + 1, 1 - slot).start()

            in_copy(c, slot).wait()
            xbuf[rows(c), :] = stage[slot].astype(bf16)
            yacc[rows(c), :] = jnp.zeros((ROW_CHUNK, D_MODEL), f32)
            return carry

        lax.fori_loop(0, n_chunks, load, 0)

    @pl.when(n_chunks > 0)
    def _():
        wgb[...] = wg_ref[...].astype(bf16)
        wub[...] = wu_ref[...].astype(bf16)
        wdb[...] = wd_ref[...].astype(bf16)
        bg = bg_ref[...]
        bu = bu_ref[...]

        def chunk(c, carry):
            xc = xbuf[rows(c), :]
            gt = jnp.dot(xc, wgb[...], preferred_element_type=f32) + bg
            up = jnp.dot(xc, wub[...], preferred_element_type=f32) + bu
            gt = jnp.minimum(gt, SWIGLU_LIMIT)
            up = jnp.clip(up, -SWIGLU_LIMIT, SWIGLU_LIMIT)
            hid = (up + 1.0) * gt * jax.nn.sigmoid(SWIGLU_ALPHA * gt)
            yacc[rows(c), :] += jnp.dot(hid.astype(bf16), wdb[...], preferred_element_type=f32)
            return carry

        lax.fori_loop(0, n_chunks, chunk, 0)

    @pl.when(jnp.logical_and(f == n_f - 1, n_chunks > 0))
    def _():
        bd = bd_ref[...]

        def store(c, carry):
            slot = c % 2

            @pl.when(c >= 2)
            def _():
                out_copy(c - 2, slot).wait()

            stage[slot] = yacc[rows(c), :] + bd
            out_copy(c, slot).start()
            return carry

        lax.fori_loop(0, n_chunks, store, 0)

        @pl.when(n_chunks >= 2)
        def _():
            out_copy(n_chunks - 2, n_chunks % 2).wait()

        out_copy(n_chunks - 1, (n_chunks - 1) % 2).wait()

    @pl.when(jnp.logical_and(u == n_u - 1, f == n_f - 1))
    def _():
        first = used_ref[0] // ROW_CHUNK
        last = y_hbm.shape[0] // ROW_CHUNK
        stage[0] = jnp.zeros((ROW_CHUNK, D_MODEL), f32)

        def tail_copy(c):
            dst = y_hbm.at[pl.ds(pl.multiple_of(c * ROW_CHUNK, ROW_CHUNK), ROW_CHUNK), :]
            return pltpu.make_async_copy(stage.at[0], dst, out_sem.at[0])

        def start(c, carry):
            tail_copy(c).start()
            return carry

        def wait(c, carry):
            tail_copy(c).wait()
            return carry

        lax.fori_loop(first, last, start, 0)
        lax.fori_loop(first, last, wait, 0)


def _moe_experts(xs, ue, u_row, u_chunks, n_used_rows, max_units, w_gate, b_gate, w_up, b_up, w_down, b_down):
    n_pad, d = xs.shape
    e, _, dff = w_gate.shape
    n_f = dff // F_TILE
    f_idx = lambda u, f, uc: jnp.where(uc[u] > 0, f, n_f - 1)
    col_w = pl.BlockSpec((None, d, F_TILE), lambda u, f, ue, ur, uc, used: (ue[u], 0, f_idx(u, f, uc)))
    col_b = pl.BlockSpec((None, 1, F_TILE), lambda u, f, ue, ur, uc, used: (ue[u], 0, f_idx(u, f, uc)))
    return pl.pallas_call(
        _moe_kernel,
        grid_spec=pltpu.PrefetchScalarGridSpec(
            num_scalar_prefetch=4,
            grid=(max_units, n_f),
            in_specs=[pl.BlockSpec(memory_space=pl.ANY),
                      col_w, col_b, col_w, col_b,
                      pl.BlockSpec((None, F_TILE, d), lambda u, f, ue, ur, uc, used: (ue[u], f_idx(u, f, uc), 0)),
                      pl.BlockSpec((None, 1, d), lambda u, f, ue, ur, uc, used: (ue[u], 0, 0))],
            out_specs=pl.BlockSpec(memory_space=pl.ANY),
            scratch_shapes=[pltpu.VMEM((UNIT_ROWS, d), bf16),
                            pltpu.VMEM((UNIT_ROWS, d), f32),
                            pltpu.VMEM((d, F_TILE), bf16),
                            pltpu.VMEM((d, F_TILE), bf16),
                            pltpu.VMEM((F_TILE, d), bf16),
                            pltpu.VMEM((2, ROW_CHUNK, d), f32),
                            pltpu.SemaphoreType.DMA((2,)),
                            pltpu.SemaphoreType.DMA((2,))],
        ),
        out_shape=jax.ShapeDtypeStruct((n_pad, d), f32),
        compiler_params=_cparams(("arbitrary", "arbitrary")),
        name="moe_experts",
    )(ue, u_row, u_chunks, n_used_rows.reshape(1), xs, w_gate, b_gate.reshape(e, 1, dff), w_up, b_up.reshape(e, 1, dff), w_down,
      b_down.reshape(e, 1, d))


COMBINE_ROWS = 256


def _combine_kernel(dest_ref, y_hbm, gate_ref, x1_ref, p_ref, lnw_ref, lnb_ref, wp_ref, pnw_ref, wg_ref,
                    o_ref, ybuf, sem):
    tm = COMBINE_ROWS

    def issue(tk, c):
        t = tk // TOP_K
        k = tk % TOP_K
        pltpu.make_async_copy(y_hbm.at[pl.ds(dest_ref[tk], 1), :], ybuf.at[k, pl.ds(t, 1), :], sem).start()
        return c

    lax.fori_loop(0, tm * TOP_K, issue, 0)
    e = jnp.dot(p_ref[...].astype(bf16), wp_ref[...], preferred_element_type=f32)
    e = e * lax.rsqrt(jnp.mean(e * e, axis=-1, keepdims=True) + LN_EPS) * pnw_ref[...]
    for k in range(TOP_K):
        pltpu.make_async_copy(y_hbm.at[pl.ds(0, tm), :], ybuf.at[k], sem).wait()
    gates = gate_ref[...]
    fsum = gates[:, 0:1] * ybuf[0]
    for k in range(1, TOP_K):
        fsum = fsum + gates[:, k:k + 1] * ybuf[k]
    x2 = _layer_norm(DN_ALPHA * x1_ref[...] + fsum, lnw_ref[...], lnb_ref[...])
    gate = jax.nn.sigmoid(jnp.dot(x2.astype(bf16), wg_ref[...], preferred_element_type=f32))
    o_ref[...] = x2 + gate * e


def _combine(y, dest, gates, x1, p2d, ln_w, ln_b, wp_bf, ple_norm_w, wg_bf):
    t, d = x1.shape
    tm = COMBINE_ROWS
    row = lambda c: pl.BlockSpec((tm, c), lambda i: (i, 0))
    full = lambda r, c: pl.BlockSpec((r, c), lambda i: (0, 0))
    return pl.pallas_call(
        _combine_kernel,
        grid=(t // tm,),
        in_specs=[pl.BlockSpec((tm * TOP_K,), lambda i: (i,), memory_space=pltpu.SMEM),
                  pl.BlockSpec(memory_space=pl.ANY),
                  row(TOP_K), row(d), row(PLE_DIM), full(1, d), full(1, d), full(PLE_DIM, d), full(1, d),
                  full(d, d)],
        out_specs=row(d),
        out_shape=jax.ShapeDtypeStruct((t, d), f32),
        scratch_shapes=[pltpu.VMEM((TOP_K, tm, d), f32), pltpu.SemaphoreType.DMA(())],
        compiler_params=_cparams(("arbitrary",)),
        name="combine_ln2_ple",
    )(dest.reshape(-1), y, gates, x1, p2d, ln_w.reshape(1, d), ln_b.reshape(1, d), wp_bf,
      ple_norm_w.reshape(1, d), wg_bf)


def _layer(h, p_i, w_in, ret_gn_w, conv_w, conv_b, lru_wa, lru_ba, lru_wx, lru_bx, lru_lam, w_out,
           ln1_w, ln1_b, w_router, b_router, w_gate, b_gate, w_up, b_up, w_down, b_down,
           ln2_w, ln2_b, w_ple_proj, ple_norm_w, w_ple_gate):
    b, s, d = h.shape
    t = b * s
    x2d = h.reshape(t, d)
    proj = _in_proj(x2d, w_in.astype(bf16)).reshape(b, s, IN_COLS)
    ret_out = _retention(proj, ret_gn_w)
    lru_out = _lru(proj, conv_w, conv_b, lru_wa, lru_ba, lru_wx, lru_bx, lru_lam)
    x1, top_e, gates = _out_router(ret_out.reshape(t, RET_WIDTH), lru_out.reshape(t, LRU_WIDTH),
                                   w_out.astype(bf16), x2d, ln1_w, ln1_b, w_router, b_router)
    dest, slot_tok, ue, u_row, u_chunks, n_used_rows, n_pad, max_units = _routing_tables(top_e, t)
    xs = _gather_rows(x1, slot_tok, n_used_rows, n_pad)
    y = _moe_experts(xs, ue, u_row, u_chunks, n_used_rows, max_units, w_gate, b_gate, w_up, b_up, w_down, b_down)
    out = _combine(y, dest, gates, x1, p_i.reshape(t, PLE_DIM), ln2_w, ln2_b, w_ple_proj.astype(bf16),
                   ple_norm_w, w_ple_gate.astype(bf16))
    return out.reshape(b, s, d)


def kernel(x, p, w_in, ret_gn_w, conv_w, conv_b, lru_wa, lru_ba, lru_wx, lru_bx, lru_lam, w_out, ln1_w, ln1_b,
           w_router, b_router, w_gate, b_gate, w_up, b_up, w_down, b_down, ln2_w, ln2_b, w_ple_proj, ple_norm_w,
           w_ple_gate):
    h = x.astype(f32)
    for i in range(w_in.shape[0]):
        h = _layer(h, p[i], w_in[i], ret_gn_w[i], conv_w[i], conv_b[i], lru_wa[i], lru_ba[i], lru_wx[i],
                   lru_bx[i], lru_lam[i], w_out[i], ln1_w[i], ln1_b[i], w_router[i], b_router[i], w_gate[i],
                   b_gate[i], w_up[i], b_up[i], w_down[i], b_down[i], ln2_w[i], ln2_b[i], w_ple_proj[i],
                   ple_norm_w[i], w_ple_gate[i])
    return h.astype(x.dtype)
```

```python
import functools
import math

import jax
import jax.numpy as jnp
from jax import lax
from jax.experimental import pallas as pl
from jax.experimental.pallas import tpu as pltpu

D_MODEL = 2048
RET_HEAD_DIM = 128
RET_HEADS = 8
RET_WIDTH = RET_HEADS * RET_HEAD_DIM
LRU_WIDTH = D_MODEL - RET_WIDTH
LRU_BLOCKS = 8
LRU_BLOCK_DIM = LRU_WIDTH // LRU_BLOCKS
IN_COLS = 4 * RET_WIDTH + 2 * LRU_WIDTH
CONV_WIDTH = 4
LRU_C = 8.0
CHUNK = 128
ROPE_BASE = 10000.0
N_EXPERTS = 32
TOP_K = 4
SWIGLU_LIMIT = 7.0
SWIGLU_ALPHA = 1.702
PLE_DIM = 256
LN_EPS = 1e-5
DEPTH = 1
DN_ALPHA = (2.0 * DEPTH) ** 0.25

LANES = 128
SUBLANES = 8
VMEM_LIMIT = 60 * 1024 * 1024

ROW_CHUNK = 256
UNIT_ROWS = 2048
F_TILE = 256

f32 = jnp.float32
bf16 = jnp.bfloat16


def _cparams(sem):
    return pltpu.CompilerParams(dimension_semantics=sem, vmem_limit_bytes=VMEM_LIMIT)


def _in_proj_kernel(x_ref, w_ref, o_ref, xb_ref):
    @pl.when(pl.program_id(1) == 0)
    def _():
        xb_ref[...] = x_ref[...].astype(bf16)

    o_ref[...] = jnp.dot(xb_ref[...], w_ref[...], preferred_element_type=f32)


def _in_proj(x2d, w_bf):
    t, d = x2d.shape
    n = w_bf.shape[1]
    tm, tn = 1024, 1024
    return pl.pallas_call(
        _in_proj_kernel,
        grid=(t // tm, n // tn),
        in_specs=[pl.BlockSpec((tm, d), lambda i, j: (i, 0)),
                  pl.BlockSpec((d, tn), lambda i, j: (0, j))],
        out_specs=pl.BlockSpec((tm, tn), lambda i, j: (i, j)),
        out_shape=jax.ShapeDtypeStruct((t, n), f32),
        scratch_shapes=[pltpu.VMEM((tm, d), bf16)],
        compiler_params=_cparams(("parallel", "arbitrary")),
        name="in_proj",
    )(x2d, w_bf)


def _retention_kernel(q_ref, k_ref, v_ref, g_ref, cos_ref, sin_ref, dec_ref, qd_ref, kd_ref, cd_ref, gnw_ref,
                      o_ref):
    s = q_ref.shape[0]
    n_chunks = s // CHUNK
    decay = dec_ref[...]
    q_dec = qd_ref[...]
    k_dec = kd_ref[...]
    c_dec = cd_ref[0:1, :]
    gnw = gnw_ref[...]
    k_scale = RET_HEAD_DIM ** -0.5

    def rope(xv, cos, sin):
        return xv * cos + pltpu.roll(xv, RET_HEAD_DIM // 2, axis=1) * sin

    def body(n, state):
        sl = pl.ds(pl.multiple_of(n * CHUNK, CHUNK), CHUNK)
        cos = cos_ref[sl, :]
        sin = sin_ref[sl, :]
        q = rope(q_ref[sl, :], cos, sin)
        k = rope(k_ref[sl, :], cos, sin) * k_scale
        vb = v_ref[sl, :].astype(bf16)
        scores = lax.dot_general(q.astype(bf16), k.astype(bf16), (((1,), (1,)), ((), ())),
                                 preferred_element_type=f32) * decay
        intra = jnp.dot(scores.astype(bf16), vb, preferred_element_type=f32)
        cross = jnp.dot((q * q_dec).astype(bf16), state.astype(bf16), preferred_element_type=f32)
        kv = lax.dot_general((k * k_dec).astype(bf16), vb, (((0,), (0,)), ((), ())),
                             preferred_element_type=f32)
        ret = intra + cross
        mu = jnp.mean(ret, axis=-1, keepdims=True)
        cen = ret - mu
        var = jnp.mean(cen * cen, axis=-1, keepdims=True)
        ret = cen * lax.rsqrt(var + LN_EPS) * gnw
        g = g_ref[sl, :]
        o_ref[sl, :] = (g * jax.nn.sigmoid(g) * ret).astype(o_ref.dtype)
        return c_dec * state + kv

    lax.fori_loop(0, n_chunks, body, jnp.zeros((RET_HEAD_DIM, RET_HEAD_DIM), f32), unroll=2)


def _retention_tables(s):
    h, d = RET_HEADS, RET_HEAD_DIM
    inv = ROPE_BASE ** (-jnp.arange(0, d, 2, dtype=f32) / d)
    ang = jnp.arange(s, dtype=f32)[:, None] * inv[None, :]
    cos = jnp.cos(ang)
    sin = jnp.sin(ang)
    cos_t = jnp.concatenate([cos, cos], axis=-1)
    sin_t = jnp.concatenate([-sin, sin], axis=-1)
    log_gamma = jnp.log1p(-jnp.exp2(-5.0 - jnp.arange(h, dtype=f32)))
    idx = jnp.arange(CHUNK, dtype=f32)
    diff = idx[:, None] - idx[None, :]
    decay = jnp.where((diff >= 0)[None], jnp.exp(jnp.maximum(diff, 0.0)[None] * log_gamma[:, None, None]), 0.0)
    q_dec = jnp.exp((idx[None, :] + 1.0) * log_gamma[:, None])
    k_dec = jnp.exp((CHUNK - 1.0 - idx)[None, :] * log_gamma[:, None])
    c_dec = jnp.exp(CHUNK * log_gamma)
    q_dec = jnp.broadcast_to(q_dec[:, :, None], (h, CHUNK, d))
    k_dec = jnp.broadcast_to(k_dec[:, :, None], (h, CHUNK, d))
    c_dec = jnp.broadcast_to(c_dec[:, None, None], (h, SUBLANES, d))
    return cos_t, sin_t, decay, q_dec, k_dec, c_dec


def _retention(proj, ret_gn_w):
    b, s, _ = proj.shape
    d = RET_HEAD_DIM
    cos_t, sin_t, decay, q_dec, k_dec, c_dec = _retention_tables(s)
    col = lambda off: pl.BlockSpec((None, s, d), lambda bi, hi: (bi, 0, off + hi))
    per_head = lambda r: pl.BlockSpec((None, r, d), lambda bi, hi: (hi, 0, 0))
    full = lambda shp: pl.BlockSpec(shp, lambda bi, hi: (0,) * len(shp))
    return pl.pallas_call(
        _retention_kernel,
        grid=(b, RET_HEADS),
        in_specs=[col(0), col(RET_HEADS), col(2 * RET_HEADS), col(3 * RET_HEADS),
                  full((s, d)), full((s, d)),
                  per_head(CHUNK), per_head(CHUNK), per_head(CHUNK), per_head(SUBLANES),
                  pl.BlockSpec((1, d), lambda bi, hi: (0, hi))],
        out_specs=pl.BlockSpec((None, s, d), lambda bi, hi: (bi, 0, hi)),
        out_shape=jax.ShapeDtypeStruct((b, s, RET_WIDTH), bf16),
        compiler_params=_cparams(("parallel", "parallel")),
        name="retention",
    )(proj, proj, proj, proj, cos_t, sin_t, decay, q_dec, k_dec, c_dec, ret_gn_w.reshape(1, RET_WIDTH))


def _gelu_tanh(x):
    return 0.5 * x * (1.0 + jnp.tanh(math.sqrt(2.0 / math.pi) * (x + 0.044715 * (x * x * x))))


def _lru_kernel(xr_ref, yg_ref, cw_ref, cb_ref, wa_ref, ba_ref, wx_ref, bx_ref, lam_ref, o_ref, a_ref, b_ref):
    s = xr_ref.shape[0]
    x = xr_ref[...]
    rows = lax.broadcasted_iota(jnp.int32, x.shape, 0)
    xc = cb_ref[...] + cw_ref[CONV_WIDTH - 1:CONV_WIDTH, :] * x
    for back in range(1, CONV_WIDTH):
        shifted = jnp.where(rows >= back, pltpu.roll(x, back, axis=0), 0.0)
        xc = xc + cw_ref[CONV_WIDTH - 1 - back:CONV_WIDTH - back, :] * shifted
    xcb = xc.astype(bf16)
    r = jax.nn.sigmoid(jnp.dot(xcb, wa_ref[...].astype(bf16), preferred_element_type=f32) + ba_ref[...])
    gi = jax.nn.sigmoid(jnp.dot(xcb, wx_ref[...].astype(bf16), preferred_element_type=f32) + bx_ref[...])
    lam = lam_ref[...]
    log_sig = jnp.minimum(lam, 0.0) - jnp.log1p(jnp.exp(-jnp.abs(lam)))
    log_a = LRU_C * r * log_sig
    a = jnp.exp(log_a)
    a_ref[...] = a
    b_ref[...] = jnp.sqrt(-jnp.tanh(log_a) * (a * a + 1.0)) * (gi * xc)

    row8 = lax.broadcasted_iota(jnp.int32, (SUBLANES, LANES), 0)

    def body(i, h_prev):
        sl = pl.ds(pl.multiple_of(i * SUBLANES, SUBLANES), SUBLANES)
        a8 = a_ref[sl, :]
        b8 = b_ref[sl, :]
        for sh in (1, 2, 4):
            a_sh = jnp.where(row8 >= sh, pltpu.roll(a8, sh, axis=0), 1.0)
            b_sh = jnp.where(row8 >= sh, pltpu.roll(b8, sh, axis=0), 0.0)
            b8 = a8 * b_sh + b8
            a8 = a8 * a_sh
        h8 = a8 * h_prev + b8
        o_ref[sl, :] = (_gelu_tanh(yg_ref[sl, :]) * h8).astype(o_ref.dtype)
        return h8[SUBLANES - 1:SUBLANES, :]

    lax.fori_loop(0, s // SUBLANES, body, jnp.zeros((1, LANES), f32), unroll=8)


def _lru(proj, conv_w, conv_b, wa, ba, wx, bx, lam):
    b, s, _ = proj.shape
    d = LRU_BLOCK_DIM
    xr_off = 4 * RET_WIDTH // d
    yg_off = xr_off + LRU_BLOCKS
    col = lambda off: pl.BlockSpec((None, s, d), lambda bi, ji: (bi, 0, off + ji))
    vec = lambda r: pl.BlockSpec((r, d), lambda bi, ji: (0, ji))
    blk = lambda r: pl.BlockSpec((None, r, d), lambda bi, ji: (ji, 0, 0))
    return pl.pallas_call(
        _lru_kernel,
        grid=(b, LRU_BLOCKS),
        in_specs=[col(xr_off), col(yg_off), vec(CONV_WIDTH), vec(1), blk(d), blk(1), blk(d), blk(1), vec(1)],
        out_specs=pl.BlockSpec((None, s, d), lambda bi, ji: (bi, 0, ji)),
        out_shape=jax.ShapeDtypeStruct((b, s, LRU_WIDTH), bf16),
        scratch_shapes=[pltpu.VMEM((s, d), f32), pltpu.VMEM((s, d), f32)],
        compiler_params=_cparams(("parallel", "parallel")),
        name="rg_lru",
    )(proj, proj, conv_w, conv_b.reshape(1, LRU_WIDTH), wa, ba.reshape(LRU_BLOCKS, 1, d), wx,
      bx.reshape(LRU_BLOCKS, 1, d), lam.reshape(1, LRU_WIDTH))


def _layer_norm(y, w, b):
    mu = jnp.mean(y, axis=-1, keepdims=True)
    cen = y - mu
    var = jnp.mean(cen * cen, axis=-1, keepdims=True)
    return cen * lax.rsqrt(var + LN_EPS) * w + b


SLAB = D_MODEL // LANES
PITCH = SLAB + 4


def _store_slabs(ref, val, zero_pad):
    n = val.shape[0]
    for j in range(SLAB):
        ref[pl.ds(j, n, stride=PITCH), :] = val[:, j * LANES:(j + 1) * LANES]
    if zero_pad:
        for j in range(SLAB, PITCH):
            ref[pl.ds(j, n, stride=PITCH), :] = jnp.zeros((n, LANES), val.dtype)


def _slab_cols(ref, j, n):
    return ref[pl.ds(j, n, stride=PITCH), :]


def _split_bf16(v):
    hi = v.astype(bf16)
    lo = (v - hi.astype(f32)).astype(bf16)
    return hi, lo


def _out_router_kernel(ret_ref, lru_ref, wo_ref, x_ref, lnw_ref, lnb_ref, wr_ref, br_ref,
                       x1_ref, x1s_ref, tope_ref, gate_ref):
    m = jnp.dot(ret_ref[...], wo_ref[0:RET_WIDTH, :], preferred_element_type=f32)
    m = m + jnp.dot(lru_ref[...], wo_ref[RET_WIDTH:D_MODEL, :], preferred_element_type=f32)
    x1 = _layer_norm(DN_ALPHA * x_ref[...] + m, lnw_ref[...], lnb_ref[...])
    x1_ref[...] = x1
    _store_slabs(x1s_ref, x1, zero_pad=True)
    xh, xl = _split_bf16(x1)
    wh, wl = _split_bf16(wr_ref[...])
    logits = (jnp.dot(xh, wh, preferred_element_type=f32) + jnp.dot(xl, wh, preferred_element_type=f32)
              + jnp.dot(xh, wl, preferred_element_type=f32)) + br_ref[...]
    tm = logits.shape[0]
    lane = lax.broadcasted_iota(jnp.int32, logits.shape, 1)
    lane_k = lax.broadcasted_iota(jnp.int32, (tm, TOP_K), 1)
    top_e = jnp.zeros((tm, TOP_K), jnp.int32)
    top_v = jnp.zeros((tm, TOP_K), f32)
    cur = logits
    for kk in range(TOP_K):
        mx = jnp.max(cur, axis=-1, keepdims=True)
        idx = jnp.min(jnp.where(cur == mx, lane, N_EXPERTS), axis=-1, keepdims=True)
        top_e = jnp.where(lane_k == kk, idx, top_e)
        top_v = jnp.where(lane_k == kk, mx, top_v)
        cur = jnp.where(lane == idx, -jnp.inf, cur)
    ex = jnp.exp(top_v - top_v[:, 0:1])
    gate_ref[...] = ex / jnp.sum(ex, axis=-1, keepdims=True)
    tope_ref[...] = top_e


def _out_router(ret_out, lru_out, wo_bf, x2d, ln_w, ln_b, w_router, b_router):
    t, d = x2d.shape
    tm = 256
    row = lambda c: pl.BlockSpec((tm, c), lambda i: (i, 0))
    full = lambda r, c: pl.BlockSpec((r, c), lambda i: (0, 0))
    return pl.pallas_call(
        _out_router_kernel,
        grid=(t // tm,),
        in_specs=[row(RET_WIDTH), row(LRU_WIDTH), full(d, d), row(d), full(1, d), full(1, d),
                  full(d, N_EXPERTS), full(1, N_EXPERTS)],
        out_specs=[row(d), pl.BlockSpec((tm * PITCH, LANES), lambda i: (i, 0)), row(TOP_K), row(TOP_K)],
        out_shape=[jax.ShapeDtypeStruct((t, d), f32), jax.ShapeDtypeStruct((t * PITCH, LANES), f32),
                   jax.ShapeDtypeStruct((t, TOP_K), jnp.int32),
                   jax.ShapeDtypeStruct((t, TOP_K), f32)],
        compiler_params=_cparams(("parallel",)),
        name="out_proj_ln1_router",
    )(ret_out, lru_out, wo_bf, x2d, ln_w.reshape(1, d), ln_b.reshape(1, d), w_router,
      b_router.reshape(1, N_EXPERTS))


def _routing_tables(top_e, t):
    n_pad = t * TOP_K + N_EXPERTS * ROW_CHUNK
    max_units = N_EXPERTS + (t * TOP_K) // UNIT_ROWS
    sel = (top_e[:, :, None] == jnp.arange(N_EXPERTS, dtype=jnp.int32)[None, None, :]).any(axis=1)
    sel = sel.astype(jnp.int32)
    counts = jnp.sum(sel, axis=0)
    rank = jnp.cumsum(sel, axis=0) - sel
    padded = (counts + ROW_CHUNK - 1) // ROW_CHUNK * ROW_CHUNK
    pad_ends = jnp.cumsum(padded)
    pad_starts = pad_ends - padded
    dest_dense = pad_starts[None, :] + rank
    dest = jnp.take_along_axis(dest_dense, top_e, axis=1)
    tok = jnp.broadcast_to(jnp.arange(t, dtype=jnp.int32)[:, None], (t, TOP_K))
    slot_tok = jnp.zeros((n_pad + UNIT_ROWS,), jnp.int32).at[dest.reshape(-1)].set(tok.reshape(-1))
    units_per_e = (padded + UNIT_ROWS - 1) // UNIT_ROWS
    unit_ends = jnp.cumsum(units_per_e)
    unit_starts = unit_ends - units_per_e
    n_units = unit_ends[-1]
    u = jnp.arange(max_units, dtype=jnp.int32)
    u_clamped = jnp.minimum(u, n_units - 1)
    ue = jnp.searchsorted(unit_ends, u_clamped, side='right').astype(jnp.int32)
    ue = jnp.minimum(ue, N_EXPERTS - 1)
    j = u_clamped - unit_starts[ue]
    u_row = pad_starts[ue] + j * UNIT_ROWS
    u_rows = jnp.minimum(UNIT_ROWS, padded[ue] - j * UNIT_ROWS)
    u_chunks = jnp.where(u < n_units, u_rows // ROW_CHUNK, 0).astype(jnp.int32)
    n_used_rows = pad_ends[-1].astype(jnp.int32)
    return dest.astype(jnp.int32), slot_tok, ue, u_row.astype(jnp.int32), u_chunks, n_used_rows, n_pad, max_units


CHUNK_PITCHED = ROW_CHUNK * PITCH


def _moe_kernel(ue_ref, urow_ref, uchunks_ref, used_ref, tok_hbm, x1s_hbm, wg_ref, bg_ref, wu_ref, bu_ref, wd_ref,
                bd_ref, y_hbm, xbuf, yacc, wgub, wdb, stage, tok_smem, tok_sem, in_sem, out_sem):
    u = pl.program_id(0)
    f = pl.program_id(1)
    n_u = pl.num_programs(0)
    n_f = pl.num_programs(1)
    n_chunks = uchunks_ref[u]
    row0 = urow_ref[u]

    def rows(c, size=ROW_CHUNK):
        return pl.ds(pl.multiple_of(c * size, size), size)

    def gather_start(c, slot):
        def issue(r, carry):
            tok = tok_smem[c * ROW_CHUNK + r]
            src = x1s_hbm.at[pl.ds(tok * PITCH, SLAB), :]
            dst = stage.at[slot, pl.ds(r * PITCH, SLAB), :]
            pltpu.make_async_copy(src, dst, in_sem.at[slot]).start()
            return carry

        lax.fori_loop(0, ROW_CHUNK, issue, 0, unroll=8)

    def gather_wait(slot):
        n = ROW_CHUNK * SLAB
        pltpu.make_async_copy(x1s_hbm.at[pl.ds(0, n), :], stage.at[slot, pl.ds(0, n), :], in_sem.at[slot]).wait()

    def out_copy(c, slot):
        dst = y_hbm.at[pl.ds(pl.multiple_of((row0 + c * ROW_CHUNK) * PITCH, CHUNK_PITCHED), CHUNK_PITCHED), :]
        return pltpu.make_async_copy(stage.at[slot], dst, out_sem.at[slot])

    @pl.when(jnp.logical_and(f == 0, n_chunks > 0))
    def _():
        tok_copy = pltpu.make_async_copy(tok_hbm.at[pl.ds(pl.multiple_of(row0, ROW_CHUNK), UNIT_ROWS)], tok_smem,
                                         tok_sem)
        tok_copy.start()
        tok_copy.wait()
        gather_start(0, 0)

        def load(c, carry):
            slot = c % 2

            @pl.when(c + 1 < n_chunks)
            def _():
                gather_start(c + 1, 1 - slot)

            gather_wait(slot)
            for j in range(SLAB):
                xbuf[rows(c), j * LANES:(j + 1) * LANES] = _slab_cols(stage.at[slot], j, ROW_CHUNK).astype(bf16)
            yacc[rows(c), :] = jnp.zeros((ROW_CHUNK, D_MODEL), f32)
            return carry

        lax.fori_loop(0, n_chunks, load, 0)

    @pl.when(n_chunks > 0)
    def _():
        wgub[:, 0:F_TILE] = wg_ref[...].astype(bf16)
        wgub[:, F_TILE:2 * F_TILE] = wu_ref[...].astype(bf16)
        wdb[...] = wd_ref[...].astype(bf16)
        bg = bg_ref[...]
        bu = bu_ref[...]

        def mlp(sl):
            gu = jnp.dot(xbuf[sl, :], wgub[...], preferred_element_type=f32)
            gt = jnp.minimum(gu[:, 0:F_TILE] + bg, SWIGLU_LIMIT)
            up = jnp.clip(gu[:, F_TILE:2 * F_TILE] + bu, -SWIGLU_LIMIT, SWIGLU_LIMIT)
            hid = (up + 1.0) * gt * jax.nn.sigmoid(SWIGLU_ALPHA * gt)
            yacc[sl, :] += jnp.dot(hid.astype(bf16), wdb[...], preferred_element_type=f32)

        def pair(c, carry):
            mlp(rows(c, 2 * ROW_CHUNK))
            return carry

        lax.fori_loop(0, n_chunks // 2, pair, 0)

        @pl.when(n_chunks % 2 == 1)
        def _():
            mlp(rows(n_chunks - 1))

    @pl.when(jnp.logical_and(f == n_f - 1, n_chunks > 0))
    def _():
        bd = bd_ref[...]

        def store(c, carry):
            slot = c % 2

            @pl.when(c >= 2)
            def _():
                out_copy(c - 2, slot).wait()

            _store_slabs(stage.at[slot], yacc[rows(c), :] + bd, zero_pad=True)
            out_copy(c, slot).start()
            return carry

        lax.fori_loop(0, n_chunks, store, 0)

        @pl.when(n_chunks >= 2)
        def _():
            out_copy(n_chunks - 2, n_chunks % 2).wait()

        out_copy(n_chunks - 1, (n_chunks - 1) % 2).wait()

    @pl.when(jnp.logical_and(u == n_u - 1, f == n_f - 1))
    def _():
        first = used_ref[0] // ROW_CHUNK
        last = y_hbm.shape[0] // CHUNK_PITCHED
        stage[0] = jnp.zeros((CHUNK_PITCHED, LANES), f32)

        def tail_copy(c):
            dst = y_hbm.at[pl.ds(pl.multiple_of(c * CHUNK_PITCHED, CHUNK_PITCHED), CHUNK_PITCHED), :]
            return pltpu.make_async_copy(stage.at[0], dst, out_sem.at[0])

        def start(c, carry):
            tail_copy(c).start()
            return carry

        def wait(c, carry):
            tail_copy(c).wait()
            return carry

        lax.fori_loop(first, last, start, 0)
        lax.fori_loop(first, last, wait, 0)


def _moe_experts(x1s, slot_tok, ue, u_row, u_chunks, n_used_rows, n_pad, max_units, w_gate, b_gate, w_up, b_up,
                 w_down, b_down):
    e, d, dff = w_gate.shape
    n_f = dff // F_TILE
    f_idx = lambda u, f, uc: jnp.where(uc[u] > 0, f, n_f - 1)
    col_w = pl.BlockSpec((None, d, F_TILE), lambda u, f, ue, ur, uc, used: (ue[u], 0, f_idx(u, f, uc)))
    col_b = pl.BlockSpec((None, 1, F_TILE), lambda u, f, ue, ur, uc, used: (ue[u], 0, f_idx(u, f, uc)))
    return pl.pallas_call(
        _moe_kernel,
        grid_spec=pltpu.PrefetchScalarGridSpec(
            num_scalar_prefetch=4,
            grid=(max_units, n_f),
            in_specs=[pl.BlockSpec(memory_space=pl.ANY), pl.BlockSpec(memory_space=pl.ANY),
                      col_w, col_b, col_w, col_b,
                      pl.BlockSpec((None, F_TILE, d), lambda u, f, ue, ur, uc, used: (ue[u], f_idx(u, f, uc), 0)),
                      pl.BlockSpec((None, 1, d), lambda u, f, ue, ur, uc, used: (ue[u], 0, 0))],
            out_specs=pl.BlockSpec(memory_space=pl.ANY),
            scratch_shapes=[pltpu.VMEM((UNIT_ROWS, d), bf16),
                            pltpu.VMEM((UNIT_ROWS, d), f32),
                            pltpu.VMEM((d, 2 * F_TILE), bf16),
                            pltpu.VMEM((F_TILE, d), bf16),
                            pltpu.VMEM((2, CHUNK_PITCHED, LANES), f32),
                            pltpu.SMEM((UNIT_ROWS,), jnp.int32),
                            pltpu.SemaphoreType.DMA(()),
                            pltpu.SemaphoreType.DMA((2,)),
                            pltpu.SemaphoreType.DMA((2,))],
        ),
        out_shape=jax.ShapeDtypeStruct((n_pad * PITCH, LANES), f32),
        compiler_params=_cparams(("arbitrary", "arbitrary")),
        name="moe_experts",
    )(ue, u_row, u_chunks, n_used_rows.reshape(1), slot_tok, x1s, w_gate, b_gate.reshape(e, 1, dff), w_up,
      b_up.reshape(e, 1, dff), w_down, b_down.reshape(e, 1, d))


COMBINE_ROWS = 256


def _combine_kernel(dest_ref, y_hbm, gate_ref, x1_ref, p_ref, lnw_ref, lnb_ref, wp_ref, pnw_ref, wg_ref,
                    o_ref, ybuf, fsum_ref, sem):
    tm = COMBINE_ROWS

    def issue(t, c):
        for k in range(TOP_K):
            src = y_hbm.at[pl.ds(dest_ref[t * TOP_K + k] * PITCH, SLAB), :]
            dst = ybuf.at[k, pl.ds(t * PITCH, SLAB), :]
            pltpu.make_async_copy(src, dst, sem).start()
        return c

    lax.fori_loop(0, tm, issue, 0, unroll=2)
    e = jnp.dot(p_ref[...].astype(bf16), wp_ref[...], preferred_element_type=f32)
    e = e * lax.rsqrt(jnp.mean(e * e, axis=-1, keepdims=True) + LN_EPS) * pnw_ref[...]
    for k in range(TOP_K):
        pltpu.make_async_copy(y_hbm.at[pl.ds(0, tm * SLAB), :], ybuf.at[k, pl.ds(0, tm * SLAB), :], sem).wait()
    gates = gate_ref[...]
    for j in range(SLAB):
        acc = gates[:, 0:1] * _slab_cols(ybuf.at[0], j, tm)
        for k in range(1, TOP_K):
            acc = acc + gates[:, k:k + 1] * _slab_cols(ybuf.at[k], j, tm)
        fsum_ref[:, j * LANES:(j + 1) * LANES] = acc
    x2 = _layer_norm(DN_ALPHA * x1_ref[...] + fsum_ref[...], lnw_ref[...], lnb_ref[...])
    gate = jax.nn.sigmoid(jnp.dot(x2.astype(bf16), wg_ref[...], preferred_element_type=f32))
    o_ref[...] = x2 + gate * e


def _combine(y, dest, gates, x1, p2d, ln_w, ln_b, wp_bf, ple_norm_w, wg_bf):
    t, d = x1.shape
    tm = COMBINE_ROWS
    row = lambda c: pl.BlockSpec((tm, c), lambda i: (i, 0))
    full = lambda r, c: pl.BlockSpec((r, c), lambda i: (0, 0))
    return pl.pallas_call(
        _combine_kernel,
        grid=(t // tm,),
        in_specs=[pl.BlockSpec((tm * TOP_K,), lambda i: (i,), memory_space=pltpu.SMEM),
                  pl.BlockSpec(memory_space=pl.ANY),
                  row(TOP_K), row(d), row(PLE_DIM), full(1, d), full(1, d), full(PLE_DIM, d), full(1, d),
                  full(d, d)],
        out_specs=row(d),
        out_shape=jax.ShapeDtypeStruct((t, d), f32),
        scratch_shapes=[pltpu.VMEM((TOP_K, tm * PITCH, LANES), f32), pltpu.VMEM((tm, d), f32),
                        pltpu.SemaphoreType.DMA(())],
        compiler_params=_cparams(("arbitrary",)),
        name="combine_ln2_ple",
    )(dest.reshape(-1), y, gates, x1, p2d, ln_w.reshape(1, d), ln_b.reshape(1, d), wp_bf,
      ple_norm_w.reshape(1, d), wg_bf)


def _layer(h, p_i, w_in, ret_gn_w, conv_w, conv_b, lru_wa, lru_ba, lru_wx, lru_bx, lru_lam, w_out,
           ln1_w, ln1_b, w_router, b_router, w_gate, b_gate, w_up, b_up, w_down, b_down,
           ln2_w, ln2_b, w_ple_proj, ple_norm_w, w_ple_gate):
    b, s, d = h.shape
    t = b * s
    x2d = h.reshape(t, d)
    proj = _in_proj(x2d, w_in.astype(bf16)).reshape(b, s, IN_COLS)
    ret_out = _retention(proj, ret_gn_w)
    lru_out = _lru(proj, conv_w, conv_b, lru_wa, lru_ba, lru_wx, lru_bx, lru_lam)
    x1, x1s, top_e, gates = _out_router(ret_out.reshape(t, RET_WIDTH), lru_out.reshape(t, LRU_WIDTH),
                                        w_out.astype(bf16), x2d, ln1_w, ln1_b, w_router, b_router)
    dest, slot_tok, ue, u_row, u_chunks, n_used_rows, n_pad, max_units = _routing_tables(top_e, t)
    y = _moe_experts(x1s, slot_tok, ue, u_row, u_chunks, n_used_rows, n_pad, max_units, w_gate, b_gate, w_up, b_up,
                     w_down, b_down)
    out = _combine(y, dest, gates, x1, p_i.reshape(t, PLE_DIM), ln2_w, ln2_b, w_ple_proj.astype(bf16),
                   ple_norm_w, w_ple_gate.astype(bf16))
    return out.reshape(b, s, d)


def kernel(x, p, w_in, ret_gn_w, conv_w, conv_b, lru_wa, lru_ba, lru_wx, lru_bx, lru_lam, w_out, ln1_w, ln1_b,
           w_router, b_router, w_gate, b_gate, w_up, b_up, w_down, b_down, ln2_w, ln2_b, w_ple_proj, ple_norm_w,
           w_ple_gate):
    h = x.astype(f32)
    for i in range(w_in.shape[0]):
        h = _layer(h, p[i], w_in[i], ret_gn_w[i], conv_w[i], conv_b[i], lru_wa[i], lru_ba[i], lru_wx[i],
                   lru_bx[i], lru_lam[i], w_out[i], ln1_w[i], ln1_b[i], w_router[i], b_router[i], w_gate[i],
                   b_gate[i], w_up[i], b_up[i], w_down[i], b_down[i], ln2_w[i], ln2_b[i], w_ple_proj[i],
                   ple_norm_w[i], w_ple_gate[i])
    return h.astype(x.dtype)
```

```python
import functools
import math

import jax
import jax.numpy as jnp
from jax import lax
from jax.experimental import pallas as pl
from jax.experimental.pallas import tpu as pltpu

D_MODEL = 2048
RET_HEAD_DIM = 128
RET_HEADS = 8
RET_WIDTH = RET_HEADS * RET_HEAD_DIM
LRU_WIDTH = D_MODEL - RET_WIDTH
LRU_BLOCKS = 8
LRU_BLOCK_DIM = LRU_WIDTH // LRU_BLOCKS
IN_COLS = 4 * RET_WIDTH + 2 * LRU_WIDTH
CONV_WIDTH = 4
LRU_C = 8.0
CHUNK = 128
ROPE_BASE = 10000.0
N_EXPERTS = 32
TOP_K = 4
SWIGLU_LIMIT = 7.0
SWIGLU_ALPHA = 1.702
PLE_DIM = 256
LN_EPS = 1e-5
DEPTH = 1
DN_ALPHA = (2.0 * DEPTH) ** 0.25

LANES = 128
SUBLANES = 8
VMEM_LIMIT = 60 * 1024 * 1024

ROW_CHUNK = 256
UNIT_ROWS = 2048
F_TILE = 256

f32 = jnp.float32
bf16 = jnp.bfloat16


def _cparams(sem):
    return pltpu.CompilerParams(dimension_semantics=sem, vmem_limit_bytes=VMEM_LIMIT)


def _in_proj_kernel(x_ref, w_ref, o_ref, xb_ref):
    @pl.when(pl.program_id(1) == 0)
    def _():
        xb_ref[...] = x_ref[...].astype(bf16)

    o_ref[...] = jnp.dot(xb_ref[...], w_ref[...], preferred_element_type=f32)


def _in_proj(x2d, w_bf):
    t, d = x2d.shape
    n = w_bf.shape[1]
    tm, tn = 1024, 1024
    return pl.pallas_call(
        _in_proj_kernel,
        grid=(t // tm, n // tn),
        in_specs=[pl.BlockSpec((tm, d), lambda i, j: (i, 0)),
                  pl.BlockSpec((d, tn), lambda i, j: (0, j))],
        out_specs=pl.BlockSpec((tm, tn), lambda i, j: (i, j)),
        out_shape=jax.ShapeDtypeStruct((t, n), f32),
        scratch_shapes=[pltpu.VMEM((tm, d), bf16)],
        compiler_params=_cparams(("parallel", "arbitrary")),
        name="in_proj",
    )(x2d, w_bf)


def _retention_kernel(q_ref, k_ref, v_ref, g_ref, cos_ref, sin_ref, dec_ref, qd_ref, kd_ref, cd_ref, gnw_ref,
                      o_ref):
    s = q_ref.shape[0]
    n_chunks = s // CHUNK
    decay = dec_ref[...]
    q_dec = qd_ref[...]
    k_dec = kd_ref[...]
    c_dec = cd_ref[0:1, :]
    gnw = gnw_ref[...]
    k_scale = RET_HEAD_DIM ** -0.5

    def rope(xv, cos, sin):
        return xv * cos + pltpu.roll(xv, RET_HEAD_DIM // 2, axis=1) * sin

    def body(n, state):
        sl = pl.ds(pl.multiple_of(n * CHUNK, CHUNK), CHUNK)
        cos = cos_ref[sl, :]
        sin = sin_ref[sl, :]
        q = rope(q_ref[sl, :], cos, sin)
        k = rope(k_ref[sl, :], cos, sin) * k_scale
        vb = v_ref[sl, :].astype(bf16)
        scores = lax.dot_general(q.astype(bf16), k.astype(bf16), (((1,), (1,)), ((), ())),
                                 preferred_element_type=f32) * decay
        intra = jnp.dot(scores.astype(bf16), vb, preferred_element_type=f32)
        cross = jnp.dot((q * q_dec).astype(bf16), state.astype(bf16), preferred_element_type=f32)
        kv = lax.dot_general((k * k_dec).astype(bf16), vb, (((0,), (0,)), ((), ())),
                             preferred_element_type=f32)
        ret = intra + cross
        mu = jnp.mean(ret, axis=-1, keepdims=True)
        cen = ret - mu
        var = jnp.mean(cen * cen, axis=-1, keepdims=True)
        ret = cen * lax.rsqrt(var + LN_EPS) * gnw
        g = g_ref[sl, :]
        o_ref[sl, :] = (g * jax.nn.sigmoid(g) * ret).astype(o_ref.dtype)
        return c_dec * state + kv

    lax.fori_loop(0, n_chunks, body, jnp.zeros((RET_HEAD_DIM, RET_HEAD_DIM), f32), unroll=2)


def _retention_tables(s):
    h, d = RET_HEADS, RET_HEAD_DIM
    inv = ROPE_BASE ** (-jnp.arange(0, d, 2, dtype=f32) / d)
    ang = jnp.arange(s, dtype=f32)[:, None] * inv[None, :]
    cos = jnp.cos(ang)
    sin = jnp.sin(ang)
    cos_t = jnp.concatenate([cos, cos], axis=-1)
    sin_t = jnp.concatenate([-sin, sin], axis=-1)
    log_gamma = jnp.log1p(-jnp.exp2(-5.0 - jnp.arange(h, dtype=f32)))
    idx = jnp.arange(CHUNK, dtype=f32)
    diff = idx[:, None] - idx[None, :]
    decay = jnp.where((diff >= 0)[None], jnp.exp(jnp.maximum(diff, 0.0)[None] * log_gamma[:, None, None]), 0.0)
    q_dec = jnp.exp((idx[None, :] + 1.0) * log_gamma[:, None])
    k_dec = jnp.exp((CHUNK - 1.0 - idx)[None, :] * log_gamma[:, None])
    c_dec = jnp.exp(CHUNK * log_gamma)
    q_dec = jnp.broadcast_to(q_dec[:, :, None], (h, CHUNK, d))
    k_dec = jnp.broadcast_to(k_dec[:, :, None], (h, CHUNK, d))
    c_dec = jnp.broadcast_to(c_dec[:, None, None], (h, SUBLANES, d))
    return cos_t, sin_t, decay, q_dec, k_dec, c_dec


def _retention(proj, ret_gn_w):
    b, s, _ = proj.shape
    d = RET_HEAD_DIM
    cos_t, sin_t, decay, q_dec, k_dec, c_dec = _retention_tables(s)
    col = lambda off: pl.BlockSpec((None, s, d), lambda bi, hi: (bi, 0, off + hi))
    per_head = lambda r: pl.BlockSpec((None, r, d), lambda bi, hi: (hi, 0, 0))
    full = lambda shp: pl.BlockSpec(shp, lambda bi, hi: (0,) * len(shp))
    return pl.pallas_call(
        _retention_kernel,
        grid=(b, RET_HEADS),
        in_specs=[col(0), col(RET_HEADS), col(2 * RET_HEADS), col(3 * RET_HEADS),
                  full((s, d)), full((s, d)),
                  per_head(CHUNK), per_head(CHUNK), per_head(CHUNK), per_head(SUBLANES),
                  pl.BlockSpec((1, d), lambda bi, hi: (0, hi))],
        out_specs=pl.BlockSpec((None, s, d), lambda bi, hi: (bi, 0, hi)),
        out_shape=jax.ShapeDtypeStruct((b, s, RET_WIDTH), bf16),
        compiler_params=_cparams(("parallel", "parallel")),
        name="retention",
    )(proj, proj, proj, proj, cos_t, sin_t, decay, q_dec, k_dec, c_dec, ret_gn_w.reshape(1, RET_WIDTH))


def _gelu_tanh(x):
    return 0.5 * x * (1.0 + jnp.tanh(math.sqrt(2.0 / math.pi) * (x + 0.044715 * (x * x * x))))


def _lru_kernel(xr_ref, yg_ref, cw_ref, cb_ref, wa_ref, ba_ref, wx_ref, bx_ref, lam_ref, o_ref, a_ref, b_ref):
    s = xr_ref.shape[0]
    x = xr_ref[...]
    rows = lax.broadcasted_iota(jnp.int32, x.shape, 0)
    xc = cb_ref[...] + cw_ref[CONV_WIDTH - 1:CONV_WIDTH, :] * x
    for back in range(1, CONV_WIDTH):
        shifted = jnp.where(rows >= back, pltpu.roll(x, back, axis=0), 0.0)
        xc = xc + cw_ref[CONV_WIDTH - 1 - back:CONV_WIDTH - back, :] * shifted
    xcb = xc.astype(bf16)
    r = jax.nn.sigmoid(jnp.dot(xcb, wa_ref[...].astype(bf16), preferred_element_type=f32) + ba_ref[...])
    gi = jax.nn.sigmoid(jnp.dot(xcb, wx_ref[...].astype(bf16), preferred_element_type=f32) + bx_ref[...])
    lam = lam_ref[...]
    log_sig = jnp.minimum(lam, 0.0) - jnp.log1p(jnp.exp(-jnp.abs(lam)))
    log_a = LRU_C * r * log_sig
    a = jnp.exp(log_a)
    a_ref[...] = a
    b_ref[...] = jnp.sqrt(-jnp.tanh(log_a) * (a * a + 1.0)) * (gi * xc)

    row8 = lax.broadcasted_iota(jnp.int32, (SUBLANES, LANES), 0)

    def body(i, h_prev):
        sl = pl.ds(pl.multiple_of(i * SUBLANES, SUBLANES), SUBLANES)
        a8 = a_ref[sl, :]
        b8 = b_ref[sl, :]
        for sh in (1, 2, 4):
            a_sh = jnp.where(row8 >= sh, pltpu.roll(a8, sh, axis=0), 1.0)
            b_sh = jnp.where(row8 >= sh, pltpu.roll(b8, sh, axis=0), 0.0)
            b8 = a8 * b_sh + b8
            a8 = a8 * a_sh
        h8 = a8 * h_prev + b8
        o_ref[sl, :] = (_gelu_tanh(yg_ref[sl, :]) * h8).astype(o_ref.dtype)
        return h8[SUBLANES - 1:SUBLANES, :]

    lax.fori_loop(0, s // SUBLANES, body, jnp.zeros((1, LANES), f32), unroll=8)


def _lru(proj, conv_w, conv_b, wa, ba, wx, bx, lam):
    b, s, _ = proj.shape
    d = LRU_BLOCK_DIM
    xr_off = 4 * RET_WIDTH // d
    yg_off = xr_off + LRU_BLOCKS
    col = lambda off: pl.BlockSpec((None, s, d), lambda bi, ji: (bi, 0, off + ji))
    vec = lambda r: pl.BlockSpec((r, d), lambda bi, ji: (0, ji))
    blk = lambda r: pl.BlockSpec((None, r, d), lambda bi, ji: (ji, 0, 0))
    return pl.pallas_call(
        _lru_kernel,
        grid=(b, LRU_BLOCKS),
        in_specs=[col(xr_off), col(yg_off), vec(CONV_WIDTH), vec(1), blk(d), blk(1), blk(d), blk(1), vec(1)],
        out_specs=pl.BlockSpec((None, s, d), lambda bi, ji: (bi, 0, ji)),
        out_shape=jax.ShapeDtypeStruct((b, s, LRU_WIDTH), bf16),
        scratch_shapes=[pltpu.VMEM((s, d), f32), pltpu.VMEM((s, d), f32)],
        compiler_params=_cparams(("parallel", "parallel")),
        name="rg_lru",
    )(proj, proj, conv_w, conv_b.reshape(1, LRU_WIDTH), wa, ba.reshape(LRU_BLOCKS, 1, d), wx,
      bx.reshape(LRU_BLOCKS, 1, d), lam.reshape(1, LRU_WIDTH))


def _layer_norm(y, w, b):
    mu = jnp.mean(y, axis=-1, keepdims=True)
    cen = y - mu
    var = jnp.mean(cen * cen, axis=-1, keepdims=True)
    return cen * lax.rsqrt(var + LN_EPS) * w + b


SLAB = D_MODEL // LANES
PITCH = SLAB + 4


def _store_slabs(ref, val, zero_pad):
    n = val.shape[0]
    for j in range(SLAB):
        ref[pl.ds(j, n, stride=PITCH), :] = val[:, j * LANES:(j + 1) * LANES]
    if zero_pad:
        for j in range(SLAB, PITCH):
            ref[pl.ds(j, n, stride=PITCH), :] = jnp.zeros((n, LANES), val.dtype)


def _slab_cols(ref, j, n):
    return ref[pl.ds(j, n, stride=PITCH), :]


def _split_bf16(v):
    hi = v.astype(bf16)
    lo = (v - hi.astype(f32)).astype(bf16)
    return hi, lo


def _out_router_kernel(ret_ref, lru_ref, wo_ref, x_ref, lnw_ref, lnb_ref, wr_ref, br_ref,
                       x1_ref, x1s_ref, tope_ref, gate_ref):
    m = jnp.dot(ret_ref[...], wo_ref[0:RET_WIDTH, :], preferred_element_type=f32)
    m = m + jnp.dot(lru_ref[...], wo_ref[RET_WIDTH:D_MODEL, :], preferred_element_type=f32)
    x1 = _layer_norm(DN_ALPHA * x_ref[...] + m, lnw_ref[...], lnb_ref[...])
    x1_ref[...] = x1
    _store_slabs(x1s_ref, x1, zero_pad=True)
    xh, xl = _split_bf16(x1)
    wh, wl = _split_bf16(wr_ref[...])
    logits = (jnp.dot(xh, wh, preferred_element_type=f32) + jnp.dot(xl, wh, preferred_element_type=f32)
              + jnp.dot(xh, wl, preferred_element_type=f32)) + br_ref[...]
    tm = logits.shape[0]
    lane = lax.broadcasted_iota(jnp.int32, logits.shape, 1)
    lane_k = lax.broadcasted_iota(jnp.int32, (tm, TOP_K), 1)
    top_e = jnp.zeros((tm, TOP_K), jnp.int32)
    top_v = jnp.zeros((tm, TOP_K), f32)
    cur = logits
    for kk in range(TOP_K):
        mx = jnp.max(cur, axis=-1, keepdims=True)
        idx = jnp.min(jnp.where(cur == mx, lane, N_EXPERTS), axis=-1, keepdims=True)
        top_e = jnp.where(lane_k == kk, idx, top_e)
        top_v = jnp.where(lane_k == kk, mx, top_v)
        cur = jnp.where(lane == idx, -jnp.inf, cur)
    ex = jnp.exp(top_v - top_v[:, 0:1])
    gate_ref[...] = ex / jnp.sum(ex, axis=-1, keepdims=True)
    tope_ref[...] = top_e


def _out_router(ret_out, lru_out, wo_bf, x2d, ln_w, ln_b, w_router, b_router):
    t, d = x2d.shape
    tm = 256
    row = lambda c: pl.BlockSpec((tm, c), lambda i: (i, 0))
    full = lambda r, c: pl.BlockSpec((r, c), lambda i: (0, 0))
    return pl.pallas_call(
        _out_router_kernel,
        grid=(t // tm,),
        in_specs=[row(RET_WIDTH), row(LRU_WIDTH), full(d, d), row(d), full(1, d), full(1, d),
                  full(d, N_EXPERTS), full(1, N_EXPERTS)],
        out_specs=[row(d), pl.BlockSpec((tm * PITCH, LANES), lambda i: (i, 0)), row(TOP_K), row(TOP_K)],
        out_shape=[jax.ShapeDtypeStruct((t, d), f32), jax.ShapeDtypeStruct((t * PITCH, LANES), f32),
                   jax.ShapeDtypeStruct((t, TOP_K), jnp.int32),
                   jax.ShapeDtypeStruct((t, TOP_K), f32)],
        compiler_params=_cparams(("parallel",)),
        name="out_proj_ln1_router",
    )(ret_out, lru_out, wo_bf, x2d, ln_w.reshape(1, d), ln_b.reshape(1, d), w_router,
      b_router.reshape(1, N_EXPERTS))


def _routing_tables(top_e, t):
    n_pad = t * TOP_K + N_EXPERTS * ROW_CHUNK
    max_units = N_EXPERTS + (t * TOP_K) // UNIT_ROWS
    sel = (top_e[:, :, None] == jnp.arange(N_EXPERTS, dtype=jnp.int32)[None, None, :]).any(axis=1)
    sel = sel.astype(jnp.int32)
    counts = jnp.sum(sel, axis=0)
    rank = jnp.cumsum(sel, axis=0) - sel
    padded = (counts + ROW_CHUNK - 1) // ROW_CHUNK * ROW_CHUNK
    pad_ends = jnp.cumsum(padded)
    pad_starts = pad_ends - padded
    dest_dense = pad_starts[None, :] + rank
    dest = jnp.take_along_axis(dest_dense, top_e, axis=1)
    tok = jnp.broadcast_to(jnp.arange(t, dtype=jnp.int32)[:, None], (t, TOP_K))
    slot_tok = jnp.zeros((n_pad + UNIT_ROWS,), jnp.int32).at[dest.reshape(-1)].set(tok.reshape(-1))
    units_per_e = (padded + UNIT_ROWS - 1) // UNIT_ROWS
    unit_ends = jnp.cumsum(units_per_e)
    unit_starts = unit_ends - units_per_e
    n_units = unit_ends[-1]
    u = jnp.arange(max_units, dtype=jnp.int32)
    u_clamped = jnp.minimum(u, n_units - 1)
    ue = jnp.searchsorted(unit_ends, u_clamped, side='right').astype(jnp.int32)
    ue = jnp.minimum(ue, N_EXPERTS - 1)
    j = u_clamped - unit_starts[ue]
    u_row = pad_starts[ue] + j * UNIT_ROWS
    u_rows = jnp.minimum(UNIT_ROWS, padded[ue] - j * UNIT_ROWS)
    u_chunks = jnp.where(u < n_units, u_rows // ROW_CHUNK, 0).astype(jnp.int32)
    n_used_rows = pad_ends[-1].astype(jnp.int32)
    return dest.astype(jnp.int32), slot_tok, ue, u_row.astype(jnp.int32), u_chunks, n_used_rows, n_pad, max_units


CHUNK_PITCHED = ROW_CHUNK * PITCH


def _moe_kernel(ue_ref, urow_ref, uchunks_ref, used_ref, tok_hbm, x1s_hbm, wg_ref, bg_ref, wu_ref, bu_ref, wd_ref,
                bd_ref, y_hbm, xbuf, yacc, wgub, wdb, gstage, ostage, tok_smem, flags, tok_sem, in_sem, out_sem):
    u = pl.program_id(0)
    f = pl.program_id(1)
    n_u = pl.num_programs(0)
    n_f = pl.num_programs(1)
    n_chunks = uchunks_ref[u]
    row0 = urow_ref[u]
    cur = u % 2
    nxt = 1 - cur
    u_next = jnp.minimum(u + 1, n_u - 1)
    n_next = jnp.where(u + 1 < n_u, uchunks_ref[u_next], 0)
    first_step = jnp.logical_and(u == 0, f == 0)
    PENDING, PEND_CHUNK, PEND_SLOT, OUT_BUSY = 0, 1, 2, 3

    def rows(c, size=ROW_CHUNK):
        return pl.ds(pl.multiple_of(c * size, size), size)

    def tok_copy(unit_row, slot):
        src = tok_hbm.at[pl.ds(pl.multiple_of(unit_row, ROW_CHUNK), UNIT_ROWS)]
        return pltpu.make_async_copy(src, tok_smem.at[slot], tok_sem)

    def gather_start(c, tslot):
        def issue(r, carry):
            tok = tok_smem[tslot, c * ROW_CHUNK + r]
            src = x1s_hbm.at[pl.ds(tok * PITCH, SLAB), :]
            dst = gstage.at[pl.ds(r * PITCH, SLAB), :]
            pltpu.make_async_copy(src, dst, in_sem).start()
            return carry

        lax.fori_loop(0, ROW_CHUNK, issue, 0, unroll=8)

    def gather_finish(c, xslot):
        n = ROW_CHUNK * SLAB
        pltpu.make_async_copy(x1s_hbm.at[pl.ds(0, n), :], gstage.at[pl.ds(0, n), :], in_sem).wait()
        for j in range(SLAB):
            xbuf[xslot, rows(c), j * LANES:(j + 1) * LANES] = _slab_cols(gstage, j, ROW_CHUNK).astype(bf16)

    def out_copy(c, slot):
        dst = y_hbm.at[pl.ds(pl.multiple_of((row0 + c * ROW_CHUNK) * PITCH, CHUNK_PITCHED), CHUNK_PITCHED), :]
        return pltpu.make_async_copy(ostage.at[slot], dst, out_sem.at[slot])

    def out_wait(slot):
        @pl.when(flags[OUT_BUSY + slot] == 1)
        def _():
            pltpu.make_async_copy(ostage.at[slot], y_hbm.at[pl.ds(0, CHUNK_PITCHED), :], out_sem.at[slot]).wait()
            flags[OUT_BUSY + slot] = 0

    @pl.when(first_step)
    def _():
        for i in range(OUT_BUSY + 2):
            flags[i] = 0

        @pl.when(n_chunks > 0)
        def _():
            first_table = tok_copy(row0, cur)
            first_table.start()
            first_table.wait()

            def load(c, carry):
                gather_start(c, cur)
                gather_finish(c, cur)
                return carry

            lax.fori_loop(0, n_chunks, load, 0)

    @pl.when(flags[PENDING] == 1)
    def _():
        gather_finish(flags[PEND_CHUNK], flags[PEND_SLOT])
        flags[PENDING] = 0

    @pl.when(jnp.logical_and(f == 0, n_next > 0))
    def _():
        next_table = tok_copy(urow_ref[u_next], nxt)
        next_table.start()
        next_table.wait()

    @pl.when(f < n_next)
    def _():
        gather_start(f, nxt)
        flags[PENDING] = 1
        flags[PEND_CHUNK] = f
        flags[PEND_SLOT] = nxt

    @pl.when(jnp.logical_and(f == 0, n_chunks > 0))
    def _():
        def clear(c, carry):
            yacc[rows(c), :] = jnp.zeros((ROW_CHUNK, D_MODEL), f32)
            return carry

        lax.fori_loop(0, n_chunks, clear, 0)

    @pl.when(n_chunks > 0)
    def _():
        wgub[:, 0:F_TILE] = wg_ref[...].astype(bf16)
        wgub[:, F_TILE:2 * F_TILE] = wu_ref[...].astype(bf16)
        wdb[...] = wd_ref[...].astype(bf16)
        bg = bg_ref[...]
        bu = bu_ref[...]

        def mlp(sl):
            gu = jnp.dot(xbuf[cur, sl, :], wgub[...], preferred_element_type=f32)
            gt = jnp.minimum(gu[:, 0:F_TILE] + bg, SWIGLU_LIMIT)
            up = jnp.clip(gu[:, F_TILE:2 * F_TILE] + bu, -SWIGLU_LIMIT, SWIGLU_LIMIT)
            hid = (up + 1.0) * gt * jax.nn.sigmoid(SWIGLU_ALPHA * gt)
            yacc[sl, :] += jnp.dot(hid.astype(bf16), wdb[...], preferred_element_type=f32)

        def pair(c, carry):
            mlp(rows(c, 2 * ROW_CHUNK))
            return carry

        lax.fori_loop(0, n_chunks // 2, pair, 0)

        @pl.when(n_chunks % 2 == 1)
        def _():
            mlp(rows(n_chunks - 1))

    @pl.when(jnp.logical_and(f == n_f - 1, n_chunks > 0))
    def _():
        bd = bd_ref[...]

        def store(c, carry):
            slot = c % 2
            out_wait(slot)
            _store_slabs(ostage.at[slot], yacc[rows(c), :] + bd, zero_pad=True)
            out_copy(c, slot).start()
            flags[OUT_BUSY + slot] = 1
            return carry

        lax.fori_loop(0, n_chunks, store, 0)

    @pl.when(jnp.logical_and(u == n_u - 1, f == n_f - 1))
    def _():
        out_wait(0)
        out_wait(1)
        first = used_ref[0] // ROW_CHUNK
        last = y_hbm.shape[0] // CHUNK_PITCHED
        ostage[0] = jnp.zeros((CHUNK_PITCHED, LANES), f32)

        def tail_copy(c):
            dst = y_hbm.at[pl.ds(pl.multiple_of(c * CHUNK_PITCHED, CHUNK_PITCHED), CHUNK_PITCHED), :]
            return pltpu.make_async_copy(ostage.at[0], dst, out_sem.at[0])

        def start(c, carry):
            tail_copy(c).start()
            return carry

        def wait(c, carry):
            tail_copy(c).wait()
            return carry

        lax.fori_loop(first, last, start, 0)
        lax.fori_loop(first, last, wait, 0)


def _moe_experts(x1s, slot_tok, ue, u_row, u_chunks, n_used_rows, n_pad, max_units, w_gate, b_gate, w_up, b_up,
                 w_down, b_down):
    e, d, dff = w_gate.shape
    n_f = dff // F_TILE
    assert n_f >= UNIT_ROWS // ROW_CHUNK, "one chunk of the next unit is gathered per f-tile step"
    f_idx = lambda u, f, uc: jnp.where(uc[u] > 0, f, n_f - 1)
    col_w = pl.BlockSpec((None, d, F_TILE), lambda u, f, ue, ur, uc, used: (ue[u], 0, f_idx(u, f, uc)))
    col_b = pl.BlockSpec((None, 1, F_TILE), lambda u, f, ue, ur, uc, used: (ue[u], 0, f_idx(u, f, uc)))
    return pl.pallas_call(
        _moe_kernel,
        grid_spec=pltpu.PrefetchScalarGridSpec(
            num_scalar_prefetch=4,
            grid=(max_units, n_f),
            in_specs=[pl.BlockSpec(memory_space=pl.ANY), pl.BlockSpec(memory_space=pl.ANY),
                      col_w, col_b, col_w, col_b,
                      pl.BlockSpec((None, F_TILE, d), lambda u, f, ue, ur, uc, used: (ue[u], f_idx(u, f, uc), 0)),
                      pl.BlockSpec((None, 1, d), lambda u, f, ue, ur, uc, used: (ue[u], 0, 0))],
            out_specs=pl.BlockSpec(memory_space=pl.ANY),
            scratch_shapes=[pltpu.VMEM((2, UNIT_ROWS, d), bf16),
                            pltpu.VMEM((UNIT_ROWS, d), f32),
                            pltpu.VMEM((d, 2 * F_TILE), bf16),
                            pltpu.VMEM((F_TILE, d), bf16),
                            pltpu.VMEM((CHUNK_PITCHED, LANES), f32),
                            pltpu.VMEM((2, CHUNK_PITCHED, LANES), f32),
                            pltpu.SMEM((2, UNIT_ROWS), jnp.int32),
                            pltpu.SMEM((8,), jnp.int32),
                            pltpu.SemaphoreType.DMA(()),
                            pltpu.SemaphoreType.DMA(()),
                            pltpu.SemaphoreType.DMA((2,))],
        ),
        out_shape=jax.ShapeDtypeStruct((n_pad * PITCH, LANES), f32),
        compiler_params=_cparams(("arbitrary", "arbitrary")),
        name="moe_experts",
    )(ue, u_row, u_chunks, n_used_rows.reshape(1), slot_tok, x1s, w_gate, b_gate.reshape(e, 1, dff), w_up,
      b_up.reshape(e, 1, dff), w_down, b_down.reshape(e, 1, d))


COMBINE_ROWS = 256


def _combine_kernel(dest_ref, y_hbm, gate_ref, x1_ref, p_ref, lnw_ref, lnb_ref, wp_ref, pnw_ref, wg_ref,
                    o_ref, ybuf, fsum_ref, sem):
    tm = COMBINE_ROWS

    def issue(t, c):
        for k in range(TOP_K):
            src = y_hbm.at[pl.ds(dest_ref[t * TOP_K + k] * PITCH, SLAB), :]
            dst = ybuf.at[k, pl.ds(t * PITCH, SLAB), :]
            pltpu.make_async_copy(src, dst, sem).start()
        return c

    lax.fori_loop(0, tm, issue, 0, unroll=2)
    e = jnp.dot(p_ref[...].astype(bf16), wp_ref[...], preferred_element_type=f32)
    e = e * lax.rsqrt(jnp.mean(e * e, axis=-1, keepdims=True) + LN_EPS) * pnw_ref[...]
    for k in range(TOP_K):
        pltpu.make_async_copy(y_hbm.at[pl.ds(0, tm * SLAB), :], ybuf.at[k, pl.ds(0, tm * SLAB), :], sem).wait()
    gates = gate_ref[...]
    for j in range(SLAB):
        acc = gates[:, 0:1] * _slab_cols(ybuf.at[0], j, tm)
        for k in range(1, TOP_K):
            acc = acc + gates[:, k:k + 1] * _slab_cols(ybuf.at[k], j, tm)
        fsum_ref[:, j * LANES:(j + 1) * LANES] = acc
    x2 = _layer_norm(DN_ALPHA * x1_ref[...] + fsum_ref[...], lnw_ref[...], lnb_ref[...])
    gate = jax.nn.sigmoid(jnp.dot(x2.astype(bf16), wg_ref[...], preferred_element_type=f32))
    o_ref[...] = x2 + gate * e


def _combine(y, dest, gates, x1, p2d, ln_w, ln_b, wp_bf, ple_norm_w, wg_bf):
    t, d = x1.shape
    tm = COMBINE_ROWS
    row = lambda c: pl.BlockSpec((tm, c), lambda i: (i, 0))
    full = lambda r, c: pl.BlockSpec((r, c), lambda i: (0, 0))
    return pl.pallas_call(
        _combine_kernel,
        grid=(t // tm,),
        in_specs=[pl.BlockSpec((tm * TOP_K,), lambda i: (i,), memory_space=pltpu.SMEM),
                  pl.BlockSpec(memory_space=pl.ANY),
                  row(TOP_K), row(d), row(PLE_DIM), full(1, d), full(1, d), full(PLE_DIM, d), full(1, d),
                  full(d, d)],
        out_specs=row(d),
        out_shape=jax.ShapeDtypeStruct((t, d), f32),
        scratch_shapes=[pltpu.VMEM((TOP_K, tm * PITCH, LANES), f32), pltpu.VMEM((tm, d), f32),
                        pltpu.SemaphoreType.DMA(())],
        compiler_params=_cparams(("arbitrary",)),
        name="combine_ln2_ple",
    )(dest.reshape(-1), y, gates, x1, p2d, ln_w.reshape(1, d), ln_b.reshape(1, d), wp_bf,
      ple_norm_w.reshape(1, d), wg_bf)


def _layer(h, p_i, w_in, ret_gn_w, conv_w, conv_b, lru_wa, lru_ba, lru_wx, lru_bx, lru_lam, w_out,
           ln1_w, ln1_b, w_router, b_router, w_gate, b_gate, w_up, b_up, w_down, b_down,
           ln2_w, ln2_b, w_ple_proj, ple_norm_w, w_ple_gate):
    b, s, d = h.shape
    t = b * s
    x2d = h.reshape(t, d)
    proj = _in_proj(x2d, w_in.astype(bf16)).reshape(b, s, IN_COLS)
    ret_out = _retention(proj, ret_gn_w)
    lru_out = _lru(proj, conv_w, conv_b, lru_wa, lru_ba, lru_wx, lru_bx, lru_lam)
    x1, x1s, top_e, gates = _out_router(ret_out.reshape(t, RET_WIDTH), lru_out.reshape(t, LRU_WIDTH),
                                        w_out.astype(bf16), x2d, ln1_w, ln1_b, w_router, b_router)
    dest, slot_tok, ue, u_row, u_chunks, n_used_rows, n_pad, max_units = _routing_tables(top_e, t)
    y = _moe_experts(x1s, slot_tok, ue, u_row, u_chunks, n_used_rows, n_pad, max_units, w_gate, b_gate, w_up, b_up,
                     w_down, b_down)
    out = _combine(y, dest, gates, x1, p_i.reshape(t, PLE_DIM), ln2_w, ln2_b, w_ple_proj.astype(bf16),
                   ple_norm_w, w_ple_gate.astype(bf16))
    return out.reshape(b, s, d)


def kernel(x, p, w_in, ret_gn_w, conv_w, conv_b, lru_wa, lru_ba, lru_wx, lru_bx, lru_lam, w_out, ln1_w, ln1_b,
           w_router, b_router, w_gate, b_gate, w_up, b_up, w_down, b_down, ln2_w, ln2_b, w_ple_proj, ple_norm_w,
           w_ple_gate):
    h = x.astype(f32)
    for i in range(w_in.shape[0]):
        h = _layer(h, p[i], w_in[i], ret_gn_w[i], conv_w[i], conv_b[i], lru_wa[i], lru_ba[i], lru_wx[i],
                   lru_bx[i], lru_lam[i], w_out[i], ln1_w[i], ln1_b[i], w_router[i], b_router[i], w_gate[i],
                   b_gate[i], w_up[i], b_up[i], w_down[i], b_down[i], ln2_w[i], ln2_b[i], w_ple_proj[i],
                   ple_norm_w[i], w_ple_gate[i])
    return h.astype(x.dtype)
```

```python
import functools
import math

import jax
import jax.numpy as jnp
from jax import lax
from jax.experimental import pallas as pl
from jax.experimental.pallas import tpu as pltpu

D_MODEL = 2048
RET_HEAD_DIM = 128
RET_HEADS = 8
RET_WIDTH = RET_HEADS * RET_HEAD_DIM
LRU_WIDTH = D_MODEL - RET_WIDTH
LRU_BLOCKS = 8
LRU_BLOCK_DIM = LRU_WIDTH // LRU_BLOCKS
IN_COLS = 4 * RET_WIDTH + 2 * LRU_WIDTH
CONV_WIDTH = 4
LRU_C = 8.0
CHUNK = 128
ROPE_BASE = 10000.0
N_EXPERTS = 32
TOP_K = 4
SWIGLU_LIMIT = 7.0
SWIGLU_ALPHA = 1.702
PLE_DIM = 256
LN_EPS = 1e-5
DEPTH = 1
DN_ALPHA = (2.0 * DEPTH) ** 0.25

LANES = 128
SUBLANES = 8
VMEM_LIMIT = 60 * 1024 * 1024

ROW_CHUNK = 256
UNIT_ROWS = 2048
F_TILE = 256

f32 = jnp.float32
bf16 = jnp.bfloat16


def _cparams(sem):
    return pltpu.CompilerParams(dimension_semantics=sem, vmem_limit_bytes=VMEM_LIMIT)


def _in_proj_kernel(x_ref, w_ref, o_ref, xb_ref):
    @pl.when(pl.program_id(1) == 0)
    def _():
        xb_ref[...] = x_ref[...].astype(bf16)

    o_ref[...] = jnp.dot(xb_ref[...], w_ref[...], preferred_element_type=f32)


def _in_proj(x2d, w_bf):
    t, d = x2d.shape
    n = w_bf.shape[1]
    tm, tn = 1024, 1024
    return pl.pallas_call(
        _in_proj_kernel,
        grid=(t // tm, n // tn),
        in_specs=[pl.BlockSpec((tm, d), lambda i, j: (i, 0)),
                  pl.BlockSpec((d, tn), lambda i, j: (0, j))],
        out_specs=pl.BlockSpec((tm, tn), lambda i, j: (i, j)),
        out_shape=jax.ShapeDtypeStruct((t, n), f32),
        scratch_shapes=[pltpu.VMEM((tm, d), bf16)],
        compiler_params=_cparams(("parallel", "arbitrary")),
        name="in_proj",
    )(x2d, w_bf)


def _retention_kernel(q_ref, k_ref, v_ref, g_ref, cos_ref, sin_ref, dec_ref, qd_ref, kd_ref, cd_ref, gnw_ref,
                      o_ref):
    s = q_ref.shape[0]
    n_chunks = s // CHUNK
    decay = dec_ref[...]
    q_dec = qd_ref[...]
    k_dec = kd_ref[...]
    c_dec = cd_ref[0:1, :]
    gnw = gnw_ref[...]
    k_scale = RET_HEAD_DIM ** -0.5

    def rope(xv, cos, sin):
        return xv * cos + pltpu.roll(xv, RET_HEAD_DIM // 2, axis=1) * sin

    def body(n, state):
        sl = pl.ds(pl.multiple_of(n * CHUNK, CHUNK), CHUNK)
        cos = cos_ref[sl, :]
        sin = sin_ref[sl, :]
        q = rope(q_ref[sl, :], cos, sin)
        k = rope(k_ref[sl, :], cos, sin) * k_scale
        vb = v_ref[sl, :].astype(bf16)
        scores = lax.dot_general(q.astype(bf16), k.astype(bf16), (((1,), (1,)), ((), ())),
                                 preferred_element_type=f32) * decay
        intra = jnp.dot(scores.astype(bf16), vb, preferred_element_type=f32)
        cross = jnp.dot((q * q_dec).astype(bf16), state.astype(bf16), preferred_element_type=f32)
        kv = lax.dot_general((k * k_dec).astype(bf16), vb, (((0,), (0,)), ((), ())),
                             preferred_element_type=f32)
        ret = intra + cross
        mu = jnp.mean(ret, axis=-1, keepdims=True)
        cen = ret - mu
        var = jnp.mean(cen * cen, axis=-1, keepdims=True)
        ret = cen * lax.rsqrt(var + LN_EPS) * gnw
        g = g_ref[sl, :]
        o_ref[sl, :] = (g * jax.nn.sigmoid(g) * ret).astype(o_ref.dtype)
        return c_dec * state + kv

    lax.fori_loop(0, n_chunks, body, jnp.zeros((RET_HEAD_DIM, RET_HEAD_DIM), f32), unroll=2)


def _retention_tables(s):
    h, d = RET_HEADS, RET_HEAD_DIM
    inv = ROPE_BASE ** (-jnp.arange(0, d, 2, dtype=f32) / d)
    ang = jnp.arange(s, dtype=f32)[:, None] * inv[None, :]
    cos = jnp.cos(ang)
    sin = jnp.sin(ang)
    cos_t = jnp.concatenate([cos, cos], axis=-1)
    sin_t = jnp.concatenate([-sin, sin], axis=-1)
    log_gamma = jnp.log1p(-jnp.exp2(-5.0 - jnp.arange(h, dtype=f32)))
    idx = jnp.arange(CHUNK, dtype=f32)
    diff = idx[:, None] - idx[None, :]
    decay = jnp.where((diff >= 0)[None], jnp.exp(jnp.maximum(diff, 0.0)[None] * log_gamma[:, None, None]), 0.0)
    q_dec = jnp.exp((idx[None, :] + 1.0) * log_gamma[:, None])
    k_dec = jnp.exp((CHUNK - 1.0 - idx)[None, :] * log_gamma[:, None])
    c_dec = jnp.exp(CHUNK * log_gamma)
    q_dec = jnp.broadcast_to(q_dec[:, :, None], (h, CHUNK, d))
    k_dec = jnp.broadcast_to(k_dec[:, :, None], (h, CHUNK, d))
    c_dec = jnp.broadcast_to(c_dec[:, None, None], (h, SUBLANES, d))
    return cos_t, sin_t, decay, q_dec, k_dec, c_dec


def _retention(proj, ret_gn_w):
    b, s, _ = proj.shape
    d = RET_HEAD_DIM
    cos_t, sin_t, decay, q_dec, k_dec, c_dec = _retention_tables(s)
    col = lambda off: pl.BlockSpec((None, s, d), lambda bi, hi: (bi, 0, off + hi))
    per_head = lambda r: pl.BlockSpec((None, r, d), lambda bi, hi: (hi, 0, 0))
    full = lambda shp: pl.BlockSpec(shp, lambda bi, hi: (0,) * len(shp))
    return pl.pallas_call(
        _retention_kernel,
        grid=(b, RET_HEADS),
        in_specs=[col(0), col(RET_HEADS), col(2 * RET_HEADS), col(3 * RET_HEADS),
                  full((s, d)), full((s, d)),
                  per_head(CHUNK), per_head(CHUNK), per_head(CHUNK), per_head(SUBLANES),
                  pl.BlockSpec((1, d), lambda bi, hi: (0, hi))],
        out_specs=pl.BlockSpec((None, s, d), lambda bi, hi: (bi, 0, hi)),
        out_shape=jax.ShapeDtypeStruct((b, s, RET_WIDTH), bf16),
        compiler_params=_cparams(("parallel", "parallel")),
        name="retention",
    )(proj, proj, proj, proj, cos_t, sin_t, decay, q_dec, k_dec, c_dec, ret_gn_w.reshape(1, RET_WIDTH))


def _gelu_tanh(x):
    return 0.5 * x * (1.0 + jnp.tanh(math.sqrt(2.0 / math.pi) * (x + 0.044715 * (x * x * x))))


def _lru_kernel(xr_ref, yg_ref, cw_ref, cb_ref, wa_ref, ba_ref, wx_ref, bx_ref, lam_ref, o_ref, a_ref, b_ref):
    s = xr_ref.shape[0]
    x = xr_ref[...]
    rows = lax.broadcasted_iota(jnp.int32, x.shape, 0)
    xc = cb_ref[...] + cw_ref[CONV_WIDTH - 1:CONV_WIDTH, :] * x
    for back in range(1, CONV_WIDTH):
        shifted = jnp.where(rows >= back, pltpu.roll(x, back, axis=0), 0.0)
        xc = xc + cw_ref[CONV_WIDTH - 1 - back:CONV_WIDTH - back, :] * shifted
    xcb = xc.astype(bf16)
    r = jax.nn.sigmoid(jnp.dot(xcb, wa_ref[...].astype(bf16), preferred_element_type=f32) + ba_ref[...])
    gi = jax.nn.sigmoid(jnp.dot(xcb, wx_ref[...].astype(bf16), preferred_element_type=f32) + bx_ref[...])
    lam = lam_ref[...]
    log_sig = jnp.minimum(lam, 0.0) - jnp.log1p(jnp.exp(-jnp.abs(lam)))
    log_a = LRU_C * r * log_sig
    a = jnp.exp(log_a)
    a_ref[...] = a
    b_ref[...] = jnp.sqrt(-jnp.tanh(log_a) * (a * a + 1.0)) * (gi * xc)

    row8 = lax.broadcasted_iota(jnp.int32, (SUBLANES, LANES), 0)

    def body(i, h_prev):
        sl = pl.ds(pl.multiple_of(i * SUBLANES, SUBLANES), SUBLANES)
        a8 = a_ref[sl, :]
        b8 = b_ref[sl, :]
        for sh in (1, 2, 4):
            a_sh = jnp.where(row8 >= sh, pltpu.roll(a8, sh, axis=0), 1.0)
            b_sh = jnp.where(row8 >= sh, pltpu.roll(b8, sh, axis=0), 0.0)
            b8 = a8 * b_sh + b8
            a8 = a8 * a_sh
        h8 = a8 * h_prev + b8
        o_ref[sl, :] = (_gelu_tanh(yg_ref[sl, :]) * h8).astype(o_ref.dtype)
        return h8[SUBLANES - 1:SUBLANES, :]

    lax.fori_loop(0, s // SUBLANES, body, jnp.zeros((1, LANES), f32), unroll=8)


def _lru(proj, conv_w, conv_b, wa, ba, wx, bx, lam):
    b, s, _ = proj.shape
    d = LRU_BLOCK_DIM
    xr_off = 4 * RET_WIDTH // d
    yg_off = xr_off + LRU_BLOCKS
    col = lambda off: pl.BlockSpec((None, s, d), lambda bi, ji: (bi, 0, off + ji))
    vec = lambda r: pl.BlockSpec((r, d), lambda bi, ji: (0, ji))
    blk = lambda r: pl.BlockSpec((None, r, d), lambda bi, ji: (ji, 0, 0))
    return pl.pallas_call(
        _lru_kernel,
        grid=(b, LRU_BLOCKS),
        in_specs=[col(xr_off), col(yg_off), vec(CONV_WIDTH), vec(1), blk(d), blk(1), blk(d), blk(1), vec(1)],
        out_specs=pl.BlockSpec((None, s, d), lambda bi, ji: (bi, 0, ji)),
        out_shape=jax.ShapeDtypeStruct((b, s, LRU_WIDTH), bf16),
        scratch_shapes=[pltpu.VMEM((s, d), f32), pltpu.VMEM((s, d), f32)],
        compiler_params=_cparams(("parallel", "parallel")),
        name="rg_lru",
    )(proj, proj, conv_w, conv_b.reshape(1, LRU_WIDTH), wa, ba.reshape(LRU_BLOCKS, 1, d), wx,
      bx.reshape(LRU_BLOCKS, 1, d), lam.reshape(1, LRU_WIDTH))


def _layer_norm(y, w, b):
    mu = jnp.mean(y, axis=-1, keepdims=True)
    cen = y - mu
    var = jnp.mean(cen * cen, axis=-1, keepdims=True)
    return cen * lax.rsqrt(var + LN_EPS) * w + b


SLAB = D_MODEL // LANES
PITCH = SLAB + 4


def _store_slabs(ref, val, zero_pad):
    n = val.shape[0]
    for j in range(SLAB):
        ref[pl.ds(j, n, stride=PITCH), :] = val[:, j * LANES:(j + 1) * LANES]
    if zero_pad:
        for j in range(SLAB, PITCH):
            ref[pl.ds(j, n, stride=PITCH), :] = jnp.zeros((n, LANES), val.dtype)


def _slab_cols(ref, j, n):
    return ref[pl.ds(j, n, stride=PITCH), :]


def _split_bf16(v):
    hi = v.astype(bf16)
    lo = (v - hi.astype(f32)).astype(bf16)
    return hi, lo


def _out_router_kernel(ret_ref, lru_ref, wo_ref, x_ref, lnw_ref, lnb_ref, wr_ref, br_ref,
                       x1_ref, x1s_ref, tope_ref, gate_ref):
    m = jnp.dot(ret_ref[...], wo_ref[0:RET_WIDTH, :], preferred_element_type=f32)
    m = m + jnp.dot(lru_ref[...], wo_ref[RET_WIDTH:D_MODEL, :], preferred_element_type=f32)
    x1 = _layer_norm(DN_ALPHA * x_ref[...] + m, lnw_ref[...], lnb_ref[...])
    x1_ref[...] = x1
    _store_slabs(x1s_ref, x1, zero_pad=True)
    xh, xl = _split_bf16(x1)
    wh, wl = _split_bf16(wr_ref[...])
    logits = (jnp.dot(xh, wh, preferred_element_type=f32) + jnp.dot(xl, wh, preferred_element_type=f32)
              + jnp.dot(xh, wl, preferred_element_type=f32)) + br_ref[...]
    tm = logits.shape[0]
    lane = lax.broadcasted_iota(jnp.int32, logits.shape, 1)
    lane_k = lax.broadcasted_iota(jnp.int32, (tm, TOP_K), 1)
    top_e = jnp.zeros((tm, TOP_K), jnp.int32)
    top_v = jnp.zeros((tm, TOP_K), f32)
    cur = logits
    for kk in range(TOP_K):
        mx = jnp.max(cur, axis=-1, keepdims=True)
        idx = jnp.min(jnp.where(cur == mx, lane, N_EXPERTS), axis=-1, keepdims=True)
        top_e = jnp.where(lane_k == kk, idx, top_e)
        top_v = jnp.where(lane_k == kk, mx, top_v)
        cur = jnp.where(lane == idx, -jnp.inf, cur)
    ex = jnp.exp(top_v - top_v[:, 0:1])
    gate_ref[...] = ex / jnp.sum(ex, axis=-1, keepdims=True)
    tope_ref[...] = top_e


def _out_router(ret_out, lru_out, wo_bf, x2d, ln_w, ln_b, w_router, b_router):
    t, d = x2d.shape
    tm = 256
    row = lambda c: pl.BlockSpec((tm, c), lambda i: (i, 0))
    full = lambda r, c: pl.BlockSpec((r, c), lambda i: (0, 0))
    return pl.pallas_call(
        _out_router_kernel,
        grid=(t // tm,),
        in_specs=[row(RET_WIDTH), row(LRU_WIDTH), full(d, d), row(d), full(1, d), full(1, d),
                  full(d, N_EXPERTS), full(1, N_EXPERTS)],
        out_specs=[row(d), pl.BlockSpec((tm * PITCH, LANES), lambda i: (i, 0)), row(TOP_K), row(TOP_K)],
        out_shape=[jax.ShapeDtypeStruct((t, d), f32), jax.ShapeDtypeStruct((t * PITCH, LANES), f32),
                   jax.ShapeDtypeStruct((t, TOP_K), jnp.int32),
                   jax.ShapeDtypeStruct((t, TOP_K), f32)],
        compiler_params=_cparams(("parallel",)),
        name="out_proj_ln1_router",
    )(ret_out, lru_out, wo_bf, x2d, ln_w.reshape(1, d), ln_b.reshape(1, d), w_router,
      b_router.reshape(1, N_EXPERTS))


def _routing_tables(top_e, t):
    n_pad = t * TOP_K + N_EXPERTS * ROW_CHUNK
    max_units = N_EXPERTS + (t * TOP_K) // UNIT_ROWS
    sel = (top_e[:, :, None] == jnp.arange(N_EXPERTS, dtype=jnp.int32)[None, None, :]).any(axis=1)
    sel = sel.astype(jnp.int32)
    counts = jnp.sum(sel, axis=0)
    rank = jnp.cumsum(sel, axis=0) - sel
    padded = (counts + ROW_CHUNK - 1) // ROW_CHUNK * ROW_CHUNK
    pad_ends = jnp.cumsum(padded)
    pad_starts = pad_ends - padded
    dest_dense = pad_starts[None, :] + rank
    dest = jnp.take_along_axis(dest_dense, top_e, axis=1)
    tok = jnp.broadcast_to(jnp.arange(t, dtype=jnp.int32)[:, None], (t, TOP_K))
    slot_tok = jnp.zeros((n_pad + UNIT_ROWS,), jnp.int32).at[dest.reshape(-1)].set(tok.reshape(-1))
    units_per_e = (padded + UNIT_ROWS - 1) // UNIT_ROWS
    unit_ends = jnp.cumsum(units_per_e)
    unit_starts = unit_ends - units_per_e
    n_units = unit_ends[-1]
    u = jnp.arange(max_units, dtype=jnp.int32)
    u_clamped = jnp.minimum(u, n_units - 1)
    ue = jnp.searchsorted(unit_ends, u_clamped, side='right').astype(jnp.int32)
    ue = jnp.minimum(ue, N_EXPERTS - 1)
    j = u_clamped - unit_starts[ue]
    u_row = pad_starts[ue] + j * UNIT_ROWS
    u_rows = jnp.minimum(UNIT_ROWS, padded[ue] - j * UNIT_ROWS)
    u_chunks = jnp.where(u < n_units, u_rows // ROW_CHUNK, 0).astype(jnp.int32)
    n_used_rows = pad_ends[-1].astype(jnp.int32)
    return dest.astype(jnp.int32), slot_tok, ue, u_row.astype(jnp.int32), u_chunks, n_used_rows, n_pad, max_units


CHUNK_PITCHED = ROW_CHUNK * PITCH


def _moe_kernel(ue_ref, urow_ref, uchunks_ref, used_ref, tok_hbm, x1s_hbm, wg_ref, bg_ref, wu_ref, bu_ref, wd_ref,
                bd_ref, y_hbm, xbuf, yacc, wgub, wdb, gstage, ostage, tok_smem, flags, tok_sem, in_sem, out_sem):
    u = pl.program_id(0)
    f = pl.program_id(1)
    n_u = pl.num_programs(0)
    n_f = pl.num_programs(1)
    n_chunks = uchunks_ref[u]
    row0 = urow_ref[u]
    cur = u % 2
    nxt = 1 - cur
    u_next = jnp.minimum(u + 1, n_u - 1)
    n_next = jnp.where(u + 1 < n_u, uchunks_ref[u_next], 0)
    first_step = jnp.logical_and(u == 0, f == 0)
    PENDING, PEND_CHUNK, PEND_SLOT, OUT_BUSY = 0, 1, 2, 3

    def rows(c, size=ROW_CHUNK):
        return pl.ds(pl.multiple_of(c * size, size), size)

    def tok_copy(unit_row, slot):
        src = tok_hbm.at[pl.ds(pl.multiple_of(unit_row, ROW_CHUNK), UNIT_ROWS)]
        return pltpu.make_async_copy(src, tok_smem.at[slot], tok_sem)

    def gather_start(c, tslot):
        def issue(r, carry):
            tok = tok_smem[tslot, c * ROW_CHUNK + r]
            src = x1s_hbm.at[pl.ds(tok * PITCH, SLAB), :]
            dst = gstage.at[pl.ds(r * PITCH, SLAB), :]
            pltpu.make_async_copy(src, dst, in_sem).start(priority=1)
            return carry

        lax.fori_loop(0, ROW_CHUNK, issue, 0, unroll=8)

    def gather_finish(c, xslot):
        n = ROW_CHUNK * SLAB
        pltpu.make_async_copy(x1s_hbm.at[pl.ds(0, n), :], gstage.at[pl.ds(0, n), :], in_sem).wait()
        for j in range(SLAB):
            xbuf[xslot, rows(c), j * LANES:(j + 1) * LANES] = _slab_cols(gstage, j, ROW_CHUNK).astype(bf16)

    def out_copy(c, slot):
        dst = y_hbm.at[pl.ds(pl.multiple_of((row0 + c * ROW_CHUNK) * PITCH, CHUNK_PITCHED), CHUNK_PITCHED), :]
        return pltpu.make_async_copy(ostage.at[slot], dst, out_sem.at[slot])

    def out_wait(slot):
        @pl.when(flags[OUT_BUSY + slot] == 1)
        def _():
            pltpu.make_async_copy(ostage.at[slot], y_hbm.at[pl.ds(0, CHUNK_PITCHED), :], out_sem.at[slot]).wait()
            flags[OUT_BUSY + slot] = 0

    @pl.when(first_step)
    def _():
        for i in range(OUT_BUSY + 2):
            flags[i] = 0

        @pl.when(n_chunks > 0)
        def _():
            first_table = tok_copy(row0, cur)
            first_table.start()
            first_table.wait()

            def load(c, carry):
                gather_start(c, cur)
                gather_finish(c, cur)
                return carry

            lax.fori_loop(0, n_chunks, load, 0)

    @pl.when(flags[PENDING] == 1)
    def _():
        gather_finish(flags[PEND_CHUNK], flags[PEND_SLOT])
        flags[PENDING] = 0

    @pl.when(jnp.logical_and(f == 0, n_next > 0))
    def _():
        next_table = tok_copy(urow_ref[u_next], nxt)
        next_table.start()
        next_table.wait()

    @pl.when(f < n_next)
    def _():
        gather_start(f, nxt)
        flags[PENDING] = 1
        flags[PEND_CHUNK] = f
        flags[PEND_SLOT] = nxt

    @pl.when(jnp.logical_and(f == 0, n_chunks > 0))
    def _():
        def clear(c, carry):
            yacc[rows(c), :] = jnp.zeros((ROW_CHUNK, D_MODEL), f32)
            return carry

        lax.fori_loop(0, n_chunks, clear, 0)

    @pl.when(n_chunks > 0)
    def _():
        wgub[:, 0:F_TILE] = wg_ref[...].astype(bf16)
        wgub[:, F_TILE:2 * F_TILE] = wu_ref[...].astype(bf16)
        wdb[...] = wd_ref[...].astype(bf16)
        bg = bg_ref[...]
        bu = bu_ref[...]

        def mlp(sl):
            gu = jnp.dot(xbuf[cur, sl, :], wgub[...], preferred_element_type=f32)
            gt = jnp.minimum(gu[:, 0:F_TILE] + bg, SWIGLU_LIMIT)
            up = jnp.clip(gu[:, F_TILE:2 * F_TILE] + bu, -SWIGLU_LIMIT, SWIGLU_LIMIT)
            hid = (up + 1.0) * gt * jax.nn.sigmoid(SWIGLU_ALPHA * gt)
            yacc[sl, :] += jnp.dot(hid.astype(bf16), wdb[...], preferred_element_type=f32)

        def pair(c, carry):
            mlp(rows(c, 2 * ROW_CHUNK))
            return carry

        lax.fori_loop(0, n_chunks // 2, pair, 0)

        @pl.when(n_chunks % 2 == 1)
        def _():
            mlp(rows(n_chunks - 1))

    @pl.when(jnp.logical_and(f == n_f - 1, n_chunks > 0))
    def _():
        bd = bd_ref[...]

        def store(c, carry):
            slot = c % 2
            out_wait(slot)
            _store_slabs(ostage.at[slot], yacc[rows(c), :] + bd, zero_pad=True)
            out_copy(c, slot).start()
            flags[OUT_BUSY + slot] = 1
            return carry

        lax.fori_loop(0, n_chunks, store, 0)

    @pl.when(jnp.logical_and(u == n_u - 1, f == n_f - 1))
    def _():
        out_wait(0)
        out_wait(1)
        first = used_ref[0] // ROW_CHUNK
        last = y_hbm.shape[0] // CHUNK_PITCHED
        ostage[0] = jnp.zeros((CHUNK_PITCHED, LANES), f32)

        def tail_copy(c):
            dst = y_hbm.at[pl.ds(pl.multiple_of(c * CHUNK_PITCHED, CHUNK_PITCHED), CHUNK_PITCHED), :]
            return pltpu.make_async_copy(ostage.at[0], dst, out_sem.at[0])

        def start(c, carry):
            tail_copy(c).start()
            return carry

        def wait(c, carry):
            tail_copy(c).wait()
            return carry

        lax.fori_loop(first, last, start, 0)
        lax.fori_loop(first, last, wait, 0)


def _moe_experts(x1s, slot_tok, ue, u_row, u_chunks, n_used_rows, n_pad, max_units, w_gate, b_gate, w_up, b_up,
                 w_down, b_down):
    e, d, dff = w_gate.shape
    n_f = dff // F_TILE
    assert n_f >= UNIT_ROWS // ROW_CHUNK, "one chunk of the next unit is gathered per f-tile step"
    f_idx = lambda u, f, uc: jnp.where(uc[u] > 0, f, n_f - 1)
    col_w = pl.BlockSpec((None, d, F_TILE), lambda u, f, ue, ur, uc, used: (ue[u], 0, f_idx(u, f, uc)))
    col_b = pl.BlockSpec((None, 1, F_TILE), lambda u, f, ue, ur, uc, used: (ue[u], 0, f_idx(u, f, uc)))
    return pl.pallas_call(
        _moe_kernel,
        grid_spec=pltpu.PrefetchScalarGridSpec(
            num_scalar_prefetch=4,
            grid=(max_units, n_f),
            in_specs=[pl.BlockSpec(memory_space=pl.ANY), pl.BlockSpec(memory_space=pl.ANY),
                      col_w, col_b, col_w, col_b,
                      pl.BlockSpec((None, F_TILE, d), lambda u, f, ue, ur, uc, used: (ue[u], f_idx(u, f, uc), 0)),
                      pl.BlockSpec((None, 1, d), lambda u, f, ue, ur, uc, used: (ue[u], 0, 0))],
            out_specs=pl.BlockSpec(memory_space=pl.ANY),
            scratch_shapes=[pltpu.VMEM((2, UNIT_ROWS, d), bf16),
                            pltpu.VMEM((UNIT_ROWS, d), f32),
                            pltpu.VMEM((d, 2 * F_TILE), bf16),
                            pltpu.VMEM((F_TILE, d), bf16),
                            pltpu.VMEM((CHUNK_PITCHED, LANES), f32),
                            pltpu.VMEM((2, CHUNK_PITCHED, LANES), f32),
                            pltpu.SMEM((2, UNIT_ROWS), jnp.int32),
                            pltpu.SMEM((8,), jnp.int32),
                            pltpu.SemaphoreType.DMA(()),
                            pltpu.SemaphoreType.DMA(()),
                            pltpu.SemaphoreType.DMA((2,))],
        ),
        out_shape=jax.ShapeDtypeStruct((n_pad * PITCH, LANES), f32),
        compiler_params=_cparams(("arbitrary", "arbitrary")),
        name="moe_experts",
    )(ue, u_row, u_chunks, n_used_rows.reshape(1), slot_tok, x1s, w_gate, b_gate.reshape(e, 1, dff), w_up,
      b_up.reshape(e, 1, dff), w_down, b_down.reshape(e, 1, d))


COMBINE_ROWS = 256


def _combine_kernel(dest_ref, y_hbm, gate_ref, x1_ref, p_ref, lnw_ref, lnb_ref, wp_ref, pnw_ref, wg_ref,
                    o_ref, ybuf, fsum_ref, sem):
    tm = COMBINE_ROWS

    def issue(t, c):
        for k in range(TOP_K):
            src = y_hbm.at[pl.ds(dest_ref[t * TOP_K + k] * PITCH, SLAB), :]
            dst = ybuf.at[k, pl.ds(t * PITCH, SLAB), :]
            pltpu.make_async_copy(src, dst, sem).start()
        return c

    lax.fori_loop(0, tm, issue, 0, unroll=2)
    e = jnp.dot(p_ref[...].astype(bf16), wp_ref[...], preferred_element_type=f32)
    e = e * lax.rsqrt(jnp.mean(e * e, axis=-1, keepdims=True) + LN_EPS) * pnw_ref[...]
    for k in range(TOP_K):
        pltpu.make_async_copy(y_hbm.at[pl.ds(0, tm * SLAB), :], ybuf.at[k, pl.ds(0, tm * SLAB), :], sem).wait()
    gates = gate_ref[...]
    for j in range(SLAB):
        acc = gates[:, 0:1] * _slab_cols(ybuf.at[0], j, tm)
        for k in range(1, TOP_K):
            acc = acc + gates[:, k:k + 1] * _slab_cols(ybuf.at[k], j, tm)
        fsum_ref[:, j * LANES:(j + 1) * LANES] = acc
    x2 = _layer_norm(DN_ALPHA * x1_ref[...] + fsum_ref[...], lnw_ref[...], lnb_ref[...])
    gate = jax.nn.sigmoid(jnp.dot(x2.astype(bf16), wg_ref[...], preferred_element_type=f32))
    o_ref[...] = x2 + gate * e


def _combine(y, dest, gates, x1, p2d, ln_w, ln_b, wp_bf, ple_norm_w, wg_bf):
    t, d = x1.shape
    tm = COMBINE_ROWS
    row = lambda c: pl.BlockSpec((tm, c), lambda i: (i, 0))
    full = lambda r, c: pl.BlockSpec((r, c), lambda i: (0, 0))
    return pl.pallas_call(
        _combine_kernel,
        grid=(t // tm,),
        in_specs=[pl.BlockSpec((tm * TOP_K,), lambda i: (i,), memory_space=pltpu.SMEM),
                  pl.BlockSpec(memory_space=pl.ANY),
                  row(TOP_K), row(d), row(PLE_DIM), full(1, d), full(1, d), full(PLE_DIM, d), full(1, d),
                  full(d, d)],
        out_specs=row(d),
        out_shape=jax.ShapeDtypeStruct((t, d), f32),
        scratch_shapes=[pltpu.VMEM((TOP_K, tm * PITCH, LANES), f32), pltpu.VMEM((tm, d), f32),
                        pltpu.SemaphoreType.DMA(())],
        compiler_params=_cparams(("arbitrary",)),
        name="combine_ln2_ple",
    )(dest.reshape(-1), y, gates, x1, p2d, ln_w.reshape(1, d), ln_b.reshape(1, d), wp_bf,
      ple_norm_w.reshape(1, d), wg_bf)


def _layer(h, p_i, w_in, ret_gn_w, conv_w, conv_b, lru_wa, lru_ba, lru_wx, lru_bx, lru_lam, w_out,
           ln1_w, ln1_b, w_router, b_router, w_gate, b_gate, w_up, b_up, w_down, b_down,
           ln2_w, ln2_b, w_ple_proj, ple_norm_w, w_ple_gate):
    b, s, d = h.shape
    t = b * s
    x2d = h.reshape(t, d)
    proj = _in_proj(x2d, w_in.astype(bf16)).reshape(b, s, IN_COLS)
    ret_out = _retention(proj, ret_gn_w)
    lru_out = _lru(proj, conv_w, conv_b, lru_wa, lru_ba, lru_wx, lru_bx, lru_lam)
    x1, x1s, top_e, gates = _out_router(ret_out.reshape(t, RET_WIDTH), lru_out.reshape(t, LRU_WIDTH),
                                        w_out.astype(bf16), x2d, ln1_w, ln1_b, w_router, b_router)
    dest, slot_tok, ue, u_row, u_chunks, n_used_rows, n_pad, max_units = _routing_tables(top_e, t)
    y = _moe_experts(x1s, slot_tok, ue, u_row, u_chunks, n_used_rows, n_pad, max_units, w_gate, b_gate, w_up, b_up,
                     w_down, b_down)
    out = _combine(y, dest, gates, x1, p_i.reshape(t, PLE_DIM), ln2_w, ln2_b, w_ple_proj.astype(bf16),
                   ple_norm_w, w_ple_gate.astype(bf16))
    return out.reshape(b, s, d)


def kernel(x, p, w_in, ret_gn_w, conv_w, conv_b, lru_wa, lru_ba, lru_wx, lru_bx, lru_lam, w_out, ln1_w, ln1_b,
           w_router, b_router, w_gate, b_gate, w_up, b_up, w_down, b_down, ln2_w, ln2_b, w_ple_proj, ple_norm_w,
           w_ple_gate):
    h = x.astype(f32)
    for i in range(w_in.shape[0]):
        h = _layer(h, p[i], w_in[i], ret_gn_w[i], conv_w[i], conv_b[i], lru_wa[i], lru_ba[i], lru_wx[i],
                   lru_bx[i], lru_lam[i], w_out[i], ln1_w[i], ln1_b[i], w_router[i], b_router[i], w_gate[i],
                   b_gate[i], w_up[i], b_up[i], w_down[i], b_down[i], ln2_w[i], ln2_b[i], w_ple_proj[i],
                   ple_norm_w[i], w_ple_gate[i])
    return h.astype(x.dtype)
```

```python
import functools
import math

import jax
import jax.numpy as jnp
from jax import lax
from jax.experimental import pallas as pl
from jax.experimental.pallas import tpu as pltpu

D_MODEL = 2048
RET_HEAD_DIM = 128
RET_HEADS = 8
RET_WIDTH = RET_HEADS * RET_HEAD_DIM
LRU_WIDTH = D_MODEL - RET_WIDTH
LRU_BLOCKS = 8
LRU_BLOCK_DIM = LRU_WIDTH // LRU_BLOCKS
IN_COLS = 4 * RET_WIDTH + 2 * LRU_WIDTH
CONV_WIDTH = 4
LRU_C = 8.0
CHUNK = 128
ROPE_BASE = 10000.0
N_EXPERTS = 32
TOP_K = 4
SWIGLU_LIMIT = 7.0
SWIGLU_ALPHA = 1.702
PLE_DIM = 256
LN_EPS = 1e-5
DEPTH = 1
DN_ALPHA = (2.0 * DEPTH) ** 0.25

LANES = 128
SUBLANES = 8
VMEM_LIMIT = 60 * 1024 * 1024

ROW_CHUNK = 256
UNIT_ROWS = 2048
F_TILE = 256

f32 = jnp.float32
bf16 = jnp.bfloat16


def _cparams(sem):
    return pltpu.CompilerParams(dimension_semantics=sem, vmem_limit_bytes=VMEM_LIMIT)


def _in_proj_kernel(x_ref, w_ref, o_ref, xb_ref):
    @pl.when(pl.program_id(1) == 0)
    def _():
        xb_ref[...] = x_ref[...].astype(bf16)

    o_ref[...] = jnp.dot(xb_ref[...], w_ref[...], preferred_element_type=f32)


def _in_proj(x2d, w_bf):
    t, d = x2d.shape
    n = w_bf.shape[1]
    tm, tn = 1024, 1024
    return pl.pallas_call(
        _in_proj_kernel,
        grid=(t // tm, n // tn),
        in_specs=[pl.BlockSpec((tm, d), lambda i, j: (i, 0)),
                  pl.BlockSpec((d, tn), lambda i, j: (0, j))],
        out_specs=pl.BlockSpec((tm, tn), lambda i, j: (i, j)),
        out_shape=jax.ShapeDtypeStruct((t, n), f32),
        scratch_shapes=[pltpu.VMEM((tm, d), bf16)],
        compiler_params=_cparams(("parallel", "arbitrary")),
        name="in_proj",
    )(x2d, w_bf)


def _retention_kernel(q_ref, k_ref, v_ref, g_ref, cos_ref, sin_ref, dec_ref, qd_ref, kd_ref, cd_ref, gnw_ref,
                      o_ref):
    s = q_ref.shape[0]
    n_chunks = s // CHUNK
    decay = dec_ref[...]
    q_dec = qd_ref[...]
    k_dec = kd_ref[...]
    c_dec = cd_ref[0:1, :]
    gnw = gnw_ref[...]
    k_scale = RET_HEAD_DIM ** -0.5

    def rope(xv, cos, sin):
        return xv * cos + pltpu.roll(xv, RET_HEAD_DIM // 2, axis=1) * sin

    def body(n, state):
        sl = pl.ds(pl.multiple_of(n * CHUNK, CHUNK), CHUNK)
        cos = cos_ref[sl, :]
        sin = sin_ref[sl, :]
        q = rope(q_ref[sl, :], cos, sin)
        k = rope(k_ref[sl, :], cos, sin) * k_scale
        vb = v_ref[sl, :].astype(bf16)
        scores = lax.dot_general(q.astype(bf16), k.astype(bf16), (((1,), (1,)), ((), ())),
                                 preferred_element_type=f32) * decay
        intra = jnp.dot(scores.astype(bf16), vb, preferred_element_type=f32)
        cross = jnp.dot((q * q_dec).astype(bf16), state.astype(bf16), preferred_element_type=f32)
        kv = lax.dot_general((k * k_dec).astype(bf16), vb, (((0,), (0,)), ((), ())),
                             preferred_element_type=f32)
        ret = intra + cross
        mu = jnp.mean(ret, axis=-1, keepdims=True)
        cen = ret - mu
        var = jnp.mean(cen * cen, axis=-1, keepdims=True)
        ret = cen * lax.rsqrt(var + LN_EPS) * gnw
        g = g_ref[sl, :]
        o_ref[sl, :] = (g * jax.nn.sigmoid(g) * ret).astype(o_ref.dtype)
        return c_dec * state + kv

    lax.fori_loop(0, n_chunks, body, jnp.zeros((RET_HEAD_DIM, RET_HEAD_DIM), f32), unroll=2)


def _retention_tables(s):
    h, d = RET_HEADS, RET_HEAD_DIM
    inv = ROPE_BASE ** (-jnp.arange(0, d, 2, dtype=f32) / d)
    ang = jnp.arange(s, dtype=f32)[:, None] * inv[None, :]
    cos = jnp.cos(ang)
    sin = jnp.sin(ang)
    cos_t = jnp.concatenate([cos, cos], axis=-1)
    sin_t = jnp.concatenate([-sin, sin], axis=-1)
    log_gamma = jnp.log1p(-jnp.exp2(-5.0 - jnp.arange(h, dtype=f32)))
    idx = jnp.arange(CHUNK, dtype=f32)
    diff = idx[:, None] - idx[None, :]
    decay = jnp.where((diff >= 0)[None], jnp.exp(jnp.maximum(diff, 0.0)[None] * log_gamma[:, None, None]), 0.0)
    q_dec = jnp.exp((idx[None, :] + 1.0) * log_gamma[:, None])
    k_dec = jnp.exp((CHUNK - 1.0 - idx)[None, :] * log_gamma[:, None])
    c_dec = jnp.exp(CHUNK * log_gamma)
    q_dec = jnp.broadcast_to(q_dec[:, :, None], (h, CHUNK, d))
    k_dec = jnp.broadcast_to(k_dec[:, :, None], (h, CHUNK, d))
    c_dec = jnp.broadcast_to(c_dec[:, None, None], (h, SUBLANES, d))
    return cos_t, sin_t, decay, q_dec, k_dec, c_dec


def _retention(proj, ret_gn_w):
    b, s, _ = proj.shape
    d = RET_HEAD_DIM
    cos_t, sin_t, decay, q_dec, k_dec, c_dec = _retention_tables(s)
    col = lambda off: pl.BlockSpec((None, s, d), lambda bi, hi: (bi, 0, off + hi))
    per_head = lambda r: pl.BlockSpec((None, r, d), lambda bi, hi: (hi, 0, 0))
    full = lambda shp: pl.BlockSpec(shp, lambda bi, hi: (0,) * len(shp))
    return pl.pallas_call(
        _retention_kernel,
        grid=(b, RET_HEADS),
        in_specs=[col(0), col(RET_HEADS), col(2 * RET_HEADS), col(3 * RET_HEADS),
                  full((s, d)), full((s, d)),
                  per_head(CHUNK), per_head(CHUNK), per_head(CHUNK), per_head(SUBLANES),
                  pl.BlockSpec((1, d), lambda bi, hi: (0, hi))],
        out_specs=pl.BlockSpec((None, s, d), lambda bi, hi: (bi, 0, hi)),
        out_shape=jax.ShapeDtypeStruct((b, s, RET_WIDTH), bf16),
        compiler_params=_cparams(("parallel", "parallel")),
        name="retention",
    )(proj, proj, proj, proj, cos_t, sin_t, decay, q_dec, k_dec, c_dec, ret_gn_w.reshape(1, RET_WIDTH))


def _gelu_tanh(x):
    return 0.5 * x * (1.0 + jnp.tanh(math.sqrt(2.0 / math.pi) * (x + 0.044715 * (x * x * x))))


def _lru_kernel(xr_ref, yg_ref, cw_ref, cb_ref, wa_ref, ba_ref, wx_ref, bx_ref, lam_ref, o_ref, a_ref, b_ref):
    s = xr_ref.shape[0]
    x = xr_ref[...]
    rows = lax.broadcasted_iota(jnp.int32, x.shape, 0)
    xc = cb_ref[...] + cw_ref[CONV_WIDTH - 1:CONV_WIDTH, :] * x
    for back in range(1, CONV_WIDTH):
        shifted = jnp.where(rows >= back, pltpu.roll(x, back, axis=0), 0.0)
        xc = xc + cw_ref[CONV_WIDTH - 1 - back:CONV_WIDTH - back, :] * shifted
    xcb = xc.astype(bf16)
    r = jax.nn.sigmoid(jnp.dot(xcb, wa_ref[...].astype(bf16), preferred_element_type=f32) + ba_ref[...])
    gi = jax.nn.sigmoid(jnp.dot(xcb, wx_ref[...].astype(bf16), preferred_element_type=f32) + bx_ref[...])
    lam = lam_ref[...]
    log_sig = jnp.minimum(lam, 0.0) - jnp.log1p(jnp.exp(-jnp.abs(lam)))
    log_a = LRU_C * r * log_sig
    a = jnp.exp(log_a)
    a_ref[...] = a
    b_ref[...] = jnp.sqrt(-jnp.tanh(log_a) * (a * a + 1.0)) * (gi * xc)

    row8 = lax.broadcasted_iota(jnp.int32, (SUBLANES, LANES), 0)

    def body(i, h_prev):
        sl = pl.ds(pl.multiple_of(i * SUBLANES, SUBLANES), SUBLANES)
        a8 = a_ref[sl, :]
        b8 = b_ref[sl, :]
        for sh in (1, 2, 4):
            a_sh = jnp.where(row8 >= sh, pltpu.roll(a8, sh, axis=0), 1.0)
            b_sh = jnp.where(row8 >= sh, pltpu.roll(b8, sh, axis=0), 0.0)
            b8 = a8 * b_sh + b8
            a8 = a8 * a_sh
        h8 = a8 * h_prev + b8
        o_ref[sl, :] = (_gelu_tanh(yg_ref[sl, :]) * h8).astype(o_ref.dtype)
        return h8[SUBLANES - 1:SUBLANES, :]

    lax.fori_loop(0, s // SUBLANES, body, jnp.zeros((1, LANES), f32), unroll=8)


def _lru(proj, conv_w, conv_b, wa, ba, wx, bx, lam):
    b, s, _ = proj.shape
    d = LRU_BLOCK_DIM
    xr_off = 4 * RET_WIDTH // d
    yg_off = xr_off + LRU_BLOCKS
    col = lambda off: pl.BlockSpec((None, s, d), lambda bi, ji: (bi, 0, off + ji))
    vec = lambda r: pl.BlockSpec((r, d), lambda bi, ji: (0, ji))
    blk = lambda r: pl.BlockSpec((None, r, d), lambda bi, ji: (ji, 0, 0))
    return pl.pallas_call(
        _lru_kernel,
        grid=(b, LRU_BLOCKS),
        in_specs=[col(xr_off), col(yg_off), vec(CONV_WIDTH), vec(1), blk(d), blk(1), blk(d), blk(1), vec(1)],
        out_specs=pl.BlockSpec((None, s, d), lambda bi, ji: (bi, 0, ji)),
        out_shape=jax.ShapeDtypeStruct((b, s, LRU_WIDTH), bf16),
        scratch_shapes=[pltpu.VMEM((s, d), f32), pltpu.VMEM((s, d), f32)],
        compiler_params=_cparams(("parallel", "parallel")),
        name="rg_lru",
    )(proj, proj, conv_w, conv_b.reshape(1, LRU_WIDTH), wa, ba.reshape(LRU_BLOCKS, 1, d), wx,
      bx.reshape(LRU_BLOCKS, 1, d), lam.reshape(1, LRU_WIDTH))


def _layer_norm(y, w, b):
    mu = jnp.mean(y, axis=-1, keepdims=True)
    cen = y - mu
    var = jnp.mean(cen * cen, axis=-1, keepdims=True)
    return cen * lax.rsqrt(var + LN_EPS) * w + b


SLAB = D_MODEL // LANES
PITCH = SLAB + 4


def _store_slabs(ref, val, zero_pad):
    n = val.shape[0]
    for j in range(SLAB):
        ref[pl.ds(j, n, stride=PITCH), :] = val[:, j * LANES:(j + 1) * LANES]
    if zero_pad:
        for j in range(SLAB, PITCH):
            ref[pl.ds(j, n, stride=PITCH), :] = jnp.zeros((n, LANES), val.dtype)


def _slab_cols(ref, j, n):
    return ref[pl.ds(j, n, stride=PITCH), :]


def _split_bf16(v):
    hi = v.astype(bf16)
    lo = (v - hi.astype(f32)).astype(bf16)
    return hi, lo


def _out_router_kernel(ret_ref, lru_ref, wo_ref, x_ref, lnw_ref, lnb_ref, wr_ref, br_ref,
                       x1_ref, x1s_ref, tope_ref, gate_ref):
    m = jnp.dot(ret_ref[...], wo_ref[0:RET_WIDTH, :], preferred_element_type=f32)
    m = m + jnp.dot(lru_ref[...], wo_ref[RET_WIDTH:D_MODEL, :], preferred_element_type=f32)
    x1 = _layer_norm(DN_ALPHA * x_ref[...] + m, lnw_ref[...], lnb_ref[...])
    x1_ref[...] = x1
    _store_slabs(x1s_ref, x1, zero_pad=True)
    xh, xl = _split_bf16(x1)
    wh, wl = _split_bf16(wr_ref[...])
    logits = (jnp.dot(xh, wh, preferred_element_type=f32) + jnp.dot(xl, wh, preferred_element_type=f32)
              + jnp.dot(xh, wl, preferred_element_type=f32)) + br_ref[...]
    tm = logits.shape[0]
    lane = lax.broadcasted_iota(jnp.int32, logits.shape, 1)
    lane_k = lax.broadcasted_iota(jnp.int32, (tm, TOP_K), 1)
    top_e = jnp.zeros((tm, TOP_K), jnp.int32)
    top_v = jnp.zeros((tm, TOP_K), f32)
    cur = logits
    for kk in range(TOP_K):
        mx = jnp.max(cur, axis=-1, keepdims=True)
        idx = jnp.min(jnp.where(cur == mx, lane, N_EXPERTS), axis=-1, keepdims=True)
        top_e = jnp.where(lane_k == kk, idx, top_e)
        top_v = jnp.where(lane_k == kk, mx, top_v)
        cur = jnp.where(lane == idx, -jnp.inf, cur)
    ex = jnp.exp(top_v - top_v[:, 0:1])
    gate_ref[...] = ex / jnp.sum(ex, axis=-1, keepdims=True)
    tope_ref[...] = top_e


def _out_router(ret_out, lru_out, wo_bf, x2d, ln_w, ln_b, w_router, b_router):
    t, d = x2d.shape
    tm = 256
    row = lambda c: pl.BlockSpec((tm, c), lambda i: (i, 0))
    full = lambda r, c: pl.BlockSpec((r, c), lambda i: (0, 0))
    return pl.pallas_call(
        _out_router_kernel,
        grid=(t // tm,),
        in_specs=[row(RET_WIDTH), row(LRU_WIDTH), full(d, d), row(d), full(1, d), full(1, d),
                  full(d, N_EXPERTS), full(1, N_EXPERTS)],
        out_specs=[row(d), pl.BlockSpec((tm * PITCH, LANES), lambda i: (i, 0)), row(TOP_K), row(TOP_K)],
        out_shape=[jax.ShapeDtypeStruct((t, d), f32), jax.ShapeDtypeStruct((t * PITCH, LANES), f32),
                   jax.ShapeDtypeStruct((t, TOP_K), jnp.int32),
                   jax.ShapeDtypeStruct((t, TOP_K), f32)],
        compiler_params=_cparams(("parallel",)),
        name="out_proj_ln1_router",
    )(ret_out, lru_out, wo_bf, x2d, ln_w.reshape(1, d), ln_b.reshape(1, d), w_router,
      b_router.reshape(1, N_EXPERTS))


def _routing_tables(top_e, t):
    n_pad = t * TOP_K + N_EXPERTS * ROW_CHUNK
    max_units = N_EXPERTS + (t * TOP_K) // UNIT_ROWS
    sel = (top_e[:, :, None] == jnp.arange(N_EXPERTS, dtype=jnp.int32)[None, None, :]).any(axis=1)
    sel = sel.astype(jnp.int32)
    counts = jnp.sum(sel, axis=0)
    rank = jnp.cumsum(sel, axis=0) - sel
    padded = (counts + ROW_CHUNK - 1) // ROW_CHUNK * ROW_CHUNK
    pad_ends = jnp.cumsum(padded)
    pad_starts = pad_ends - padded
    dest_dense = pad_starts[None, :] + rank
    dest = jnp.take_along_axis(dest_dense, top_e, axis=1)
    tok = jnp.broadcast_to(jnp.arange(t, dtype=jnp.int32)[:, None], (t, TOP_K))
    slot_tok = jnp.zeros((n_pad + UNIT_ROWS,), jnp.int32).at[dest.reshape(-1)].set(tok.reshape(-1))
    units_per_e = (padded + UNIT_ROWS - 1) // UNIT_ROWS
    unit_ends = jnp.cumsum(units_per_e)
    unit_starts = unit_ends - units_per_e
    n_units = unit_ends[-1]
    u = jnp.arange(max_units, dtype=jnp.int32)
    u_clamped = jnp.minimum(u, n_units - 1)
    ue = jnp.searchsorted(unit_ends, u_clamped, side='right').astype(jnp.int32)
    ue = jnp.minimum(ue, N_EXPERTS - 1)
    j = u_clamped - unit_starts[ue]
    u_row = pad_starts[ue] + j * UNIT_ROWS
    u_rows = jnp.minimum(UNIT_ROWS, padded[ue] - j * UNIT_ROWS)
    u_chunks = jnp.where(u < n_units, u_rows // ROW_CHUNK, 0).astype(jnp.int32)
    n_used_rows = pad_ends[-1].astype(jnp.int32)
    return dest.astype(jnp.int32), slot_tok, ue, u_row.astype(jnp.int32), u_chunks, n_used_rows, n_pad, max_units


CHUNK_PITCHED = ROW_CHUNK * PITCH


def _moe_kernel(ue_ref, urow_ref, uchunks_ref, used_ref, tok_hbm, x1s_hbm, wg_ref, bg_ref, wu_ref, bu_ref, wd_ref,
                bd_ref, y_hbm, xbuf, yacc, wgub, wdb, gstage, ostage, tok_smem, flags, tok_sem, in_sem, out_sem):
    u = pl.program_id(0)
    f = pl.program_id(1)
    n_u = pl.num_programs(0)
    n_f = pl.num_programs(1)
    n_chunks = uchunks_ref[u]
    row0 = urow_ref[u]
    cur = u % 2
    nxt = 1 - cur
    u_next = jnp.minimum(u + 1, n_u - 1)
    n_next = jnp.where(u + 1 < n_u, uchunks_ref[u_next], 0)
    first_step = jnp.logical_and(u == 0, f == 0)
    PENDING, PEND_CHUNK, PEND_SLOT, OUT_BUSY = 0, 1, 2, 3

    def rows(c, size=ROW_CHUNK):
        return pl.ds(pl.multiple_of(c * size, size), size)

    def tok_copy(unit_row, slot):
        src = tok_hbm.at[pl.ds(pl.multiple_of(unit_row, ROW_CHUNK), UNIT_ROWS)]
        return pltpu.make_async_copy(src, tok_smem.at[slot], tok_sem)

    ISSUE_GROUP = 8

    def gather_start(c, tslot, lo=0, hi=ROW_CHUNK):
        def issue(g, carry):
            for i in range(ISSUE_GROUP):
                r = g * ISSUE_GROUP + i
                tok = tok_smem[tslot, c * ROW_CHUNK + r]
                src = x1s_hbm.at[pl.ds(tok * PITCH, SLAB), :]
                dst = gstage.at[pl.ds(r * PITCH, SLAB), :]
                pltpu.make_async_copy(src, dst, in_sem).start()
            return carry

        lax.fori_loop(lo // ISSUE_GROUP, hi // ISSUE_GROUP, issue, 0)

    def gather_finish(c, xslot):
        n = ROW_CHUNK * SLAB
        pltpu.make_async_copy(x1s_hbm.at[pl.ds(0, n), :], gstage.at[pl.ds(0, n), :], in_sem).wait()
        for j in range(SLAB):
            xbuf[xslot, rows(c), j * LANES:(j + 1) * LANES] = _slab_cols(gstage, j, ROW_CHUNK).astype(bf16)

    def out_copy(c, slot):
        dst = y_hbm.at[pl.ds(pl.multiple_of((row0 + c * ROW_CHUNK) * PITCH, CHUNK_PITCHED), CHUNK_PITCHED), :]
        return pltpu.make_async_copy(ostage.at[slot], dst, out_sem.at[slot])

    def out_wait(slot):
        @pl.when(flags[OUT_BUSY + slot] == 1)
        def _():
            pltpu.make_async_copy(ostage.at[slot], y_hbm.at[pl.ds(0, CHUNK_PITCHED), :], out_sem.at[slot]).wait()
            flags[OUT_BUSY + slot] = 0

    @pl.when(first_step)
    def _():
        for i in range(OUT_BUSY + 2):
            flags[i] = 0

        @pl.when(n_chunks > 0)
        def _():
            first_table = tok_copy(row0, cur)
            first_table.start()
            first_table.wait()

            def load(c, carry):
                gather_start(c, cur)
                gather_finish(c, cur)
                return carry

            lax.fori_loop(0, n_chunks, load, 0)

    @pl.when(flags[PENDING] == 1)
    def _():
        gather_finish(flags[PEND_CHUNK], flags[PEND_SLOT])
        flags[PENDING] = 0

    @pl.when(jnp.logical_and(f == 0, n_next > 0))
    def _():
        next_table = tok_copy(urow_ref[u_next], nxt)
        next_table.start()
        next_table.wait()

    prefetching = f < n_next
    n_blocks = n_chunks // 2 + n_chunks % 2
    share = (ROW_CHUNK // jnp.maximum(n_blocks, 1) + ISSUE_GROUP - 1) // ISSUE_GROUP * ISSUE_GROUP

    def prefetch_share(i):
        lo = jnp.minimum(i * share, ROW_CHUNK)
        hi = jnp.where(i == n_blocks - 1, ROW_CHUNK, jnp.minimum(lo + share, ROW_CHUNK))
        gather_start(f, nxt, lo, jnp.where(prefetching, hi, lo))

    @pl.when(prefetching)
    def _():
        flags[PENDING] = 1
        flags[PEND_CHUNK] = f
        flags[PEND_SLOT] = nxt

    @pl.when(jnp.logical_and(f == 0, n_chunks > 0))
    def _():
        def clear(c, carry):
            yacc[rows(c), :] = jnp.zeros((ROW_CHUNK, D_MODEL), f32)
            return carry

        lax.fori_loop(0, n_chunks, clear, 0)

    @pl.when(n_chunks > 0)
    def _():
        wgub[:, 0:F_TILE] = wg_ref[...].astype(bf16)
        wgub[:, F_TILE:2 * F_TILE] = wu_ref[...].astype(bf16)
        wdb[...] = wd_ref[...].astype(bf16)
        bg = bg_ref[...]
        bu = bu_ref[...]

        def mlp(sl):
            gu = jnp.dot(xbuf[cur, sl, :], wgub[...], preferred_element_type=f32)
            gt = jnp.minimum(gu[:, 0:F_TILE] + bg, SWIGLU_LIMIT)
            up = jnp.clip(gu[:, F_TILE:2 * F_TILE] + bu, -SWIGLU_LIMIT, SWIGLU_LIMIT)
            hid = (up + 1.0) * gt * jax.nn.sigmoid(SWIGLU_ALPHA * gt)
            yacc[sl, :] += jnp.dot(hid.astype(bf16), wdb[...], preferred_element_type=f32)

        def pair(c, carry):
            prefetch_share(c)
            mlp(rows(c, 2 * ROW_CHUNK))
            return carry

        lax.fori_loop(0, n_chunks // 2, pair, 0)

        @pl.when(n_chunks % 2 == 1)
        def _():
            prefetch_share(n_chunks // 2)
            mlp(rows(n_chunks - 1))

    @pl.when(jnp.logical_and(f == n_f - 1, n_chunks > 0))
    def _():
        bd = bd_ref[...]

        def store(c, carry):
            slot = c % 2
            out_wait(slot)
            _store_slabs(ostage.at[slot], yacc[rows(c), :] + bd, zero_pad=True)
            out_copy(c, slot).start()
            flags[OUT_BUSY + slot] = 1
            return carry

        lax.fori_loop(0, n_chunks, store, 0)

    @pl.when(jnp.logical_and(u == n_u - 1, f == n_f - 1))
    def _():
        out_wait(0)
        out_wait(1)
        first = used_ref[0] // ROW_CHUNK
        last = y_hbm.shape[0] // CHUNK_PITCHED
        ostage[0] = jnp.zeros((CHUNK_PITCHED, LANES), f32)

        def tail_copy(c):
            dst = y_hbm.at[pl.ds(pl.multiple_of(c * CHUNK_PITCHED, CHUNK_PITCHED), CHUNK_PITCHED), :]
            return pltpu.make_async_copy(ostage.at[0], dst, out_sem.at[0])

        def start(c, carry):
            tail_copy(c).start()
            return carry

        def wait(c, carry):
            tail_copy(c).wait()
            return carry

        lax.fori_loop(first, last, start, 0)
        lax.fori_loop(first, last, wait, 0)


def _moe_experts(x1s, slot_tok, ue, u_row, u_chunks, n_used_rows, n_pad, max_units, w_gate, b_gate, w_up, b_up,
                 w_down, b_down):
    e, d, dff = w_gate.shape
    n_f = dff // F_TILE
    assert n_f >= UNIT_ROWS // ROW_CHUNK, "one chunk of the next unit is gathered per f-tile step"
    f_idx = lambda u, f, uc: jnp.where(uc[u] > 0, f, n_f - 1)
    col_w = pl.BlockSpec((None, d, F_TILE), lambda u, f, ue, ur, uc, used: (ue[u], 0, f_idx(u, f, uc)))
    col_b = pl.BlockSpec((None, 1, F_TILE), lambda u, f, ue, ur, uc, used: (ue[u], 0, f_idx(u, f, uc)))
    return pl.pallas_call(
        _moe_kernel,
        grid_spec=pltpu.PrefetchScalarGridSpec(
            num_scalar_prefetch=4,
            grid=(max_units, n_f),
            in_specs=[pl.BlockSpec(memory_space=pl.ANY), pl.BlockSpec(memory_space=pl.ANY),
                      col_w, col_b, col_w, col_b,
                      pl.BlockSpec((None, F_TILE, d), lambda u, f, ue, ur, uc, used: (ue[u], f_idx(u, f, uc), 0)),
                      pl.BlockSpec((None, 1, d), lambda u, f, ue, ur, uc, used: (ue[u], 0, 0))],
            out_specs=pl.BlockSpec(memory_space=pl.ANY),
            scratch_shapes=[pltpu.VMEM((2, UNIT_ROWS, d), bf16),
                            pltpu.VMEM((UNIT_ROWS, d), f32),
                            pltpu.VMEM((d, 2 * F_TILE), bf16),
                            pltpu.VMEM((F_TILE, d), bf16),
                            pltpu.VMEM((CHUNK_PITCHED, LANES), f32),
                            pltpu.VMEM((2, CHUNK_PITCHED, LANES), f32),
                            pltpu.SMEM((2, UNIT_ROWS), jnp.int32),
                            pltpu.SMEM((8,), jnp.int32),
                            pltpu.SemaphoreType.DMA(()),
                            pltpu.SemaphoreType.DMA(()),
                            pltpu.SemaphoreType.DMA((2,))],
        ),
        out_shape=jax.ShapeDtypeStruct((n_pad * PITCH, LANES), f32),
        compiler_params=_cparams(("arbitrary", "arbitrary")),
        name="moe_experts",
    )(ue, u_row, u_chunks, n_used_rows.reshape(1), slot_tok, x1s, w_gate, b_gate.reshape(e, 1, dff), w_up,
      b_up.reshape(e, 1, dff), w_down, b_down.reshape(e, 1, d))


COMBINE_ROWS = 256


def _combine_kernel(dest_ref, y_hbm, gate_ref, x1_ref, p_ref, lnw_ref, lnb_ref, wp_ref, pnw_ref, wg_ref,
                    o_ref, ybuf, fsum_ref, sem):
    tm = COMBINE_ROWS

    def issue(t, c):
        for k in range(TOP_K):
            src = y_hbm.at[pl.ds(dest_ref[t * TOP_K + k] * PITCH, SLAB), :]
            dst = ybuf.at[k, pl.ds(t * PITCH, SLAB), :]
            pltpu.make_async_copy(src, dst, sem).start()
        return c

    lax.fori_loop(0, tm, issue, 0, unroll=2)
    e = jnp.dot(p_ref[...].astype(bf16), wp_ref[...], preferred_element_type=f32)
    e = e * lax.rsqrt(jnp.mean(e * e, axis=-1, keepdims=True) + LN_EPS) * pnw_ref[...]
    for k in range(TOP_K):
        pltpu.make_async_copy(y_hbm.at[pl.ds(0, tm * SLAB), :], ybuf.at[k, pl.ds(0, tm * SLAB), :], sem).wait()
    gates = gate_ref[...]
    for j in range(SLAB):
        acc = gates[:, 0:1] * _slab_cols(ybuf.at[0], j, tm)
        for k in range(1, TOP_K):
            acc = acc + gates[:, k:k + 1] * _slab_cols(ybuf.at[k], j, tm)
        fsum_ref[:, j * LANES:(j + 1) * LANES] = acc
    x2 = _layer_norm(DN_ALPHA * x1_ref[...] + fsum_ref[...], lnw_ref[...], lnb_ref[...])
    gate = jax.nn.sigmoid(jnp.dot(x2.astype(bf16), wg_ref[...], preferred_element_type=f32))
    o_ref[...] = x2 + gate * e


def _combine(y, dest, gates, x1, p2d, ln_w, ln_b, wp_bf, ple_norm_w, wg_bf):
    t, d = x1.shape
    tm = COMBINE_ROWS
    row = lambda c: pl.BlockSpec((tm, c), lambda i: (i, 0))
    full = lambda r, c: pl.BlockSpec((r, c), lambda i: (0, 0))
    return pl.pallas_call(
        _combine_kernel,
        grid=(t // tm,),
        in_specs=[pl.BlockSpec((tm * TOP_K,), lambda i: (i,), memory_space=pltpu.SMEM),
                  pl.BlockSpec(memory_space=pl.ANY),
                  row(TOP_K), row(d), row(PLE_DIM), full(1, d), full(1, d), full(PLE_DIM, d), full(1, d),
                  full(d, d)],
        out_specs=row(d),
        out_shape=jax.ShapeDtypeStruct((t, d), f32),
        scratch_shapes=[pltpu.VMEM((TOP_K, tm * PITCH, LANES), f32), pltpu.VMEM((tm, d), f32),
                        pltpu.SemaphoreType.DMA(())],
        compiler_params=_cparams(("arbitrary",)),
        name="combine_ln2_ple",
    )(dest.reshape(-1), y, gates, x1, p2d, ln_w.reshape(1, d), ln_b.reshape(1, d), wp_bf,
      ple_norm_w.reshape(1, d), wg_bf)


def _layer(h, p_i, w_in, ret_gn_w, conv_w, conv_b, lru_wa, lru_ba, lru_wx, lru_bx, lru_lam, w_out,
           ln1_w, ln1_b, w_router, b_router, w_gate, b_gate, w_up, b_up, w_down, b_down,
           ln2_w, ln2_b, w_ple_proj, ple_norm_w, w_ple_gate):
    b, s, d = h.shape
    t = b * s
    x2d = h.reshape(t, d)
    proj = _in_proj(x2d, w_in.astype(bf16)).reshape(b, s, IN_COLS)
    ret_out = _retention(proj, ret_gn_w)
    lru_out = _lru(proj, conv_w, conv_b, lru_wa, lru_ba, lru_wx, lru_bx, lru_lam)
    x1, x1s, top_e, gates = _out_router(ret_out.reshape(t, RET_WIDTH), lru_out.reshape(t, LRU_WIDTH),
                                        w_out.astype(bf16), x2d, ln1_w, ln1_b, w_router, b_router)
    dest, slot_tok, ue, u_row, u_chunks, n_used_rows, n_pad, max_units = _routing_tables(top_e, t)
    y = _moe_experts(x1s, slot_tok, ue, u_row, u_chunks, n_used_rows, n_pad, max_units, w_gate, b_gate, w_up, b_up,
                     w_down, b_down)
    out = _combine(y, dest, gates, x1, p_i.reshape(t, PLE_DIM), ln2_w, ln2_b, w_ple_proj.astype(bf16),
                   ple_norm_w, w_ple_gate.astype(bf16))
    return out.reshape(b, s, d)


def kernel(x, p, w_in, ret_gn_w, conv_w, conv_b, lru_wa, lru_ba, lru_wx, lru_bx, lru_lam, w_out, ln1_w, ln1_b,
           w_router, b_router, w_gate, b_gate, w_up, b_up, w_down, b_down, ln2_w, ln2_b, w_ple_proj, ple_norm_w,
           w_ple_gate):
    h = x.astype(f32)
    for i in range(w_in.shape[0]):
        h = _layer(h, p[i], w_in[i], ret_gn_w[i], conv_w[i], conv_b[i], lru_wa[i], lru_ba[i], lru_wx[i],
                   lru_bx[i], lru_lam[i], w_out[i], ln1_w[i], ln1_b[i], w_router[i], b_router[i], w_gate[i],
                   b_gate[i], w_up[i], b_up[i], w_down[i], b_down[i], ln2_w[i], ln2_b[i], w_ple_proj[i],
                   ple_norm_w[i], w_ple_gate[i])
    return h.astype(x.dtype)
```

```python
import functools
import math

import jax
import jax.numpy as jnp
from jax import lax
from jax.experimental import pallas as pl
from jax.experimental.pallas import tpu as pltpu

D_MODEL = 2048
RET_HEAD_DIM = 128
RET_HEADS = 8
RET_WIDTH = RET_HEADS * RET_HEAD_DIM
LRU_WIDTH = D_MODEL - RET_WIDTH
LRU_BLOCKS = 8
LRU_BLOCK_DIM = LRU_WIDTH // LRU_BLOCKS
IN_COLS = 4 * RET_WIDTH + 2 * LRU_WIDTH
CONV_WIDTH = 4
LRU_C = 8.0
CHUNK = 128
ROPE_BASE = 10000.0
N_EXPERTS = 32
TOP_K = 4
SWIGLU_LIMIT = 7.0
SWIGLU_ALPHA = 1.702
PLE_DIM = 256
LN_EPS = 1e-5
DEPTH = 1
DN_ALPHA = (2.0 * DEPTH) ** 0.25

LANES = 128
SUBLANES = 8
VMEM_LIMIT = 60 * 1024 * 1024

ROW_CHUNK = 256
UNIT_ROWS = 2048
F_TILE = 256

f32 = jnp.float32
bf16 = jnp.bfloat16


def _cparams(sem):
    return pltpu.CompilerParams(dimension_semantics=sem, vmem_limit_bytes=VMEM_LIMIT)


def _in_proj_kernel(x_ref, w_ref, o_ref, xb_ref):
    @pl.when(pl.program_id(1) == 0)
    def _():
        xb_ref[...] = x_ref[...].astype(bf16)

    o_ref[...] = jnp.dot(xb_ref[...], w_ref[...], preferred_element_type=f32)


def _in_proj(x2d, w_bf):
    t, d = x2d.shape
    n = w_bf.shape[1]
    tm, tn = 1024, 1024
    return pl.pallas_call(
        _in_proj_kernel,
        grid=(t // tm, n // tn),
        in_specs=[pl.BlockSpec((tm, d), lambda i, j: (i, 0)),
                  pl.BlockSpec((d, tn), lambda i, j: (0, j))],
        out_specs=pl.BlockSpec((tm, tn), lambda i, j: (i, j)),
        out_shape=jax.ShapeDtypeStruct((t, n), f32),
        scratch_shapes=[pltpu.VMEM((tm, d), bf16)],
        compiler_params=_cparams(("parallel", "arbitrary")),
        name="in_proj",
    )(x2d, w_bf)


def _retention_kernel(q_ref, k_ref, v_ref, g_ref, cos_ref, sin_ref, dec_ref, qd_ref, kd_ref, cd_ref, gnw_ref,
                      o_ref):
    s = q_ref.shape[0]
    n_chunks = s // CHUNK
    decay = dec_ref[...]
    q_dec = qd_ref[...]
    k_dec = kd_ref[...]
    c_dec = cd_ref[0:1, :]
    gnw = gnw_ref[...]
    k_scale = RET_HEAD_DIM ** -0.5

    def rope(xv, cos, sin):
        return xv * cos + pltpu.roll(xv, RET_HEAD_DIM // 2, axis=1) * sin

    def body(n, state):
        sl = pl.ds(pl.multiple_of(n * CHUNK, CHUNK), CHUNK)
        cos = cos_ref[sl, :]
        sin = sin_ref[sl, :]
        q = rope(q_ref[sl, :], cos, sin)
        k = rope(k_ref[sl, :], cos, sin) * k_scale
        vb = v_ref[sl, :].astype(bf16)
        scores = lax.dot_general(q.astype(bf16), k.astype(bf16), (((1,), (1,)), ((), ())),
                                 preferred_element_type=f32) * decay
        intra = jnp.dot(scores.astype(bf16), vb, preferred_element_type=f32)
        cross = jnp.dot((q * q_dec).astype(bf16), state.astype(bf16), preferred_element_type=f32)
        kv = lax.dot_general((k * k_dec).astype(bf16), vb, (((0,), (0,)), ((), ())),
                             preferred_element_type=f32)
        ret = intra + cross
        mu = jnp.mean(ret, axis=-1, keepdims=True)
        cen = ret - mu
        var = jnp.mean(cen * cen, axis=-1, keepdims=True)
        ret = cen * lax.rsqrt(var + LN_EPS) * gnw
        g = g_ref[sl, :]
        o_ref[sl, :] = (g * jax.nn.sigmoid(g) * ret).astype(o_ref.dtype)
        return c_dec * state + kv

    lax.fori_loop(0, n_chunks, body, jnp.zeros((RET_HEAD_DIM, RET_HEAD_DIM), f32), unroll=2)


def _retention_tables(s):
    h, d = RET_HEADS, RET_HEAD_DIM
    inv = ROPE_BASE ** (-jnp.arange(0, d, 2, dtype=f32) / d)
    ang = jnp.arange(s, dtype=f32)[:, None] * inv[None, :]
    cos = jnp.cos(ang)
    sin = jnp.sin(ang)
    cos_t = jnp.concatenate([cos, cos], axis=-1)
    sin_t = jnp.concatenate([-sin, sin], axis=-1)
    log_gamma = jnp.log1p(-jnp.exp2(-5.0 - jnp.arange(h, dtype=f32)))
    idx = jnp.arange(CHUNK, dtype=f32)
    diff = idx[:, None] - idx[None, :]
    decay = jnp.where((diff >= 0)[None], jnp.exp(jnp.maximum(diff, 0.0)[None] * log_gamma[:, None, None]), 0.0)
    q_dec = jnp.exp((idx[None, :] + 1.0) * log_gamma[:, None])
    k_dec = jnp.exp((CHUNK - 1.0 - idx)[None, :] * log_gamma[:, None])
    c_dec = jnp.exp(CHUNK * log_gamma)
    q_dec = jnp.broadcast_to(q_dec[:, :, None], (h, CHUNK, d))
    k_dec = jnp.broadcast_to(k_dec[:, :, None], (h, CHUNK, d))
    c_dec = jnp.broadcast_to(c_dec[:, None, None], (h, SUBLANES, d))
    return cos_t, sin_t, decay, q_dec, k_dec, c_dec


def _retention(proj, ret_gn_w):
    b, s, _ = proj.shape
    d = RET_HEAD_DIM
    cos_t, sin_t, decay, q_dec, k_dec, c_dec = _retention_tables(s)
    col = lambda off: pl.BlockSpec((None, s, d), lambda bi, hi: (bi, 0, off + hi))
    per_head = lambda r: pl.BlockSpec((None, r, d), lambda bi, hi: (hi, 0, 0))
    full = lambda shp: pl.BlockSpec(shp, lambda bi, hi: (0,) * len(shp))
    return pl.pallas_call(
        _retention_kernel,
        grid=(b, RET_HEADS),
        in_specs=[col(0), col(RET_HEADS), col(2 * RET_HEADS), col(3 * RET_HEADS),
                  full((s, d)), full((s, d)),
                  per_head(CHUNK), per_head(CHUNK), per_head(CHUNK), per_head(SUBLANES),
                  pl.BlockSpec((1, d), lambda bi, hi: (0, hi))],
        out_specs=pl.BlockSpec((None, s, d), lambda bi, hi: (bi, 0, hi)),
        out_shape=jax.ShapeDtypeStruct((b, s, RET_WIDTH), bf16),
        compiler_params=_cparams(("parallel", "parallel")),
        name="retention",
    )(proj, proj, proj, proj, cos_t, sin_t, decay, q_dec, k_dec, c_dec, ret_gn_w.reshape(1, RET_WIDTH))


def _gelu_tanh(x):
    return 0.5 * x * (1.0 + jnp.tanh(math.sqrt(2.0 / math.pi) * (x + 0.044715 * (x * x * x))))


def _lru_kernel(xr_ref, yg_ref, cw_ref, cb_ref, wa_ref, ba_ref, wx_ref, bx_ref, lam_ref, o_ref, a_ref, b_ref):
    s = xr_ref.shape[0]
    x = xr_ref[...]
    rows = lax.broadcasted_iota(jnp.int32, x.shape, 0)
    xc = cb_ref[...] + cw_ref[CONV_WIDTH - 1:CONV_WIDTH, :] * x
    for back in range(1, CONV_WIDTH):
        shifted = jnp.where(rows >= back, pltpu.roll(x, back, axis=0), 0.0)
        xc = xc + cw_ref[CONV_WIDTH - 1 - back:CONV_WIDTH - back, :] * shifted
    xcb = xc.astype(bf16)
    r = jax.nn.sigmoid(jnp.dot(xcb, wa_ref[...].astype(bf16), preferred_element_type=f32) + ba_ref[...])
    gi = jax.nn.sigmoid(jnp.dot(xcb, wx_ref[...].astype(bf16), preferred_element_type=f32) + bx_ref[...])
    lam = lam_ref[...]
    log_sig = jnp.minimum(lam, 0.0) - jnp.log1p(jnp.exp(-jnp.abs(lam)))
    log_a = LRU_C * r * log_sig
    a = jnp.exp(log_a)
    a_ref[...] = a
    b_ref[...] = jnp.sqrt(-jnp.tanh(log_a) * (a * a + 1.0)) * (gi * xc)

    row8 = lax.broadcasted_iota(jnp.int32, (SUBLANES, LANES), 0)

    def body(i, h_prev):
        sl = pl.ds(pl.multiple_of(i * SUBLANES, SUBLANES), SUBLANES)
        a8 = a_ref[sl, :]
        b8 = b_ref[sl, :]
        for sh in (1, 2, 4):
            a_sh = jnp.where(row8 >= sh, pltpu.roll(a8, sh, axis=0), 1.0)
            b_sh = jnp.where(row8 >= sh, pltpu.roll(b8, sh, axis=0), 0.0)
            b8 = a8 * b_sh + b8
            a8 = a8 * a_sh
        h8 = a8 * h_prev + b8
        o_ref[sl, :] = (_gelu_tanh(yg_ref[sl, :]) * h8).astype(o_ref.dtype)
        return h8[SUBLANES - 1:SUBLANES, :]

    lax.fori_loop(0, s // SUBLANES, body, jnp.zeros((1, LANES), f32), unroll=8)


def _lru(proj, conv_w, conv_b, wa, ba, wx, bx, lam):
    b, s, _ = proj.shape
    d = LRU_BLOCK_DIM
    xr_off = 4 * RET_WIDTH // d
    yg_off = xr_off + LRU_BLOCKS
    col = lambda off: pl.BlockSpec((None, s, d), lambda bi, ji: (bi, 0, off + ji))
    vec = lambda r: pl.BlockSpec((r, d), lambda bi, ji: (0, ji))
    blk = lambda r: pl.BlockSpec((None, r, d), lambda bi, ji: (ji, 0, 0))
    return pl.pallas_call(
        _lru_kernel,
        grid=(b, LRU_BLOCKS),
        in_specs=[col(xr_off), col(yg_off), vec(CONV_WIDTH), vec(1), blk(d), blk(1), blk(d), blk(1), vec(1)],
        out_specs=pl.BlockSpec((None, s, d), lambda bi, ji: (bi, 0, ji)),
        out_shape=jax.ShapeDtypeStruct((b, s, LRU_WIDTH), bf16),
        scratch_shapes=[pltpu.VMEM((s, d), f32), pltpu.VMEM((s, d), f32)],
        compiler_params=_cparams(("parallel", "parallel")),
        name="rg_lru",
    )(proj, proj, conv_w, conv_b.reshape(1, LRU_WIDTH), wa, ba.reshape(LRU_BLOCKS, 1, d), wx,
      bx.reshape(LRU_BLOCKS, 1, d), lam.reshape(1, LRU_WIDTH))


def _layer_norm(y, w, b):
    mu = jnp.mean(y, axis=-1, keepdims=True)
    cen = y - mu
    var = jnp.mean(cen * cen, axis=-1, keepdims=True)
    return cen * lax.rsqrt(var + LN_EPS) * w + b


SLAB = D_MODEL // LANES
PITCH = SLAB + 4


def _store_slabs(ref, val, zero_pad):
    n = val.shape[0]
    for j in range(SLAB):
        ref[pl.ds(j, n, stride=PITCH), :] = val[:, j * LANES:(j + 1) * LANES]
    if zero_pad:
        for j in range(SLAB, PITCH):
            ref[pl.ds(j, n, stride=PITCH), :] = jnp.zeros((n, LANES), val.dtype)


def _slab_cols(ref, j, n):
    return ref[pl.ds(j, n, stride=PITCH), :]


def _split_bf16(v):
    hi = v.astype(bf16)
    lo = (v - hi.astype(f32)).astype(bf16)
    return hi, lo


def _out_router_kernel(ret_ref, lru_ref, wo_ref, x_ref, lnw_ref, lnb_ref, wr_ref, br_ref,
                       x1_ref, x1s_ref, tope_ref, gate_ref):
    m = jnp.dot(ret_ref[...], wo_ref[0:RET_WIDTH, :], preferred_element_type=f32)
    m = m + jnp.dot(lru_ref[...], wo_ref[RET_WIDTH:D_MODEL, :], preferred_element_type=f32)
    x1 = _layer_norm(DN_ALPHA * x_ref[...] + m, lnw_ref[...], lnb_ref[...])
    x1_ref[...] = x1
    _store_slabs(x1s_ref, x1, zero_pad=True)
    xh, xl = _split_bf16(x1)
    wh, wl = _split_bf16(wr_ref[...])
    logits = (jnp.dot(xh, wh, preferred_element_type=f32) + jnp.dot(xl, wh, preferred_element_type=f32)
              + jnp.dot(xh, wl, preferred_element_type=f32)) + br_ref[...]
    tm = logits.shape[0]
    lane = lax.broadcasted_iota(jnp.int32, logits.shape, 1)
    lane_k = lax.broadcasted_iota(jnp.int32, (tm, TOP_K), 1)
    top_e = jnp.zeros((tm, TOP_K), jnp.int32)
    top_v = jnp.zeros((tm, TOP_K), f32)
    cur = logits
    for kk in range(TOP_K):
        mx = jnp.max(cur, axis=-1, keepdims=True)
        idx = jnp.min(jnp.where(cur == mx, lane, N_EXPERTS), axis=-1, keepdims=True)
        top_e = jnp.where(lane_k == kk, idx, top_e)
        top_v = jnp.where(lane_k == kk, mx, top_v)
        cur = jnp.where(lane == idx, -jnp.inf, cur)
    ex = jnp.exp(top_v - top_v[:, 0:1])
    gate_ref[...] = ex / jnp.sum(ex, axis=-1, keepdims=True)
    tope_ref[...] = top_e


def _out_router(ret_out, lru_out, wo_bf, x2d, ln_w, ln_b, w_router, b_router):
    t, d = x2d.shape
    tm = 256
    row = lambda c: pl.BlockSpec((tm, c), lambda i: (i, 0))
    full = lambda r, c: pl.BlockSpec((r, c), lambda i: (0, 0))
    return pl.pallas_call(
        _out_router_kernel,
        grid=(t // tm,),
        in_specs=[row(RET_WIDTH), row(LRU_WIDTH), full(d, d), row(d), full(1, d), full(1, d),
                  full(d, N_EXPERTS), full(1, N_EXPERTS)],
        out_specs=[row(d), pl.BlockSpec((tm * PITCH, LANES), lambda i: (i, 0)), row(TOP_K), row(TOP_K)],
        out_shape=[jax.ShapeDtypeStruct((t, d), f32), jax.ShapeDtypeStruct((t * PITCH, LANES), f32),
                   jax.ShapeDtypeStruct((t, TOP_K), jnp.int32),
                   jax.ShapeDtypeStruct((t, TOP_K), f32)],
        compiler_params=_cparams(("parallel",)),
        name="out_proj_ln1_router",
    )(ret_out, lru_out, wo_bf, x2d, ln_w.reshape(1, d), ln_b.reshape(1, d), w_router,
      b_router.reshape(1, N_EXPERTS))


def _routing_tables(top_e, t):
    n_pad = t * TOP_K + N_EXPERTS * ROW_CHUNK
    max_units = N_EXPERTS + (t * TOP_K) // UNIT_ROWS
    sel = (top_e[:, :, None] == jnp.arange(N_EXPERTS, dtype=jnp.int32)[None, None, :]).any(axis=1)
    sel = sel.astype(jnp.int32)
    counts = jnp.sum(sel, axis=0)
    rank = jnp.cumsum(sel, axis=0) - sel
    padded = (counts + ROW_CHUNK - 1) // ROW_CHUNK * ROW_CHUNK
    pad_ends = jnp.cumsum(padded)
    pad_starts = pad_ends - padded
    dest_dense = pad_starts[None, :] + rank
    dest = jnp.take_along_axis(dest_dense, top_e, axis=1)
    tok = jnp.broadcast_to(jnp.arange(t, dtype=jnp.int32)[:, None], (t, TOP_K))
    slot_tok = jnp.zeros((n_pad + UNIT_ROWS,), jnp.int32).at[dest.reshape(-1)].set(tok.reshape(-1) * PITCH)
    units_per_e = (padded + UNIT_ROWS - 1) // UNIT_ROWS
    unit_ends = jnp.cumsum(units_per_e)
    unit_starts = unit_ends - units_per_e
    n_units = unit_ends[-1]
    u = jnp.arange(max_units, dtype=jnp.int32)
    u_clamped = jnp.minimum(u, n_units - 1)
    ue = jnp.searchsorted(unit_ends, u_clamped, side='right').astype(jnp.int32)
    ue = jnp.minimum(ue, N_EXPERTS - 1)
    j = u_clamped - unit_starts[ue]
    u_row = pad_starts[ue] + j * UNIT_ROWS
    u_rows = jnp.minimum(UNIT_ROWS, padded[ue] - j * UNIT_ROWS)
    u_chunks = jnp.where(u < n_units, u_rows // ROW_CHUNK, 0).astype(jnp.int32)
    n_used_rows = pad_ends[-1].astype(jnp.int32)
    return dest.astype(jnp.int32), slot_tok, ue, u_row.astype(jnp.int32), u_chunks, n_used_rows, n_pad, max_units


CHUNK_PITCHED = ROW_CHUNK * PITCH


def _moe_kernel(ue_ref, urow_ref, uchunks_ref, used_ref, tok_hbm, x1s_hbm, wg_ref, bg_ref, wu_ref, bu_ref, wd_ref,
                bd_ref, y_hbm, xbuf, yacc, gstage, ostage, tok_smem, flags, tok_sem, in_sem, out_sem):
    u = pl.program_id(0)
    f = pl.program_id(1)
    n_u = pl.num_programs(0)
    n_f = pl.num_programs(1)
    n_chunks = uchunks_ref[u]
    row0 = urow_ref[u]
    cur = u % 2
    nxt = 1 - cur
    u_next = jnp.minimum(u + 1, n_u - 1)
    n_next = jnp.where(u + 1 < n_u, uchunks_ref[u_next], 0)
    first_step = jnp.logical_and(u == 0, f == 0)
    PENDING, PEND_CHUNK, PEND_SLOT, OUT_BUSY = 0, 1, 2, 3

    def rows(c, size=ROW_CHUNK):
        return pl.ds(pl.multiple_of(c * size, size), size)

    def tok_copy(unit_row, slot):
        src = tok_hbm.at[pl.ds(pl.multiple_of(unit_row, ROW_CHUNK), UNIT_ROWS)]
        dst = tok_smem.at[pl.ds(pl.multiple_of(slot * UNIT_ROWS, UNIT_ROWS), UNIT_ROWS)]
        return pltpu.make_async_copy(src, dst, tok_sem)

    ISSUE_GROUP = 8

    def gather_start(c, tslot, lo=0, hi=ROW_CHUNK):
        base = tslot * UNIT_ROWS + c * ROW_CHUNK

        def issue(g, carry):
            for i in range(ISSUE_GROUP):
                r = g * ISSUE_GROUP + i
                src = x1s_hbm.at[pl.ds(tok_smem[base + r], SLAB), :]
                dst = gstage.at[pl.ds(r * PITCH, SLAB), :]
                pltpu.make_async_copy(src, dst, in_sem).start()
            return carry

        lax.fori_loop(lo // ISSUE_GROUP, hi // ISSUE_GROUP, issue, 0)

    def gather_finish(c, xslot):
        n = ROW_CHUNK * SLAB
        pltpu.make_async_copy(x1s_hbm.at[pl.ds(0, n), :], gstage.at[pl.ds(0, n), :], in_sem).wait()
        for j in range(SLAB):
            xbuf[xslot, rows(c), j * LANES:(j + 1) * LANES] = _slab_cols(gstage, j, ROW_CHUNK).astype(bf16)

    def out_copy(c, slot):
        dst = y_hbm.at[pl.ds(pl.multiple_of((row0 + c * ROW_CHUNK) * PITCH, CHUNK_PITCHED), CHUNK_PITCHED), :]
        return pltpu.make_async_copy(ostage.at[slot], dst, out_sem.at[slot])

    def out_wait(slot):
        @pl.when(flags[OUT_BUSY + slot] == 1)
        def _():
            pltpu.make_async_copy(ostage.at[slot], y_hbm.at[pl.ds(0, CHUNK_PITCHED), :], out_sem.at[slot]).wait()
            flags[OUT_BUSY + slot] = 0

    @pl.when(first_step)
    def _():
        for i in range(OUT_BUSY + 2):
            flags[i] = 0

        @pl.when(n_chunks > 0)
        def _():
            first_table = tok_copy(row0, cur)
            first_table.start()
            first_table.wait()

            def load(c, carry):
                gather_start(c, cur)
                gather_finish(c, cur)
                return carry

            lax.fori_loop(0, n_chunks, load, 0)

    @pl.when(flags[PENDING] == 1)
    def _():
        gather_finish(flags[PEND_CHUNK], flags[PEND_SLOT])
        flags[PENDING] = 0

    @pl.when(jnp.logical_and(f == 0, n_next > 0))
    def _():
        next_table = tok_copy(urow_ref[u_next], nxt)
        next_table.start()
        next_table.wait()

    prefetching = f < n_next
    n_blocks = n_chunks // 2 + n_chunks % 2
    share = (ROW_CHUNK // jnp.maximum(n_blocks, 1) + ISSUE_GROUP - 1) // ISSUE_GROUP * ISSUE_GROUP

    def prefetch_share(i):
        lo = jnp.minimum(i * share, ROW_CHUNK)
        hi = jnp.where(i == n_blocks - 1, ROW_CHUNK, jnp.minimum(lo + share, ROW_CHUNK))
        gather_start(f, nxt, lo, jnp.where(prefetching, hi, lo))

    @pl.when(prefetching)
    def _():
        flags[PENDING] = 1
        flags[PEND_CHUNK] = f
        flags[PEND_SLOT] = nxt

    @pl.when(n_chunks > 0)
    def _():
        bg = bg_ref[...]
        bu = bu_ref[...]
        mm = lambda a, w: lax.dot_general(a, w, (((1,), (0,)), ((), ())), preferred_element_type=f32)

        def mlp(sl, first):
            xc = xbuf[cur, sl, :]
            gt = jnp.minimum(mm(xc, wg_ref[...]) + bg, SWIGLU_LIMIT)
            up = jnp.clip(mm(xc, wu_ref[...]) + bu, -SWIGLU_LIMIT, SWIGLU_LIMIT)
            hid = (up + 1.0) * gt * jax.nn.sigmoid(SWIGLU_ALPHA * gt)
            part = mm(hid.astype(bf16), wd_ref[...])
            if first:
                yacc[sl, :] = part
            else:
                yacc[sl, :] += part

        def blocks(first):
            def pair(c, carry):
                prefetch_share(c)
                mlp(rows(c, 2 * ROW_CHUNK), first)
                return carry

            lax.fori_loop(0, n_chunks // 2, pair, 0)

            @pl.when(n_chunks % 2 == 1)
            def _():
                prefetch_share(n_chunks // 2)
                mlp(rows(n_chunks - 1), first)

        @pl.when(f == 0)
        def _():
            blocks(True)

        @pl.when(f > 0)
        def _():
            blocks(False)

    @pl.when(jnp.logical_and(f == n_f - 1, n_chunks > 0))
    def _():
        bd = bd_ref[...]

        def store(c, carry):
            slot = c % 2
            out_wait(slot)
            _store_slabs(ostage.at[slot], yacc[rows(c), :] + bd, zero_pad=True)
            out_copy(c, slot).start()
            flags[OUT_BUSY + slot] = 1
            return carry

        lax.fori_loop(0, n_chunks, store, 0)

    @pl.when(jnp.logical_and(u == n_u - 1, f == n_f - 1))
    def _():
        out_wait(0)
        out_wait(1)
        first = used_ref[0] // ROW_CHUNK
        last = y_hbm.shape[0] // CHUNK_PITCHED
        ostage[0] = jnp.zeros((CHUNK_PITCHED, LANES), f32)

        def tail_copy(c):
            dst = y_hbm.at[pl.ds(pl.multiple_of(c * CHUNK_PITCHED, CHUNK_PITCHED), CHUNK_PITCHED), :]
            return pltpu.make_async_copy(ostage.at[0], dst, out_sem.at[0])

        def start(c, carry):
            tail_copy(c).start()
            return carry

        def wait(c, carry):
            tail_copy(c).wait()
            return carry

        lax.fori_loop(first, last, start, 0)
        lax.fori_loop(first, last, wait, 0)


def _moe_experts(x1s, slot_tok, ue, u_row, u_chunks, n_used_rows, n_pad, max_units, w_gate, b_gate, w_up, b_up,
                 w_down, b_down):
    e, d, dff = w_gate.shape
    n_f = dff // F_TILE
    assert n_f >= UNIT_ROWS // ROW_CHUNK, "one chunk of the next unit is gathered per f-tile step"
    f_idx = lambda u, f, uc: jnp.where(uc[u] > 0, f, n_f - 1)
    col_w = pl.BlockSpec((None, d, F_TILE), lambda u, f, ue, ur, uc, used: (ue[u], 0, f_idx(u, f, uc)))
    col_b = pl.BlockSpec((None, 1, F_TILE), lambda u, f, ue, ur, uc, used: (ue[u], 0, f_idx(u, f, uc)))
    return pl.pallas_call(
        _moe_kernel,
        grid_spec=pltpu.PrefetchScalarGridSpec(
            num_scalar_prefetch=4,
            grid=(max_units, n_f),
            in_specs=[pl.BlockSpec(memory_space=pl.ANY), pl.BlockSpec(memory_space=pl.ANY),
                      col_w, col_b, col_w, col_b,
                      pl.BlockSpec((None, F_TILE, d), lambda u, f, ue, ur, uc, used: (ue[u], f_idx(u, f, uc), 0)),
                      pl.BlockSpec((None, 1, d), lambda u, f, ue, ur, uc, used: (ue[u], 0, 0))],
            out_specs=pl.BlockSpec(memory_space=pl.ANY),
            scratch_shapes=[pltpu.VMEM((2, UNIT_ROWS, d), bf16),
                            pltpu.VMEM((UNIT_ROWS, d), f32),
                            pltpu.VMEM((CHUNK_PITCHED, LANES), f32),
                            pltpu.VMEM((2, CHUNK_PITCHED, LANES), f32),
                            pltpu.SMEM((2 * UNIT_ROWS,), jnp.int32),
                            pltpu.SMEM((8,), jnp.int32),
                            pltpu.SemaphoreType.DMA(()),
                            pltpu.SemaphoreType.DMA(()),
                            pltpu.SemaphoreType.DMA((2,))],
        ),
        out_shape=jax.ShapeDtypeStruct((n_pad * PITCH, LANES), f32),
        compiler_params=_cparams(("arbitrary", "arbitrary")),
        name="moe_experts",
    )(ue, u_row, u_chunks, n_used_rows.reshape(1), slot_tok, x1s, w_gate, b_gate.reshape(e, 1, dff), w_up,
      b_up.reshape(e, 1, dff), w_down, b_down.reshape(e, 1, d))


COMBINE_ROWS = 256


def _combine_kernel(dest_ref, y_hbm, gate_ref, x1_ref, p_ref, lnw_ref, lnb_ref, wp_ref, pnw_ref, wg_ref,
                    o_ref, ybuf, fsum_ref, sem):
    tm = COMBINE_ROWS

    def issue(t, c):
        for k in range(TOP_K):
            src = y_hbm.at[pl.ds(dest_ref[t * TOP_K + k] * PITCH, SLAB), :]
            dst = ybuf.at[k, pl.ds(t * PITCH, SLAB), :]
            pltpu.make_async_copy(src, dst, sem).start()
        return c

    lax.fori_loop(0, tm, issue, 0, unroll=2)
    e = jnp.dot(p_ref[...].astype(bf16), wp_ref[...], preferred_element_type=f32)
    e = e * lax.rsqrt(jnp.mean(e * e, axis=-1, keepdims=True) + LN_EPS) * pnw_ref[...]
    for k in range(TOP_K):
        pltpu.make_async_copy(y_hbm.at[pl.ds(0, tm * SLAB), :], ybuf.at[k, pl.ds(0, tm * SLAB), :], sem).wait()
    gates = gate_ref[...]
    for j in range(SLAB):
        acc = gates[:, 0:1] * _slab_cols(ybuf.at[0], j, tm)
        for k in range(1, TOP_K):
            acc = acc + gates[:, k:k + 1] * _slab_cols(ybuf.at[k], j, tm)
        fsum_ref[:, j * LANES:(j + 1) * LANES] = acc
    x2 = _layer_norm(DN_ALPHA * x1_ref[...] + fsum_ref[...], lnw_ref[...], lnb_ref[...])
    gate = jax.nn.sigmoid(jnp.dot(x2.astype(bf16), wg_ref[...], preferred_element_type=f32))
    o_ref[...] = x2 + gate * e


def _combine(y, dest, gates, x1, p2d, ln_w, ln_b, wp_bf, ple_norm_w, wg_bf):
    t, d = x1.shape
    tm = COMBINE_ROWS
    row = lambda c: pl.BlockSpec((tm, c), lambda i: (i, 0))
    full = lambda r, c: pl.BlockSpec((r, c), lambda i: (0, 0))
    return pl.pallas_call(
        _combine_kernel,
        grid=(t // tm,),
        in_specs=[pl.BlockSpec((tm * TOP_K,), lambda i: (i,), memory_space=pltpu.SMEM),
                  pl.BlockSpec(memory_space=pl.ANY),
                  row(TOP_K), row(d), row(PLE_DIM), full(1, d), full(1, d), full(PLE_DIM, d), full(1, d),
                  full(d, d)],
        out_specs=row(d),
        out_shape=jax.ShapeDtypeStruct((t, d), f32),
        scratch_shapes=[pltpu.VMEM((TOP_K, tm * PITCH, LANES), f32), pltpu.VMEM((tm, d), f32),
                        pltpu.SemaphoreType.DMA(())],
        compiler_params=_cparams(("arbitrary",)),
        name="combine_ln2_ple",
    )(dest.reshape(-1), y, gates, x1, p2d, ln_w.reshape(1, d), ln_b.reshape(1, d), wp_bf,
      ple_norm_w.reshape(1, d), wg_bf)


def _layer(h, p_i, w_in, ret_gn_w, conv_w, conv_b, lru_wa, lru_ba, lru_wx, lru_bx, lru_lam, w_out,
           ln1_w, ln1_b, w_router, b_router, w_gate, b_gate, w_up, b_up, w_down, b_down,
           ln2_w, ln2_b, w_ple_proj, ple_norm_w, w_ple_gate):
    b, s, d = h.shape
    t = b * s
    x2d = h.reshape(t, d)
    proj = _in_proj(x2d, w_in.astype(bf16)).reshape(b, s, IN_COLS)
    ret_out = _retention(proj, ret_gn_w)
    lru_out = _lru(proj, conv_w, conv_b, lru_wa, lru_ba, lru_wx, lru_bx, lru_lam)
    x1, x1s, top_e, gates = _out_router(ret_out.reshape(t, RET_WIDTH), lru_out.reshape(t, LRU_WIDTH),
                                        w_out.astype(bf16), x2d, ln1_w, ln1_b, w_router, b_router)
    dest, slot_tok, ue, u_row, u_chunks, n_used_rows, n_pad, max_units = _routing_tables(top_e, t)
    y = _moe_experts(x1s, slot_tok, ue, u_row, u_chunks, n_used_rows, n_pad, max_units, w_gate, b_gate, w_up, b_up,
                     w_down, b_down)
    out = _combine(y, dest, gates, x1, p_i.reshape(t, PLE_DIM), ln2_w, ln2_b, w_ple_proj.astype(bf16),
                   ple_norm_w, w_ple_gate.astype(bf16))
    return out.reshape(b, s, d)


def kernel(x, p, w_in, ret_gn_w, conv_w, conv_b, lru_wa, lru_ba, lru_wx, lru_bx, lru_lam, w_out, ln1_w, ln1_b,
           w_router, b_router, w_gate, b_gate, w_up, b_up, w_down, b_down, ln2_w, ln2_b, w_ple_proj, ple_norm_w,
           w_ple_gate):
    h = x.astype(f32)
    for i in range(w_in.shape[0]):
        h = _layer(h, p[i], w_in[i], ret_gn_w[i], conv_w[i], conv_b[i], lru_wa[i], lru_ba[i], lru_wx[i],
                   lru_bx[i], lru_lam[i], w_out[i], ln1_w[i], ln1_b[i], w_router[i], b_router[i], w_gate[i],
                   b_gate[i], w_up[i], b_up[i], w_down[i], b_down[i], ln2_w[i], ln2_b[i], w_ple_proj[i],
                   ple_norm_w[i], w_ple_gate[i])
    return h.astype(x.dtype)
```

```python
import functools
import math

import jax
import jax.numpy as jnp
from jax import lax
from jax.experimental import pallas as pl
from jax.experimental.pallas import tpu as pltpu

D_MODEL = 2048
RET_HEAD_DIM = 128
RET_HEADS = 8
RET_WIDTH = RET_HEADS * RET_HEAD_DIM
LRU_WIDTH = D_MODEL - RET_WIDTH
LRU_BLOCKS = 8
LRU_BLOCK_DIM = LRU_WIDTH // LRU_BLOCKS
IN_COLS = 4 * RET_WIDTH + 2 * LRU_WIDTH
CONV_WIDTH = 4
LRU_C = 8.0
CHUNK = 128
ROPE_BASE = 10000.0
N_EXPERTS = 32
TOP_K = 4
SWIGLU_LIMIT = 7.0
SWIGLU_ALPHA = 1.702
PLE_DIM = 256
LN_EPS = 1e-5
DEPTH = 1
DN_ALPHA = (2.0 * DEPTH) ** 0.25

LANES = 128
SUBLANES = 8
VMEM_LIMIT = 60 * 1024 * 1024

ROW_CHUNK = 256
UNIT_ROWS = 2048
F_TILE = 256

f32 = jnp.float32
bf16 = jnp.bfloat16


def _cparams(sem):
    return pltpu.CompilerParams(dimension_semantics=sem, vmem_limit_bytes=VMEM_LIMIT)


def _in_proj_kernel(x_ref, w_ref, o_ref, xb_ref):
    @pl.when(pl.program_id(1) == 0)
    def _():
        xb_ref[...] = x_ref[...].astype(bf16)

    o_ref[...] = lax.dot_general(xb_ref[...], w_ref[...], (((1,), (0,)), ((), ())), preferred_element_type=f32)


def _in_proj(x2d, w_bf):
    t, d = x2d.shape
    n = w_bf.shape[1]
    tm, tn = 1024, 1024
    return pl.pallas_call(
        _in_proj_kernel,
        grid=(t // tm, n // tn),
        in_specs=[pl.BlockSpec((tm, d), lambda i, j: (i, 0)),
                  pl.BlockSpec((d, tn), lambda i, j: (0, j))],
        out_specs=pl.BlockSpec((tm, tn), lambda i, j: (i, j)),
        out_shape=jax.ShapeDtypeStruct((t, n), f32),
        scratch_shapes=[pltpu.VMEM((tm, d), bf16)],
        compiler_params=_cparams(("parallel", "arbitrary")),
        name="in_proj",
    )(x2d, w_bf)


def _retention_kernel(q_ref, k_ref, v_ref, g_ref, cos_ref, sin_ref, dec_ref, qd_ref, kd_ref, cd_ref, gnw_ref,
                      o_ref):
    s = q_ref.shape[0]
    n_chunks = s // CHUNK
    decay = dec_ref[...]
    q_dec = qd_ref[...]
    k_dec = kd_ref[...]
    c_dec = cd_ref[0:1, :]
    gnw = gnw_ref[...]
    k_scale = RET_HEAD_DIM ** -0.5

    def rope(xv, cos, sin):
        return xv * cos + pltpu.roll(xv, RET_HEAD_DIM // 2, axis=1) * sin

    def body(n, state):
        sl = pl.ds(pl.multiple_of(n * CHUNK, CHUNK), CHUNK)
        cos = cos_ref[sl, :]
        sin = sin_ref[sl, :]
        q = rope(q_ref[sl, :], cos, sin)
        k = rope(k_ref[sl, :], cos, sin) * k_scale
        vb = v_ref[sl, :].astype(bf16)
        scores = lax.dot_general(q.astype(bf16), k.astype(bf16), (((1,), (1,)), ((), ())),
                                 preferred_element_type=f32) * decay
        intra = jnp.dot(scores.astype(bf16), vb, preferred_element_type=f32)
        cross = jnp.dot((q * q_dec).astype(bf16), state.astype(bf16), preferred_element_type=f32)
        kv = lax.dot_general((k * k_dec).astype(bf16), vb, (((0,), (0,)), ((), ())),
                             preferred_element_type=f32)
        ret = intra + cross
        mu = jnp.mean(ret, axis=-1, keepdims=True)
        cen = ret - mu
        var = jnp.mean(cen * cen, axis=-1, keepdims=True)
        ret = cen * lax.rsqrt(var + LN_EPS) * gnw
        g = g_ref[sl, :]
        o_ref[sl, :] = (g * jax.nn.sigmoid(g) * ret).astype(o_ref.dtype)
        return c_dec * state + kv

    lax.fori_loop(0, n_chunks, body, jnp.zeros((RET_HEAD_DIM, RET_HEAD_DIM), f32), unroll=4)


def _retention_tables(s):
    h, d = RET_HEADS, RET_HEAD_DIM
    inv = ROPE_BASE ** (-jnp.arange(0, d, 2, dtype=f32) / d)
    ang = jnp.arange(s, dtype=f32)[:, None] * inv[None, :]
    cos = jnp.cos(ang)
    sin = jnp.sin(ang)
    cos_t = jnp.concatenate([cos, cos], axis=-1)
    sin_t = jnp.concatenate([-sin, sin], axis=-1)
    log_gamma = jnp.log1p(-jnp.exp2(-5.0 - jnp.arange(h, dtype=f32)))
    idx = jnp.arange(CHUNK, dtype=f32)
    diff = idx[:, None] - idx[None, :]
    decay = jnp.where((diff >= 0)[None], jnp.exp(jnp.maximum(diff, 0.0)[None] * log_gamma[:, None, None]), 0.0)
    q_dec = jnp.exp((idx[None, :] + 1.0) * log_gamma[:, None])
    k_dec = jnp.exp((CHUNK - 1.0 - idx)[None, :] * log_gamma[:, None])
    c_dec = jnp.exp(CHUNK * log_gamma)
    q_dec = jnp.broadcast_to(q_dec[:, :, None], (h, CHUNK, d))
    k_dec = jnp.broadcast_to(k_dec[:, :, None], (h, CHUNK, d))
    c_dec = jnp.broadcast_to(c_dec[:, None, None], (h, SUBLANES, d))
    return cos_t, sin_t, decay, q_dec, k_dec, c_dec


def _retention(proj, ret_gn_w):
    b, s, _ = proj.shape
    d = RET_HEAD_DIM
    cos_t, sin_t, decay, q_dec, k_dec, c_dec = _retention_tables(s)
    col = lambda off: pl.BlockSpec((None, s, d), lambda bi, hi: (bi, 0, off + hi))
    per_head = lambda r: pl.BlockSpec((None, r, d), lambda bi, hi: (hi, 0, 0))
    full = lambda shp: pl.BlockSpec(shp, lambda bi, hi: (0,) * len(shp))
    return pl.pallas_call(
        _retention_kernel,
        grid=(b, RET_HEADS),
        in_specs=[col(0), col(RET_HEADS), col(2 * RET_HEADS), col(3 * RET_HEADS),
                  full((s, d)), full((s, d)),
                  per_head(CHUNK), per_head(CHUNK), per_head(CHUNK), per_head(SUBLANES),
                  pl.BlockSpec((1, d), lambda bi, hi: (0, hi))],
        out_specs=pl.BlockSpec((None, s, d), lambda bi, hi: (bi, 0, hi)),
        out_shape=jax.ShapeDtypeStruct((b, s, RET_WIDTH), bf16),
        compiler_params=_cparams(("parallel", "parallel")),
        name="retention",
    )(proj, proj, proj, proj, cos_t, sin_t, decay, q_dec, k_dec, c_dec, ret_gn_w.reshape(1, RET_WIDTH))


def _gelu_tanh(x):
    return 0.5 * x * (1.0 + jnp.tanh(math.sqrt(2.0 / math.pi) * (x + 0.044715 * (x * x * x))))


def _lru_kernel(xr_ref, yg_ref, cw_ref, cb_ref, wa_ref, ba_ref, wx_ref, bx_ref, lam_ref, o_ref, a_ref, b_ref):
    s = xr_ref.shape[0]
    x = xr_ref[...]
    rows = lax.broadcasted_iota(jnp.int32, x.shape, 0)
    xc = cb_ref[...] + cw_ref[CONV_WIDTH - 1:CONV_WIDTH, :] * x
    for back in range(1, CONV_WIDTH):
        shifted = jnp.where(rows >= back, pltpu.roll(x, back, axis=0), 0.0)
        xc = xc + cw_ref[CONV_WIDTH - 1 - back:CONV_WIDTH - back, :] * shifted
    xcb = xc.astype(bf16)
    r = jax.nn.sigmoid(jnp.dot(xcb, wa_ref[...].astype(bf16), preferred_element_type=f32) + ba_ref[...])
    gi = jax.nn.sigmoid(jnp.dot(xcb, wx_ref[...].astype(bf16), preferred_element_type=f32) + bx_ref[...])
    lam = lam_ref[...]
    log_sig = jnp.minimum(lam, 0.0) - jnp.log1p(jnp.exp(-jnp.abs(lam)))
    log_a = LRU_C * r * log_sig
    a = jnp.exp(log_a)
    a_ref[...] = a
    b_ref[...] = jnp.sqrt(-jnp.tanh(log_a) * (a * a + 1.0)) * (gi * xc)

    row8 = lax.broadcasted_iota(jnp.int32, (SUBLANES, LANES), 0)

    def body(i, h_prev):
        sl = pl.ds(pl.multiple_of(i * SUBLANES, SUBLANES), SUBLANES)
        a8 = a_ref[sl, :]
        b8 = b_ref[sl, :]
        for sh in (1, 2, 4):
            a_sh = jnp.where(row8 >= sh, pltpu.roll(a8, sh, axis=0), 1.0)
            b_sh = jnp.where(row8 >= sh, pltpu.roll(b8, sh, axis=0), 0.0)
            b8 = a8 * b_sh + b8
            a8 = a8 * a_sh
        h8 = a8 * h_prev + b8
        o_ref[sl, :] = (_gelu_tanh(yg_ref[sl, :]) * h8).astype(o_ref.dtype)
        return h8[SUBLANES - 1:SUBLANES, :]

    lax.fori_loop(0, s // SUBLANES, body, jnp.zeros((1, LANES), f32), unroll=8)


def _lru(proj, conv_w, conv_b, wa, ba, wx, bx, lam):
    b, s, _ = proj.shape
    d = LRU_BLOCK_DIM
    xr_off = 4 * RET_WIDTH // d
    yg_off = xr_off + LRU_BLOCKS
    col = lambda off: pl.BlockSpec((None, s, d), lambda bi, ji: (bi, 0, off + ji))
    vec = lambda r: pl.BlockSpec((r, d), lambda bi, ji: (0, ji))
    blk = lambda r: pl.BlockSpec((None, r, d), lambda bi, ji: (ji, 0, 0))
    return pl.pallas_call(
        _lru_kernel,
        grid=(b, LRU_BLOCKS),
        in_specs=[col(xr_off), col(yg_off), vec(CONV_WIDTH), vec(1), blk(d), blk(1), blk(d), blk(1), vec(1)],
        out_specs=pl.BlockSpec((None, s, d), lambda bi, ji: (bi, 0, ji)),
        out_shape=jax.ShapeDtypeStruct((b, s, LRU_WIDTH), bf16),
        scratch_shapes=[pltpu.VMEM((s, d), f32), pltpu.VMEM((s, d), f32)],
        compiler_params=_cparams(("parallel", "parallel")),
        name="rg_lru",
    )(proj, proj, conv_w, conv_b.reshape(1, LRU_WIDTH), wa, ba.reshape(LRU_BLOCKS, 1, d), wx,
      bx.reshape(LRU_BLOCKS, 1, d), lam.reshape(1, LRU_WIDTH))


def _layer_norm(y, w, b):
    mu = jnp.mean(y, axis=-1, keepdims=True)
    cen = y - mu
    var = jnp.mean(cen * cen, axis=-1, keepdims=True)
    return cen * lax.rsqrt(var + LN_EPS) * w + b


HALF = D_MODEL // 2
SLAB = HALF // LANES
PITCH = SLAB + 4
u32 = jnp.uint32
HIGH_MASK = 0xFFFF0000


def _bf16_bits(v):
    return lax.bitcast_convert_type(v.astype(bf16).astype(f32), u32)


def _store_slabs(ref, val):
    n = val.shape[0]
    for j in range(SLAB):
        lo = _bf16_bits(val[:, j * LANES:(j + 1) * LANES])
        hi = _bf16_bits(val[:, HALF + j * LANES:HALF + (j + 1) * LANES])
        ref[pl.ds(j, n, stride=PITCH), :] = hi | lax.shift_right_logical(lo, jnp.full_like(lo, 16))
    for j in range(SLAB, PITCH):
        ref[pl.ds(j, n, stride=PITCH), :] = jnp.zeros((n, LANES), u32)


def _slab_cols(ref, j, n):
    w = ref[pl.ds(j, n, stride=PITCH), :]
    lo = lax.bitcast_convert_type(lax.shift_left(w, jnp.full_like(w, 16)), f32)
    hi = lax.bitcast_convert_type(w & jnp.full_like(w, HIGH_MASK), f32)
    return lo, hi


def _split_bf16(v):
    hi = v.astype(bf16)
    lo = (v - hi.astype(f32)).astype(bf16)
    return hi, lo


def _out_router_kernel(ret_ref, lru_ref, wo_ref, x_ref, lnw_ref, lnb_ref, wr_ref, br_ref,
                       x1_ref, x1s_ref, tope_ref, gate_ref):
    m = jnp.dot(ret_ref[...], wo_ref[0:RET_WIDTH, :], preferred_element_type=f32)
    m = m + jnp.dot(lru_ref[...], wo_ref[RET_WIDTH:D_MODEL, :], preferred_element_type=f32)
    x1 = _layer_norm(DN_ALPHA * x_ref[...] + m, lnw_ref[...], lnb_ref[...])
    x1_ref[...] = x1
    _store_slabs(x1s_ref, x1)
    xh, xl = _split_bf16(x1)
    wh, wl = _split_bf16(wr_ref[...])
    logits = (jnp.dot(xh, wh, preferred_element_type=f32) + jnp.dot(xl, wh, preferred_element_type=f32)
              + jnp.dot(xh, wl, preferred_element_type=f32)) + br_ref[...]
    tm = logits.shape[0]
    lane = lax.broadcasted_iota(jnp.int32, logits.shape, 1)
    lane_k = lax.broadcasted_iota(jnp.int32, (tm, TOP_K), 1)
    top_e = jnp.zeros((tm, TOP_K), jnp.int32)
    top_v = jnp.zeros((tm, TOP_K), f32)
    cur = logits
    for kk in range(TOP_K):
        mx = jnp.max(cur, axis=-1, keepdims=True)
        idx = jnp.min(jnp.where(cur == mx, lane, N_EXPERTS), axis=-1, keepdims=True)
        top_e = jnp.where(lane_k == kk, idx, top_e)
        top_v = jnp.where(lane_k == kk, mx, top_v)
        cur = jnp.where(lane == idx, -jnp.inf, cur)
    ex = jnp.exp(top_v - top_v[:, 0:1])
    gate_ref[...] = ex / jnp.sum(ex, axis=-1, keepdims=True)
    tope_ref[...] = top_e


def _out_router(ret_out, lru_out, wo_bf, x2d, ln_w, ln_b, w_router, b_router):
    t, d = x2d.shape
    tm = 256
    row = lambda c: pl.BlockSpec((tm, c), lambda i: (i, 0))
    full = lambda r, c: pl.BlockSpec((r, c), lambda i: (0, 0))
    return pl.pallas_call(
        _out_router_kernel,
        grid=(t // tm,),
        in_specs=[row(RET_WIDTH), row(LRU_WIDTH), full(d, d), row(d), full(1, d), full(1, d),
                  full(d, N_EXPERTS), full(1, N_EXPERTS)],
        out_specs=[row(d), pl.BlockSpec((tm * PITCH, LANES), lambda i: (i, 0)), row(TOP_K), row(TOP_K)],
        out_shape=[jax.ShapeDtypeStruct((t, d), f32), jax.ShapeDtypeStruct((t * PITCH, LANES), u32),
                   jax.ShapeDtypeStruct((t, TOP_K), jnp.int32),
                   jax.ShapeDtypeStruct((t, TOP_K), f32)],
        compiler_params=_cparams(("parallel",)),
        name="out_proj_ln1_router",
    )(ret_out, lru_out, wo_bf, x2d, ln_w.reshape(1, d), ln_b.reshape(1, d), w_router,
      b_router.reshape(1, N_EXPERTS))


def _routing_tables(top_e, t):
    n_pad = t * TOP_K + N_EXPERTS * ROW_CHUNK
    max_units = N_EXPERTS + (t * TOP_K) // UNIT_ROWS
    sel = (top_e[:, :, None] == jnp.arange(N_EXPERTS, dtype=jnp.int32)[None, None, :]).any(axis=1)
    sel = sel.astype(jnp.int32)
    counts = jnp.sum(sel, axis=0)
    rank = jnp.cumsum(sel, axis=0) - sel
    padded = (counts + ROW_CHUNK - 1) // ROW_CHUNK * ROW_CHUNK
    pad_ends = jnp.cumsum(padded)
    pad_starts = pad_ends - padded
    dest_dense = pad_starts[None, :] + rank
    dest = jnp.take_along_axis(dest_dense, top_e, axis=1)
    tok = jnp.broadcast_to(jnp.arange(t, dtype=jnp.int32)[:, None], (t, TOP_K))
    slot_tok = jnp.zeros((n_pad + UNIT_ROWS,), jnp.int32).at[dest.reshape(-1)].set(tok.reshape(-1) * PITCH)
    units_per_e = (padded + UNIT_ROWS - 1) // UNIT_ROWS
    unit_ends = jnp.cumsum(units_per_e)
    unit_starts = unit_ends - units_per_e
    n_units = unit_ends[-1]
    u = jnp.arange(max_units, dtype=jnp.int32)
    u_clamped = jnp.minimum(u, n_units - 1)
    ue = jnp.searchsorted(unit_ends, u_clamped, side='right').astype(jnp.int32)
    ue = jnp.minimum(ue, N_EXPERTS - 1)
    j = u_clamped - unit_starts[ue]
    u_row = pad_starts[ue] + j * UNIT_ROWS
    u_rows = jnp.minimum(UNIT_ROWS, padded[ue] - j * UNIT_ROWS)
    u_chunks = jnp.where(u < n_units, u_rows // ROW_CHUNK, 0).astype(jnp.int32)
    n_used_rows = pad_ends[-1].astype(jnp.int32)
    return dest.astype(jnp.int32), slot_tok, ue, u_row.astype(jnp.int32), u_chunks, n_used_rows, n_pad, max_units


CHUNK_PITCHED = ROW_CHUNK * PITCH


def _moe_kernel(ue_ref, urow_ref, uchunks_ref, used_ref, tok_hbm, x1s_hbm, wg_ref, bg_ref, wu_ref, bu_ref, wd_ref,
                bd_ref, y_hbm, xbuf, yacc, gstage, ostage, tok_smem, flags, tok_sem, in_sem, out_sem):
    u = pl.program_id(0)
    f = pl.program_id(1)
    n_u = pl.num_programs(0)
    n_f = pl.num_programs(1)
    n_chunks = uchunks_ref[u]
    row0 = urow_ref[u]
    cur = u % 2
    nxt = 1 - cur
    u_next = jnp.minimum(u + 1, n_u - 1)
    n_next = jnp.where(u + 1 < n_u, uchunks_ref[u_next], 0)
    first_step = jnp.logical_and(u == 0, f == 0)
    PENDING, PEND_CHUNK, PEND_SLOT, OUT_BUSY = 0, 1, 2, 3

    def rows(c, size=ROW_CHUNK):
        return pl.ds(pl.multiple_of(c * size, size), size)

    def tok_copy(unit_row, slot):
        src = tok_hbm.at[pl.ds(pl.multiple_of(unit_row, ROW_CHUNK), UNIT_ROWS)]
        dst = tok_smem.at[pl.ds(pl.multiple_of(slot * UNIT_ROWS, UNIT_ROWS), UNIT_ROWS)]
        return pltpu.make_async_copy(src, dst, tok_sem)

    ISSUE_GROUP = 8

    def gather_start(c, tslot, lo=0, hi=ROW_CHUNK):
        base = tslot * UNIT_ROWS + c * ROW_CHUNK

        def issue(g, carry):
            for i in range(ISSUE_GROUP):
                r = g * ISSUE_GROUP + i
                src = x1s_hbm.at[pl.ds(tok_smem[base + r], SLAB), :]
                dst = gstage.at[pl.ds(r * PITCH, SLAB), :]
                pltpu.make_async_copy(src, dst, in_sem).start()
            return carry

        lax.fori_loop(lo // ISSUE_GROUP, hi // ISSUE_GROUP, issue, 0)

    def gather_finish(c, xslot):
        n = ROW_CHUNK * SLAB
        pltpu.make_async_copy(x1s_hbm.at[pl.ds(0, n), :], gstage.at[pl.ds(0, n), :], in_sem).wait()
        for j in range(SLAB):
            lo, hi = _slab_cols(gstage, j, ROW_CHUNK)
            xbuf[xslot, rows(c), j * LANES:(j + 1) * LANES] = lo.astype(bf16)
            xbuf[xslot, rows(c), HALF + j * LANES:HALF + (j + 1) * LANES] = hi.astype(bf16)

    def out_copy(c, slot):
        dst = y_hbm.at[pl.ds(pl.multiple_of((row0 + c * ROW_CHUNK) * PITCH, CHUNK_PITCHED), CHUNK_PITCHED), :]
        return pltpu.make_async_copy(ostage.at[slot], dst, out_sem.at[slot])

    def out_wait(slot):
        @pl.when(flags[OUT_BUSY + slot] == 1)
        def _():
            pltpu.make_async_copy(ostage.at[slot], y_hbm.at[pl.ds(0, CHUNK_PITCHED), :], out_sem.at[slot]).wait()
            flags[OUT_BUSY + slot] = 0

    @pl.when(first_step)
    def _():
        for i in range(OUT_BUSY + 2):
            flags[i] = 0

        @pl.when(n_chunks > 0)
        def _():
            first_table = tok_copy(row0, cur)
            first_table.start()
            first_table.wait()

            def load(c, carry):
                gather_start(c, cur)
                gather_finish(c, cur)
                return carry

            lax.fori_loop(0, n_chunks, load, 0)

    @pl.when(flags[PENDING] == 1)
    def _():
        gather_finish(flags[PEND_CHUNK], flags[PEND_SLOT])
        flags[PENDING] = 0

    @pl.when(jnp.logical_and(f == 0, n_next > 0))
    def _():
        next_table = tok_copy(urow_ref[u_next], nxt)
        next_table.start()
        next_table.wait()

    prefetching = f < n_next
    n_blocks = n_chunks // 2 + n_chunks % 2
    share = (ROW_CHUNK // jnp.maximum(n_blocks, 1) + ISSUE_GROUP - 1) // ISSUE_GROUP * ISSUE_GROUP

    def prefetch_share(i):
        lo = jnp.minimum(i * share, ROW_CHUNK)
        hi = jnp.where(i == n_blocks - 1, ROW_CHUNK, jnp.minimum(lo + share, ROW_CHUNK))
        gather_start(f, nxt, lo, jnp.where(prefetching, hi, lo))

    @pl.when(prefetching)
    def _():
        flags[PENDING] = 1
        flags[PEND_CHUNK] = f
        flags[PEND_SLOT] = nxt

    @pl.when(n_chunks > 0)
    def _():
        bg = bg_ref[...]
        bu = bu_ref[...]
        mm = lambda a, w: lax.dot_general(a, w, (((1,), (0,)), ((), ())), preferred_element_type=f32)

        def mlp(sl, first):
            xc = xbuf[cur, sl, :]
            gt = jnp.minimum(mm(xc, wg_ref[...]) + bg, SWIGLU_LIMIT)
            up = jnp.clip(mm(xc, wu_ref[...]) + bu, -SWIGLU_LIMIT, SWIGLU_LIMIT)
            hid = (up + 1.0) * gt * jax.nn.sigmoid(SWIGLU_ALPHA * gt)
            part = mm(hid.astype(bf16), wd_ref[...])
            if first:
                yacc[sl, :] = part
            else:
                yacc[sl, :] += part

        def blocks(first):
            def pair(c, carry):
                prefetch_share(c)
                mlp(rows(c, 2 * ROW_CHUNK), first)
                return carry

            lax.fori_loop(0, n_chunks // 2, pair, 0)

            @pl.when(n_chunks % 2 == 1)
            def _():
                prefetch_share(n_chunks // 2)
                mlp(rows(n_chunks - 1), first)

        @pl.when(f == 0)
        def _():
            blocks(True)

        @pl.when(f > 0)
        def _():
            blocks(False)

    @pl.when(jnp.logical_and(f == n_f - 1, n_chunks > 0))
    def _():
        bd = bd_ref[...]

        def store(c, carry):
            slot = c % 2
            out_wait(slot)
            _store_slabs(ostage.at[slot], yacc[rows(c), :] + bd)
            out_copy(c, slot).start()
            flags[OUT_BUSY + slot] = 1
            return carry

        lax.fori_loop(0, n_chunks, store, 0)

    @pl.when(jnp.logical_and(u == n_u - 1, f == n_f - 1))
    def _():
        out_wait(0)
        out_wait(1)
        first = used_ref[0] // ROW_CHUNK
        last = y_hbm.shape[0] // CHUNK_PITCHED
        ostage[0] = jnp.zeros((CHUNK_PITCHED, LANES), u32)

        def tail_copy(c):
            dst = y_hbm.at[pl.ds(pl.multiple_of(c * CHUNK_PITCHED, CHUNK_PITCHED), CHUNK_PITCHED), :]
            return pltpu.make_async_copy(ostage.at[0], dst, out_sem.at[0])

        def start(c, carry):
            tail_copy(c).start()
            return carry

        def wait(c, carry):
            tail_copy(c).wait()
            return carry

        lax.fori_loop(first, last, start, 0)
        lax.fori_loop(first, last, wait, 0)


def _moe_experts(x1s, slot_tok, ue, u_row, u_chunks, n_used_rows, n_pad, max_units, w_gate, b_gate, w_up, b_up,
                 w_down, b_down):
    e, d, dff = w_gate.shape
    n_f = dff // F_TILE
    assert n_f >= UNIT_ROWS // ROW_CHUNK, "one chunk of the next unit is gathered per f-tile step"
    f_idx = lambda u, f, uc: jnp.where(uc[u] > 0, f, n_f - 1)
    col_w = pl.BlockSpec((None, d, F_TILE), lambda u, f, ue, ur, uc, used: (ue[u], 0, f_idx(u, f, uc)))
    col_b = pl.BlockSpec((None, 1, F_TILE), lambda u, f, ue, ur, uc, used: (ue[u], 0, f_idx(u, f, uc)))
    return pl.pallas_call(
        _moe_kernel,
        grid_spec=pltpu.PrefetchScalarGridSpec(
            num_scalar_prefetch=4,
            grid=(max_units, n_f),
            in_specs=[pl.BlockSpec(memory_space=pl.ANY), pl.BlockSpec(memory_space=pl.ANY),
                      col_w, col_b, col_w, col_b,
                      pl.BlockSpec((None, F_TILE, d), lambda u, f, ue, ur, uc, used: (ue[u], f_idx(u, f, uc), 0)),
                      pl.BlockSpec((None, 1, d), lambda u, f, ue, ur, uc, used: (ue[u], 0, 0))],
            out_specs=pl.BlockSpec(memory_space=pl.ANY),
            scratch_shapes=[pltpu.VMEM((2, UNIT_ROWS, d), bf16),
                            pltpu.VMEM((UNIT_ROWS, d), f32),
                            pltpu.VMEM((CHUNK_PITCHED, LANES), u32),
                            pltpu.VMEM((2, CHUNK_PITCHED, LANES), u32),
                            pltpu.SMEM((2 * UNIT_ROWS,), jnp.int32),
                            pltpu.SMEM((8,), jnp.int32),
                            pltpu.SemaphoreType.DMA(()),
                            pltpu.SemaphoreType.DMA(()),
                            pltpu.SemaphoreType.DMA((2,))],
        ),
        out_shape=jax.ShapeDtypeStruct((n_pad * PITCH, LANES), u32),
        compiler_params=_cparams(("arbitrary", "arbitrary")),
        name="moe_experts",
    )(ue, u_row, u_chunks, n_used_rows.reshape(1), slot_tok, x1s, w_gate, b_gate.reshape(e, 1, dff), w_up,
      b_up.reshape(e, 1, dff), w_down, b_down.reshape(e, 1, d))


COMBINE_ROWS = 256


def _combine_kernel(dest_ref, y_hbm, gate_ref, x1_ref, p_ref, lnw_ref, lnb_ref, wp_ref, pnw_ref, wg_ref,
                    o_ref, ybuf, fsum_ref, sem):
    tm = COMBINE_ROWS

    def issue(t, c):
        for k in range(TOP_K):
            src = y_hbm.at[pl.ds(dest_ref[t * TOP_K + k] * PITCH, SLAB), :]
            dst = ybuf.at[k, pl.ds(t * PITCH, SLAB), :]
            pltpu.make_async_copy(src, dst, sem).start()
        return c

    lax.fori_loop(0, tm, issue, 0, unroll=2)
    e = jnp.dot(p_ref[...].astype(bf16), wp_ref[...], preferred_element_type=f32)
    e = e * lax.rsqrt(jnp.mean(e * e, axis=-1, keepdims=True) + LN_EPS) * pnw_ref[...]
    for k in range(TOP_K):
        pltpu.make_async_copy(y_hbm.at[pl.ds(0, tm * SLAB), :], ybuf.at[k, pl.ds(0, tm * SLAB), :], sem).wait()
    gates = gate_ref[...]
    for j in range(SLAB):
        acc_lo, acc_hi = None, None
        for k in range(TOP_K):
            lo, hi = _slab_cols(ybuf.at[k], j, tm)
            g = gates[:, k:k + 1]
            acc_lo = g * lo if k == 0 else acc_lo + g * lo
            acc_hi = g * hi if k == 0 else acc_hi + g * hi
        fsum_ref[:, j * LANES:(j + 1) * LANES] = acc_lo
        fsum_ref[:, HALF + j * LANES:HALF + (j + 1) * LANES] = acc_hi
    x2 = _layer_norm(DN_ALPHA * x1_ref[...] + fsum_ref[...], lnw_ref[...], lnb_ref[...])
    gate = jax.nn.sigmoid(jnp.dot(x2.astype(bf16), wg_ref[...], preferred_element_type=f32))
    o_ref[...] = x2 + gate * e


def _combine(y, dest, gates, x1, p2d, ln_w, ln_b, wp_bf, ple_norm_w, wg_bf):
    t, d = x1.shape
    tm = COMBINE_ROWS
    row = lambda c: pl.BlockSpec((tm, c), lambda i: (i, 0))
    full = lambda r, c: pl.BlockSpec((r, c), lambda i: (0, 0))
    return pl.pallas_call(
        _combine_kernel,
        grid=(t // tm,),
        in_specs=[pl.BlockSpec((tm * TOP_K,), lambda i: (i,), memory_space=pltpu.SMEM),
                  pl.BlockSpec(memory_space=pl.ANY),
                  row(TOP_K), row(d), row(PLE_DIM), full(1, d), full(1, d), full(PLE_DIM, d), full(1, d),
                  full(d, d)],
        out_specs=row(d),
        out_shape=jax.ShapeDtypeStruct((t, d), f32),
        scratch_shapes=[pltpu.VMEM((TOP_K, tm * PITCH, LANES), u32), pltpu.VMEM((tm, d), f32),
                        pltpu.SemaphoreType.DMA(())],
        compiler_params=_cparams(("arbitrary",)),
        name="combine_ln2_ple",
    )(dest.reshape(-1), y, gates, x1, p2d, ln_w.reshape(1, d), ln_b.reshape(1, d), wp_bf,
      ple_norm_w.reshape(1, d), wg_bf)


def _layer(h, p_i, w_in, ret_gn_w, conv_w, conv_b, lru_wa, lru_ba, lru_wx, lru_bx, lru_lam, w_out,
           ln1_w, ln1_b, w_router, b_router, w_gate, b_gate, w_up, b_up, w_down, b_down,
           ln2_w, ln2_b, w_ple_proj, ple_norm_w, w_ple_gate):
    b, s, d = h.shape
    t = b * s
    x2d = h.reshape(t, d)
    proj = _in_proj(x2d, w_in).reshape(b, s, IN_COLS)
    ret_out = _retention(proj, ret_gn_w)
    lru_out = _lru(proj, conv_w, conv_b, lru_wa, lru_ba, lru_wx, lru_bx, lru_lam)
    x1, x1s, top_e, gates = _out_router(ret_out.reshape(t, RET_WIDTH), lru_out.reshape(t, LRU_WIDTH),
                                        w_out.astype(bf16), x2d, ln1_w, ln1_b, w_router, b_router)
    dest, slot_tok, ue, u_row, u_chunks, n_used_rows, n_pad, max_units = _routing_tables(top_e, t)
    y = _moe_experts(x1s, slot_tok, ue, u_row, u_chunks, n_used_rows, n_pad, max_units, w_gate, b_gate, w_up, b_up,
                     w_down, b_down)
    out = _combine(y, dest, gates, x1, p_i.reshape(t, PLE_DIM), ln2_w, ln2_b, w_ple_proj.astype(bf16),
                   ple_norm_w, w_ple_gate.astype(bf16))
    return out.reshape(b, s, d)


def kernel(x, p, w_in, ret_gn_w, conv_w, conv_b, lru_wa, lru_ba, lru_wx, lru_bx, lru_lam, w_out, ln1_w, ln1_b,
           w_router, b_router, w_gate, b_gate, w_up, b_up, w_down, b_down, ln2_w, ln2_b, w_ple_proj, ple_norm_w,
           w_ple_gate):
    h = x.astype(f32)
    for i in range(w_in.shape[0]):
        h = _layer(h, p[i], w_in[i], ret_gn_w[i], conv_w[i], conv_b[i], lru_wa[i], lru_ba[i], lru_wx[i],
                   lru_bx[i], lru_lam[i], w_out[i], ln1_w[i], ln1_b[i], w_router[i], b_router[i], w_gate[i],
                   b_gate[i], w_up[i], b_up[i], w_down[i], b_down[i], ln2_w[i], ln2_b[i], w_ple_proj[i],
                   ple_norm_w[i], w_ple_gate[i])
    return h.astype(x.dtype)
```

```python
import functools
import math

import jax
import jax.numpy as jnp
from jax import lax
from jax.experimental import pallas as pl
from jax.experimental.pallas import tpu as pltpu

D_MODEL = 2048
RET_HEAD_DIM = 128
RET_HEADS = 8
RET_WIDTH = RET_HEADS * RET_HEAD_DIM
LRU_WIDTH = D_MODEL - RET_WIDTH
LRU_BLOCKS = 8
LRU_BLOCK_DIM = LRU_WIDTH // LRU_BLOCKS
IN_COLS = 4 * RET_WIDTH + 2 * LRU_WIDTH
CONV_WIDTH = 4
LRU_C = 8.0
CHUNK = 128
ROPE_BASE = 10000.0
N_EXPERTS = 32
TOP_K = 4
SWIGLU_LIMIT = 7.0
SWIGLU_ALPHA = 1.702
PLE_DIM = 256
LN_EPS = 1e-5
DEPTH = 1
DN_ALPHA = (2.0 * DEPTH) ** 0.25

LANES = 128
SUBLANES = 8
VMEM_LIMIT = 60 * 1024 * 1024

ROW_CHUNK = 256
UNIT_ROWS = 2048
F_TILE = 256

f32 = jnp.float32
bf16 = jnp.bfloat16


def _cparams(sem):
    return pltpu.CompilerParams(dimension_semantics=sem, vmem_limit_bytes=VMEM_LIMIT)


def _in_proj_kernel(x_ref, w_ref, o_ref, xb_ref):
    @pl.when(pl.program_id(1) == 0)
    def _():
        xb_ref[...] = x_ref[...].astype(bf16)

    o_ref[...] = lax.dot_general(xb_ref[...], w_ref[...], (((1,), (0,)), ((), ())), preferred_element_type=f32)


def _in_proj(x2d, w_bf):
    t, d = x2d.shape
    n = w_bf.shape[1]
    tm, tn = 1024, 1024
    return pl.pallas_call(
        _in_proj_kernel,
        grid=(t // tm, n // tn),
        in_specs=[pl.BlockSpec((tm, d), lambda i, j: (i, 0)),
                  pl.BlockSpec((d, tn), lambda i, j: (0, j))],
        out_specs=pl.BlockSpec((tm, tn), lambda i, j: (i, j)),
        out_shape=jax.ShapeDtypeStruct((t, n), f32),
        scratch_shapes=[pltpu.VMEM((tm, d), bf16)],
        compiler_params=_cparams(("parallel", "arbitrary")),
        name="in_proj",
    )(x2d, w_bf)


def _retention_kernel(q_ref, k_ref, v_ref, g_ref, cos_ref, sin_ref, dec_ref, qd_ref, kd_ref, cd_ref, gnw_ref,
                      o_ref):
    s = q_ref.shape[0]
    n_chunks = s // CHUNK
    decay = dec_ref[...]
    q_dec = qd_ref[...]
    k_dec = kd_ref[...]
    c_dec = cd_ref[0:1, :]
    gnw = gnw_ref[...]
    k_scale = RET_HEAD_DIM ** -0.5

    def rope(xv, cos, sin):
        return xv * cos + pltpu.roll(xv, RET_HEAD_DIM // 2, axis=1) * sin

    def body(n, state):
        sl = pl.ds(pl.multiple_of(n * CHUNK, CHUNK), CHUNK)
        cos = cos_ref[sl, :]
        sin = sin_ref[sl, :]
        q = rope(q_ref[sl, :], cos, sin)
        k = rope(k_ref[sl, :], cos, sin) * k_scale
        vb = v_ref[sl, :].astype(bf16)
        scores = lax.dot_general(q.astype(bf16), k.astype(bf16), (((1,), (1,)), ((), ())),
                                 preferred_element_type=f32) * decay
        intra = jnp.dot(scores.astype(bf16), vb, preferred_element_type=f32)
        cross = jnp.dot((q * q_dec).astype(bf16), state.astype(bf16), preferred_element_type=f32)
        kv = lax.dot_general((k * k_dec).astype(bf16), vb, (((0,), (0,)), ((), ())),
                             preferred_element_type=f32)
        ret = intra + cross
        mu = jnp.mean(ret, axis=-1, keepdims=True)
        cen = ret - mu
        var = jnp.mean(cen * cen, axis=-1, keepdims=True)
        ret = cen * lax.rsqrt(var + LN_EPS) * gnw
        g = g_ref[sl, :]
        o_ref[sl, :] = (g * jax.nn.sigmoid(g) * ret).astype(o_ref.dtype)
        return c_dec * state + kv

    lax.fori_loop(0, n_chunks, body, jnp.zeros((RET_HEAD_DIM, RET_HEAD_DIM), f32), unroll=4)


def _retention_tables(s):
    h, d = RET_HEADS, RET_HEAD_DIM
    inv = ROPE_BASE ** (-jnp.arange(0, d, 2, dtype=f32) / d)
    ang = jnp.arange(s, dtype=f32)[:, None] * inv[None, :]
    cos = jnp.cos(ang)
    sin = jnp.sin(ang)
    cos_t = jnp.concatenate([cos, cos], axis=-1)
    sin_t = jnp.concatenate([-sin, sin], axis=-1)
    log_gamma = jnp.log1p(-jnp.exp2(-5.0 - jnp.arange(h, dtype=f32)))
    idx = jnp.arange(CHUNK, dtype=f32)
    diff = idx[:, None] - idx[None, :]
    decay = jnp.where((diff >= 0)[None], jnp.exp(jnp.maximum(diff, 0.0)[None] * log_gamma[:, None, None]), 0.0)
    q_dec = jnp.exp((idx[None, :] + 1.0) * log_gamma[:, None])
    k_dec = jnp.exp((CHUNK - 1.0 - idx)[None, :] * log_gamma[:, None])
    c_dec = jnp.exp(CHUNK * log_gamma)
    q_dec = jnp.broadcast_to(q_dec[:, :, None], (h, CHUNK, d))
    k_dec = jnp.broadcast_to(k_dec[:, :, None], (h, CHUNK, d))
    c_dec = jnp.broadcast_to(c_dec[:, None, None], (h, SUBLANES, d))
    return cos_t, sin_t, decay, q_dec, k_dec, c_dec


def _retention(proj, ret_gn_w):
    b, s, _ = proj.shape
    d = RET_HEAD_DIM
    cos_t, sin_t, decay, q_dec, k_dec, c_dec = _retention_tables(s)
    col = lambda off: pl.BlockSpec((None, s, d), lambda bi, hi: (bi, 0, off + hi))
    per_head = lambda r: pl.BlockSpec((None, r, d), lambda bi, hi: (hi, 0, 0))
    full = lambda shp: pl.BlockSpec(shp, lambda bi, hi: (0,) * len(shp))
    return pl.pallas_call(
        _retention_kernel,
        grid=(b, RET_HEADS),
        in_specs=[col(0), col(RET_HEADS), col(2 * RET_HEADS), col(3 * RET_HEADS),
                  full((s, d)), full((s, d)),
                  per_head(CHUNK), per_head(CHUNK), per_head(CHUNK), per_head(SUBLANES),
                  pl.BlockSpec((1, d), lambda bi, hi: (0, hi))],
        out_specs=pl.BlockSpec((None, s, d), lambda bi, hi: (bi, 0, hi)),
        out_shape=jax.ShapeDtypeStruct((b, s, RET_WIDTH), bf16),
        compiler_params=_cparams(("parallel", "parallel")),
        name="retention",
    )(proj, proj, proj, proj, cos_t, sin_t, decay, q_dec, k_dec, c_dec, ret_gn_w.reshape(1, RET_WIDTH))


def _gelu_tanh(x):
    return 0.5 * x * (1.0 + jnp.tanh(math.sqrt(2.0 / math.pi) * (x + 0.044715 * (x * x * x))))


def _lru_kernel(xr_ref, yg_ref, cw_ref, cb_ref, wa_ref, ba_ref, wx_ref, bx_ref, lam_ref, o_ref, a_ref, b_ref):
    s = xr_ref.shape[0]
    x = xr_ref[...]
    rows = lax.broadcasted_iota(jnp.int32, x.shape, 0)
    xc = cb_ref[...] + cw_ref[CONV_WIDTH - 1:CONV_WIDTH, :] * x
    for back in range(1, CONV_WIDTH):
        shifted = jnp.where(rows >= back, pltpu.roll(x, back, axis=0), 0.0)
        xc = xc + cw_ref[CONV_WIDTH - 1 - back:CONV_WIDTH - back, :] * shifted
    xcb = xc.astype(bf16)
    r = jax.nn.sigmoid(jnp.dot(xcb, wa_ref[...].astype(bf16), preferred_element_type=f32) + ba_ref[...])
    gi = jax.nn.sigmoid(jnp.dot(xcb, wx_ref[...].astype(bf16), preferred_element_type=f32) + bx_ref[...])
    lam = lam_ref[...]
    log_sig = jnp.minimum(lam, 0.0) - jnp.log1p(jnp.exp(-jnp.abs(lam)))
    log_a = LRU_C * r * log_sig
    a = jnp.exp(log_a)
    a_ref[...] = a
    b_ref[...] = jnp.sqrt(-jnp.tanh(log_a) * (a * a + 1.0)) * (gi * xc)

    row8 = lax.broadcasted_iota(jnp.int32, (SUBLANES, LANES), 0)

    def body(i, h_prev):
        sl = pl.ds(pl.multiple_of(i * SUBLANES, SUBLANES), SUBLANES)
        a8 = a_ref[sl, :]
        b8 = b_ref[sl, :]
        for sh in (1, 2, 4):
            a_sh = jnp.where(row8 >= sh, pltpu.roll(a8, sh, axis=0), 1.0)
            b_sh = jnp.where(row8 >= sh, pltpu.roll(b8, sh, axis=0), 0.0)
            b8 = a8 * b_sh + b8
            a8 = a8 * a_sh
        h8 = a8 * h_prev + b8
        o_ref[sl, :] = (_gelu_tanh(yg_ref[sl, :]) * h8).astype(o_ref.dtype)
        return h8[SUBLANES - 1:SUBLANES, :]

    lax.fori_loop(0, s // SUBLANES, body, jnp.zeros((1, LANES), f32), unroll=8)


def _lru(proj, conv_w, conv_b, wa, ba, wx, bx, lam):
    b, s, _ = proj.shape
    d = LRU_BLOCK_DIM
    xr_off = 4 * RET_WIDTH // d
    yg_off = xr_off + LRU_BLOCKS
    col = lambda off: pl.BlockSpec((None, s, d), lambda bi, ji: (bi, 0, off + ji))
    vec = lambda r: pl.BlockSpec((r, d), lambda bi, ji: (0, ji))
    blk = lambda r: pl.BlockSpec((None, r, d), lambda bi, ji: (ji, 0, 0))
    return pl.pallas_call(
        _lru_kernel,
        grid=(b, LRU_BLOCKS),
        in_specs=[col(xr_off), col(yg_off), vec(CONV_WIDTH), vec(1), blk(d), blk(1), blk(d), blk(1), vec(1)],
        out_specs=pl.BlockSpec((None, s, d), lambda bi, ji: (bi, 0, ji)),
        out_shape=jax.ShapeDtypeStruct((b, s, LRU_WIDTH), bf16),
        scratch_shapes=[pltpu.VMEM((s, d), f32), pltpu.VMEM((s, d), f32)],
        compiler_params=_cparams(("parallel", "parallel")),
        name="rg_lru",
    )(proj, proj, conv_w, conv_b.reshape(1, LRU_WIDTH), wa, ba.reshape(LRU_BLOCKS, 1, d), wx,
      bx.reshape(LRU_BLOCKS, 1, d), lam.reshape(1, LRU_WIDTH))


def _layer_norm(y, w, b):
    mu = jnp.mean(y, axis=-1, keepdims=True)
    cen = y - mu
    var = jnp.mean(cen * cen, axis=-1, keepdims=True)
    return cen * lax.rsqrt(var + LN_EPS) * w + b


HALF = D_MODEL // 2
SLAB = HALF // LANES
PITCH = SLAB + 4
u32 = jnp.uint32
HIGH_MASK = 0xFFFF0000


def _bf16_bits(v):
    return lax.bitcast_convert_type(v.astype(bf16).astype(f32), u32)


def _store_slabs(ref, val):
    n = val.shape[0]
    for j in range(SLAB):
        lo = _bf16_bits(val[:, j * LANES:(j + 1) * LANES])
        hi = _bf16_bits(val[:, HALF + j * LANES:HALF + (j + 1) * LANES])
        ref[pl.ds(j, n, stride=PITCH), :] = hi | lax.shift_right_logical(lo, jnp.full_like(lo, 16))
    for j in range(SLAB, PITCH):
        ref[pl.ds(j, n, stride=PITCH), :] = jnp.zeros((n, LANES), u32)


def _slab_cols(ref, j, n):
    w = ref[pl.ds(j, n, stride=PITCH), :]
    lo = lax.bitcast_convert_type(lax.shift_left(w, jnp.full_like(w, 16)), f32)
    hi = lax.bitcast_convert_type(w & jnp.full_like(w, HIGH_MASK), f32)
    return lo, hi


def _split_bf16(v):
    hi = v.astype(bf16)
    lo = (v - hi.astype(f32)).astype(bf16)
    return hi, lo


def _out_router_kernel(ret_ref, lru_ref, wo_ref, x_ref, lnw_ref, lnb_ref, wr_ref, br_ref,
                       x1_ref, x1s_ref, tope_ref, gate_ref):
    m = jnp.dot(ret_ref[...], wo_ref[0:RET_WIDTH, :], preferred_element_type=f32)
    m = m + jnp.dot(lru_ref[...], wo_ref[RET_WIDTH:D_MODEL, :], preferred_element_type=f32)
    x1 = _layer_norm(DN_ALPHA * x_ref[...] + m, lnw_ref[...], lnb_ref[...])
    x1_ref[...] = x1
    _store_slabs(x1s_ref, x1)
    xh, xl = _split_bf16(x1)
    wh, wl = _split_bf16(wr_ref[...])
    logits = (jnp.dot(xh, wh, preferred_element_type=f32) + jnp.dot(xl, wh, preferred_element_type=f32)
              + jnp.dot(xh, wl, preferred_element_type=f32)) + br_ref[...]
    tm = logits.shape[0]
    lane = lax.broadcasted_iota(jnp.int32, logits.shape, 1)
    lane_k = lax.broadcasted_iota(jnp.int32, (tm, TOP_K), 1)
    top_e = jnp.zeros((tm, TOP_K), jnp.int32)
    top_v = jnp.zeros((tm, TOP_K), f32)
    cur = logits
    for kk in range(TOP_K):
        mx = jnp.max(cur, axis=-1, keepdims=True)
        idx = jnp.min(jnp.where(cur == mx, lane, N_EXPERTS), axis=-1, keepdims=True)
        top_e = jnp.where(lane_k == kk, idx, top_e)
        top_v = jnp.where(lane_k == kk, mx, top_v)
        cur = jnp.where(lane == idx, -jnp.inf, cur)
    ex = jnp.exp(top_v - top_v[:, 0:1])
    gate_ref[...] = ex / jnp.sum(ex, axis=-1, keepdims=True)
    tope_ref[...] = top_e


def _out_router(ret_out, lru_out, wo_bf, x2d, ln_w, ln_b, w_router, b_router):
    t, d = x2d.shape
    tm = 256
    row = lambda c: pl.BlockSpec((tm, c), lambda i: (i, 0))
    full = lambda r, c: pl.BlockSpec((r, c), lambda i: (0, 0))
    return pl.pallas_call(
        _out_router_kernel,
        grid=(t // tm,),
        in_specs=[row(RET_WIDTH), row(LRU_WIDTH), full(d, d), row(d), full(1, d), full(1, d),
                  full(d, N_EXPERTS), full(1, N_EXPERTS)],
        out_specs=[row(d), pl.BlockSpec((tm * PITCH, LANES), lambda i: (i, 0)), row(TOP_K), row(TOP_K)],
        out_shape=[jax.ShapeDtypeStruct((t, d), f32), jax.ShapeDtypeStruct((t * PITCH, LANES), u32),
                   jax.ShapeDtypeStruct((t, TOP_K), jnp.int32),
                   jax.ShapeDtypeStruct((t, TOP_K), f32)],
        compiler_params=_cparams(("parallel",)),
        name="out_proj_ln1_router",
    )(ret_out, lru_out, wo_bf, x2d, ln_w.reshape(1, d), ln_b.reshape(1, d), w_router,
      b_router.reshape(1, N_EXPERTS))


def _routing_tables(top_e, t):
    n_pad = t * TOP_K + N_EXPERTS * ROW_CHUNK
    max_units = N_EXPERTS + (t * TOP_K) // UNIT_ROWS
    sel = (top_e[:, :, None] == jnp.arange(N_EXPERTS, dtype=jnp.int32)[None, None, :]).any(axis=1)
    sel = sel.astype(jnp.int32)
    counts = jnp.sum(sel, axis=0)
    rank = jnp.cumsum(sel, axis=0) - sel
    padded = (counts + ROW_CHUNK - 1) // ROW_CHUNK * ROW_CHUNK
    pad_ends = jnp.cumsum(padded)
    pad_starts = pad_ends - padded
    dest_dense = pad_starts[None, :] + rank
    dest = jnp.take_along_axis(dest_dense, top_e, axis=1)
    tok = jnp.broadcast_to(jnp.arange(t, dtype=jnp.int32)[:, None], (t, TOP_K))
    slot_tok = jnp.zeros((n_pad + UNIT_ROWS,), jnp.int32).at[dest.reshape(-1)].set(tok.reshape(-1) * PITCH)
    units_per_e = (padded + UNIT_ROWS - 1) // UNIT_ROWS
    unit_ends = jnp.cumsum(units_per_e)
    unit_starts = unit_ends - units_per_e
    n_units = unit_ends[-1]
    u = jnp.arange(max_units, dtype=jnp.int32)
    u_clamped = jnp.minimum(u, n_units - 1)
    ue = jnp.searchsorted(unit_ends, u_clamped, side='right').astype(jnp.int32)
    ue = jnp.minimum(ue, N_EXPERTS - 1)
    j = u_clamped - unit_starts[ue]
    u_row = pad_starts[ue] + j * UNIT_ROWS
    u_rows = jnp.minimum(UNIT_ROWS, padded[ue] - j * UNIT_ROWS)
    u_chunks = jnp.where(u < n_units, u_rows // ROW_CHUNK, 0).astype(jnp.int32)
    n_used_rows = pad_ends[-1].astype(jnp.int32)
    return dest.astype(jnp.int32), slot_tok, ue, u_row.astype(jnp.int32), u_chunks, n_used_rows, n_pad, max_units


CHUNK_PITCHED = ROW_CHUNK * PITCH
OUT_SLOTS = 4


def _moe_kernel(ue_ref, urow_ref, uchunks_ref, used_ref, tok_hbm, x1s_hbm, wg_ref, bg_ref, wu_ref, bu_ref, wd_ref,
                bd_ref, y_hbm, xbuf, yacc, gstage, ostage, tok_smem, flags, tok_sem, in_sem, out_sem):
    u = pl.program_id(0)
    f = pl.program_id(1)
    n_u = pl.num_programs(0)
    n_f = pl.num_programs(1)
    n_chunks = uchunks_ref[u]
    row0 = urow_ref[u]
    cur = u % 2
    nxt = 1 - cur
    u_next = jnp.minimum(u + 1, n_u - 1)
    n_next = jnp.where(u + 1 < n_u, uchunks_ref[u_next], 0)
    u_next2 = jnp.minimum(u + 2, n_u - 1)
    n_next2 = jnp.where(u + 2 < n_u, uchunks_ref[u_next2], 0)
    TABLES = 3
    tcur = u % TABLES
    tnxt = (u + 1) % TABLES
    first_step = jnp.logical_and(u == 0, f == 0)
    PENDING, PEND_CHUNK, PEND_SLOT, OUT_BUSY = 0, 1, 2, 3

    def rows(c, size=ROW_CHUNK):
        return pl.ds(pl.multiple_of(c * size, size), size)

    def tok_copy(unit_row, slot):
        src = tok_hbm.at[pl.ds(pl.multiple_of(unit_row, ROW_CHUNK), UNIT_ROWS)]
        dst = tok_smem.at[pl.ds(pl.multiple_of(slot * UNIT_ROWS, UNIT_ROWS), UNIT_ROWS)]
        return pltpu.make_async_copy(src, dst, tok_sem)

    ISSUE_GROUP = 8

    def gather_start(c, tslot, lo=0, hi=ROW_CHUNK):
        base = tslot * UNIT_ROWS + c * ROW_CHUNK

        def issue(g, carry):
            for i in range(ISSUE_GROUP):
                r = g * ISSUE_GROUP + i
                src = x1s_hbm.at[pl.ds(tok_smem[base + r], SLAB), :]
                dst = gstage.at[pl.ds(r * PITCH, SLAB), :]
                pltpu.make_async_copy(src, dst, in_sem).start()
            return carry

        lax.fori_loop(lo // ISSUE_GROUP, hi // ISSUE_GROUP, issue, 0)

    def gather_finish(c, xslot):
        n = ROW_CHUNK * SLAB
        pltpu.make_async_copy(x1s_hbm.at[pl.ds(0, n), :], gstage.at[pl.ds(0, n), :], in_sem).wait()
        for j in range(SLAB):
            lo, hi = _slab_cols(gstage, j, ROW_CHUNK)
            xbuf[xslot, rows(c), j * LANES:(j + 1) * LANES] = lo.astype(bf16)
            xbuf[xslot, rows(c), HALF + j * LANES:HALF + (j + 1) * LANES] = hi.astype(bf16)

    def out_copy(c, slot):
        dst = y_hbm.at[pl.ds(pl.multiple_of((row0 + c * ROW_CHUNK) * PITCH, CHUNK_PITCHED), CHUNK_PITCHED), :]
        return pltpu.make_async_copy(ostage.at[slot], dst, out_sem.at[slot])

    def out_wait(slot):
        @pl.when(flags[OUT_BUSY + slot] == 1)
        def _():
            pltpu.make_async_copy(ostage.at[slot], y_hbm.at[pl.ds(0, CHUNK_PITCHED), :], out_sem.at[slot]).wait()
            flags[OUT_BUSY + slot] = 0

    @pl.when(first_step)
    def _():
        for i in range(OUT_BUSY + OUT_SLOTS):
            flags[i] = 0

        @pl.when(n_chunks > 0)
        def _():
            first_table = tok_copy(row0, tcur)
            first_table.start()
            first_table.wait()

            def load(c, carry):
                gather_start(c, tcur)
                gather_finish(c, cur)
                return carry

            lax.fori_loop(0, n_chunks, load, 0)

        @pl.when(n_next > 0)
        def _():
            tok_copy(urow_ref[u_next], tnxt).start()

    @pl.when(flags[PENDING] == 1)
    def _():
        gather_finish(flags[PEND_CHUNK], flags[PEND_SLOT])
        flags[PENDING] = 0

    @pl.when(jnp.logical_and(f == 0, n_next > 0))
    def _():
        tok_copy(urow_ref[u_next], tnxt).wait()

        @pl.when(n_next2 > 0)
        def _():
            tok_copy(urow_ref[u_next2], (u + 2) % TABLES).start()

    prefetching = f < n_next
    n_blocks = n_chunks // 2 + n_chunks % 2
    share = (ROW_CHUNK // jnp.maximum(n_blocks, 1) + ISSUE_GROUP - 1) // ISSUE_GROUP * ISSUE_GROUP

    def prefetch_share(i):
        lo = jnp.minimum(i * share, ROW_CHUNK)
        hi = jnp.where(i == n_blocks - 1, ROW_CHUNK, jnp.minimum(lo + share, ROW_CHUNK))
        gather_start(f, tnxt, lo, jnp.where(prefetching, hi, lo))

    @pl.when(prefetching)
    def _():
        flags[PENDING] = 1
        flags[PEND_CHUNK] = f
        flags[PEND_SLOT] = nxt

    @pl.when(n_chunks > 0)
    def _():
        bg = bg_ref[...]
        bu = bu_ref[...]
        mm = lambda a, w: lax.dot_general(a, w, (((1,), (0,)), ((), ())), preferred_element_type=f32)

        def mlp(sl, first):
            xc = xbuf[cur, sl, :]
            gt = jnp.minimum(mm(xc, wg_ref[...]) + bg, SWIGLU_LIMIT)
            up = jnp.clip(mm(xc, wu_ref[...]) + bu, -SWIGLU_LIMIT, SWIGLU_LIMIT)
            hid = (up + 1.0) * gt * jax.nn.sigmoid(SWIGLU_ALPHA * gt)
            part = mm(hid.astype(bf16), wd_ref[...])
            if first:
                yacc[sl, :] = part
            else:
                yacc[sl, :] += part

        def blocks(first):
            def pair(c, carry):
                prefetch_share(c)
                mlp(rows(c, 2 * ROW_CHUNK), first)
                return carry

            lax.fori_loop(0, n_chunks // 2, pair, 0)

            @pl.when(n_chunks % 2 == 1)
            def _():
                prefetch_share(n_chunks // 2)
                mlp(rows(n_chunks - 1), first)

        @pl.when(f == 0)
        def _():
            blocks(True)

        @pl.when(f > 0)
        def _():
            blocks(False)

    @pl.when(jnp.logical_and(f == n_f - 1, n_chunks > 0))
    def _():
        bd = bd_ref[...]

        def store(c, carry):
            slot = c % OUT_SLOTS
            out_wait(slot)
            _store_slabs(ostage.at[slot], yacc[rows(c), :] + bd)
            out_copy(c, slot).start()
            flags[OUT_BUSY + slot] = 1
            return carry

        lax.fori_loop(0, n_chunks, store, 0)

    @pl.when(jnp.logical_and(u == n_u - 1, f == n_f - 1))
    def _():
        for s in range(OUT_SLOTS):
            out_wait(s)
        first = used_ref[0] // ROW_CHUNK
        last = y_hbm.shape[0] // CHUNK_PITCHED
        ostage[0] = jnp.zeros((CHUNK_PITCHED, LANES), u32)

        def tail_copy(c):
            dst = y_hbm.at[pl.ds(pl.multiple_of(c * CHUNK_PITCHED, CHUNK_PITCHED), CHUNK_PITCHED), :]
            return pltpu.make_async_copy(ostage.at[0], dst, out_sem.at[0])

        def start(c, carry):
            tail_copy(c).start()
            return carry

        def wait(c, carry):
            tail_copy(c).wait()
            return carry

        lax.fori_loop(first, last, start, 0)
        lax.fori_loop(first, last, wait, 0)


def _moe_experts(x1s, slot_tok, ue, u_row, u_chunks, n_used_rows, n_pad, max_units, w_gate, b_gate, w_up, b_up,
                 w_down, b_down):
    e, d, dff = w_gate.shape
    n_f = dff // F_TILE
    assert n_f >= UNIT_ROWS // ROW_CHUNK, "one chunk of the next unit is gathered per f-tile step"
    f_idx = lambda u, f, uc: jnp.where(uc[u] > 0, f, n_f - 1)
    col_w = pl.BlockSpec((None, d, F_TILE), lambda u, f, ue, ur, uc, used: (ue[u], 0, f_idx(u, f, uc)))
    col_b = pl.BlockSpec((None, 1, F_TILE), lambda u, f, ue, ur, uc, used: (ue[u], 0, f_idx(u, f, uc)))
    return pl.pallas_call(
        _moe_kernel,
        grid_spec=pltpu.PrefetchScalarGridSpec(
            num_scalar_prefetch=4,
            grid=(max_units, n_f),
            in_specs=[pl.BlockSpec(memory_space=pl.ANY), pl.BlockSpec(memory_space=pl.ANY),
                      col_w, col_b, col_w, col_b,
                      pl.BlockSpec((None, F_TILE, d), lambda u, f, ue, ur, uc, used: (ue[u], f_idx(u, f, uc), 0)),
                      pl.BlockSpec((None, 1, d), lambda u, f, ue, ur, uc, used: (ue[u], 0, 0))],
            out_specs=pl.BlockSpec(memory_space=pl.ANY),
            scratch_shapes=[pltpu.VMEM((2, UNIT_ROWS, d), bf16),
                            pltpu.VMEM((UNIT_ROWS, d), f32),
                            pltpu.VMEM((CHUNK_PITCHED, LANES), u32),
                            pltpu.VMEM((OUT_SLOTS, CHUNK_PITCHED, LANES), u32),
                            pltpu.SMEM((3 * UNIT_ROWS,), jnp.int32),
                            pltpu.SMEM((3 + OUT_SLOTS,), jnp.int32),
                            pltpu.SemaphoreType.DMA(()),
                            pltpu.SemaphoreType.DMA(()),
                            pltpu.SemaphoreType.DMA((OUT_SLOTS,))],
        ),
        out_shape=jax.ShapeDtypeStruct((n_pad * PITCH, LANES), u32),
        compiler_params=_cparams(("arbitrary", "arbitrary")),
        name="moe_experts",
    )(ue, u_row, u_chunks, n_used_rows.reshape(1), slot_tok, x1s, w_gate, b_gate.reshape(e, 1, dff), w_up,
      b_up.reshape(e, 1, dff), w_down, b_down.reshape(e, 1, d))


COMBINE_ROWS = 256


def _combine_kernel(dest_ref, y_hbm, gate_ref, x1_ref, p_ref, lnw_ref, lnb_ref, wp_ref, pnw_ref, wg_ref,
                    o_ref, ybuf, fsum_ref, sem):
    tm = COMBINE_ROWS

    def issue(t, c):
        for k in range(TOP_K):
            src = y_hbm.at[pl.ds(dest_ref[t * TOP_K + k] * PITCH, SLAB), :]
            dst = ybuf.at[k, pl.ds(t * PITCH, SLAB), :]
            pltpu.make_async_copy(src, dst, sem).start()
        return c

    lax.fori_loop(0, tm, issue, 0, unroll=2)
    e = jnp.dot(p_ref[...].astype(bf16), wp_ref[...], preferred_element_type=f32)
    e = e * lax.rsqrt(jnp.mean(e * e, axis=-1, keepdims=True) + LN_EPS) * pnw_ref[...]
    for k in range(TOP_K):
        pltpu.make_async_copy(y_hbm.at[pl.ds(0, tm * SLAB), :], ybuf.at[k, pl.ds(0, tm * SLAB), :], sem).wait()
    gates = gate_ref[...]
    for j in range(SLAB):
        acc_lo, acc_hi = None, None
        for k in range(TOP_K):
            lo, hi = _slab_cols(ybuf.at[k], j, tm)
            g = gates[:, k:k + 1]
            acc_lo = g * lo if k == 0 else acc_lo + g * lo
            acc_hi = g * hi if k == 0 else acc_hi + g * hi
        fsum_ref[:, j * LANES:(j + 1) * LANES] = acc_lo
        fsum_ref[:, HALF + j * LANES:HALF + (j + 1) * LANES] = acc_hi
    x2 = _layer_norm(DN_ALPHA * x1_ref[...] + fsum_ref[...], lnw_ref[...], lnb_ref[...])
    gate = jax.nn.sigmoid(jnp.dot(x2.astype(bf16), wg_ref[...], preferred_element_type=f32))
    o_ref[...] = x2 + gate * e


def _combine(y, dest, gates, x1, p2d, ln_w, ln_b, wp_bf, ple_norm_w, wg_bf):
    t, d = x1.shape
    tm = COMBINE_ROWS
    row = lambda c: pl.BlockSpec((tm, c), lambda i: (i, 0))
    full = lambda r, c: pl.BlockSpec((r, c), lambda i: (0, 0))
    return pl.pallas_call(
        _combine_kernel,
        grid=(t // tm,),
        in_specs=[pl.BlockSpec((tm * TOP_K,), lambda i: (i,), memory_space=pltpu.SMEM),
                  pl.BlockSpec(memory_space=pl.ANY),
                  row(TOP_K), row(d), row(PLE_DIM), full(1, d), full(1, d), full(PLE_DIM, d), full(1, d),
                  full(d, d)],
        out_specs=row(d),
        out_shape=jax.ShapeDtypeStruct((t, d), f32),
        scratch_shapes=[pltpu.VMEM((TOP_K, tm * PITCH, LANES), u32), pltpu.VMEM((tm, d), f32),
                        pltpu.SemaphoreType.DMA(())],
        compiler_params=_cparams(("arbitrary",)),
        name="combine_ln2_ple",
    )(dest.reshape(-1), y, gates, x1, p2d, ln_w.reshape(1, d), ln_b.reshape(1, d), wp_bf,
      ple_norm_w.reshape(1, d), wg_bf)


def _layer(h, p_i, w_in, ret_gn_w, conv_w, conv_b, lru_wa, lru_ba, lru_wx, lru_bx, lru_lam, w_out,
           ln1_w, ln1_b, w_router, b_router, w_gate, b_gate, w_up, b_up, w_down, b_down,
           ln2_w, ln2_b, w_ple_proj, ple_norm_w, w_ple_gate):
    b, s, d = h.shape
    t = b * s
    x2d = h.reshape(t, d)
    proj = _in_proj(x2d, w_in).reshape(b, s, IN_COLS)
    ret_out = _retention(proj, ret_gn_w)
    lru_out = _lru(proj, conv_w, conv_b, lru_wa, lru_ba, lru_wx, lru_bx, lru_lam)
    x1, x1s, top_e, gates = _out_router(ret_out.reshape(t, RET_WIDTH), lru_out.reshape(t, LRU_WIDTH),
                                        w_out.astype(bf16), x2d, ln1_w, ln1_b, w_router, b_router)
    dest, slot_tok, ue, u_row, u_chunks, n_used_rows, n_pad, max_units = _routing_tables(top_e, t)
    y = _moe_experts(x1s, slot_tok, ue, u_row, u_chunks, n_used_rows, n_pad, max_units, w_gate, b_gate, w_up, b_up,
                     w_down, b_down)
    out = _combine(y, dest, gates, x1, p_i.reshape(t, PLE_DIM), ln2_w, ln2_b, w_ple_proj.astype(bf16),
                   ple_norm_w, w_ple_gate.astype(bf16))
    return out.reshape(b, s, d)


def kernel(x, p, w_in, ret_gn_w, conv_w, conv_b, lru_wa, lru_ba, lru_wx, lru_bx, lru_lam, w_out, ln1_w, ln1_b,
           w_router, b_router, w_gate, b_gate, w_up, b_up, w_down, b_down, ln2_w, ln2_b, w_ple_proj, ple_norm_w,
           w_ple_gate):
    h = x.astype(f32)
    for i in range(w_in.shape[0]):
        h = _layer(h, p[i], w_in[i], ret_gn_w[i], conv_w[i], conv_b[i], lru_wa[i], lru_ba[i], lru_wx[i],
                   lru_bx[i], lru_lam[i], w_out[i], ln1_w[i], ln1_b[i], w_router[i], b_router[i], w_gate[i],
                   b_gate[i], w_up[i], b_up[i], w_down[i], b_down[i], ln2_w[i], ln2_b[i], w_ple_proj[i],
                   ple_norm_w[i], w_ple_gate[i])
    return h.astype(x.dtype)
```

```python
import functools
import math

import jax
import jax.numpy as jnp
from jax import lax
from jax.experimental import pallas as pl
from jax.experimental.pallas import tpu as pltpu

D_MODEL = 2048
RET_HEAD_DIM = 128
RET_HEADS = 8
RET_WIDTH = RET_HEADS * RET_HEAD_DIM
LRU_WIDTH = D_MODEL - RET_WIDTH
LRU_BLOCKS = 8
LRU_BLOCK_DIM = LRU_WIDTH // LRU_BLOCKS
IN_COLS = 4 * RET_WIDTH + 2 * LRU_WIDTH
CONV_WIDTH = 4
LRU_C = 8.0
CHUNK = 128
ROPE_BASE = 10000.0
N_EXPERTS = 32
TOP_K = 4
SWIGLU_LIMIT = 7.0
SWIGLU_ALPHA = 1.702
PLE_DIM = 256
LN_EPS = 1e-5
DEPTH = 1
DN_ALPHA = (2.0 * DEPTH) ** 0.25

LANES = 128
SUBLANES = 8
VMEM_LIMIT = 60 * 1024 * 1024

ROW_CHUNK = 256
UNIT_ROWS = 2048
F_TILE = 256

f32 = jnp.float32
bf16 = jnp.bfloat16


def _cparams(sem):
    return pltpu.CompilerParams(dimension_semantics=sem, vmem_limit_bytes=VMEM_LIMIT)


def _in_proj_kernel(x_ref, w_ref, o_ref, xb_ref):
    @pl.when(pl.program_id(1) == 0)
    def _():
        xb_ref[...] = x_ref[...].astype(bf16)

    o_ref[...] = lax.dot_general(xb_ref[...], w_ref[...], (((1,), (0,)), ((), ())), preferred_element_type=f32)


def _in_proj(x2d, w_bf):
    t, d = x2d.shape
    n = w_bf.shape[1]
    tm, tn = 1024, 1024
    return pl.pallas_call(
        _in_proj_kernel,
        grid=(t // tm, n // tn),
        in_specs=[pl.BlockSpec((tm, d), lambda i, j: (i, 0)),
                  pl.BlockSpec((d, tn), lambda i, j: (0, j))],
        out_specs=pl.BlockSpec((tm, tn), lambda i, j: (i, j)),
        out_shape=jax.ShapeDtypeStruct((t, n), f32),
        scratch_shapes=[pltpu.VMEM((tm, d), bf16)],
        compiler_params=_cparams(("parallel", "arbitrary")),
        name="in_proj",
    )(x2d, w_bf)


def _retention_kernel(q_ref, k_ref, v_ref, g_ref, cos_ref, sin_ref, dec_ref, qd_ref, kd_ref, cd_ref, gnw_ref,
                      o_ref):
    s = q_ref.shape[0]
    n_chunks = s // CHUNK
    decay = dec_ref[...]
    q_dec = qd_ref[...]
    k_dec = kd_ref[...]
    c_dec = cd_ref[0:1, :]
    gnw = gnw_ref[...]
    k_scale = RET_HEAD_DIM ** -0.5

    def rope(xv, cos, sin):
        return xv * cos + pltpu.roll(xv, RET_HEAD_DIM // 2, axis=1) * sin

    def body(n, state):
        sl = pl.ds(pl.multiple_of(n * CHUNK, CHUNK), CHUNK)
        cos = cos_ref[sl, :]
        sin = sin_ref[sl, :]
        q = rope(q_ref[sl, :], cos, sin)
        k = rope(k_ref[sl, :], cos, sin) * k_scale
        vb = v_ref[sl, :].astype(bf16)
        scores = lax.dot_general(q.astype(bf16), k.astype(bf16), (((1,), (1,)), ((), ())),
                                 preferred_element_type=f32) * decay
        intra = jnp.dot(scores.astype(bf16), vb, preferred_element_type=f32)
        cross = jnp.dot((q * q_dec).astype(bf16), state.astype(bf16), preferred_element_type=f32)
        kv = lax.dot_general((k * k_dec).astype(bf16), vb, (((0,), (0,)), ((), ())),
                             preferred_element_type=f32)
        ret = intra + cross
        mu = jnp.mean(ret, axis=-1, keepdims=True)
        cen = ret - mu
        var = jnp.mean(cen * cen, axis=-1, keepdims=True)
        ret = cen * lax.rsqrt(var + LN_EPS) * gnw
        g = g_ref[sl, :]
        o_ref[sl, :] = (g * jax.nn.sigmoid(g) * ret).astype(o_ref.dtype)
        return c_dec * state + kv

    lax.fori_loop(0, n_chunks, body, jnp.zeros((RET_HEAD_DIM, RET_HEAD_DIM), f32), unroll=4)


def _retention_tables(s):
    h, d = RET_HEADS, RET_HEAD_DIM
    inv = ROPE_BASE ** (-jnp.arange(0, d, 2, dtype=f32) / d)
    ang = jnp.arange(s, dtype=f32)[:, None] * inv[None, :]
    cos = jnp.cos(ang)
    sin = jnp.sin(ang)
    cos_t = jnp.concatenate([cos, cos], axis=-1)
    sin_t = jnp.concatenate([-sin, sin], axis=-1)
    log_gamma = jnp.log1p(-jnp.exp2(-5.0 - jnp.arange(h, dtype=f32)))
    idx = jnp.arange(CHUNK, dtype=f32)
    diff = idx[:, None] - idx[None, :]
    decay = jnp.where((diff >= 0)[None], jnp.exp(jnp.maximum(diff, 0.0)[None] * log_gamma[:, None, None]), 0.0)
    q_dec = jnp.exp((idx[None, :] + 1.0) * log_gamma[:, None])
    k_dec = jnp.exp((CHUNK - 1.0 - idx)[None, :] * log_gamma[:, None])
    c_dec = jnp.exp(CHUNK * log_gamma)
    q_dec = jnp.broadcast_to(q_dec[:, :, None], (h, CHUNK, d))
    k_dec = jnp.broadcast_to(k_dec[:, :, None], (h, CHUNK, d))
    c_dec = jnp.broadcast_to(c_dec[:, None, None], (h, SUBLANES, d))
    return cos_t, sin_t, decay, q_dec, k_dec, c_dec


def _retention(proj, ret_gn_w):
    b, s, _ = proj.shape
    d = RET_HEAD_DIM
    cos_t, sin_t, decay, q_dec, k_dec, c_dec = _retention_tables(s)
    col = lambda off: pl.BlockSpec((None, s, d), lambda bi, hi: (bi, 0, off + hi))
    per_head = lambda r: pl.BlockSpec((None, r, d), lambda bi, hi: (hi, 0, 0))
    full = lambda shp: pl.BlockSpec(shp, lambda bi, hi: (0,) * len(shp))
    return pl.pallas_call(
        _retention_kernel,
        grid=(b, RET_HEADS),
        in_specs=[col(0), col(RET_HEADS), col(2 * RET_HEADS), col(3 * RET_HEADS),
                  full((s, d)), full((s, d)),
                  per_head(CHUNK), per_head(CHUNK), per_head(CHUNK), per_head(SUBLANES),
                  pl.BlockSpec((1, d), lambda bi, hi: (0, hi))],
        out_specs=pl.BlockSpec((None, s, d), lambda bi, hi: (bi, 0, hi)),
        out_shape=jax.ShapeDtypeStruct((b, s, RET_WIDTH), bf16),
        compiler_params=_cparams(("parallel", "parallel")),
        name="retention",
    )(proj, proj, proj, proj, cos_t, sin_t, decay, q_dec, k_dec, c_dec, ret_gn_w.reshape(1, RET_WIDTH))


def _gelu_tanh(x):
    return 0.5 * x * (1.0 + jnp.tanh(math.sqrt(2.0 / math.pi) * (x + 0.044715 * (x * x * x))))


def _lru_kernel(xr_ref, yg_ref, cw_ref, cb_ref, wa_ref, ba_ref, wx_ref, bx_ref, lam_ref, o_ref, a_ref, b_ref):
    s = xr_ref.shape[0]
    x = xr_ref[...]
    rows = lax.broadcasted_iota(jnp.int32, x.shape, 0)
    xc = cb_ref[...] + cw_ref[CONV_WIDTH - 1:CONV_WIDTH, :] * x
    for back in range(1, CONV_WIDTH):
        shifted = jnp.where(rows >= back, pltpu.roll(x, back, axis=0), 0.0)
        xc = xc + cw_ref[CONV_WIDTH - 1 - back:CONV_WIDTH - back, :] * shifted
    xcb = xc.astype(bf16)
    r = jax.nn.sigmoid(jnp.dot(xcb, wa_ref[...].astype(bf16), preferred_element_type=f32) + ba_ref[...])
    gi = jax.nn.sigmoid(jnp.dot(xcb, wx_ref[...].astype(bf16), preferred_element_type=f32) + bx_ref[...])
    lam = lam_ref[...]
    log_sig = jnp.minimum(lam, 0.0) - jnp.log1p(jnp.exp(-jnp.abs(lam)))
    log_a = LRU_C * r * log_sig
    a = jnp.exp(log_a)
    a_ref[...] = a
    b_ref[...] = jnp.sqrt(-jnp.tanh(log_a) * (a * a + 1.0)) * (gi * xc)

    row8 = lax.broadcasted_iota(jnp.int32, (SUBLANES, LANES), 0)

    def body(i, h_prev):
        sl = pl.ds(pl.multiple_of(i * SUBLANES, SUBLANES), SUBLANES)
        a8 = a_ref[sl, :]
        b8 = b_ref[sl, :]
        for sh in (1, 2, 4):
            a_sh = jnp.where(row8 >= sh, pltpu.roll(a8, sh, axis=0), 1.0)
            b_sh = jnp.where(row8 >= sh, pltpu.roll(b8, sh, axis=0), 0.0)
            b8 = a8 * b_sh + b8
            a8 = a8 * a_sh
        h8 = a8 * h_prev + b8
        o_ref[sl, :] = (_gelu_tanh(yg_ref[sl, :]) * h8).astype(o_ref.dtype)
        return h8[SUBLANES - 1:SUBLANES, :]

    lax.fori_loop(0, s // SUBLANES, body, jnp.zeros((1, LANES), f32), unroll=8)


def _lru(proj, conv_w, conv_b, wa, ba, wx, bx, lam):
    b, s, _ = proj.shape
    d = LRU_BLOCK_DIM
    xr_off = 4 * RET_WIDTH // d
    yg_off = xr_off + LRU_BLOCKS
    col = lambda off: pl.BlockSpec((None, s, d), lambda bi, ji: (bi, 0, off + ji))
    vec = lambda r: pl.BlockSpec((r, d), lambda bi, ji: (0, ji))
    blk = lambda r: pl.BlockSpec((None, r, d), lambda bi, ji: (ji, 0, 0))
    return pl.pallas_call(
        _lru_kernel,
        grid=(b, LRU_BLOCKS),
        in_specs=[col(xr_off), col(yg_off), vec(CONV_WIDTH), vec(1), blk(d), blk(1), blk(d), blk(1), vec(1)],
        out_specs=pl.BlockSpec((None, s, d), lambda bi, ji: (bi, 0, ji)),
        out_shape=jax.ShapeDtypeStruct((b, s, LRU_WIDTH), bf16),
        scratch_shapes=[pltpu.VMEM((s, d), f32), pltpu.VMEM((s, d), f32)],
        compiler_params=_cparams(("parallel", "parallel")),
        name="rg_lru",
    )(proj, proj, conv_w, conv_b.reshape(1, LRU_WIDTH), wa, ba.reshape(LRU_BLOCKS, 1, d), wx,
      bx.reshape(LRU_BLOCKS, 1, d), lam.reshape(1, LRU_WIDTH))


def _layer_norm(y, w, b):
    mu = jnp.mean(y, axis=-1, keepdims=True)
    cen = y - mu
    var = jnp.mean(cen * cen, axis=-1, keepdims=True)
    return cen * lax.rsqrt(var + LN_EPS) * w + b


HALF = D_MODEL // 2
SLAB = HALF // LANES
PITCH = SLAB + 4
u32 = jnp.uint32
HIGH_MASK = 0xFFFF0000


def _bf16_bits(v):
    return lax.bitcast_convert_type(v.astype(bf16).astype(f32), u32)


def _store_slabs(ref, val):
    n = val.shape[0]
    for j in range(SLAB):
        lo = _bf16_bits(val[:, j * LANES:(j + 1) * LANES])
        hi = _bf16_bits(val[:, HALF + j * LANES:HALF + (j + 1) * LANES])
        ref[pl.ds(j, n, stride=PITCH), :] = hi | lax.shift_right_logical(lo, jnp.full_like(lo, 16))
    for j in range(SLAB, PITCH):
        ref[pl.ds(j, n, stride=PITCH), :] = jnp.zeros((n, LANES), u32)


def _slab_cols(ref, j, n):
    w = ref[pl.ds(j, n, stride=PITCH), :]
    lo = lax.bitcast_convert_type(lax.shift_left(w, jnp.full_like(w, 16)), f32)
    hi = lax.bitcast_convert_type(w & jnp.full_like(w, HIGH_MASK), f32)
    return lo, hi


def _split_bf16(v):
    hi = v.astype(bf16)
    lo = (v - hi.astype(f32)).astype(bf16)
    return hi, lo


def _out_router_kernel(ret_ref, lru_ref, wo_ref, x_ref, lnw_ref, lnb_ref, wr_ref, br_ref,
                       x1_ref, x1s_ref, tope_ref, gate_ref):
    m = jnp.dot(ret_ref[...], wo_ref[0:RET_WIDTH, :], preferred_element_type=f32)
    m = m + jnp.dot(lru_ref[...], wo_ref[RET_WIDTH:D_MODEL, :], preferred_element_type=f32)
    x1 = _layer_norm(DN_ALPHA * x_ref[...] + m, lnw_ref[...], lnb_ref[...])
    x1_ref[...] = x1
    _store_slabs(x1s_ref, x1)
    xh, xl = _split_bf16(x1)
    wh, wl = _split_bf16(wr_ref[...])
    logits = (jnp.dot(xh, wh, preferred_element_type=f32) + jnp.dot(xl, wh, preferred_element_type=f32)
              + jnp.dot(xh, wl, preferred_element_type=f32)) + br_ref[...]
    tm = logits.shape[0]
    lane = lax.broadcasted_iota(jnp.int32, logits.shape, 1)
    lane_k = lax.broadcasted_iota(jnp.int32, (tm, TOP_K), 1)
    top_e = jnp.zeros((tm, TOP_K), jnp.int32)
    top_v = jnp.zeros((tm, TOP_K), f32)
    cur = logits
    for kk in range(TOP_K):
        mx = jnp.max(cur, axis=-1, keepdims=True)
        idx = jnp.min(jnp.where(cur == mx, lane, N_EXPERTS), axis=-1, keepdims=True)
        top_e = jnp.where(lane_k == kk, idx, top_e)
        top_v = jnp.where(lane_k == kk, mx, top_v)
        cur = jnp.where(lane == idx, -jnp.inf, cur)
    ex = jnp.exp(top_v - top_v[:, 0:1])
    gate_ref[...] = ex / jnp.sum(ex, axis=-1, keepdims=True)
    tope_ref[...] = top_e


def _out_router(ret_out, lru_out, wo_bf, x2d, ln_w, ln_b, w_router, b_router):
    t, d = x2d.shape
    tm = 256
    row = lambda c: pl.BlockSpec((tm, c), lambda i: (i, 0))
    full = lambda r, c: pl.BlockSpec((r, c), lambda i: (0, 0))
    return pl.pallas_call(
        _out_router_kernel,
        grid=(t // tm,),
        in_specs=[row(RET_WIDTH), row(LRU_WIDTH), full(d, d), row(d), full(1, d), full(1, d),
                  full(d, N_EXPERTS), full(1, N_EXPERTS)],
        out_specs=[row(d), pl.BlockSpec((tm * PITCH, LANES), lambda i: (i, 0)), row(TOP_K), row(TOP_K)],
        out_shape=[jax.ShapeDtypeStruct((t, d), f32), jax.ShapeDtypeStruct((t * PITCH, LANES), u32),
                   jax.ShapeDtypeStruct((t, TOP_K), jnp.int32),
                   jax.ShapeDtypeStruct((t, TOP_K), f32)],
        compiler_params=_cparams(("parallel",)),
        name="out_proj_ln1_router",
    )(ret_out, lru_out, wo_bf, x2d, ln_w.reshape(1, d), ln_b.reshape(1, d), w_router,
      b_router.reshape(1, N_EXPERTS))


def _routing_tables(top_e, t):
    n_pad = t * TOP_K + N_EXPERTS * ROW_CHUNK
    max_units = N_EXPERTS + (t * TOP_K) // UNIT_ROWS
    sel = (top_e[:, :, None] == jnp.arange(N_EXPERTS, dtype=jnp.int32)[None, None, :]).any(axis=1)
    sel = sel.astype(jnp.int32)
    counts = jnp.sum(sel, axis=0)
    rank = jnp.cumsum(sel, axis=0) - sel
    padded = (counts + ROW_CHUNK - 1) // ROW_CHUNK * ROW_CHUNK
    pad_ends = jnp.cumsum(padded)
    pad_starts = pad_ends - padded
    dest_dense = pad_starts[None, :] + rank
    dest = jnp.take_along_axis(dest_dense, top_e, axis=1)
    tok = jnp.broadcast_to(jnp.arange(t, dtype=jnp.int32)[:, None], (t, TOP_K))
    slot_tok = jnp.zeros((n_pad + UNIT_ROWS,), jnp.int32).at[dest.reshape(-1)].set(tok.reshape(-1) * PITCH)
    units_per_e = (padded + UNIT_ROWS - 1) // UNIT_ROWS
    unit_ends = jnp.cumsum(units_per_e)
    unit_starts = unit_ends - units_per_e
    n_units = unit_ends[-1]
    u = jnp.arange(max_units, dtype=jnp.int32)
    u_clamped = jnp.minimum(u, n_units - 1)
    ue = jnp.searchsorted(unit_ends, u_clamped, side='right').astype(jnp.int32)
    ue = jnp.minimum(ue, N_EXPERTS - 1)
    j = u_clamped - unit_starts[ue]
    u_row = pad_starts[ue] + j * UNIT_ROWS
    u_rows = jnp.minimum(UNIT_ROWS, padded[ue] - j * UNIT_ROWS)
    u_chunks = jnp.where(u < n_units, u_rows // ROW_CHUNK, 0).astype(jnp.int32)
    n_used_rows = pad_ends[-1].astype(jnp.int32)
    return dest.astype(jnp.int32), slot_tok, ue, u_row.astype(jnp.int32), u_chunks, n_used_rows, n_pad, max_units


CHUNK_PITCHED = ROW_CHUNK * PITCH
OUT_SLOTS = 4


def _moe_kernel(ue_ref, urow_ref, uchunks_ref, used_ref, tok_hbm, x1s_hbm, wg_ref, bg_ref, wu_ref, bu_ref, wd_ref,
                bd_ref, y_hbm, xbuf, yacc, gstage, ostage, tok_smem, flags, tok_sem, in_sem, out_sem):
    u = pl.program_id(0)
    f = pl.program_id(1)
    n_u = pl.num_programs(0)
    n_f = pl.num_programs(1)
    n_chunks = uchunks_ref[u]
    row0 = urow_ref[u]
    cur = u % 2
    nxt = 1 - cur
    u_next = jnp.minimum(u + 1, n_u - 1)
    n_next = jnp.where(u + 1 < n_u, uchunks_ref[u_next], 0)
    u_next2 = jnp.minimum(u + 2, n_u - 1)
    n_next2 = jnp.where(u + 2 < n_u, uchunks_ref[u_next2], 0)
    TABLES = 3
    tcur = u % TABLES
    tnxt = (u + 1) % TABLES
    first_step = jnp.logical_and(u == 0, f == 0)
    PENDING, PEND_CHUNK, PEND_SLOT, OUT_BUSY = 0, 1, 2, 3

    def rows(c, size=ROW_CHUNK):
        return pl.ds(pl.multiple_of(c * size, size), size)

    def tok_copy(unit_row, slot):
        src = tok_hbm.at[pl.ds(pl.multiple_of(unit_row, ROW_CHUNK), UNIT_ROWS)]
        dst = tok_smem.at[pl.ds(pl.multiple_of(slot * UNIT_ROWS, UNIT_ROWS), UNIT_ROWS)]
        return pltpu.make_async_copy(src, dst, tok_sem)

    ISSUE_GROUP = 8

    def gather_start(c, tslot, lo=0, hi=ROW_CHUNK):
        base = tslot * UNIT_ROWS + c * ROW_CHUNK

        def issue(g, carry):
            for i in range(ISSUE_GROUP):
                r = g * ISSUE_GROUP + i
                src = x1s_hbm.at[pl.ds(tok_smem[base + r], SLAB), :]
                dst = gstage.at[pl.ds(r * PITCH, SLAB), :]
                pltpu.make_async_copy(src, dst, in_sem).start()
            return carry

        lax.fori_loop(lo // ISSUE_GROUP, hi // ISSUE_GROUP, issue, 0)

    def gather_finish(c, xslot):
        n = ROW_CHUNK * SLAB
        pltpu.make_async_copy(x1s_hbm.at[pl.ds(0, n), :], gstage.at[pl.ds(0, n), :], in_sem).wait()
        for j in range(SLAB):
            lo, hi = _slab_cols(gstage, j, ROW_CHUNK)
            xbuf[xslot, rows(c), j * LANES:(j + 1) * LANES] = lo.astype(bf16)
            xbuf[xslot, rows(c), HALF + j * LANES:HALF + (j + 1) * LANES] = hi.astype(bf16)

    def out_copy(c, slot):
        dst = y_hbm.at[pl.ds(pl.multiple_of((row0 + c * ROW_CHUNK) * PITCH, CHUNK_PITCHED), CHUNK_PITCHED), :]
        return pltpu.make_async_copy(ostage.at[slot], dst, out_sem.at[slot])

    def out_wait(slot):
        @pl.when(flags[OUT_BUSY + slot] == 1)
        def _():
            pltpu.make_async_copy(ostage.at[slot], y_hbm.at[pl.ds(0, CHUNK_PITCHED), :], out_sem.at[slot]).wait()
            flags[OUT_BUSY + slot] = 0

    @pl.when(first_step)
    def _():
        for i in range(OUT_BUSY + OUT_SLOTS):
            flags[i] = 0

        @pl.when(n_chunks > 0)
        def _():
            first_table = tok_copy(row0, tcur)
            first_table.start()
            first_table.wait()

            def load(c, carry):
                gather_start(c, tcur)
                gather_finish(c, cur)
                return carry

            lax.fori_loop(0, n_chunks, load, 0)

        @pl.when(n_next > 0)
        def _():
            tok_copy(urow_ref[u_next], tnxt).start()

    @pl.when(flags[PENDING] == 1)
    def _():
        gather_finish(flags[PEND_CHUNK], flags[PEND_SLOT])
        flags[PENDING] = 0

    @pl.when(jnp.logical_and(f == 0, n_next > 0))
    def _():
        tok_copy(urow_ref[u_next], tnxt).wait()

        @pl.when(n_next2 > 0)
        def _():
            tok_copy(urow_ref[u_next2], (u + 2) % TABLES).start()

    prefetching = f < n_next
    n_quads = n_chunks // 4
    has_pair = (n_chunks % 4) // 2
    has_single = n_chunks % 2
    n_blocks = n_quads + has_pair + has_single
    share = (ROW_CHUNK // jnp.maximum(n_blocks, 1) + ISSUE_GROUP - 1) // ISSUE_GROUP * ISSUE_GROUP

    def prefetch_share(i):
        lo = jnp.minimum(i * share, ROW_CHUNK)
        hi = jnp.where(i == n_blocks - 1, ROW_CHUNK, jnp.minimum(lo + share, ROW_CHUNK))
        gather_start(f, tnxt, lo, jnp.where(prefetching, hi, lo))

    @pl.when(prefetching)
    def _():
        flags[PENDING] = 1
        flags[PEND_CHUNK] = f
        flags[PEND_SLOT] = nxt

    @pl.when(n_chunks > 0)
    def _():
        bg = bg_ref[...]
        bu = bu_ref[...]
        mm = lambda a, w: lax.dot_general(a, w, (((1,), (0,)), ((), ())), preferred_element_type=f32)

        def mlp(sl, first):
            xc = xbuf[cur, sl, :]
            gt = jnp.minimum(mm(xc, wg_ref[...]) + bg, SWIGLU_LIMIT)
            up = jnp.clip(mm(xc, wu_ref[...]) + bu, -SWIGLU_LIMIT, SWIGLU_LIMIT)
            hid = (up + 1.0) * gt * jax.nn.sigmoid(SWIGLU_ALPHA * gt)
            part = mm(hid.astype(bf16), wd_ref[...])
            if first:
                yacc[sl, :] = part
            else:
                yacc[sl, :] += part

        def blocks(first):
            def quad(c, carry):
                prefetch_share(c)
                mlp(rows(c, 4 * ROW_CHUNK), first)
                return carry

            lax.fori_loop(0, n_quads, quad, 0)

            @pl.when(has_pair == 1)
            def _():
                prefetch_share(n_quads)
                mlp(pl.ds(pl.multiple_of(n_quads * 4 * ROW_CHUNK, 2 * ROW_CHUNK), 2 * ROW_CHUNK), first)

            @pl.when(has_single == 1)
            def _():
                prefetch_share(n_quads + has_pair)
                mlp(rows(n_chunks - 1), first)

        @pl.when(f == 0)
        def _():
            blocks(True)

        @pl.when(f > 0)
        def _():
            blocks(False)

    @pl.when(jnp.logical_and(f == n_f - 1, n_chunks > 0))
    def _():
        bd = bd_ref[...]

        def store(c, carry):
            slot = c % OUT_SLOTS
            out_wait(slot)
            _store_slabs(ostage.at[slot], yacc[rows(c), :] + bd)
            out_copy(c, slot).start()
            flags[OUT_BUSY + slot] = 1
            return carry

        lax.fori_loop(0, n_chunks, store, 0)

    @pl.when(jnp.logical_and(u == n_u - 1, f == n_f - 1))
    def _():
        for s in range(OUT_SLOTS):
            out_wait(s)
        first = used_ref[0] // ROW_CHUNK
        last = y_hbm.shape[0] // CHUNK_PITCHED
        ostage[0] = jnp.zeros((CHUNK_PITCHED, LANES), u32)

        def tail_copy(c):
            dst = y_hbm.at[pl.ds(pl.multiple_of(c * CHUNK_PITCHED, CHUNK_PITCHED), CHUNK_PITCHED), :]
            return pltpu.make_async_copy(ostage.at[0], dst, out_sem.at[0])

        def start(c, carry):
            tail_copy(c).start()
            return carry

        def wait(c, carry):
            tail_copy(c).wait()
            return carry

        lax.fori_loop(first, last, start, 0)
        lax.fori_loop(first, last, wait, 0)


def _moe_experts(x1s, slot_tok, ue, u_row, u_chunks, n_used_rows, n_pad, max_units, w_gate, b_gate, w_up, b_up,
                 w_down, b_down):
    e, d, dff = w_gate.shape
    n_f = dff // F_TILE
    assert n_f >= UNIT_ROWS // ROW_CHUNK, "one chunk of the next unit is gathered per f-tile step"
    f_idx = lambda u, f, uc: jnp.where(uc[u] > 0, f, n_f - 1)
    col_w = pl.BlockSpec((None, d, F_TILE), lambda u, f, ue, ur, uc, used: (ue[u], 0, f_idx(u, f, uc)))
    col_b = pl.BlockSpec((None, 1, F_TILE), lambda u, f, ue, ur, uc, used: (ue[u], 0, f_idx(u, f, uc)))
    return pl.pallas_call(
        _moe_kernel,
        grid_spec=pltpu.PrefetchScalarGridSpec(
            num_scalar_prefetch=4,
            grid=(max_units, n_f),
            in_specs=[pl.BlockSpec(memory_space=pl.ANY), pl.BlockSpec(memory_space=pl.ANY),
                      col_w, col_b, col_w, col_b,
                      pl.BlockSpec((None, F_TILE, d), lambda u, f, ue, ur, uc, used: (ue[u], f_idx(u, f, uc), 0)),
                      pl.BlockSpec((None, 1, d), lambda u, f, ue, ur, uc, used: (ue[u], 0, 0))],
            out_specs=pl.BlockSpec(memory_space=pl.ANY),
            scratch_shapes=[pltpu.VMEM((2, UNIT_ROWS, d), bf16),
                            pltpu.VMEM((UNIT_ROWS, d), f32),
                            pltpu.VMEM((CHUNK_PITCHED, LANES), u32),
                            pltpu.VMEM((OUT_SLOTS, CHUNK_PITCHED, LANES), u32),
                            pltpu.SMEM((3 * UNIT_ROWS,), jnp.int32),
                            pltpu.SMEM((3 + OUT_SLOTS,), jnp.int32),
                            pltpu.SemaphoreType.DMA(()),
                            pltpu.SemaphoreType.DMA(()),
                            pltpu.SemaphoreType.DMA((OUT_SLOTS,))],
        ),
        out_shape=jax.ShapeDtypeStruct((n_pad * PITCH, LANES), u32),
        compiler_params=_cparams(("arbitrary", "arbitrary")),
        name="moe_experts",
    )(ue, u_row, u_chunks, n_used_rows.reshape(1), slot_tok, x1s, w_gate, b_gate.reshape(e, 1, dff), w_up,
      b_up.reshape(e, 1, dff), w_down, b_down.reshape(e, 1, d))


COMBINE_ROWS = 256
COMBINE_GROUPS = 2


def _combine_kernel(dest_ref, y_hbm, gate_ref, x1_ref, p_ref, lnw_ref, lnb_ref, wp_ref, pnw_ref, wg_ref,
                    o_ref, ybuf, fsum_ref, sem):
    tm = COMBINE_ROWS

    def issue(t, c):
        for k in range(TOP_K):
            src = y_hbm.at[pl.ds(dest_ref[t * TOP_K + k] * PITCH, SLAB), :]
            dst = ybuf.at[k, pl.ds(t * PITCH, SLAB), :]
            pltpu.make_async_copy(src, dst, sem).start()
        return c

    lax.fori_loop(0, tm, issue, 0, unroll=2)
    for k in range(TOP_K):
        pltpu.make_async_copy(y_hbm.at[pl.ds(0, tm * SLAB), :], ybuf.at[k, pl.ds(0, tm * SLAB), :], sem).wait()

    n = tm // COMBINE_GROUPS
    for h in range(COMBINE_GROUPS):
        tok = pl.ds(h * n, n)
        e = jnp.dot(p_ref[tok, :].astype(bf16), wp_ref[...], preferred_element_type=f32)
        e = e * lax.rsqrt(jnp.mean(e * e, axis=-1, keepdims=True) + LN_EPS) * pnw_ref[...]
        gates = gate_ref[tok, :]
        for j in range(SLAB):
            acc_lo, acc_hi = None, None
            for k in range(TOP_K):
                lo, hi = _slab_cols(ybuf.at[k, pl.ds(h * n * PITCH, n * PITCH), :], j, n)
                g = gates[:, k:k + 1]
                acc_lo = g * lo if k == 0 else acc_lo + g * lo
                acc_hi = g * hi if k == 0 else acc_hi + g * hi
            fsum_ref[tok, j * LANES:(j + 1) * LANES] = acc_lo
            fsum_ref[tok, HALF + j * LANES:HALF + (j + 1) * LANES] = acc_hi
        x2 = _layer_norm(DN_ALPHA * x1_ref[tok, :] + fsum_ref[tok, :], lnw_ref[...], lnb_ref[...])
        gate = jax.nn.sigmoid(jnp.dot(x2.astype(bf16), wg_ref[...], preferred_element_type=f32))
        o_ref[tok, :] = x2 + gate * e


def _combine(y, dest, gates, x1, p2d, ln_w, ln_b, wp_bf, ple_norm_w, wg_bf):
    t, d = x1.shape
    tm = COMBINE_ROWS
    row = lambda c: pl.BlockSpec((tm, c), lambda i: (i, 0))
    full = lambda r, c: pl.BlockSpec((r, c), lambda i: (0, 0))
    return pl.pallas_call(
        _combine_kernel,
        grid=(t // tm,),
        in_specs=[pl.BlockSpec((tm * TOP_K,), lambda i: (i,), memory_space=pltpu.SMEM),
                  pl.BlockSpec(memory_space=pl.ANY),
                  row(TOP_K), row(d), row(PLE_DIM), full(1, d), full(1, d), full(PLE_DIM, d), full(1, d),
                  full(d, d)],
        out_specs=row(d),
        out_shape=jax.ShapeDtypeStruct((t, d), f32),
        scratch_shapes=[pltpu.VMEM((TOP_K, tm * PITCH, LANES), u32), pltpu.VMEM((tm, d), f32),
                        pltpu.SemaphoreType.DMA(())],
        compiler_params=_cparams(("arbitrary",)),
        name="combine_ln2_ple",
    )(dest.reshape(-1), y, gates, x1, p2d, ln_w.reshape(1, d), ln_b.reshape(1, d), wp_bf,
      ple_norm_w.reshape(1, d), wg_bf)


def _layer(h, p_i, w_in, ret_gn_w, conv_w, conv_b, lru_wa, lru_ba, lru_wx, lru_bx, lru_lam, w_out,
           ln1_w, ln1_b, w_router, b_router, w_gate, b_gate, w_up, b_up, w_down, b_down,
           ln2_w, ln2_b, w_ple_proj, ple_norm_w, w_ple_gate):
    b, s, d = h.shape
    t = b * s
    x2d = h.reshape(t, d)
    proj = _in_proj(x2d, w_in).reshape(b, s, IN_COLS)
    ret_out = _retention(proj, ret_gn_w)
    lru_out = _lru(proj, conv_w, conv_b, lru_wa, lru_ba, lru_wx, lru_bx, lru_lam)
    x1, x1s, top_e, gates = _out_router(ret_out.reshape(t, RET_WIDTH), lru_out.reshape(t, LRU_WIDTH),
                                        w_out.astype(bf16), x2d, ln1_w, ln1_b, w_router, b_router)
    dest, slot_tok, ue, u_row, u_chunks, n_used_rows, n_pad, max_units = _routing_tables(top_e, t)
    y = _moe_experts(x1s, slot_tok, ue, u_row, u_chunks, n_used_rows, n_pad, max_units, w_gate, b_gate, w_up, b_up,
                     w_down, b_down)
    out = _combine(y, dest, gates, x1, p_i.reshape(t, PLE_DIM), ln2_w, ln2_b, w_ple_proj.astype(bf16),
                   ple_norm_w, w_ple_gate.astype(bf16))
    return out.reshape(b, s, d)


def kernel(x, p, w_in, ret_gn_w, conv_w, conv_b, lru_wa, lru_ba, lru_wx, lru_bx, lru_lam, w_out, ln1_w, ln1_b,
           w_router, b_router, w_gate, b_gate, w_up, b_up, w_down, b_down, ln2_w, ln2_b, w_ple_proj, ple_norm_w,
           w_ple_gate):
    h = x.astype(f32)
    for i in range(w_in.shape[0]):
        h = _layer(h, p[i], w_in[i], ret_gn_w[i], conv_w[i], conv_b[i], lru_wa[i], lru_ba[i], lru_wx[i],
                   lru_bx[i], lru_lam[i], w_out[i], ln1_w[i], ln1_b[i], w_router[i], b_router[i], w_gate[i],
                   b_gate[i], w_up[i], b_up[i], w_down[i], b_down[i], ln2_w[i], ln2_b[i], w_ple_proj[i],
                   ple_norm_w[i], w_ple_gate[i])
    return h.astype(x.dtype)
```

```python
import functools
import math

import jax
import jax.numpy as jnp
from jax import lax
from jax.experimental import pallas as pl
from jax.experimental.pallas import tpu as pltpu

D_MODEL = 2048
RET_HEAD_DIM = 128
RET_HEADS = 8
RET_WIDTH = RET_HEADS * RET_HEAD_DIM
LRU_WIDTH = D_MODEL - RET_WIDTH
LRU_BLOCKS = 8
LRU_BLOCK_DIM = LRU_WIDTH // LRU_BLOCKS
IN_COLS = 4 * RET_WIDTH + 2 * LRU_WIDTH
CONV_WIDTH = 4
LRU_C = 8.0
CHUNK = 128
ROPE_BASE = 10000.0
N_EXPERTS = 32
TOP_K = 4
SWIGLU_LIMIT = 7.0
SWIGLU_ALPHA = 1.702
PLE_DIM = 256
LN_EPS = 1e-5
DEPTH = 1
DN_ALPHA = (2.0 * DEPTH) ** 0.25

LANES = 128
SUBLANES = 8
VMEM_LIMIT = 60 * 1024 * 1024

ROW_CHUNK = 256
UNIT_ROWS = 2048
F_TILE = 256

f32 = jnp.float32
bf16 = jnp.bfloat16


def _cparams(sem):
    return pltpu.CompilerParams(dimension_semantics=sem, vmem_limit_bytes=VMEM_LIMIT)


def _in_proj_kernel(x_ref, w_ref, o_ref, xb_ref):
    @pl.when(pl.program_id(1) == 0)
    def _():
        xb_ref[...] = x_ref[...].astype(bf16)

    o_ref[...] = lax.dot_general(xb_ref[...], w_ref[...], (((1,), (0,)), ((), ())), preferred_element_type=f32)


def _in_proj(x2d, w_bf):
    t, d = x2d.shape
    n = w_bf.shape[1]
    tm, tn = 1024, 1024
    return pl.pallas_call(
        _in_proj_kernel,
        grid=(t // tm, n // tn),
        in_specs=[pl.BlockSpec((tm, d), lambda i, j: (i, 0)),
                  pl.BlockSpec((d, tn), lambda i, j: (0, j))],
        out_specs=pl.BlockSpec((tm, tn), lambda i, j: (i, j)),
        out_shape=jax.ShapeDtypeStruct((t, n), f32),
        scratch_shapes=[pltpu.VMEM((tm, d), bf16)],
        compiler_params=_cparams(("parallel", "arbitrary")),
        name="in_proj",
    )(x2d, w_bf)


def _retention_kernel(q_ref, k_ref, v_ref, g_ref, cos_ref, sin_ref, dec_ref, qd_ref, kd_ref, cd_ref, gnw_ref,
                      o_ref):
    s = q_ref.shape[0]
    n_chunks = s // CHUNK
    decay = dec_ref[...]
    q_dec = qd_ref[...]
    k_dec = kd_ref[...]
    c_dec = cd_ref[0:1, :]
    gnw = gnw_ref[...]
    k_scale = RET_HEAD_DIM ** -0.5

    def rope(xv, cos, sin):
        return xv * cos + pltpu.roll(xv, RET_HEAD_DIM // 2, axis=1) * sin

    def body(n, state):
        sl = pl.ds(pl.multiple_of(n * CHUNK, CHUNK), CHUNK)
        cos = cos_ref[sl, :]
        sin = sin_ref[sl, :]
        q = rope(q_ref[sl, :], cos, sin)
        k = rope(k_ref[sl, :], cos, sin) * k_scale
        vb = v_ref[sl, :].astype(bf16)
        scores = lax.dot_general(q.astype(bf16), k.astype(bf16), (((1,), (1,)), ((), ())),
                                 preferred_element_type=f32) * decay
        intra = jnp.dot(scores.astype(bf16), vb, preferred_element_type=f32)
        cross = jnp.dot((q * q_dec).astype(bf16), state.astype(bf16), preferred_element_type=f32)
        kv = lax.dot_general((k * k_dec).astype(bf16), vb, (((0,), (0,)), ((), ())),
                             preferred_element_type=f32)
        ret = intra + cross
        mu = jnp.mean(ret, axis=-1, keepdims=True)
        cen = ret - mu
        var = jnp.mean(cen * cen, axis=-1, keepdims=True)
        ret = cen * lax.rsqrt(var + LN_EPS) * gnw
        g = g_ref[sl, :]
        o_ref[sl, :] = (g * jax.nn.sigmoid(g) * ret).astype(o_ref.dtype)
        return c_dec * state + kv

    lax.fori_loop(0, n_chunks, body, jnp.zeros((RET_HEAD_DIM, RET_HEAD_DIM), f32), unroll=4)


def _retention_tables(s):
    h, d = RET_HEADS, RET_HEAD_DIM
    inv = ROPE_BASE ** (-jnp.arange(0, d, 2, dtype=f32) / d)
    ang = jnp.arange(s, dtype=f32)[:, None] * inv[None, :]
    cos = jnp.cos(ang)
    sin = jnp.sin(ang)
    cos_t = jnp.concatenate([cos, cos], axis=-1)
    sin_t = jnp.concatenate([-sin, sin], axis=-1)
    log_gamma = jnp.log1p(-jnp.exp2(-5.0 - jnp.arange(h, dtype=f32)))
    idx = jnp.arange(CHUNK, dtype=f32)
    diff = idx[:, None] - idx[None, :]
    decay = jnp.where((diff >= 0)[None], jnp.exp(jnp.maximum(diff, 0.0)[None] * log_gamma[:, None, None]), 0.0)
    q_dec = jnp.exp((idx[None, :] + 1.0) * log_gamma[:, None])
    k_dec = jnp.exp((CHUNK - 1.0 - idx)[None, :] * log_gamma[:, None])
    c_dec = jnp.exp(CHUNK * log_gamma)
    q_dec = jnp.broadcast_to(q_dec[:, :, None], (h, CHUNK, d))
    k_dec = jnp.broadcast_to(k_dec[:, :, None], (h, CHUNK, d))
    c_dec = jnp.broadcast_to(c_dec[:, None, None], (h, SUBLANES, d))
    return cos_t, sin_t, decay, q_dec, k_dec, c_dec


def _retention(proj, ret_gn_w):
    b, s, _ = proj.shape
    d = RET_HEAD_DIM
    cos_t, sin_t, decay, q_dec, k_dec, c_dec = _retention_tables(s)
    col = lambda off: pl.BlockSpec((None, s, d), lambda bi, hi: (bi, 0, off + hi))
    per_head = lambda r: pl.BlockSpec((None, r, d), lambda bi, hi: (hi, 0, 0))
    full = lambda shp: pl.BlockSpec(shp, lambda bi, hi: (0,) * len(shp))
    return pl.pallas_call(
        _retention_kernel,
        grid=(b, RET_HEADS),
        in_specs=[col(0), col(RET_HEADS), col(2 * RET_HEADS), col(3 * RET_HEADS),
                  full((s, d)), full((s, d)),
                  per_head(CHUNK), per_head(CHUNK), per_head(CHUNK), per_head(SUBLANES),
                  pl.BlockSpec((1, d), lambda bi, hi: (0, hi))],
        out_specs=pl.BlockSpec((None, s, d), lambda bi, hi: (bi, 0, hi)),
        out_shape=jax.ShapeDtypeStruct((b, s, RET_WIDTH), bf16),
        compiler_params=_cparams(("parallel", "parallel")),
        name="retention",
    )(proj, proj, proj, proj, cos_t, sin_t, decay, q_dec, k_dec, c_dec, ret_gn_w.reshape(1, RET_WIDTH))


def _gelu_tanh(x):
    return 0.5 * x * (1.0 + jnp.tanh(math.sqrt(2.0 / math.pi) * (x + 0.044715 * (x * x * x))))


def _lru_kernel(xr_ref, yg_ref, cw_ref, cb_ref, wa_ref, ba_ref, wx_ref, bx_ref, lam_ref, o_ref, a_ref, b_ref):
    s = xr_ref.shape[0]
    x = xr_ref[...]
    rows = lax.broadcasted_iota(jnp.int32, x.shape, 0)
    xc = cb_ref[...] + cw_ref[CONV_WIDTH - 1:CONV_WIDTH, :] * x
    for back in range(1, CONV_WIDTH):
        shifted = jnp.where(rows >= back, pltpu.roll(x, back, axis=0), 0.0)
        xc = xc + cw_ref[CONV_WIDTH - 1 - back:CONV_WIDTH - back, :] * shifted
    xcb = xc.astype(bf16)
    r = jax.nn.sigmoid(jnp.dot(xcb, wa_ref[...].astype(bf16), preferred_element_type=f32) + ba_ref[...])
    gi = jax.nn.sigmoid(jnp.dot(xcb, wx_ref[...].astype(bf16), preferred_element_type=f32) + bx_ref[...])
    lam = lam_ref[...]
    log_sig = jnp.minimum(lam, 0.0) - jnp.log1p(jnp.exp(-jnp.abs(lam)))
    log_a = LRU_C * r * log_sig
    a = jnp.exp(log_a)
    a_ref[...] = a
    b_ref[...] = jnp.sqrt(-jnp.tanh(log_a) * (a * a + 1.0)) * (gi * xc)

    row8 = lax.broadcasted_iota(jnp.int32, (SUBLANES, LANES), 0)

    def body(i, h_prev):
        sl = pl.ds(pl.multiple_of(i * SUBLANES, SUBLANES), SUBLANES)
        a8 = a_ref[sl, :]
        b8 = b_ref[sl, :]
        for sh in (1, 2, 4):
            a_sh = jnp.where(row8 >= sh, pltpu.roll(a8, sh, axis=0), 1.0)
            b_sh = jnp.where(row8 >= sh, pltpu.roll(b8, sh, axis=0), 0.0)
            b8 = a8 * b_sh + b8
            a8 = a8 * a_sh
        h8 = a8 * h_prev + b8
        o_ref[sl, :] = (_gelu_tanh(yg_ref[sl, :]) * h8).astype(o_ref.dtype)
        return h8[SUBLANES - 1:SUBLANES, :]

    lax.fori_loop(0, s // SUBLANES, body, jnp.zeros((1, LANES), f32), unroll=8)


def _lru(proj, conv_w, conv_b, wa, ba, wx, bx, lam):
    b, s, _ = proj.shape
    d = LRU_BLOCK_DIM
    xr_off = 4 * RET_WIDTH // d
    yg_off = xr_off + LRU_BLOCKS
    col = lambda off: pl.BlockSpec((None, s, d), lambda bi, ji: (bi, 0, off + ji))
    vec = lambda r: pl.BlockSpec((r, d), lambda bi, ji: (0, ji))
    blk = lambda r: pl.BlockSpec((None, r, d), lambda bi, ji: (ji, 0, 0))
    return pl.pallas_call(
        _lru_kernel,
        grid=(b, LRU_BLOCKS),
        in_specs=[col(xr_off), col(yg_off), vec(CONV_WIDTH), vec(1), blk(d), blk(1), blk(d), blk(1), vec(1)],
        out_specs=pl.BlockSpec((None, s, d), lambda bi, ji: (bi, 0, ji)),
        out_shape=jax.ShapeDtypeStruct((b, s, LRU_WIDTH), bf16),
        scratch_shapes=[pltpu.VMEM((s, d), f32), pltpu.VMEM((s, d), f32)],
        compiler_params=_cparams(("parallel", "parallel")),
        name="rg_lru",
    )(proj, proj, conv_w, conv_b.reshape(1, LRU_WIDTH), wa, ba.reshape(LRU_BLOCKS, 1, d), wx,
      bx.reshape(LRU_BLOCKS, 1, d), lam.reshape(1, LRU_WIDTH))


def _layer_norm(y, w, b):
    mu = jnp.mean(y, axis=-1, keepdims=True)
    cen = y - mu
    var = jnp.mean(cen * cen, axis=-1, keepdims=True)
    return cen * lax.rsqrt(var + LN_EPS) * w + b


HALF = D_MODEL // 2
SLAB = HALF // LANES
PITCH = SLAB + 4
u32 = jnp.uint32
HIGH_MASK = 0xFFFF0000


def _bf16_bits(v):
    return lax.bitcast_convert_type(v.astype(bf16).astype(f32), u32)


def _store_slabs(ref, val):
    n = val.shape[0]
    for j in range(SLAB):
        lo = _bf16_bits(val[:, j * LANES:(j + 1) * LANES])
        hi = _bf16_bits(val[:, HALF + j * LANES:HALF + (j + 1) * LANES])
        ref[pl.ds(j, n, stride=PITCH), :] = hi | lax.shift_right_logical(lo, jnp.full_like(lo, 16))
    for j in range(SLAB, PITCH):
        ref[pl.ds(j, n, stride=PITCH), :] = jnp.zeros((n, LANES), u32)


def _slab_cols(ref, j, n):
    w = ref[pl.ds(j, n, stride=PITCH), :]
    lo = lax.bitcast_convert_type(lax.shift_left(w, jnp.full_like(w, 16)), f32)
    hi = lax.bitcast_convert_type(w & jnp.full_like(w, HIGH_MASK), f32)
    return lo, hi


def _split_bf16(v):
    hi = v.astype(bf16)
    lo = (v - hi.astype(f32)).astype(bf16)
    return hi, lo


def _out_router_kernel(ret_ref, lru_ref, wo_ref, x_ref, lnw_ref, lnb_ref, wr_ref, br_ref,
                       x1_ref, x1s_ref, tope_ref, gate_ref):
    m = jnp.dot(ret_ref[...], wo_ref[0:RET_WIDTH, :], preferred_element_type=f32)
    m = m + jnp.dot(lru_ref[...], wo_ref[RET_WIDTH:D_MODEL, :], preferred_element_type=f32)
    x1 = _layer_norm(DN_ALPHA * x_ref[...] + m, lnw_ref[...], lnb_ref[...])
    x1_ref[...] = x1
    _store_slabs(x1s_ref, x1)
    xh, xl = _split_bf16(x1)
    wh, wl = _split_bf16(wr_ref[...])
    logits = (jnp.dot(xh, wh, preferred_element_type=f32) + jnp.dot(xl, wh, preferred_element_type=f32)
              + jnp.dot(xh, wl, preferred_element_type=f32)) + br_ref[...]
    tm = logits.shape[0]
    lane = lax.broadcasted_iota(jnp.int32, logits.shape, 1)
    lane_k = lax.broadcasted_iota(jnp.int32, (tm, TOP_K), 1)
    top_e = jnp.zeros((tm, TOP_K), jnp.int32)
    top_v = jnp.zeros((tm, TOP_K), f32)
    cur = logits
    for kk in range(TOP_K):
        mx = jnp.max(cur, axis=-1, keepdims=True)
        idx = jnp.min(jnp.where(cur == mx, lane, N_EXPERTS), axis=-1, keepdims=True)
        top_e = jnp.where(lane_k == kk, idx, top_e)
        top_v = jnp.where(lane_k == kk, mx, top_v)
        cur = jnp.where(lane == idx, -jnp.inf, cur)
    ex = jnp.exp(top_v - top_v[:, 0:1])
    gate_ref[...] = ex / jnp.sum(ex, axis=-1, keepdims=True)
    tope_ref[...] = top_e


def _out_router(ret_out, lru_out, wo_bf, x2d, ln_w, ln_b, w_router, b_router):
    t, d = x2d.shape
    tm = 256
    row = lambda c: pl.BlockSpec((tm, c), lambda i: (i, 0))
    full = lambda r, c: pl.BlockSpec((r, c), lambda i: (0, 0))
    return pl.pallas_call(
        _out_router_kernel,
        grid=(t // tm,),
        in_specs=[row(RET_WIDTH), row(LRU_WIDTH), full(d, d), row(d), full(1, d), full(1, d),
                  full(d, N_EXPERTS), full(1, N_EXPERTS)],
        out_specs=[row(d), pl.BlockSpec((tm * PITCH, LANES), lambda i: (i, 0)), row(TOP_K), row(TOP_K)],
        out_shape=[jax.ShapeDtypeStruct((t, d), f32), jax.ShapeDtypeStruct((t * PITCH, LANES), u32),
                   jax.ShapeDtypeStruct((t, TOP_K), jnp.int32),
                   jax.ShapeDtypeStruct((t, TOP_K), f32)],
        compiler_params=_cparams(("parallel",)),
        name="out_proj_ln1_router",
    )(ret_out, lru_out, wo_bf, x2d, ln_w.reshape(1, d), ln_b.reshape(1, d), w_router,
      b_router.reshape(1, N_EXPERTS))


def _routing_tables(top_e, t):
    n_pad = t * TOP_K + N_EXPERTS * ROW_CHUNK
    max_units = N_EXPERTS + (t * TOP_K) // UNIT_ROWS
    sel = (top_e[:, :, None] == jnp.arange(N_EXPERTS, dtype=jnp.int32)[None, None, :]).any(axis=1)
    sel = sel.astype(jnp.int32)
    counts = jnp.sum(sel, axis=0)
    rank = jnp.cumsum(sel, axis=0) - sel
    padded = (counts + ROW_CHUNK - 1) // ROW_CHUNK * ROW_CHUNK
    pad_ends = jnp.cumsum(padded)
    pad_starts = pad_ends - padded
    dest_dense = pad_starts[None, :] + rank
    dest = jnp.take_along_axis(dest_dense, top_e, axis=1)
    tok = jnp.broadcast_to(jnp.arange(t, dtype=jnp.int32)[:, None], (t, TOP_K))
    slot_tok = jnp.zeros((n_pad + UNIT_ROWS,), jnp.int32).at[dest.reshape(-1)].set(tok.reshape(-1) * PITCH)
    units_per_e = (padded + UNIT_ROWS - 1) // UNIT_ROWS
    unit_ends = jnp.cumsum(units_per_e)
    unit_starts = unit_ends - units_per_e
    n_units = unit_ends[-1]
    u = jnp.arange(max_units, dtype=jnp.int32)
    u_clamped = jnp.minimum(u, n_units - 1)
    ue = jnp.searchsorted(unit_ends, u_clamped, side='right').astype(jnp.int32)
    ue = jnp.minimum(ue, N_EXPERTS - 1)
    j = u_clamped - unit_starts[ue]
    u_row = pad_starts[ue] + j * UNIT_ROWS
    u_rows = jnp.minimum(UNIT_ROWS, padded[ue] - j * UNIT_ROWS)
    u_chunks = jnp.where(u < n_units, u_rows // ROW_CHUNK, 0).astype(jnp.int32)
    n_used_rows = pad_ends[-1].astype(jnp.int32)
    return dest.astype(jnp.int32), slot_tok, ue, u_row.astype(jnp.int32), u_chunks, n_used_rows, n_pad, max_units


CHUNK_PITCHED = ROW_CHUNK * PITCH
OUT_SLOTS = 4


def _moe_kernel(ue_ref, urow_ref, uchunks_ref, used_ref, tok_hbm, x1s_hbm, wg_ref, bg_ref, wu_ref, bu_ref, wd_ref,
                bd_ref, y_hbm, xbuf, yacc, gstage, ostage, tok_smem, flags, tok_sem, in_sem, out_sem):
    u = pl.program_id(0)
    f = pl.program_id(1)
    n_u = pl.num_programs(0)
    n_f = pl.num_programs(1)
    n_chunks = uchunks_ref[u]
    row0 = urow_ref[u]
    cur = u % 2
    nxt = 1 - cur
    u_next = jnp.minimum(u + 1, n_u - 1)
    n_next = jnp.where(u + 1 < n_u, uchunks_ref[u_next], 0)
    u_next2 = jnp.minimum(u + 2, n_u - 1)
    n_next2 = jnp.where(u + 2 < n_u, uchunks_ref[u_next2], 0)
    TABLES = 3
    tcur = u % TABLES
    tnxt = (u + 1) % TABLES
    first_step = jnp.logical_and(u == 0, f == 0)
    PENDING, PEND_CHUNK, PEND_SLOT, OUT_BUSY = 0, 1, 2, 3

    def rows(c, size=ROW_CHUNK):
        return pl.ds(pl.multiple_of(c * size, size), size)

    def tok_copy(unit_row, slot):
        src = tok_hbm.at[pl.ds(pl.multiple_of(unit_row, ROW_CHUNK), UNIT_ROWS)]
        dst = tok_smem.at[pl.ds(pl.multiple_of(slot * UNIT_ROWS, UNIT_ROWS), UNIT_ROWS)]
        return pltpu.make_async_copy(src, dst, tok_sem)

    ISSUE_GROUP = 8

    def gather_start(c, tslot, lo=0, hi=ROW_CHUNK):
        base = tslot * UNIT_ROWS + c * ROW_CHUNK

        def issue(g, carry):
            for i in range(ISSUE_GROUP):
                r = g * ISSUE_GROUP + i
                src = x1s_hbm.at[pl.ds(tok_smem[base + r], SLAB), :]
                dst = gstage.at[pl.ds(r * PITCH, SLAB), :]
                pltpu.make_async_copy(src, dst, in_sem).start()
            return carry

        lax.fori_loop(lo // ISSUE_GROUP, hi // ISSUE_GROUP, issue, 0)

    def gather_finish(c, xslot):
        n = ROW_CHUNK * SLAB
        pltpu.make_async_copy(x1s_hbm.at[pl.ds(0, n), :], gstage.at[pl.ds(0, n), :], in_sem).wait()
        for j in range(SLAB):
            lo, hi = _slab_cols(gstage, j, ROW_CHUNK)
            xbuf[xslot, rows(c), j * LANES:(j + 1) * LANES] = lo.astype(bf16)
            xbuf[xslot, rows(c), HALF + j * LANES:HALF + (j + 1) * LANES] = hi.astype(bf16)

    def out_copy(c, slot):
        dst = y_hbm.at[pl.ds(pl.multiple_of((row0 + c * ROW_CHUNK) * PITCH, CHUNK_PITCHED), CHUNK_PITCHED), :]
        return pltpu.make_async_copy(ostage.at[slot], dst, out_sem.at[slot])

    def out_wait(slot):
        @pl.when(flags[OUT_BUSY + slot] == 1)
        def _():
            pltpu.make_async_copy(ostage.at[slot], y_hbm.at[pl.ds(0, CHUNK_PITCHED), :], out_sem.at[slot]).wait()
            flags[OUT_BUSY + slot] = 0

    @pl.when(first_step)
    def _():
        for i in range(OUT_BUSY + OUT_SLOTS):
            flags[i] = 0

        @pl.when(n_chunks > 0)
        def _():
            first_table = tok_copy(row0, tcur)
            first_table.start()
            first_table.wait()

            def load(c, carry):
                gather_start(c, tcur)
                gather_finish(c, cur)
                return carry

            lax.fori_loop(0, n_chunks, load, 0)

        @pl.when(n_next > 0)
        def _():
            tok_copy(urow_ref[u_next], tnxt).start()

    @pl.when(flags[PENDING] == 1)
    def _():
        gather_finish(flags[PEND_CHUNK], flags[PEND_SLOT])
        flags[PENDING] = 0

    @pl.when(jnp.logical_and(f == 0, n_next > 0))
    def _():
        tok_copy(urow_ref[u_next], tnxt).wait()

        @pl.when(n_next2 > 0)
        def _():
            tok_copy(urow_ref[u_next2], (u + 2) % TABLES).start()

    prefetching = f < n_next
    n_quads = n_chunks // 4
    has_pair = (n_chunks % 4) // 2
    has_single = n_chunks % 2
    n_blocks = n_quads + has_pair + has_single
    share = (ROW_CHUNK // jnp.maximum(n_blocks, 1) + ISSUE_GROUP - 1) // ISSUE_GROUP * ISSUE_GROUP

    def prefetch_share(i):
        lo = jnp.minimum(i * share, ROW_CHUNK)
        hi = jnp.where(i == n_blocks - 1, ROW_CHUNK, jnp.minimum(lo + share, ROW_CHUNK))
        gather_start(f, tnxt, lo, jnp.where(prefetching, hi, lo))

    @pl.when(prefetching)
    def _():
        flags[PENDING] = 1
        flags[PEND_CHUNK] = f
        flags[PEND_SLOT] = nxt

    @pl.when(n_chunks > 0)
    def _():
        bg = bg_ref[...]
        bu = bu_ref[...]
        mm = lambda a, w: lax.dot_general(a, w, (((1,), (0,)), ((), ())), preferred_element_type=f32)

        def mlp(sl, first):
            xc = xbuf[cur, sl, :]
            gt = jnp.minimum(mm(xc, wg_ref[...]) + bg, SWIGLU_LIMIT)
            up = jnp.clip(mm(xc, wu_ref[...]) + bu, -SWIGLU_LIMIT, SWIGLU_LIMIT)
            hid = (up + 1.0) * gt * jax.nn.sigmoid(SWIGLU_ALPHA * gt)
            part = mm(hid.astype(bf16), wd_ref[...])
            if first:
                yacc[sl, :] = part
            else:
                yacc[sl, :] += part

        def blocks(first):
            def quad(c, carry):
                prefetch_share(c)
                mlp(rows(c, 4 * ROW_CHUNK), first)
                return carry

            lax.fori_loop(0, n_quads, quad, 0)

            @pl.when(has_pair == 1)
            def _():
                prefetch_share(n_quads)
                mlp(pl.ds(pl.multiple_of(n_quads * 4 * ROW_CHUNK, 2 * ROW_CHUNK), 2 * ROW_CHUNK), first)

            @pl.when(has_single == 1)
            def _():
                prefetch_share(n_quads + has_pair)
                mlp(rows(n_chunks - 1), first)

        @pl.when(f == 0)
        def _():
            blocks(True)

        @pl.when(f > 0)
        def _():
            blocks(False)

    @pl.when(jnp.logical_and(f == n_f - 1, n_chunks > 0))
    def _():
        bd = bd_ref[...]

        def store(c, carry):
            slot = c % OUT_SLOTS
            out_wait(slot)
            _store_slabs(ostage.at[slot], yacc[rows(c), :] + bd)
            out_copy(c, slot).start()
            flags[OUT_BUSY + slot] = 1
            return carry

        lax.fori_loop(0, n_chunks, store, 0)

    @pl.when(jnp.logical_and(u == n_u - 1, f == n_f - 1))
    def _():
        for s in range(OUT_SLOTS):
            out_wait(s)
        first = used_ref[0] // ROW_CHUNK
        last = y_hbm.shape[0] // CHUNK_PITCHED
        ostage[0] = jnp.zeros((CHUNK_PITCHED, LANES), u32)

        def tail_copy(c):
            dst = y_hbm.at[pl.ds(pl.multiple_of(c * CHUNK_PITCHED, CHUNK_PITCHED), CHUNK_PITCHED), :]
            return pltpu.make_async_copy(ostage.at[0], dst, out_sem.at[0])

        def start(c, carry):
            tail_copy(c).start()
            return carry

        def wait(c, carry):
            tail_copy(c).wait()
            return carry

        lax.fori_loop(first, last, start, 0)
        lax.fori_loop(first, last, wait, 0)


def _moe_experts(x1s, slot_tok, ue, u_row, u_chunks, n_used_rows, n_pad, max_units, w_gate, b_gate, w_up, b_up,
                 w_down, b_down):
    e, d, dff = w_gate.shape
    n_f = dff // F_TILE
    assert n_f >= UNIT_ROWS // ROW_CHUNK, "one chunk of the next unit is gathered per f-tile step"
    f_idx = lambda u, f, uc: jnp.where(uc[u] > 0, f, n_f - 1)
    col_w = pl.BlockSpec((None, d, F_TILE), lambda u, f, ue, ur, uc, used: (ue[u], 0, f_idx(u, f, uc)))
    col_b = pl.BlockSpec((None, 1, F_TILE), lambda u, f, ue, ur, uc, used: (ue[u], 0, f_idx(u, f, uc)))
    return pl.pallas_call(
        _moe_kernel,
        grid_spec=pltpu.PrefetchScalarGridSpec(
            num_scalar_prefetch=4,
            grid=(max_units, n_f),
            in_specs=[pl.BlockSpec(memory_space=pl.ANY), pl.BlockSpec(memory_space=pl.ANY),
                      col_w, col_b, col_w, col_b,
                      pl.BlockSpec((None, F_TILE, d), lambda u, f, ue, ur, uc, used: (ue[u], f_idx(u, f, uc), 0)),
                      pl.BlockSpec((None, 1, d), lambda u, f, ue, ur, uc, used: (ue[u], 0, 0))],
            out_specs=pl.BlockSpec(memory_space=pl.ANY),
            scratch_shapes=[pltpu.VMEM((2, UNIT_ROWS, d), bf16),
                            pltpu.VMEM((UNIT_ROWS, d), f32),
                            pltpu.VMEM((CHUNK_PITCHED, LANES), u32),
                            pltpu.VMEM((OUT_SLOTS, CHUNK_PITCHED, LANES), u32),
                            pltpu.SMEM((3 * UNIT_ROWS,), jnp.int32),
                            pltpu.SMEM((3 + OUT_SLOTS,), jnp.int32),
                            pltpu.SemaphoreType.DMA(()),
                            pltpu.SemaphoreType.DMA(()),
                            pltpu.SemaphoreType.DMA((OUT_SLOTS,))],
        ),
        out_shape=jax.ShapeDtypeStruct((n_pad * PITCH, LANES), u32),
        compiler_params=_cparams(("arbitrary", "arbitrary")),
        name="moe_experts",
    )(ue, u_row, u_chunks, n_used_rows.reshape(1), slot_tok, x1s, w_gate, b_gate.reshape(e, 1, dff), w_up,
      b_up.reshape(e, 1, dff), w_down, b_down.reshape(e, 1, d))


COMBINE_ROWS = 256
COMBINE_GROUPS = 2


def _combine_kernel(dest_ref, dest_next_ref, y_hbm, gate_ref, x1_ref, p_ref, lnw_ref, lnb_ref, wp_ref, pnw_ref,
                    wg_ref, o_ref, ybuf_a, ybuf_b, fsum_ref, sem_a, sem_b):
    tm = COMBINE_ROWS
    s = pl.program_id(0)

    def row_copy(table, table_off, t, k, ybuf, sem):
        src = y_hbm.at[pl.ds(table[(table_off + t) * TOP_K + k] * PITCH, SLAB), :]
        return pltpu.make_async_copy(src, ybuf.at[k, pl.ds(t * PITCH, SLAB), :], sem)

    def wait_tile(ybuf, sem):
        for k in range(TOP_K):
            pltpu.make_async_copy(y_hbm.at[pl.ds(0, tm * SLAB), :], ybuf.at[k, pl.ds(0, tm * SLAB), :], sem).wait()

    @pl.when(s == 0)
    def _():
        def issue(t, c):
            for k in range(TOP_K):
                row_copy(dest_ref, 0, t, k, ybuf_a, sem_a).start()
            return c

        lax.fori_loop(0, tm, issue, 0, unroll=2)

    def finish_tile(row0, ybuf, next_table, next_off, next_buf, next_sem):
        tok = pl.ds(row0, tm)
        pieces = SLAB * TOP_K
        per_piece = tm // pieces
        e = jnp.dot(p_ref[tok, :].astype(bf16), wp_ref[...], preferred_element_type=f32)
        e = e * lax.rsqrt(jnp.mean(e * e, axis=-1, keepdims=True) + LN_EPS) * pnw_ref[...]
        gates = gate_ref[tok, :]
        for j in range(SLAB):
            acc_lo, acc_hi = None, None
            for k in range(TOP_K):
                piece = j * TOP_K + k
                for t in range(piece * per_piece, (piece + 1) * per_piece):
                    for kk in range(TOP_K):
                        row_copy(next_table, next_off, t, kk, next_buf, next_sem).start()
                lo, hi = _slab_cols(ybuf.at[k], j, tm)
                g = gates[:, k:k + 1]
                acc_lo = g * lo if k == 0 else acc_lo + g * lo
                acc_hi = g * hi if k == 0 else acc_hi + g * hi
            fsum_ref[:, j * LANES:(j + 1) * LANES] = acc_lo
            fsum_ref[:, HALF + j * LANES:HALF + (j + 1) * LANES] = acc_hi
        x2 = _layer_norm(DN_ALPHA * x1_ref[tok, :] + fsum_ref[...], lnw_ref[...], lnb_ref[...])
        gate = jax.nn.sigmoid(jnp.dot(x2.astype(bf16), wg_ref[...], preferred_element_type=f32))
        o_ref[tok, :] = x2 + gate * e

    wait_tile(ybuf_a, sem_a)
    finish_tile(0, ybuf_a, dest_ref, tm, ybuf_b, sem_b)
    wait_tile(ybuf_b, sem_b)
    finish_tile(tm, ybuf_b, dest_next_ref, 0, ybuf_a, sem_a)

    @pl.when(s == pl.num_programs(0) - 1)
    def _():
        wait_tile(ybuf_a, sem_a)


def _combine(y, dest, gates, x1, p2d, ln_w, ln_b, wp_bf, ple_norm_w, wg_bf):
    t, d = x1.shape
    tm = COMBINE_ROWS
    n_steps = t // (2 * tm)
    row = lambda c: pl.BlockSpec((2 * tm, c), lambda i: (i, 0))
    full = lambda r, c: pl.BlockSpec((r, c), lambda i: (0, 0))
    table = lambda nxt: pl.BlockSpec((2 * tm * TOP_K,), lambda i: (jnp.minimum(i + nxt, n_steps - 1),),
                                     memory_space=pltpu.SMEM)
    dest_flat = dest.reshape(-1)
    return pl.pallas_call(
        _combine_kernel,
        grid=(n_steps,),
        in_specs=[table(0), table(1), pl.BlockSpec(memory_space=pl.ANY),
                  row(TOP_K), row(d), row(PLE_DIM), full(1, d), full(1, d), full(PLE_DIM, d), full(1, d),
                  full(d, d)],
        out_specs=row(d),
        out_shape=jax.ShapeDtypeStruct((t, d), f32),
        scratch_shapes=[pltpu.VMEM((TOP_K, tm * PITCH, LANES), u32), pltpu.VMEM((TOP_K, tm * PITCH, LANES), u32),
                        pltpu.VMEM((tm, d), f32), pltpu.SemaphoreType.DMA(()), pltpu.SemaphoreType.DMA(())],
        compiler_params=_cparams(("arbitrary",)),
        name="combine_ln2_ple",
    )(dest_flat, dest_flat, y, gates, x1, p2d, ln_w.reshape(1, d), ln_b.reshape(1, d), wp_bf,
      ple_norm_w.reshape(1, d), wg_bf)


def _layer(h, p_i, w_in, ret_gn_w, conv_w, conv_b, lru_wa, lru_ba, lru_wx, lru_bx, lru_lam, w_out,
           ln1_w, ln1_b, w_router, b_router, w_gate, b_gate, w_up, b_up, w_down, b_down,
           ln2_w, ln2_b, w_ple_proj, ple_norm_w, w_ple_gate):
    b, s, d = h.shape
    t = b * s
    x2d = h.reshape(t, d)
    proj = _in_proj(x2d, w_in).reshape(b, s, IN_COLS)
    ret_out = _retention(proj, ret_gn_w)
    lru_out = _lru(proj, conv_w, conv_b, lru_wa, lru_ba, lru_wx, lru_bx, lru_lam)
    x1, x1s, top_e, gates = _out_router(ret_out.reshape(t, RET_WIDTH), lru_out.reshape(t, LRU_WIDTH),
                                        w_out.astype(bf16), x2d, ln1_w, ln1_b, w_router, b_router)
    dest, slot_tok, ue, u_row, u_chunks, n_used_rows, n_pad, max_units = _routing_tables(top_e, t)
    y = _moe_experts(x1s, slot_tok, ue, u_row, u_chunks, n_used_rows, n_pad, max_units, w_gate, b_gate, w_up, b_up,
                     w_down, b_down)
    out = _combine(y, dest, gates, x1, p_i.reshape(t, PLE_DIM), ln2_w, ln2_b, w_ple_proj.astype(bf16),
                   ple_norm_w, w_ple_gate.astype(bf16))
    return out.reshape(b, s, d)


def kernel(x, p, w_in, ret_gn_w, conv_w, conv_b, lru_wa, lru_ba, lru_wx, lru_bx, lru_lam, w_out, ln1_w, ln1_b,
           w_router, b_router, w_gate, b_gate, w_up, b_up, w_down, b_down, ln2_w, ln2_b, w_ple_proj, ple_norm_w,
           w_ple_gate):
    h = x.astype(f32)
    for i in range(w_in.shape[0]):
        h = _layer(h, p[i], w_in[i], ret_gn_w[i], conv_w[i], conv_b[i], lru_wa[i], lru_ba[i], lru_wx[i],
                   lru_bx[i], lru_lam[i], w_out[i], ln1_w[i], ln1_b[i], w_router[i], b_router[i], w_gate[i],
                   b_gate[i], w_up[i], b_up[i], w_down[i], b_down[i], ln2_w[i], ln2_b[i], w_ple_proj[i],
                   ple_norm_w[i], w_ple_gate[i])
    return h.astype(x.dtype)
```

```python
import functools
import math

import jax
import jax.numpy as jnp
from jax import lax
from jax.experimental import pallas as pl
from jax.experimental.pallas import tpu as pltpu

D_MODEL = 2048
RET_HEAD_DIM = 128
RET_HEADS = 8
RET_WIDTH = RET_HEADS * RET_HEAD_DIM
LRU_WIDTH = D_MODEL - RET_WIDTH
LRU_BLOCKS = 8
LRU_BLOCK_DIM = LRU_WIDTH // LRU_BLOCKS
IN_COLS = 4 * RET_WIDTH + 2 * LRU_WIDTH
CONV_WIDTH = 4
LRU_C = 8.0
CHUNK = 128
ROPE_BASE = 10000.0
N_EXPERTS = 32
TOP_K = 4
SWIGLU_LIMIT = 7.0
SWIGLU_ALPHA = 1.702
PLE_DIM = 256
LN_EPS = 1e-5
DEPTH = 1
DN_ALPHA = (2.0 * DEPTH) ** 0.25

LANES = 128
SUBLANES = 8
VMEM_LIMIT = 60 * 1024 * 1024

ROW_CHUNK = 256
UNIT_ROWS = 2048
F_TILE = 256

f32 = jnp.float32
bf16 = jnp.bfloat16


def _cparams(sem):
    return pltpu.CompilerParams(dimension_semantics=sem, vmem_limit_bytes=VMEM_LIMIT)


def _in_proj_kernel(x_ref, w_ref, o_ref, xb_ref):
    @pl.when(pl.program_id(1) == 0)
    def _():
        xb_ref[...] = x_ref[...].astype(bf16)

    o_ref[...] = lax.dot_general(xb_ref[...], w_ref[...], (((1,), (0,)), ((), ())), preferred_element_type=f32)


def _in_proj(x2d, w_bf):
    t, d = x2d.shape
    n = w_bf.shape[1]
    tm, tn = 1024, 1024
    return pl.pallas_call(
        _in_proj_kernel,
        grid=(t // tm, n // tn),
        in_specs=[pl.BlockSpec((tm, d), lambda i, j: (i, 0)),
                  pl.BlockSpec((d, tn), lambda i, j: (0, j))],
        out_specs=pl.BlockSpec((tm, tn), lambda i, j: (i, j)),
        out_shape=jax.ShapeDtypeStruct((t, n), f32),
        scratch_shapes=[pltpu.VMEM((tm, d), bf16)],
        compiler_params=_cparams(("parallel", "arbitrary")),
        name="in_proj",
    )(x2d, w_bf)


def _retention_kernel(q_ref, k_ref, v_ref, g_ref, cos_ref, sin_ref, dec_ref, qd_ref, kd_ref, cd_ref, gnw_ref,
                      o_ref):
    s = q_ref.shape[0]
    n_chunks = s // CHUNK
    decay = dec_ref[...]
    q_dec = qd_ref[...]
    k_dec = kd_ref[...]
    c_dec = cd_ref[0:1, :]
    gnw = gnw_ref[...]
    k_scale = RET_HEAD_DIM ** -0.5

    def rope(xv, cos, sin):
        return xv * cos + pltpu.roll(xv, RET_HEAD_DIM // 2, axis=1) * sin

    def body(n, state):
        sl = pl.ds(pl.multiple_of(n * CHUNK, CHUNK), CHUNK)
        cos = cos_ref[sl, :]
        sin = sin_ref[sl, :]
        q = rope(q_ref[sl, :], cos, sin)
        k = rope(k_ref[sl, :], cos, sin) * k_scale
        vb = v_ref[sl, :].astype(bf16)
        scores = lax.dot_general(q.astype(bf16), k.astype(bf16), (((1,), (1,)), ((), ())),
                                 preferred_element_type=f32) * decay
        intra = jnp.dot(scores.astype(bf16), vb, preferred_element_type=f32)
        cross = jnp.dot((q * q_dec).astype(bf16), state.astype(bf16), preferred_element_type=f32)
        kv = lax.dot_general((k * k_dec).astype(bf16), vb, (((0,), (0,)), ((), ())),
                             preferred_element_type=f32)
        ret = intra + cross
        mu = jnp.mean(ret, axis=-1, keepdims=True)
        cen = ret - mu
        var = jnp.mean(cen * cen, axis=-1, keepdims=True)
        ret = cen * lax.rsqrt(var + LN_EPS) * gnw
        g = g_ref[sl, :]
        o_ref[sl, :] = (g * jax.nn.sigmoid(g) * ret).astype(o_ref.dtype)
        return c_dec * state + kv

    lax.fori_loop(0, n_chunks, body, jnp.zeros((RET_HEAD_DIM, RET_HEAD_DIM), f32), unroll=4)


def _retention_tables(s):
    h, d = RET_HEADS, RET_HEAD_DIM
    inv = ROPE_BASE ** (-jnp.arange(0, d, 2, dtype=f32) / d)
    ang = jnp.arange(s, dtype=f32)[:, None] * inv[None, :]
    cos = jnp.cos(ang)
    sin = jnp.sin(ang)
    cos_t = jnp.concatenate([cos, cos], axis=-1)
    sin_t = jnp.concatenate([-sin, sin], axis=-1)
    log_gamma = jnp.log1p(-jnp.exp2(-5.0 - jnp.arange(h, dtype=f32)))
    idx = jnp.arange(CHUNK, dtype=f32)
    diff = idx[:, None] - idx[None, :]
    decay = jnp.where((diff >= 0)[None], jnp.exp(jnp.maximum(diff, 0.0)[None] * log_gamma[:, None, None]), 0.0)
    q_dec = jnp.exp((idx[None, :] + 1.0) * log_gamma[:, None])
    k_dec = jnp.exp((CHUNK - 1.0 - idx)[None, :] * log_gamma[:, None])
    c_dec = jnp.exp(CHUNK * log_gamma)
    q_dec = jnp.broadcast_to(q_dec[:, :, None], (h, CHUNK, d))
    k_dec = jnp.broadcast_to(k_dec[:, :, None], (h, CHUNK, d))
    c_dec = jnp.broadcast_to(c_dec[:, None, None], (h, SUBLANES, d))
    return cos_t, sin_t, decay, q_dec, k_dec, c_dec


def _retention(proj, ret_gn_w):
    b, s, _ = proj.shape
    d = RET_HEAD_DIM
    cos_t, sin_t, decay, q_dec, k_dec, c_dec = _retention_tables(s)
    col = lambda off: pl.BlockSpec((None, s, d), lambda bi, hi: (bi, 0, off + hi))
    per_head = lambda r: pl.BlockSpec((None, r, d), lambda bi, hi: (hi, 0, 0))
    full = lambda shp: pl.BlockSpec(shp, lambda bi, hi: (0,) * len(shp))
    return pl.pallas_call(
        _retention_kernel,
        grid=(b, RET_HEADS),
        in_specs=[col(0), col(RET_HEADS), col(2 * RET_HEADS), col(3 * RET_HEADS),
                  full((s, d)), full((s, d)),
                  per_head(CHUNK), per_head(CHUNK), per_head(CHUNK), per_head(SUBLANES),
                  pl.BlockSpec((1, d), lambda bi, hi: (0, hi))],
        out_specs=pl.BlockSpec((None, s, d), lambda bi, hi: (bi, 0, hi)),
        out_shape=jax.ShapeDtypeStruct((b, s, RET_WIDTH), bf16),
        compiler_params=_cparams(("parallel", "parallel")),
        name="retention",
    )(proj, proj, proj, proj, cos_t, sin_t, decay, q_dec, k_dec, c_dec, ret_gn_w.reshape(1, RET_WIDTH))


def _gelu_tanh(x):
    return 0.5 * x * (1.0 + jnp.tanh(math.sqrt(2.0 / math.pi) * (x + 0.044715 * (x * x * x))))


def _lru_kernel(xr_ref, yg_ref, cw_ref, cb_ref, wa_ref, ba_ref, wx_ref, bx_ref, lam_ref, o_ref, a_ref, b_ref):
    s = xr_ref.shape[0]
    x = xr_ref[...]
    rows = lax.broadcasted_iota(jnp.int32, x.shape, 0)
    xc = cb_ref[...] + cw_ref[CONV_WIDTH - 1:CONV_WIDTH, :] * x
    for back in range(1, CONV_WIDTH):
        shifted = jnp.where(rows >= back, pltpu.roll(x, back, axis=0), 0.0)
        xc = xc + cw_ref[CONV_WIDTH - 1 - back:CONV_WIDTH - back, :] * shifted
    xcb = xc.astype(bf16)
    r = jax.nn.sigmoid(jnp.dot(xcb, wa_ref[...].astype(bf16), preferred_element_type=f32) + ba_ref[...])
    gi = jax.nn.sigmoid(jnp.dot(xcb, wx_ref[...].astype(bf16), preferred_element_type=f32) + bx_ref[...])
    lam = lam_ref[...]
    log_sig = jnp.minimum(lam, 0.0) - jnp.log1p(jnp.exp(-jnp.abs(lam)))
    log_a = LRU_C * r * log_sig
    a = jnp.exp(log_a)
    a_ref[...] = a
    b_ref[...] = jnp.sqrt(-jnp.tanh(log_a) * (a * a + 1.0)) * (gi * xc)

    row8 = lax.broadcasted_iota(jnp.int32, (SUBLANES, LANES), 0)

    def body(i, h_prev):
        sl = pl.ds(pl.multiple_of(i * SUBLANES, SUBLANES), SUBLANES)
        a8 = a_ref[sl, :]
        b8 = b_ref[sl, :]
        for sh in (1, 2, 4):
            a_sh = jnp.where(row8 >= sh, pltpu.roll(a8, sh, axis=0), 1.0)
            b_sh = jnp.where(row8 >= sh, pltpu.roll(b8, sh, axis=0), 0.0)
            b8 = a8 * b_sh + b8
            a8 = a8 * a_sh
        h8 = a8 * h_prev + b8
        o_ref[sl, :] = (_gelu_tanh(yg_ref[sl, :]) * h8).astype(o_ref.dtype)
        return h8[SUBLANES - 1:SUBLANES, :]

    lax.fori_loop(0, s // SUBLANES, body, jnp.zeros((1, LANES), f32), unroll=8)


def _lru(proj, conv_w, conv_b, wa, ba, wx, bx, lam):
    b, s, _ = proj.shape
    d = LRU_BLOCK_DIM
    xr_off = 4 * RET_WIDTH // d
    yg_off = xr_off + LRU_BLOCKS
    col = lambda off: pl.BlockSpec((None, s, d), lambda bi, ji: (bi, 0, off + ji))
    vec = lambda r: pl.BlockSpec((r, d), lambda bi, ji: (0, ji))
    blk = lambda r: pl.BlockSpec((None, r, d), lambda bi, ji: (ji, 0, 0))
    return pl.pallas_call(
        _lru_kernel,
        grid=(b, LRU_BLOCKS),
        in_specs=[col(xr_off), col(yg_off), vec(CONV_WIDTH), vec(1), blk(d), blk(1), blk(d), blk(1), vec(1)],
        out_specs=pl.BlockSpec((None, s, d), lambda bi, ji: (bi, 0, ji)),
        out_shape=jax.ShapeDtypeStruct((b, s, LRU_WIDTH), bf16),
        scratch_shapes=[pltpu.VMEM((s, d), f32), pltpu.VMEM((s, d), f32)],
        compiler_params=_cparams(("parallel", "parallel")),
        name="rg_lru",
    )(proj, proj, conv_w, conv_b.reshape(1, LRU_WIDTH), wa, ba.reshape(LRU_BLOCKS, 1, d), wx,
      bx.reshape(LRU_BLOCKS, 1, d), lam.reshape(1, LRU_WIDTH))


def _layer_norm(y, w, b):
    mu = jnp.mean(y, axis=-1, keepdims=True)
    cen = y - mu
    var = jnp.mean(cen * cen, axis=-1, keepdims=True)
    return cen * lax.rsqrt(var + LN_EPS) * w + b


HALF = D_MODEL // 2
SLAB = HALF // LANES
PITCH = SLAB + 4
u32 = jnp.uint32
HIGH_MASK = 0xFFFF0000


def _bf16_bits(v):
    return lax.bitcast_convert_type(v.astype(bf16).astype(f32), u32)


def _store_slabs(ref, val):
    n = val.shape[0]
    for j in range(SLAB):
        lo = _bf16_bits(val[:, j * LANES:(j + 1) * LANES])
        hi = _bf16_bits(val[:, HALF + j * LANES:HALF + (j + 1) * LANES])
        ref[pl.ds(j, n, stride=PITCH), :] = hi | lax.shift_right_logical(lo, jnp.full_like(lo, 16))
    for j in range(SLAB, PITCH):
        ref[pl.ds(j, n, stride=PITCH), :] = jnp.zeros((n, LANES), u32)


def _slab_cols(ref, j, n):
    w = ref[pl.ds(j, n, stride=PITCH), :]
    lo = lax.bitcast_convert_type(lax.shift_left(w, jnp.full_like(w, 16)), f32)
    hi = lax.bitcast_convert_type(w & jnp.full_like(w, HIGH_MASK), f32)
    return lo, hi


def _split_bf16(v):
    hi = v.astype(bf16)
    lo = (v - hi.astype(f32)).astype(bf16)
    return hi, lo


def _out_router_kernel(ret_ref, lru_ref, wo_ref, x_ref, lnw_ref, lnb_ref, wr_ref, br_ref,
                       x1_ref, x1s_ref, tope_ref, gate_ref):
    m = jnp.dot(ret_ref[...], wo_ref[0:RET_WIDTH, :], preferred_element_type=f32)
    m = m + jnp.dot(lru_ref[...], wo_ref[RET_WIDTH:D_MODEL, :], preferred_element_type=f32)
    x1 = _layer_norm(DN_ALPHA * x_ref[...] + m, lnw_ref[...], lnb_ref[...])
    x1_ref[...] = x1
    _store_slabs(x1s_ref, x1)
    xh, xl = _split_bf16(x1)
    wh, wl = _split_bf16(wr_ref[...])
    logits = (jnp.dot(xh, wh, preferred_element_type=f32) + jnp.dot(xl, wh, preferred_element_type=f32)
              + jnp.dot(xh, wl, preferred_element_type=f32)) + br_ref[...]
    tm = logits.shape[0]
    lane = lax.broadcasted_iota(jnp.int32, logits.shape, 1)
    lane_k = lax.broadcasted_iota(jnp.int32, (tm, TOP_K), 1)
    top_e = jnp.zeros((tm, TOP_K), jnp.int32)
    top_v = jnp.zeros((tm, TOP_K), f32)
    cur = logits
    for kk in range(TOP_K):
        mx = jnp.max(cur, axis=-1, keepdims=True)
        idx = jnp.min(jnp.where(cur == mx, lane, N_EXPERTS), axis=-1, keepdims=True)
        top_e = jnp.where(lane_k == kk, idx, top_e)
        top_v = jnp.where(lane_k == kk, mx, top_v)
        cur = jnp.where(lane == idx, -jnp.inf, cur)
    ex = jnp.exp(top_v - top_v[:, 0:1])
    gate_ref[...] = ex / jnp.sum(ex, axis=-1, keepdims=True)
    tope_ref[...] = top_e


def _out_router(ret_out, lru_out, wo_bf, x2d, ln_w, ln_b, w_router, b_router):
    t, d = x2d.shape
    tm = 256
    row = lambda c: pl.BlockSpec((tm, c), lambda i: (i, 0))
    full = lambda r, c: pl.BlockSpec((r, c), lambda i: (0, 0))
    return pl.pallas_call(
        _out_router_kernel,
        grid=(t // tm,),
        in_specs=[row(RET_WIDTH), row(LRU_WIDTH), full(d, d), row(d), full(1, d), full(1, d),
                  full(d, N_EXPERTS), full(1, N_EXPERTS)],
        out_specs=[row(d), pl.BlockSpec((tm * PITCH, LANES), lambda i: (i, 0)), row(TOP_K), row(TOP_K)],
        out_shape=[jax.ShapeDtypeStruct((t, d), f32), jax.ShapeDtypeStruct((t * PITCH, LANES), u32),
                   jax.ShapeDtypeStruct((t, TOP_K), jnp.int32),
                   jax.ShapeDtypeStruct((t, TOP_K), f32)],
        compiler_params=_cparams(("parallel",)),
        name="out_proj_ln1_router",
    )(ret_out, lru_out, wo_bf, x2d, ln_w.reshape(1, d), ln_b.reshape(1, d), w_router,
      b_router.reshape(1, N_EXPERTS))


def _routing_tables(top_e, t):
    n_pad = t * TOP_K + N_EXPERTS * ROW_CHUNK
    max_units = N_EXPERTS + (t * TOP_K) // UNIT_ROWS
    sel = (top_e[:, :, None] == jnp.arange(N_EXPERTS, dtype=jnp.int32)[None, None, :]).any(axis=1)
    sel = sel.astype(jnp.int32)
    counts = jnp.sum(sel, axis=0)
    rank = jnp.cumsum(sel, axis=0) - sel
    padded = (counts + ROW_CHUNK - 1) // ROW_CHUNK * ROW_CHUNK
    pad_ends = jnp.cumsum(padded)
    pad_starts = pad_ends - padded
    dest_dense = pad_starts[None, :] + rank
    dest = jnp.take_along_axis(dest_dense, top_e, axis=1)
    tok = jnp.broadcast_to(jnp.arange(t, dtype=jnp.int32)[:, None], (t, TOP_K))
    slot_tok = jnp.zeros((n_pad + UNIT_ROWS,), jnp.int32).at[dest.reshape(-1)].set(tok.reshape(-1) * PITCH)
    units_per_e = (padded + UNIT_ROWS - 1) // UNIT_ROWS
    unit_ends = jnp.cumsum(units_per_e)
    unit_starts = unit_ends - units_per_e
    n_units = unit_ends[-1]
    u = jnp.arange(max_units, dtype=jnp.int32)
    u_clamped = jnp.minimum(u, n_units - 1)
    ue = jnp.searchsorted(unit_ends, u_clamped, side='right').astype(jnp.int32)
    ue = jnp.minimum(ue, N_EXPERTS - 1)
    j = u_clamped - unit_starts[ue]
    u_row = pad_starts[ue] + j * UNIT_ROWS
    u_rows = jnp.minimum(UNIT_ROWS, padded[ue] - j * UNIT_ROWS)
    u_chunks = jnp.where(u < n_units, u_rows // ROW_CHUNK, 0).astype(jnp.int32)
    n_used_rows = pad_ends[-1].astype(jnp.int32)
    return dest.astype(jnp.int32), slot_tok, ue, u_row.astype(jnp.int32), u_chunks, n_used_rows, n_pad, max_units


CHUNK_PITCHED = ROW_CHUNK * PITCH
OUT_SLOTS = 4


def _moe_kernel(ue_ref, urow_ref, uchunks_ref, used_ref, tok_hbm, x1s_hbm, wg_ref, bg_ref, wu_ref, bu_ref, wd_ref,
                bd_ref, y_hbm, xbuf, yacc, gstage, ostage, tok_smem, flags, tok_sem, in_sem, out_sem):
    u = pl.program_id(0)
    f = pl.program_id(1)
    n_u = pl.num_programs(0)
    n_f = pl.num_programs(1)
    n_chunks = uchunks_ref[u]
    row0 = urow_ref[u]
    cur = u % 2
    nxt = 1 - cur
    u_next = jnp.minimum(u + 1, n_u - 1)
    n_next = jnp.where(u + 1 < n_u, uchunks_ref[u_next], 0)
    u_next2 = jnp.minimum(u + 2, n_u - 1)
    n_next2 = jnp.where(u + 2 < n_u, uchunks_ref[u_next2], 0)
    TABLES = 3
    tcur = u % TABLES
    tnxt = (u + 1) % TABLES
    first_step = jnp.logical_and(u == 0, f == 0)
    PENDING, PEND_CHUNK, PEND_SLOT, OUT_BUSY = 0, 1, 2, 3

    def rows(c, size=ROW_CHUNK):
        return pl.ds(pl.multiple_of(c * size, size), size)

    def tok_copy(unit_row, slot):
        src = tok_hbm.at[pl.ds(pl.multiple_of(unit_row, ROW_CHUNK), UNIT_ROWS)]
        dst = tok_smem.at[pl.ds(pl.multiple_of(slot * UNIT_ROWS, UNIT_ROWS), UNIT_ROWS)]
        return pltpu.make_async_copy(src, dst, tok_sem)

    ISSUE_GROUP = 8

    def gather_start(c, tslot, lo=0, hi=ROW_CHUNK):
        base = tslot * UNIT_ROWS + c * ROW_CHUNK

        def issue(g, carry):
            for i in range(ISSUE_GROUP):
                r = g * ISSUE_GROUP + i
                src = x1s_hbm.at[pl.ds(tok_smem[base + r], SLAB), :]
                dst = gstage.at[pl.ds(r * PITCH, SLAB), :]
                pltpu.make_async_copy(src, dst, in_sem).start()
            return carry

        lax.fori_loop(lo // ISSUE_GROUP, hi // ISSUE_GROUP, issue, 0)

    def gather_finish(c, xslot):
        n = ROW_CHUNK * SLAB
        pltpu.make_async_copy(x1s_hbm.at[pl.ds(0, n), :], gstage.at[pl.ds(0, n), :], in_sem).wait()
        for j in range(SLAB):
            lo, hi = _slab_cols(gstage, j, ROW_CHUNK)
            xbuf[xslot, rows(c), j * LANES:(j + 1) * LANES] = lo.astype(bf16)
            xbuf[xslot, rows(c), HALF + j * LANES:HALF + (j + 1) * LANES] = hi.astype(bf16)

    def out_copy(c, slot):
        dst = y_hbm.at[pl.ds(pl.multiple_of((row0 + c * ROW_CHUNK) * PITCH, CHUNK_PITCHED), CHUNK_PITCHED), :]
        return pltpu.make_async_copy(ostage.at[slot], dst, out_sem.at[slot])

    def out_wait(slot):
        @pl.when(flags[OUT_BUSY + slot] == 1)
        def _():
            pltpu.make_async_copy(ostage.at[slot], y_hbm.at[pl.ds(0, CHUNK_PITCHED), :], out_sem.at[slot]).wait()
            flags[OUT_BUSY + slot] = 0

    @pl.when(first_step)
    def _():
        for i in range(OUT_BUSY + OUT_SLOTS):
            flags[i] = 0

        @pl.when(n_chunks > 0)
        def _():
            first_table = tok_copy(row0, tcur)
            first_table.start()
            first_table.wait()

            def load(c, carry):
                gather_start(c, tcur)
                gather_finish(c, cur)
                return carry

            lax.fori_loop(0, n_chunks, load, 0)

        @pl.when(n_next > 0)
        def _():
            tok_copy(urow_ref[u_next], tnxt).start()

    @pl.when(flags[PENDING] == 1)
    def _():
        gather_finish(flags[PEND_CHUNK], flags[PEND_SLOT])
        flags[PENDING] = 0

    @pl.when(jnp.logical_and(f == 0, n_next > 0))
    def _():
        tok_copy(urow_ref[u_next], tnxt).wait()

        @pl.when(n_next2 > 0)
        def _():
            tok_copy(urow_ref[u_next2], (u + 2) % TABLES).start()

    prefetching = f < n_next
    n_quads = n_chunks // 4
    has_pair = (n_chunks % 4) // 2
    has_single = n_chunks % 2
    one_block = n_quads + has_pair + has_single == 1
    NONE, SHARED, ALL = 0, 1, 2
    ROWS_PER_COPY = 8
    upfront = ROW_CHUNK - n_chunks * (ROW_CHUNK // ROWS_PER_COPY)
    table_base = tnxt * UNIT_ROWS + f * ROW_CHUNK

    @pl.when(prefetching)
    def _():
        flags[PENDING] = 1
        flags[PEND_CHUNK] = f
        flags[PEND_SLOT] = nxt

    @pl.when(jnp.logical_and(prefetching, jnp.logical_not(one_block)))
    def _():
        gather_start(f, tnxt, 0, upfront)

    def start_copies(first_row, count):
        for i in range(count):
            r = first_row + i
            src = x1s_hbm.at[pl.ds(tok_smem[table_base + r], SLAB), :]
            pltpu.make_async_copy(src, gstage.at[pl.ds(r * PITCH, SLAB), :], in_sem).start()

    @pl.when(n_chunks > 0)
    def _():
        bg = bg_ref[...]
        bu = bu_ref[...]
        mm = lambda a, w: lax.dot_general(a, w, (((1,), (0,)), ((), ())), preferred_element_type=f32)

        def mlp(sl, first, copy_row, n_copies):
            q = n_copies // 4
            start_copies(copy_row, q)
            xc = xbuf[cur, sl, :]
            gt = jnp.minimum(mm(xc, wg_ref[...]) + bg, SWIGLU_LIMIT)
            start_copies(copy_row + q, q)
            up = jnp.clip(mm(xc, wu_ref[...]) + bu, -SWIGLU_LIMIT, SWIGLU_LIMIT)
            start_copies(copy_row + 2 * q, q)
            hid = (up + 1.0) * gt * jax.nn.sigmoid(SWIGLU_ALPHA * gt)
            part = mm(hid.astype(bf16), wd_ref[...])
            start_copies(copy_row + 3 * q, n_copies - 3 * q)
            if first:
                yacc[sl, :] = part
            else:
                yacc[sl, :] += part

        def blocks(first, mode):
            def copies(chunks_in_block):
                return {NONE: 0, ALL: ROW_CHUNK, SHARED: chunks_in_block * ROW_CHUNK // ROWS_PER_COPY}[mode]

            first_copy = 0 if mode == ALL else upfront

            def quad(c, carry):
                mlp(rows(c, 4 * ROW_CHUNK), first, first_copy + c * copies(4), copies(4))
                return carry

            lax.fori_loop(0, n_quads, quad, 0)

            @pl.when(has_pair == 1)
            def _():
                mlp(pl.ds(pl.multiple_of(n_quads * 4 * ROW_CHUNK, 2 * ROW_CHUNK), 2 * ROW_CHUNK), first,
                    first_copy + n_quads * copies(4), copies(2))

            @pl.when(has_single == 1)
            def _():
                mlp(rows(n_chunks - 1), first, first_copy + n_quads * copies(4) + has_pair * copies(2), copies(1))

        for first in (True, False):
            at_f = (f == 0) if first else (f > 0)
            for mode in (NONE, SHARED, ALL):
                in_mode = {NONE: jnp.logical_not(prefetching),
                           SHARED: jnp.logical_and(prefetching, jnp.logical_not(one_block)),
                           ALL: jnp.logical_and(prefetching, one_block)}[mode]
                pl.when(jnp.logical_and(at_f, in_mode))(functools.partial(blocks, first, mode))

    @pl.when(jnp.logical_and(f == n_f - 1, n_chunks > 0))
    def _():
        bd = bd_ref[...]

        def store(c, carry):
            slot = c % OUT_SLOTS
            out_wait(slot)
            _store_slabs(ostage.at[slot], yacc[rows(c), :] + bd)
            out_copy(c, slot).start()
            flags[OUT_BUSY + slot] = 1
            return carry

        lax.fori_loop(0, n_chunks, store, 0)

    @pl.when(jnp.logical_and(u == n_u - 1, f == n_f - 1))
    def _():
        for s in range(OUT_SLOTS):
            out_wait(s)
        first = used_ref[0] // ROW_CHUNK
        last = y_hbm.shape[0] // CHUNK_PITCHED
        ostage[0] = jnp.zeros((CHUNK_PITCHED, LANES), u32)

        def tail_copy(c):
            dst = y_hbm.at[pl.ds(pl.multiple_of(c * CHUNK_PITCHED, CHUNK_PITCHED), CHUNK_PITCHED), :]
            return pltpu.make_async_copy(ostage.at[0], dst, out_sem.at[0])

        def start(c, carry):
            tail_copy(c).start()
            return carry

        def wait(c, carry):
            tail_copy(c).wait()
            return carry

        lax.fori_loop(first, last, start, 0)
        lax.fori_loop(first, last, wait, 0)


def _moe_experts(x1s, slot_tok, ue, u_row, u_chunks, n_used_rows, n_pad, max_units, w_gate, b_gate, w_up, b_up,
                 w_down, b_down):
    e, d, dff = w_gate.shape
    n_f = dff // F_TILE
    assert n_f >= UNIT_ROWS // ROW_CHUNK, "one chunk of the next unit is gathered per f-tile step"
    f_idx = lambda u, f, uc: jnp.where(uc[u] > 0, f, n_f - 1)
    col_w = pl.BlockSpec((None, d, F_TILE), lambda u, f, ue, ur, uc, used: (ue[u], 0, f_idx(u, f, uc)))
    col_b = pl.BlockSpec((None, 1, F_TILE), lambda u, f, ue, ur, uc, used: (ue[u], 0, f_idx(u, f, uc)))
    return pl.pallas_call(
        _moe_kernel,
        grid_spec=pltpu.PrefetchScalarGridSpec(
            num_scalar_prefetch=4,
            grid=(max_units, n_f),
            in_specs=[pl.BlockSpec(memory_space=pl.ANY), pl.BlockSpec(memory_space=pl.ANY),
                      col_w, col_b, col_w, col_b,
                      pl.BlockSpec((None, F_TILE, d), lambda u, f, ue, ur, uc, used: (ue[u], f_idx(u, f, uc), 0)),
                      pl.BlockSpec((None, 1, d), lambda u, f, ue, ur, uc, used: (ue[u], 0, 0))],
            out_specs=pl.BlockSpec(memory_space=pl.ANY),
            scratch_shapes=[pltpu.VMEM((2, UNIT_ROWS, d), bf16),
                            pltpu.VMEM((UNIT_ROWS, d), f32),
                            pltpu.VMEM((CHUNK_PITCHED, LANES), u32),
                            pltpu.VMEM((OUT_SLOTS, CHUNK_PITCHED, LANES), u32),
                            pltpu.SMEM((3 * UNIT_ROWS,), jnp.int32),
                            pltpu.SMEM((3 + OUT_SLOTS,), jnp.int32),
                            pltpu.SemaphoreType.DMA(()),
                            pltpu.SemaphoreType.DMA(()),
                            pltpu.SemaphoreType.DMA((OUT_SLOTS,))],
        ),
        out_shape=jax.ShapeDtypeStruct((n_pad * PITCH, LANES), u32),
        compiler_params=_cparams(("arbitrary", "arbitrary")),
        name="moe_experts",
    )(ue, u_row, u_chunks, n_used_rows.reshape(1), slot_tok, x1s, w_gate, b_gate.reshape(e, 1, dff), w_up,
      b_up.reshape(e, 1, dff), w_down, b_down.reshape(e, 1, d))


COMBINE_ROWS = 256
COMBINE_GROUPS = 2


def _combine_kernel(dest_ref, dest_next_ref, y_hbm, gate_ref, x1_ref, p_ref, lnw_ref, lnb_ref, wp_ref, pnw_ref,
                    wg_ref, o_ref, ybuf_a, ybuf_b, fsum_ref, sem_a, sem_b):
    tm = COMBINE_ROWS
    s = pl.program_id(0)

    def row_copy(table, table_off, t, k, ybuf, sem):
        src = y_hbm.at[pl.ds(table[(table_off + t) * TOP_K + k] * PITCH, SLAB), :]
        return pltpu.make_async_copy(src, ybuf.at[k, pl.ds(t * PITCH, SLAB), :], sem)

    def wait_tile(ybuf, sem):
        for k in range(TOP_K):
            pltpu.make_async_copy(y_hbm.at[pl.ds(0, tm * SLAB), :], ybuf.at[k, pl.ds(0, tm * SLAB), :], sem).wait()

    @pl.when(s == 0)
    def _():
        def issue(t, c):
            for k in range(TOP_K):
                row_copy(dest_ref, 0, t, k, ybuf_a, sem_a).start()
            return c

        lax.fori_loop(0, tm, issue, 0, unroll=2)

    def finish_tile(row0, ybuf, next_table, next_off, next_buf, next_sem):
        tok = pl.ds(row0, tm)
        pieces = SLAB * TOP_K
        per_piece = tm // pieces
        e = jnp.dot(p_ref[tok, :].astype(bf16), wp_ref[...], preferred_element_type=f32)
        e = e * lax.rsqrt(jnp.mean(e * e, axis=-1, keepdims=True) + LN_EPS) * pnw_ref[...]
        gates = gate_ref[tok, :]
        for j in range(SLAB):
            acc_lo, acc_hi = None, None
            for k in range(TOP_K):
                piece = j * TOP_K + k
                for t in range(piece * per_piece, (piece + 1) * per_piece):
                    for kk in range(TOP_K):
                        row_copy(next_table, next_off, t, kk, next_buf, next_sem).start()
                lo, hi = _slab_cols(ybuf.at[k], j, tm)
                g = gates[:, k:k + 1]
                acc_lo = g * lo if k == 0 else acc_lo + g * lo
                acc_hi = g * hi if k == 0 else acc_hi + g * hi
            fsum_ref[:, j * LANES:(j + 1) * LANES] = acc_lo
            fsum_ref[:, HALF + j * LANES:HALF + (j + 1) * LANES] = acc_hi
        x2 = _layer_norm(DN_ALPHA * x1_ref[tok, :] + fsum_ref[...], lnw_ref[...], lnb_ref[...])
        gate = jax.nn.sigmoid(jnp.dot(x2.astype(bf16), wg_ref[...], preferred_element_type=f32))
        o_ref[tok, :] = x2 + gate * e

    wait_tile(ybuf_a, sem_a)
    finish_tile(0, ybuf_a, dest_ref, tm, ybuf_b, sem_b)
    wait_tile(ybuf_b, sem_b)
    finish_tile(tm, ybuf_b, dest_next_ref, 0, ybuf_a, sem_a)

    @pl.when(s == pl.num_programs(0) - 1)
    def _():
        wait_tile(ybuf_a, sem_a)


def _combine(y, dest, gates, x1, p2d, ln_w, ln_b, wp_bf, ple_norm_w, wg_bf):
    t, d = x1.shape
    tm = COMBINE_ROWS
    n_steps = t // (2 * tm)
    row = lambda c: pl.BlockSpec((2 * tm, c), lambda i: (i, 0))
    full = lambda r, c: pl.BlockSpec((r, c), lambda i: (0, 0))
    table = lambda nxt: pl.BlockSpec((2 * tm * TOP_K,), lambda i: (jnp.minimum(i + nxt, n_steps - 1),),
                                     memory_space=pltpu.SMEM)
    dest_flat = dest.reshape(-1)
    return pl.pallas_call(
        _combine_kernel,
        grid=(n_steps,),
        in_specs=[table(0), table(1), pl.BlockSpec(memory_space=pl.ANY),
                  row(TOP_K), row(d), row(PLE_DIM), full(1, d), full(1, d), full(PLE_DIM, d), full(1, d),
                  full(d, d)],
        out_specs=row(d),
        out_shape=jax.ShapeDtypeStruct((t, d), f32),
        scratch_shapes=[pltpu.VMEM((TOP_K, tm * PITCH, LANES), u32), pltpu.VMEM((TOP_K, tm * PITCH, LANES), u32),
                        pltpu.VMEM((tm, d), f32), pltpu.SemaphoreType.DMA(()), pltpu.SemaphoreType.DMA(())],
        compiler_params=_cparams(("arbitrary",)),
        name="combine_ln2_ple",
    )(dest_flat, dest_flat, y, gates, x1, p2d, ln_w.reshape(1, d), ln_b.reshape(1, d), wp_bf,
      ple_norm_w.reshape(1, d), wg_bf)


def _layer(h, p_i, w_in, ret_gn_w, conv_w, conv_b, lru_wa, lru_ba, lru_wx, lru_bx, lru_lam, w_out,
           ln1_w, ln1_b, w_router, b_router, w_gate, b_gate, w_up, b_up, w_down, b_down,
           ln2_w, ln2_b, w_ple_proj, ple_norm_w, w_ple_gate):
    b, s, d = h.shape
    t = b * s
    x2d = h.reshape(t, d)
    proj = _in_proj(x2d, w_in).reshape(b, s, IN_COLS)
    ret_out = _retention(proj, ret_gn_w)
    lru_out = _lru(proj, conv_w, conv_b, lru_wa, lru_ba, lru_wx, lru_bx, lru_lam)
    x1, x1s, top_e, gates = _out_router(ret_out.reshape(t, RET_WIDTH), lru_out.reshape(t, LRU_WIDTH),
                                        w_out.astype(bf16), x2d, ln1_w, ln1_b, w_router, b_router)
    dest, slot_tok, ue, u_row, u_chunks, n_used_rows, n_pad, max_units = _routing_tables(top_e, t)
    y = _moe_experts(x1s, slot_tok, ue, u_row, u_chunks, n_used_rows, n_pad, max_units, w_gate, b_gate, w_up, b_up,
                     w_down, b_down)
    out = _combine(y, dest, gates, x1, p_i.reshape(t, PLE_DIM), ln2_w, ln2_b, w_ple_proj.astype(bf16),
                   ple_norm_w, w_ple_gate.astype(bf16))
    return out.reshape(b, s, d)


def kernel(x, p, w_in, ret_gn_w, conv_w, conv_b, lru_wa, lru_ba, lru_wx, lru_bx, lru_lam, w_out, ln1_w, ln1_b,
           w_router, b_router, w_gate, b_gate, w_up, b_up, w_down, b_down, ln2_w, ln2_b, w_ple_proj, ple_norm_w,
           w_ple_gate):
    h = x.astype(f32)
    for i in range(w_in.shape[0]):
        h = _layer(h, p[i], w_in[i], ret_gn_w[i], conv_w[i], conv_b[i], lru_wa[i], lru_ba[i], lru_wx[i],
                   lru_bx[i], lru_lam[i], w_out[i], ln1_w[i], ln1_b[i], w_router[i], b_router[i], w_gate[i],
                   b_gate[i], w_up[i], b_up[i], w_down[i], b_down[i], ln2_w[i], ln2_b[i], w_ple_proj[i],
                   ple_norm_w[i], w_ple_gate[i])
    return h.astype(x.dtype)
```

```python
import functools
import math

import jax
import jax.numpy as jnp
from jax import lax
from jax.experimental import pallas as pl
from jax.experimental.pallas import tpu as pltpu

D_MODEL = 2048
RET_HEAD_DIM = 128
RET_HEADS = 8
RET_WIDTH = RET_HEADS * RET_HEAD_DIM
LRU_WIDTH = D_MODEL - RET_WIDTH
LRU_BLOCKS = 8
LRU_BLOCK_DIM = LRU_WIDTH // LRU_BLOCKS
IN_COLS = 4 * RET_WIDTH + 2 * LRU_WIDTH
CONV_WIDTH = 4
LRU_C = 8.0
CHUNK = 128
ROPE_BASE = 10000.0
N_EXPERTS = 32
TOP_K = 4
SWIGLU_LIMIT = 7.0
SWIGLU_ALPHA = 1.702
PLE_DIM = 256
LN_EPS = 1e-5
DEPTH = 1
DN_ALPHA = (2.0 * DEPTH) ** 0.25

LANES = 128
SUBLANES = 8
VMEM_LIMIT = 60 * 1024 * 1024

ROW_CHUNK = 256
UNIT_ROWS = 2048
F_TILE = 256

f32 = jnp.float32
bf16 = jnp.bfloat16


def _cparams(sem):
    return pltpu.CompilerParams(dimension_semantics=sem, vmem_limit_bytes=VMEM_LIMIT)


def _in_proj_kernel(x_ref, w_ref, o_ref):
    o_ref[...] = lax.dot_general(x_ref[...].astype(bf16), w_ref[...], (((1,), (0,)), ((), ())),
                                 preferred_element_type=f32)


def _in_proj(x2d, w):
    t, d = x2d.shape
    n = w.shape[1]
    tm, tn = 512, 2048
    return pl.pallas_call(
        _in_proj_kernel,
        grid=(n // tn, t // tm),
        in_specs=[pl.BlockSpec((tm, d), lambda j, i: (i, 0)),
                  pl.BlockSpec((d, tn), lambda j, i: (0, j))],
        out_specs=pl.BlockSpec((tm, tn), lambda j, i: (i, j)),
        out_shape=jax.ShapeDtypeStruct((t, n), f32),
        compiler_params=_cparams(("parallel", "parallel")),
        name="in_proj",
    )(x2d, w)


def _retention_kernel(q_ref, k_ref, v_ref, g_ref, cos_ref, sin_ref, dec_ref, qd_ref, kd_ref, cd_ref, gnw_ref,
                      o_ref):
    s = q_ref.shape[0]
    n_chunks = s // CHUNK
    decay = dec_ref[...]
    q_dec = qd_ref[...]
    k_dec = kd_ref[...]
    c_dec = cd_ref[0:1, :]
    gnw = gnw_ref[...]
    k_scale = RET_HEAD_DIM ** -0.5

    def rope(xv, cos, sin):
        return xv * cos + pltpu.roll(xv, RET_HEAD_DIM // 2, axis=1) * sin

    def body(n, state):
        sl = pl.ds(pl.multiple_of(n * CHUNK, CHUNK), CHUNK)
        cos = cos_ref[sl, :]
        sin = sin_ref[sl, :]
        q = rope(q_ref[sl, :], cos, sin)
        k = rope(k_ref[sl, :], cos, sin) * k_scale
        vb = v_ref[sl, :].astype(bf16)
        scores = lax.dot_general(q.astype(bf16), k.astype(bf16), (((1,), (1,)), ((), ())),
                                 preferred_element_type=f32) * decay
        intra = jnp.dot(scores.astype(bf16), vb, preferred_element_type=f32)
        cross = jnp.dot((q * q_dec).astype(bf16), state.astype(bf16), preferred_element_type=f32)
        kv = lax.dot_general((k * k_dec).astype(bf16), vb, (((0,), (0,)), ((), ())),
                             preferred_element_type=f32)
        ret = intra + cross
        mu = jnp.mean(ret, axis=-1, keepdims=True)
        cen = ret - mu
        var = jnp.mean(cen * cen, axis=-1, keepdims=True)
        ret = cen * lax.rsqrt(var + LN_EPS) * gnw
        g = g_ref[sl, :]
        o_ref[sl, :] = (g * jax.nn.sigmoid(g) * ret).astype(o_ref.dtype)
        return c_dec * state + kv

    lax.fori_loop(0, n_chunks, body, jnp.zeros((RET_HEAD_DIM, RET_HEAD_DIM), f32), unroll=4)


def _retention_tables(s):
    h, d = RET_HEADS, RET_HEAD_DIM
    inv = ROPE_BASE ** (-jnp.arange(0, d, 2, dtype=f32) / d)
    ang = jnp.arange(s, dtype=f32)[:, None] * inv[None, :]
    cos = jnp.cos(ang)
    sin = jnp.sin(ang)
    cos_t = jnp.concatenate([cos, cos], axis=-1)
    sin_t = jnp.concatenate([-sin, sin], axis=-1)
    log_gamma = jnp.log1p(-jnp.exp2(-5.0 - jnp.arange(h, dtype=f32)))
    idx = jnp.arange(CHUNK, dtype=f32)
    diff = idx[:, None] - idx[None, :]
    decay = jnp.where((diff >= 0)[None], jnp.exp(jnp.maximum(diff, 0.0)[None] * log_gamma[:, None, None]), 0.0)
    q_dec = jnp.exp((idx[None, :] + 1.0) * log_gamma[:, None])
    k_dec = jnp.exp((CHUNK - 1.0 - idx)[None, :] * log_gamma[:, None])
    c_dec = jnp.exp(CHUNK * log_gamma)
    q_dec = jnp.broadcast_to(q_dec[:, :, None], (h, CHUNK, d))
    k_dec = jnp.broadcast_to(k_dec[:, :, None], (h, CHUNK, d))
    c_dec = jnp.broadcast_to(c_dec[:, None, None], (h, SUBLANES, d))
    return cos_t, sin_t, decay, q_dec, k_dec, c_dec


def _retention(proj, ret_gn_w):
    b, s, _ = proj.shape
    d = RET_HEAD_DIM
    cos_t, sin_t, decay, q_dec, k_dec, c_dec = _retention_tables(s)
    col = lambda off: pl.BlockSpec((None, s, d), lambda bi, hi: (bi, 0, off + hi))
    per_head = lambda r: pl.BlockSpec((None, r, d), lambda bi, hi: (hi, 0, 0))
    full = lambda shp: pl.BlockSpec(shp, lambda bi, hi: (0,) * len(shp))
    return pl.pallas_call(
        _retention_kernel,
        grid=(b, RET_HEADS),
        in_specs=[col(0), col(RET_HEADS), col(2 * RET_HEADS), col(3 * RET_HEADS),
                  full((s, d)), full((s, d)),
                  per_head(CHUNK), per_head(CHUNK), per_head(CHUNK), per_head(SUBLANES),
                  pl.BlockSpec((1, d), lambda bi, hi: (0, hi))],
        out_specs=pl.BlockSpec((None, s, d), lambda bi, hi: (bi, 0, hi)),
        out_shape=jax.ShapeDtypeStruct((b, s, RET_WIDTH), bf16),
        compiler_params=_cparams(("parallel", "parallel")),
        name="retention",
    )(proj, proj, proj, proj, cos_t, sin_t, decay, q_dec, k_dec, c_dec, ret_gn_w.reshape(1, RET_WIDTH))


def _gelu_tanh(x):
    return 0.5 * x * (1.0 + jnp.tanh(math.sqrt(2.0 / math.pi) * (x + 0.044715 * (x * x * x))))


def _lru_kernel(xr_ref, yg_ref, cw_ref, cb_ref, wa_ref, ba_ref, wx_ref, bx_ref, lam_ref, o_ref, a_ref, b_ref):
    s = xr_ref.shape[0]
    x = xr_ref[...]
    rows = lax.broadcasted_iota(jnp.int32, x.shape, 0)
    xc = cb_ref[...] + cw_ref[CONV_WIDTH - 1:CONV_WIDTH, :] * x
    for back in range(1, CONV_WIDTH):
        shifted = jnp.where(rows >= back, pltpu.roll(x, back, axis=0), 0.0)
        xc = xc + cw_ref[CONV_WIDTH - 1 - back:CONV_WIDTH - back, :] * shifted
    xcb = xc.astype(bf16)
    r = jax.nn.sigmoid(jnp.dot(xcb, wa_ref[...].astype(bf16), preferred_element_type=f32) + ba_ref[...])
    gi = jax.nn.sigmoid(jnp.dot(xcb, wx_ref[...].astype(bf16), preferred_element_type=f32) + bx_ref[...])
    lam = lam_ref[...]
    log_sig = jnp.minimum(lam, 0.0) - jnp.log1p(jnp.exp(-jnp.abs(lam)))
    log_a = LRU_C * r * log_sig
    a = jnp.exp(log_a)
    a_ref[...] = a
    b_ref[...] = jnp.sqrt(-jnp.tanh(log_a) * (a * a + 1.0)) * (gi * xc)

    row8 = lax.broadcasted_iota(jnp.int32, (SUBLANES, LANES), 0)

    def body(i, h_prev):
        sl = pl.ds(pl.multiple_of(i * SUBLANES, SUBLANES), SUBLANES)
        a8 = a_ref[sl, :]
        b8 = b_ref[sl, :]
        for sh in (1, 2, 4):
            a_sh = jnp.where(row8 >= sh, pltpu.roll(a8, sh, axis=0), 1.0)
            b_sh = jnp.where(row8 >= sh, pltpu.roll(b8, sh, axis=0), 0.0)
            b8 = a8 * b_sh + b8
            a8 = a8 * a_sh
        h8 = a8 * h_prev + b8
        o_ref[sl, :] = (_gelu_tanh(yg_ref[sl, :]) * h8).astype(o_ref.dtype)
        return h8[SUBLANES - 1:SUBLANES, :]

    lax.fori_loop(0, s // SUBLANES, body, jnp.zeros((1, LANES), f32), unroll=8)


def _lru(proj, conv_w, conv_b, wa, ba, wx, bx, lam):
    b, s, _ = proj.shape
    d = LRU_BLOCK_DIM
    xr_off = 4 * RET_WIDTH // d
    yg_off = xr_off + LRU_BLOCKS
    col = lambda off: pl.BlockSpec((None, s, d), lambda bi, ji: (bi, 0, off + ji))
    vec = lambda r: pl.BlockSpec((r, d), lambda bi, ji: (0, ji))
    blk = lambda r: pl.BlockSpec((None, r, d), lambda bi, ji: (ji, 0, 0))
    return pl.pallas_call(
        _lru_kernel,
        grid=(b, LRU_BLOCKS),
        in_specs=[col(xr_off), col(yg_off), vec(CONV_WIDTH), vec(1), blk(d), blk(1), blk(d), blk(1), vec(1)],
        out_specs=pl.BlockSpec((None, s, d), lambda bi, ji: (bi, 0, ji)),
        out_shape=jax.ShapeDtypeStruct((b, s, LRU_WIDTH), bf16),
        scratch_shapes=[pltpu.VMEM((s, d), f32), pltpu.VMEM((s, d), f32)],
        compiler_params=_cparams(("parallel", "parallel")),
        name="rg_lru",
    )(proj, proj, conv_w, conv_b.reshape(1, LRU_WIDTH), wa, ba.reshape(LRU_BLOCKS, 1, d), wx,
      bx.reshape(LRU_BLOCKS, 1, d), lam.reshape(1, LRU_WIDTH))


def _layer_norm(y, w, b):
    mu = jnp.mean(y, axis=-1, keepdims=True)
    cen = y - mu
    var = jnp.mean(cen * cen, axis=-1, keepdims=True)
    return cen * lax.rsqrt(var + LN_EPS) * w + b


HALF = D_MODEL // 2
SLAB = HALF // LANES
PITCH = SLAB + 4
u32 = jnp.uint32
HIGH_MASK = 0xFFFF0000


def _bf16_bits(v):
    return lax.bitcast_convert_type(v.astype(bf16).astype(f32), u32)


def _store_slabs(ref, val):
    n = val.shape[0]
    for j in range(SLAB):
        lo = _bf16_bits(val[:, j * LANES:(j + 1) * LANES])
        hi = _bf16_bits(val[:, HALF + j * LANES:HALF + (j + 1) * LANES])
        ref[pl.ds(j, n, stride=PITCH), :] = hi | lax.shift_right_logical(lo, jnp.full_like(lo, 16))
    for j in range(SLAB, PITCH):
        ref[pl.ds(j, n, stride=PITCH), :] = jnp.zeros((n, LANES), u32)


def _slab_cols(ref, j, n):
    w = ref[pl.ds(j, n, stride=PITCH), :]
    lo = lax.bitcast_convert_type(lax.shift_left(w, jnp.full_like(w, 16)), f32)
    hi = lax.bitcast_convert_type(w & jnp.full_like(w, HIGH_MASK), f32)
    return lo, hi


def _split_bf16(v):
    hi = v.astype(bf16)
    lo = (v - hi.astype(f32)).astype(bf16)
    return hi, lo


def _out_router_kernel(ret_ref, lru_ref, wo_ref, x_ref, lnw_ref, lnb_ref, wr_ref, br_ref,
                       x1_ref, x1s_ref, tope_ref, gate_ref):
    m = jnp.dot(ret_ref[...], wo_ref[0:RET_WIDTH, :], preferred_element_type=f32)
    m = m + jnp.dot(lru_ref[...], wo_ref[RET_WIDTH:D_MODEL, :], preferred_element_type=f32)
    x1 = _layer_norm(DN_ALPHA * x_ref[...] + m, lnw_ref[...], lnb_ref[...])
    x1_ref[...] = x1
    _store_slabs(x1s_ref, x1)
    xh, xl = _split_bf16(x1)
    wh, wl = _split_bf16(wr_ref[...])
    logits = (jnp.dot(xh, wh, preferred_element_type=f32) + jnp.dot(xl, wh, preferred_element_type=f32)
              + jnp.dot(xh, wl, preferred_element_type=f32)) + br_ref[...]
    tm = logits.shape[0]
    lane = lax.broadcasted_iota(jnp.int32, logits.shape, 1)
    lane_k = lax.broadcasted_iota(jnp.int32, (tm, TOP_K), 1)
    top_e = jnp.zeros((tm, TOP_K), jnp.int32)
    top_v = jnp.zeros((tm, TOP_K), f32)
    cur = logits
    for kk in range(TOP_K):
        mx = jnp.max(cur, axis=-1, keepdims=True)
        idx = jnp.min(jnp.where(cur == mx, lane, N_EXPERTS), axis=-1, keepdims=True)
        top_e = jnp.where(lane_k == kk, idx, top_e)
        top_v = jnp.where(lane_k == kk, mx, top_v)
        cur = jnp.where(lane == idx, -jnp.inf, cur)
    ex = jnp.exp(top_v - top_v[:, 0:1])
    gate_ref[...] = ex / jnp.sum(ex, axis=-1, keepdims=True)
    tope_ref[...] = top_e


def _out_router(ret_out, lru_out, wo_bf, x2d, ln_w, ln_b, w_router, b_router):
    t, d = x2d.shape
    tm = 512
    row = lambda c: pl.BlockSpec((tm, c), lambda i: (i, 0))
    full = lambda r, c: pl.BlockSpec((r, c), lambda i: (0, 0))
    return pl.pallas_call(
        _out_router_kernel,
        grid=(t // tm,),
        in_specs=[row(RET_WIDTH), row(LRU_WIDTH), full(d, d), row(d), full(1, d), full(1, d),
                  full(d, N_EXPERTS), full(1, N_EXPERTS)],
        out_specs=[row(d), pl.BlockSpec((tm * PITCH, LANES), lambda i: (i, 0)), row(TOP_K), row(TOP_K)],
        out_shape=[jax.ShapeDtypeStruct((t, d), f32), jax.ShapeDtypeStruct((t * PITCH, LANES), u32),
                   jax.ShapeDtypeStruct((t, TOP_K), jnp.int32),
                   jax.ShapeDtypeStruct((t, TOP_K), f32)],
        compiler_params=_cparams(("parallel",)),
        name="out_proj_ln1_router",
    )(ret_out, lru_out, wo_bf, x2d, ln_w.reshape(1, d), ln_b.reshape(1, d), w_router,
      b_router.reshape(1, N_EXPERTS))


def _routing_tables(top_e, t):
    n_pad = t * TOP_K + N_EXPERTS * ROW_CHUNK
    max_units = N_EXPERTS + (t * TOP_K) // UNIT_ROWS
    sel = (top_e[:, :, None] == jnp.arange(N_EXPERTS, dtype=jnp.int32)[None, None, :]).any(axis=1)
    sel = sel.astype(jnp.int32)
    counts = jnp.sum(sel, axis=0)
    rank = jnp.cumsum(sel, axis=0) - sel
    padded = (counts + ROW_CHUNK - 1) // ROW_CHUNK * ROW_CHUNK
    pad_ends = jnp.cumsum(padded)
    pad_starts = pad_ends - padded
    dest_dense = pad_starts[None, :] + rank
    dest = jnp.take_along_axis(dest_dense, top_e, axis=1)
    tok = jnp.broadcast_to(jnp.arange(t, dtype=jnp.int32)[:, None], (t, TOP_K))
    slot_tok = jnp.zeros((n_pad + UNIT_ROWS,), jnp.int32).at[dest.reshape(-1)].set(tok.reshape(-1) * PITCH)
    units_per_e = (padded + UNIT_ROWS - 1) // UNIT_ROWS
    unit_ends = jnp.cumsum(units_per_e)
    unit_starts = unit_ends - units_per_e
    n_units = unit_ends[-1]
    u = jnp.arange(max_units, dtype=jnp.int32)
    u_clamped = jnp.minimum(u, n_units - 1)
    ue = jnp.searchsorted(unit_ends, u_clamped, side='right').astype(jnp.int32)
    ue = jnp.minimum(ue, N_EXPERTS - 1)
    j = u_clamped - unit_starts[ue]
    u_row = pad_starts[ue] + j * UNIT_ROWS
    u_rows = jnp.minimum(UNIT_ROWS, padded[ue] - j * UNIT_ROWS)
    u_chunks = jnp.where(u < n_units, u_rows // ROW_CHUNK, 0).astype(jnp.int32)
    n_used_rows = pad_ends[-1].astype(jnp.int32)
    return dest.astype(jnp.int32), slot_tok, ue, u_row.astype(jnp.int32), u_chunks, n_used_rows, n_pad, max_units


CHUNK_PITCHED = ROW_CHUNK * PITCH
OUT_SLOTS = 4


def _moe_kernel(ue_ref, urow_ref, uchunks_ref, used_ref, tok_hbm, x1s_hbm, wg_ref, bg_ref, wu_ref, bu_ref, wd_ref,
                bd_ref, y_hbm, xbuf, yacc, gstage, ostage, tok_smem, flags, tok_sem, in_sem, out_sem):
    u = pl.program_id(0)
    f = pl.program_id(1)
    n_u = pl.num_programs(0)
    n_f = pl.num_programs(1)
    n_chunks = uchunks_ref[u]
    row0 = urow_ref[u]
    cur = u % 2
    nxt = 1 - cur
    u_next = jnp.minimum(u + 1, n_u - 1)
    n_next = jnp.where(u + 1 < n_u, uchunks_ref[u_next], 0)
    u_next2 = jnp.minimum(u + 2, n_u - 1)
    n_next2 = jnp.where(u + 2 < n_u, uchunks_ref[u_next2], 0)
    TABLES = 3
    tcur = u % TABLES
    tnxt = (u + 1) % TABLES
    first_step = jnp.logical_and(u == 0, f == 0)
    PENDING, PEND_CHUNK, PEND_SLOT, OUT_BUSY = 0, 1, 2, 3

    def rows(c, size=ROW_CHUNK):
        return pl.ds(pl.multiple_of(c * size, size), size)

    def tok_copy(unit_row, slot):
        src = tok_hbm.at[pl.ds(pl.multiple_of(unit_row, ROW_CHUNK), UNIT_ROWS)]
        dst = tok_smem.at[pl.ds(pl.multiple_of(slot * UNIT_ROWS, UNIT_ROWS), UNIT_ROWS)]
        return pltpu.make_async_copy(src, dst, tok_sem)

    ISSUE_GROUP = 8

    def gather_start(c, tslot, lo=0, hi=ROW_CHUNK):
        base = tslot * UNIT_ROWS + c * ROW_CHUNK

        def issue(g, carry):
            for i in range(ISSUE_GROUP):
                r = g * ISSUE_GROUP + i
                src = x1s_hbm.at[pl.ds(tok_smem[base + r], SLAB), :]
                dst = gstage.at[pl.ds(r * PITCH, SLAB), :]
                pltpu.make_async_copy(src, dst, in_sem).start()
            return carry

        lax.fori_loop(lo // ISSUE_GROUP, hi // ISSUE_GROUP, issue, 0)

    def gather_finish(c, xslot):
        n = ROW_CHUNK * SLAB
        pltpu.make_async_copy(x1s_hbm.at[pl.ds(0, n), :], gstage.at[pl.ds(0, n), :], in_sem).wait()
        for j in range(SLAB):
            lo, hi = _slab_cols(gstage, j, ROW_CHUNK)
            xbuf[xslot, rows(c), j * LANES:(j + 1) * LANES] = lo.astype(bf16)
            xbuf[xslot, rows(c), HALF + j * LANES:HALF + (j + 1) * LANES] = hi.astype(bf16)

    def out_copy(c, slot):
        dst = y_hbm.at[pl.ds(pl.multiple_of((row0 + c * ROW_CHUNK) * PITCH, CHUNK_PITCHED), CHUNK_PITCHED), :]
        return pltpu.make_async_copy(ostage.at[slot], dst, out_sem.at[slot])

    def out_wait(slot):
        @pl.when(flags[OUT_BUSY + slot] == 1)
        def _():
            pltpu.make_async_copy(ostage.at[slot], y_hbm.at[pl.ds(0, CHUNK_PITCHED), :], out_sem.at[slot]).wait()
            flags[OUT_BUSY + slot] = 0

    @pl.when(first_step)
    def _():
        for i in range(OUT_BUSY + OUT_SLOTS):
            flags[i] = 0

        @pl.when(n_chunks > 0)
        def _():
            first_table = tok_copy(row0, tcur)
            first_table.start()
            first_table.wait()

            def load(c, carry):
                gather_start(c, tcur)
                gather_finish(c, cur)
                return carry

            lax.fori_loop(0, n_chunks, load, 0)

        @pl.when(n_next > 0)
        def _():
            tok_copy(urow_ref[u_next], tnxt).start()

    @pl.when(flags[PENDING] == 1)
    def _():
        gather_finish(flags[PEND_CHUNK], flags[PEND_SLOT])
        flags[PENDING] = 0

    @pl.when(jnp.logical_and(f == 0, n_next > 0))
    def _():
        tok_copy(urow_ref[u_next], tnxt).wait()

        @pl.when(n_next2 > 0)
        def _():
            tok_copy(urow_ref[u_next2], (u + 2) % TABLES).start()

    prefetching = f < n_next
    n_quads = n_chunks // 4
    has_pair = (n_chunks % 4) // 2
    has_single = n_chunks % 2
    n_blocks = n_quads + has_pair + has_single
    share = (ROW_CHUNK // jnp.maximum(n_blocks, 1) + ISSUE_GROUP - 1) // ISSUE_GROUP * ISSUE_GROUP

    def prefetch_share(i):
        lo = jnp.minimum(i * share, ROW_CHUNK)
        hi = jnp.where(i == n_blocks - 1, ROW_CHUNK, jnp.minimum(lo + share, ROW_CHUNK))
        gather_start(f, tnxt, lo, jnp.where(prefetching, hi, lo))

    @pl.when(prefetching)
    def _():
        flags[PENDING] = 1
        flags[PEND_CHUNK] = f
        flags[PEND_SLOT] = nxt

    @pl.when(n_chunks > 0)
    def _():
        bg = bg_ref[...]
        bu = bu_ref[...]
        mm = lambda a, w: lax.dot_general(a, w, (((1,), (0,)), ((), ())), preferred_element_type=f32)

        def mlp(sl, first):
            xc = xbuf[cur, sl, :]
            gt = jnp.minimum(mm(xc, wg_ref[...]) + bg, SWIGLU_LIMIT)
            up = jnp.clip(mm(xc, wu_ref[...]) + bu, -SWIGLU_LIMIT, SWIGLU_LIMIT)
            hid = (up + 1.0) * gt * jax.nn.sigmoid(SWIGLU_ALPHA * gt)
            part = mm(hid.astype(bf16), wd_ref[...])
            if first:
                yacc[sl, :] = part
            else:
                yacc[sl, :] += part

        def blocks(first):
            def quad(c, carry):
                prefetch_share(c)
                mlp(rows(c, 4 * ROW_CHUNK), first)
                return carry

            lax.fori_loop(0, n_quads, quad, 0)

            @pl.when(has_pair == 1)
            def _():
                prefetch_share(n_quads)
                mlp(pl.ds(pl.multiple_of(n_quads * 4 * ROW_CHUNK, 2 * ROW_CHUNK), 2 * ROW_CHUNK), first)

            @pl.when(has_single == 1)
            def _():
                prefetch_share(n_quads + has_pair)
                mlp(rows(n_chunks - 1), first)

        @pl.when(f == 0)
        def _():
            blocks(True)

        @pl.when(f > 0)
        def _():
            blocks(False)

    @pl.when(jnp.logical_and(f == n_f - 1, n_chunks > 0))
    def _():
        bd = bd_ref[...]

        def store(c, carry):
            slot = c % OUT_SLOTS
            out_wait(slot)
            _store_slabs(ostage.at[slot], yacc[rows(c), :] + bd)
            out_copy(c, slot).start()
            flags[OUT_BUSY + slot] = 1
            return carry

        lax.fori_loop(0, n_chunks, store, 0)

    @pl.when(jnp.logical_and(u == n_u - 1, f == n_f - 1))
    def _():
        for s in range(OUT_SLOTS):
            out_wait(s)
        first = used_ref[0] // ROW_CHUNK
        last = y_hbm.shape[0] // CHUNK_PITCHED
        ostage[0] = jnp.zeros((CHUNK_PITCHED, LANES), u32)

        def tail_copy(c):
            dst = y_hbm.at[pl.ds(pl.multiple_of(c * CHUNK_PITCHED, CHUNK_PITCHED), CHUNK_PITCHED), :]
            return pltpu.make_async_copy(ostage.at[0], dst, out_sem.at[0])

        def start(c, carry):
            tail_copy(c).start()
            return carry

        def wait(c, carry):
            tail_copy(c).wait()
            return carry

        lax.fori_loop(first, last, start, 0)
        lax.fori_loop(first, last, wait, 0)


def _moe_experts(x1s, slot_tok, ue, u_row, u_chunks, n_used_rows, n_pad, max_units, w_gate, b_gate, w_up, b_up,
                 w_down, b_down):
    e, d, dff = w_gate.shape
    n_f = dff // F_TILE
    assert n_f >= UNIT_ROWS // ROW_CHUNK, "one chunk of the next unit is gathered per f-tile step"
    f_idx = lambda u, f, uc: jnp.where(uc[u] > 0, f, n_f - 1)
    col_w = pl.BlockSpec((None, d, F_TILE), lambda u, f, ue, ur, uc, used: (ue[u], 0, f_idx(u, f, uc)))
    col_b = pl.BlockSpec((None, 1, F_TILE), lambda u, f, ue, ur, uc, used: (ue[u], 0, f_idx(u, f, uc)))
    return pl.pallas_call(
        _moe_kernel,
        grid_spec=pltpu.PrefetchScalarGridSpec(
            num_scalar_prefetch=4,
            grid=(max_units, n_f),
            in_specs=[pl.BlockSpec(memory_space=pl.ANY), pl.BlockSpec(memory_space=pl.ANY),
                      col_w, col_b, col_w, col_b,
                      pl.BlockSpec((None, F_TILE, d), lambda u, f, ue, ur, uc, used: (ue[u], f_idx(u, f, uc), 0)),
                      pl.BlockSpec((None, 1, d), lambda u, f, ue, ur, uc, used: (ue[u], 0, 0))],
            out_specs=pl.BlockSpec(memory_space=pl.ANY),
            scratch_shapes=[pltpu.VMEM((2, UNIT_ROWS, d), bf16),
                            pltpu.VMEM((UNIT_ROWS, d), f32),
                            pltpu.VMEM((CHUNK_PITCHED, LANES), u32),
                            pltpu.VMEM((OUT_SLOTS, CHUNK_PITCHED, LANES), u32),
                            pltpu.SMEM((3 * UNIT_ROWS,), jnp.int32),
                            pltpu.SMEM((3 + OUT_SLOTS,), jnp.int32),
                            pltpu.SemaphoreType.DMA(()),
                            pltpu.SemaphoreType.DMA(()),
                            pltpu.SemaphoreType.DMA((OUT_SLOTS,))],
        ),
        out_shape=jax.ShapeDtypeStruct((n_pad * PITCH, LANES), u32),
        compiler_params=_cparams(("arbitrary", "arbitrary")),
        name="moe_experts",
    )(ue, u_row, u_chunks, n_used_rows.reshape(1), slot_tok, x1s, w_gate, b_gate.reshape(e, 1, dff), w_up,
      b_up.reshape(e, 1, dff), w_down, b_down.reshape(e, 1, d))


COMBINE_ROWS = 256
COMBINE_GROUPS = 2


def _combine_kernel(dest_ref, dest_next_ref, y_hbm, gate_ref, x1_ref, p_ref, lnw_ref, lnb_ref, wp_ref, pnw_ref,
                    wg_ref, o_ref, ybuf_a, ybuf_b, fsum_ref, sem_a, sem_b):
    tm = COMBINE_ROWS
    s = pl.program_id(0)

    def row_copy(table, table_off, t, k, ybuf, sem):
        src = y_hbm.at[pl.ds(table[(table_off + t) * TOP_K + k] * PITCH, SLAB), :]
        return pltpu.make_async_copy(src, ybuf.at[k, pl.ds(t * PITCH, SLAB), :], sem)

    def wait_tile(ybuf, sem):
        for k in range(TOP_K):
            pltpu.make_async_copy(y_hbm.at[pl.ds(0, tm * SLAB), :], ybuf.at[k, pl.ds(0, tm * SLAB), :], sem).wait()

    @pl.when(s == 0)
    def _():
        def issue(t, c):
            for k in range(TOP_K):
                row_copy(dest_ref, 0, t, k, ybuf_a, sem_a).start()
            return c

        lax.fori_loop(0, tm, issue, 0, unroll=2)

    def finish_tile(row0, ybuf, next_table, next_off, next_buf, next_sem):
        tok = pl.ds(row0, tm)
        pieces = SLAB * TOP_K
        per_piece = tm // pieces
        e = jnp.dot(p_ref[tok, :].astype(bf16), wp_ref[...], preferred_element_type=f32)
        e = e * lax.rsqrt(jnp.mean(e * e, axis=-1, keepdims=True) + LN_EPS) * pnw_ref[...]
        gates = gate_ref[tok, :]
        for j in range(SLAB):
            acc_lo, acc_hi = None, None
            for k in range(TOP_K):
                piece = j * TOP_K + k
                for t in range(piece * per_piece, (piece + 1) * per_piece):
                    for kk in range(TOP_K):
                        row_copy(next_table, next_off, t, kk, next_buf, next_sem).start()
                lo, hi = _slab_cols(ybuf.at[k], j, tm)
                g = gates[:, k:k + 1]
                acc_lo = g * lo if k == 0 else acc_lo + g * lo
                acc_hi = g * hi if k == 0 else acc_hi + g * hi
            fsum_ref[:, j * LANES:(j + 1) * LANES] = acc_lo
            fsum_ref[:, HALF + j * LANES:HALF + (j + 1) * LANES] = acc_hi
        x2 = _layer_norm(DN_ALPHA * x1_ref[tok, :] + fsum_ref[...], lnw_ref[...], lnb_ref[...])
        gate = jax.nn.sigmoid(jnp.dot(x2.astype(bf16), wg_ref[...], preferred_element_type=f32))
        o_ref[tok, :] = x2 + gate * e

    wait_tile(ybuf_a, sem_a)
    finish_tile(0, ybuf_a, dest_ref, tm, ybuf_b, sem_b)
    wait_tile(ybuf_b, sem_b)
    finish_tile(tm, ybuf_b, dest_next_ref, 0, ybuf_a, sem_a)

    @pl.when(s == pl.num_programs(0) - 1)
    def _():
        wait_tile(ybuf_a, sem_a)


def _combine(y, dest, gates, x1, p2d, ln_w, ln_b, wp_bf, ple_norm_w, wg_bf):
    t, d = x1.shape
    tm = COMBINE_ROWS
    n_steps = t // (2 * tm)
    row = lambda c: pl.BlockSpec((2 * tm, c), lambda i: (i, 0))
    full = lambda r, c: pl.BlockSpec((r, c), lambda i: (0, 0))
    table = lambda nxt: pl.BlockSpec((2 * tm * TOP_K,), lambda i: (jnp.minimum(i + nxt, n_steps - 1),),
                                     memory_space=pltpu.SMEM)
    dest_flat = dest.reshape(-1)
    return pl.pallas_call(
        _combine_kernel,
        grid=(n_steps,),
        in_specs=[table(0), table(1), pl.BlockSpec(memory_space=pl.ANY),
                  row(TOP_K), row(d), row(PLE_DIM), full(1, d), full(1, d), full(PLE_DIM, d), full(1, d),
                  full(d, d)],
        out_specs=row(d),
        out_shape=jax.ShapeDtypeStruct((t, d), f32),
        scratch_shapes=[pltpu.VMEM((TOP_K, tm * PITCH, LANES), u32), pltpu.VMEM((TOP_K, tm * PITCH, LANES), u32),
                        pltpu.VMEM((tm, d), f32), pltpu.SemaphoreType.DMA(()), pltpu.SemaphoreType.DMA(())],
        compiler_params=_cparams(("arbitrary",)),
        name="combine_ln2_ple",
    )(dest_flat, dest_flat, y, gates, x1, p2d, ln_w.reshape(1, d), ln_b.reshape(1, d), wp_bf,
      ple_norm_w.reshape(1, d), wg_bf)


def _layer(h, p_i, w_in, ret_gn_w, conv_w, conv_b, lru_wa, lru_ba, lru_wx, lru_bx, lru_lam, w_out,
           ln1_w, ln1_b, w_router, b_router, w_gate, b_gate, w_up, b_up, w_down, b_down,
           ln2_w, ln2_b, w_ple_proj, ple_norm_w, w_ple_gate):
    b, s, d = h.shape
    t = b * s
    x2d = h.reshape(t, d)
    proj = _in_proj(x2d, w_in).reshape(b, s, IN_COLS)
    ret_out = _retention(proj, ret_gn_w)
    lru_out = _lru(proj, conv_w, conv_b, lru_wa, lru_ba, lru_wx, lru_bx, lru_lam)
    x1, x1s, top_e, gates = _out_router(ret_out.reshape(t, RET_WIDTH), lru_out.reshape(t, LRU_WIDTH),
                                        w_out.astype(bf16), x2d, ln1_w, ln1_b, w_router, b_router)
    dest, slot_tok, ue, u_row, u_chunks, n_used_rows, n_pad, max_units = _routing_tables(top_e, t)
    y = _moe_experts(x1s, slot_tok, ue, u_row, u_chunks, n_used_rows, n_pad, max_units, w_gate, b_gate, w_up, b_up,
                     w_down, b_down)
    out = _combine(y, dest, gates, x1, p_i.reshape(t, PLE_DIM), ln2_w, ln2_b, w_ple_proj.astype(bf16),
                   ple_norm_w, w_ple_gate.astype(bf16))
    return out.reshape(b, s, d)


def kernel(x, p, w_in, ret_gn_w, conv_w, conv_b, lru_wa, lru_ba, lru_wx, lru_bx, lru_lam, w_out, ln1_w, ln1_b,
           w_router, b_router, w_gate, b_gate, w_up, b_up, w_down, b_down, ln2_w, ln2_b, w_ple_proj, ple_norm_w,
           w_ple_gate):
    h = x.astype(f32)
    for i in range(w_in.shape[0]):
        h = _layer(h, p[i], w_in[i], ret_gn_w[i], conv_w[i], conv_b[i], lru_wa[i], lru_ba[i], lru_wx[i],
                   lru_bx[i], lru_lam[i], w_out[i], ln1_w[i], ln1_b[i], w_router[i], b_router[i], w_gate[i],
                   b_gate[i], w_up[i], b_up[i], w_down[i], b_down[i], ln2_w[i], ln2_b[i], w_ple_proj[i],
                   ple_norm_w[i], w_ple_gate[i])
    return h.astype(x.dtype)
```

```python
import functools
import math

import jax
import jax.numpy as jnp
from jax import lax
from jax.experimental import pallas as pl
from jax.experimental.pallas import tpu as pltpu

D_MODEL = 2048
RET_HEAD_DIM = 128
RET_HEADS = 8
RET_WIDTH = RET_HEADS * RET_HEAD_DIM
LRU_WIDTH = D_MODEL - RET_WIDTH
LRU_BLOCKS = 8
LRU_BLOCK_DIM = LRU_WIDTH // LRU_BLOCKS
IN_COLS = 4 * RET_WIDTH + 2 * LRU_WIDTH
CONV_WIDTH = 4
LRU_C = 8.0
CHUNK = 128
ROPE_BASE = 10000.0
N_EXPERTS = 32
TOP_K = 4
SWIGLU_LIMIT = 7.0
SWIGLU_ALPHA = 1.702
PLE_DIM = 256
LN_EPS = 1e-5
DEPTH = 1
DN_ALPHA = (2.0 * DEPTH) ** 0.25

LANES = 128
SUBLANES = 8
VMEM_LIMIT = 60 * 1024 * 1024

ROW_CHUNK = 256
UNIT_ROWS = 2048
F_TILE = 256

f32 = jnp.float32
bf16 = jnp.bfloat16


def _cparams(sem):
    return pltpu.CompilerParams(dimension_semantics=sem, vmem_limit_bytes=VMEM_LIMIT)


def _in_proj_kernel(x_ref, w_ref, o_ref):
    o_ref[...] = lax.dot_general(x_ref[...].astype(bf16), w_ref[...], (((1,), (0,)), ((), ())),
                                 preferred_element_type=f32)


def _in_proj(x2d, w):
    t, d = x2d.shape
    n = w.shape[1]
    tm, tn = 512, 2048
    return pl.pallas_call(
        _in_proj_kernel,
        grid=(n // tn, t // tm),
        in_specs=[pl.BlockSpec((tm, d), lambda j, i: (i, 0)),
                  pl.BlockSpec((d, tn), lambda j, i: (0, j))],
        out_specs=pl.BlockSpec((tm, tn), lambda j, i: (i, j)),
        out_shape=jax.ShapeDtypeStruct((t, n), f32),
        compiler_params=_cparams(("parallel", "parallel")),
        name="in_proj",
    )(x2d, w)


def _retention_kernel(q_ref, k_ref, v_ref, g_ref, cos_ref, sin_ref, dec_ref, qd_ref, kd_ref, cd_ref, gnw_ref,
                      o_ref):
    s = q_ref.shape[0]
    n_chunks = s // CHUNK
    decay = dec_ref[...]
    q_dec = qd_ref[...]
    k_dec = kd_ref[...]
    c_dec = cd_ref[0:1, :]
    gnw = gnw_ref[...]
    k_scale = RET_HEAD_DIM ** -0.5

    def rope(xv, cos, sin):
        return xv * cos + pltpu.roll(xv, RET_HEAD_DIM // 2, axis=1) * sin

    def body(n, state):
        sl = pl.ds(pl.multiple_of(n * CHUNK, CHUNK), CHUNK)
        cos = cos_ref[sl, :]
        sin = sin_ref[sl, :]
        q = rope(q_ref[sl, :], cos, sin)
        k = rope(k_ref[sl, :], cos, sin) * k_scale
        vb = v_ref[sl, :].astype(bf16)
        scores = lax.dot_general(q.astype(bf16), k.astype(bf16), (((1,), (1,)), ((), ())),
                                 preferred_element_type=f32) * decay
        intra = jnp.dot(scores.astype(bf16), vb, preferred_element_type=f32)
        cross = jnp.dot((q * q_dec).astype(bf16), state.astype(bf16), preferred_element_type=f32)
        kv = lax.dot_general((k * k_dec).astype(bf16), vb, (((0,), (0,)), ((), ())),
                             preferred_element_type=f32)
        ret = intra + cross
        mu = jnp.mean(ret, axis=-1, keepdims=True)
        cen = ret - mu
        var = jnp.mean(cen * cen, axis=-1, keepdims=True)
        ret = cen * lax.rsqrt(var + LN_EPS) * gnw
        g = g_ref[sl, :]
        o_ref[sl, :] = (g * jax.nn.sigmoid(g) * ret).astype(o_ref.dtype)
        return c_dec * state + kv

    lax.fori_loop(0, n_chunks, body, jnp.zeros((RET_HEAD_DIM, RET_HEAD_DIM), f32), unroll=4)


def _retention_tables(s):
    h, d = RET_HEADS, RET_HEAD_DIM
    inv = ROPE_BASE ** (-jnp.arange(0, d, 2, dtype=f32) / d)
    ang = jnp.arange(s, dtype=f32)[:, None] * inv[None, :]
    cos = jnp.cos(ang)
    sin = jnp.sin(ang)
    cos_t = jnp.concatenate([cos, cos], axis=-1)
    sin_t = jnp.concatenate([-sin, sin], axis=-1)
    log_gamma = jnp.log1p(-jnp.exp2(-5.0 - jnp.arange(h, dtype=f32)))
    idx = jnp.arange(CHUNK, dtype=f32)
    diff = idx[:, None] - idx[None, :]
    decay = jnp.where((diff >= 0)[None], jnp.exp(jnp.maximum(diff, 0.0)[None] * log_gamma[:, None, None]), 0.0)
    q_dec = jnp.exp((idx[None, :] + 1.0) * log_gamma[:, None])
    k_dec = jnp.exp((CHUNK - 1.0 - idx)[None, :] * log_gamma[:, None])
    c_dec = jnp.exp(CHUNK * log_gamma)
    q_dec = jnp.broadcast_to(q_dec[:, :, None], (h, CHUNK, d))
    k_dec = jnp.broadcast_to(k_dec[:, :, None], (h, CHUNK, d))
    c_dec = jnp.broadcast_to(c_dec[:, None, None], (h, SUBLANES, d))
    return cos_t, sin_t, decay, q_dec, k_dec, c_dec


def _retention(proj, ret_gn_w):
    b, s, _ = proj.shape
    d = RET_HEAD_DIM
    cos_t, sin_t, decay, q_dec, k_dec, c_dec = _retention_tables(s)
    col = lambda off: pl.BlockSpec((None, s, d), lambda bi, hi: (bi, 0, off + hi))
    per_head = lambda r: pl.BlockSpec((None, r, d), lambda bi, hi: (hi, 0, 0))
    full = lambda shp: pl.BlockSpec(shp, lambda bi, hi: (0,) * len(shp))
    return pl.pallas_call(
        _retention_kernel,
        grid=(b, RET_HEADS),
        in_specs=[col(0), col(RET_HEADS), col(2 * RET_HEADS), col(3 * RET_HEADS),
                  full((s, d)), full((s, d)),
                  per_head(CHUNK), per_head(CHUNK), per_head(CHUNK), per_head(SUBLANES),
                  pl.BlockSpec((1, d), lambda bi, hi: (0, hi))],
        out_specs=pl.BlockSpec((None, s, d), lambda bi, hi: (bi, 0, hi)),
        out_shape=jax.ShapeDtypeStruct((b, s, RET_WIDTH), bf16),
        compiler_params=_cparams(("parallel", "parallel")),
        name="retention",
    )(proj, proj, proj, proj, cos_t, sin_t, decay, q_dec, k_dec, c_dec, ret_gn_w.reshape(1, RET_WIDTH))


def _gelu_tanh(x):
    return 0.5 * x * (1.0 + jnp.tanh(math.sqrt(2.0 / math.pi) * (x + 0.044715 * (x * x * x))))


def _lru_kernel(xr_ref, yg_ref, cw_ref, cb_ref, wa_ref, ba_ref, wx_ref, bx_ref, lam_ref, o_ref, a_ref, b_ref):
    s = xr_ref.shape[0]
    x = xr_ref[...]
    rows = lax.broadcasted_iota(jnp.int32, x.shape, 0)
    xc = cb_ref[...] + cw_ref[CONV_WIDTH - 1:CONV_WIDTH, :] * x
    for back in range(1, CONV_WIDTH):
        shifted = jnp.where(rows >= back, pltpu.roll(x, back, axis=0), 0.0)
        xc = xc + cw_ref[CONV_WIDTH - 1 - back:CONV_WIDTH - back, :] * shifted
    xcb = xc.astype(bf16)
    r = jax.nn.sigmoid(jnp.dot(xcb, wa_ref[...].astype(bf16), preferred_element_type=f32) + ba_ref[...])
    gi = jax.nn.sigmoid(jnp.dot(xcb, wx_ref[...].astype(bf16), preferred_element_type=f32) + bx_ref[...])
    lam = lam_ref[...]
    log_sig = jnp.minimum(lam, 0.0) - jnp.log1p(jnp.exp(-jnp.abs(lam)))
    log_a = LRU_C * r * log_sig
    a = jnp.exp(log_a)
    a_ref[...] = a
    b_ref[...] = jnp.sqrt(-jnp.tanh(log_a) * (a * a + 1.0)) * (gi * xc)

    row8 = lax.broadcasted_iota(jnp.int32, (SUBLANES, LANES), 0)

    def body(i, h_prev):
        sl = pl.ds(pl.multiple_of(i * SUBLANES, SUBLANES), SUBLANES)
        a8 = a_ref[sl, :]
        b8 = b_ref[sl, :]
        for sh in (1, 2, 4):
            a_sh = jnp.where(row8 >= sh, pltpu.roll(a8, sh, axis=0), 1.0)
            b_sh = jnp.where(row8 >= sh, pltpu.roll(b8, sh, axis=0), 0.0)
            b8 = a8 * b_sh + b8
            a8 = a8 * a_sh
        h8 = a8 * h_prev + b8
        o_ref[sl, :] = (_gelu_tanh(yg_ref[sl, :]) * h8).astype(o_ref.dtype)
        return h8[SUBLANES - 1:SUBLANES, :]

    lax.fori_loop(0, s // SUBLANES, body, jnp.zeros((1, LANES), f32), unroll=8)


def _lru(proj, conv_w, conv_b, wa, ba, wx, bx, lam):
    b, s, _ = proj.shape
    d = LRU_BLOCK_DIM
    xr_off = 4 * RET_WIDTH // d
    yg_off = xr_off + LRU_BLOCKS
    col = lambda off: pl.BlockSpec((None, s, d), lambda bi, ji: (bi, 0, off + ji))
    vec = lambda r: pl.BlockSpec((r, d), lambda bi, ji: (0, ji))
    blk = lambda r: pl.BlockSpec((None, r, d), lambda bi, ji: (ji, 0, 0))
    return pl.pallas_call(
        _lru_kernel,
        grid=(b, LRU_BLOCKS),
        in_specs=[col(xr_off), col(yg_off), vec(CONV_WIDTH), vec(1), blk(d), blk(1), blk(d), blk(1), vec(1)],
        out_specs=pl.BlockSpec((None, s, d), lambda bi, ji: (bi, 0, ji)),
        out_shape=jax.ShapeDtypeStruct((b, s, LRU_WIDTH), bf16),
        scratch_shapes=[pltpu.VMEM((s, d), f32), pltpu.VMEM((s, d), f32)],
        compiler_params=_cparams(("parallel", "parallel")),
        name="rg_lru",
    )(proj, proj, conv_w, conv_b.reshape(1, LRU_WIDTH), wa, ba.reshape(LRU_BLOCKS, 1, d), wx,
      bx.reshape(LRU_BLOCKS, 1, d), lam.reshape(1, LRU_WIDTH))


def _layer_norm(y, w, b):
    mu = jnp.mean(y, axis=-1, keepdims=True)
    cen = y - mu
    var = jnp.mean(cen * cen, axis=-1, keepdims=True)
    return cen * lax.rsqrt(var + LN_EPS) * w + b


HALF = D_MODEL // 2
SLAB = HALF // LANES
PITCH = SLAB + 4
u32 = jnp.uint32
HIGH_MASK = 0xFFFF0000


def _bf16_bits(v):
    return lax.bitcast_convert_type(v.astype(bf16).astype(f32), u32)


def _store_slabs(ref, val):
    n = val.shape[0]
    for j in range(SLAB):
        lo = _bf16_bits(val[:, j * LANES:(j + 1) * LANES])
        hi = _bf16_bits(val[:, HALF + j * LANES:HALF + (j + 1) * LANES])
        ref[pl.ds(j, n, stride=PITCH), :] = hi | lax.shift_right_logical(lo, jnp.full_like(lo, 16))
    for j in range(SLAB, PITCH):
        ref[pl.ds(j, n, stride=PITCH), :] = jnp.zeros((n, LANES), u32)


def _slab_cols(ref, j, n):
    w = ref[pl.ds(j, n, stride=PITCH), :]
    lo = lax.bitcast_convert_type(lax.shift_left(w, jnp.full_like(w, 16)), f32)
    hi = lax.bitcast_convert_type(w & jnp.full_like(w, HIGH_MASK), f32)
    return lo, hi


def _split_bf16(v):
    hi = v.astype(bf16)
    lo = (v - hi.astype(f32)).astype(bf16)
    return hi, lo


def _out_router_kernel(ret_ref, lru_ref, wo_ref, x_ref, lnw_ref, lnb_ref, wr_ref, br_ref,
                       x1_ref, x1s_ref, tope_ref, gate_ref):
    m = jnp.dot(ret_ref[...], wo_ref[0:RET_WIDTH, :], preferred_element_type=f32)
    m = m + jnp.dot(lru_ref[...], wo_ref[RET_WIDTH:D_MODEL, :], preferred_element_type=f32)
    x1 = _layer_norm(DN_ALPHA * x_ref[...] + m, lnw_ref[...], lnb_ref[...])
    x1_ref[...] = x1
    _store_slabs(x1s_ref, x1)
    xh, xl = _split_bf16(x1)
    wh, wl = _split_bf16(wr_ref[...])
    logits = (jnp.dot(xh, wh, preferred_element_type=f32) + jnp.dot(xl, wh, preferred_element_type=f32)
              + jnp.dot(xh, wl, preferred_element_type=f32)) + br_ref[...]
    tm = logits.shape[0]
    lane = lax.broadcasted_iota(jnp.int32, logits.shape, 1)
    lane_k = lax.broadcasted_iota(jnp.int32, (tm, TOP_K), 1)
    top_e = jnp.zeros((tm, TOP_K), jnp.int32)
    top_v = jnp.zeros((tm, TOP_K), f32)
    cur = logits
    for kk in range(TOP_K):
        mx = jnp.max(cur, axis=-1, keepdims=True)
        idx = jnp.min(jnp.where(cur == mx, lane, N_EXPERTS), axis=-1, keepdims=True)
        top_e = jnp.where(lane_k == kk, idx, top_e)
        top_v = jnp.where(lane_k == kk, mx, top_v)
        cur = jnp.where(lane == idx, -jnp.inf, cur)
    ex = jnp.exp(top_v - top_v[:, 0:1])
    gate_ref[...] = ex / jnp.sum(ex, axis=-1, keepdims=True)
    tope_ref[...] = top_e


def _out_router(ret_out, lru_out, wo_bf, x2d, ln_w, ln_b, w_router, b_router):
    t, d = x2d.shape
    tm = 512
    row = lambda c: pl.BlockSpec((tm, c), lambda i: (i, 0))
    full = lambda r, c: pl.BlockSpec((r, c), lambda i: (0, 0))
    return pl.pallas_call(
        _out_router_kernel,
        grid=(t // tm,),
        in_specs=[row(RET_WIDTH), row(LRU_WIDTH), full(d, d), row(d), full(1, d), full(1, d),
                  full(d, N_EXPERTS), full(1, N_EXPERTS)],
        out_specs=[row(d), pl.BlockSpec((tm * PITCH, LANES), lambda i: (i, 0)), row(TOP_K), row(TOP_K)],
        out_shape=[jax.ShapeDtypeStruct((t, d), f32), jax.ShapeDtypeStruct((t * PITCH, LANES), u32),
                   jax.ShapeDtypeStruct((t, TOP_K), jnp.int32),
                   jax.ShapeDtypeStruct((t, TOP_K), f32)],
        compiler_params=_cparams(("parallel",)),
        name="out_proj_ln1_router",
    )(ret_out, lru_out, wo_bf, x2d, ln_w.reshape(1, d), ln_b.reshape(1, d), w_router,
      b_router.reshape(1, N_EXPERTS))


def _routing_tables(top_e, t):
    n_pad = t * TOP_K + N_EXPERTS * ROW_CHUNK
    max_units = N_EXPERTS + (t * TOP_K) // UNIT_ROWS
    sel = (top_e[:, :, None] == jnp.arange(N_EXPERTS, dtype=jnp.int32)[None, None, :]).any(axis=1)
    sel = sel.astype(jnp.int32)
    counts = jnp.sum(sel, axis=0)
    rank = jnp.cumsum(sel, axis=0) - sel
    padded = (counts + ROW_CHUNK - 1) // ROW_CHUNK * ROW_CHUNK
    pad_ends = jnp.cumsum(padded)
    pad_starts = pad_ends - padded
    dest_dense = pad_starts[None, :] + rank
    dest = jnp.take_along_axis(dest_dense, top_e, axis=1)
    tok = jnp.broadcast_to(jnp.arange(t, dtype=jnp.int32)[:, None], (t, TOP_K))
    slot_tok = jnp.zeros((n_pad + UNIT_ROWS,), jnp.int32).at[dest.reshape(-1)].set(tok.reshape(-1) * PITCH)
    units_per_e = (padded + UNIT_ROWS - 1) // UNIT_ROWS
    unit_ends = jnp.cumsum(units_per_e)
    unit_starts = unit_ends - units_per_e
    n_units = unit_ends[-1]
    u = jnp.arange(max_units, dtype=jnp.int32)
    u_clamped = jnp.minimum(u, n_units - 1)
    ue = jnp.searchsorted(unit_ends, u_clamped, side='right').astype(jnp.int32)
    ue = jnp.minimum(ue, N_EXPERTS - 1)
    j = u_clamped - unit_starts[ue]
    u_row = pad_starts[ue] + j * UNIT_ROWS
    u_rows = jnp.minimum(UNIT_ROWS, padded[ue] - j * UNIT_ROWS)
    u_chunks = jnp.where(u < n_units, u_rows // ROW_CHUNK, 0).astype(jnp.int32)
    n_used_rows = pad_ends[-1].astype(jnp.int32)
    return dest.astype(jnp.int32), slot_tok, ue, u_row.astype(jnp.int32), u_chunks, n_used_rows, n_pad, max_units


CHUNK_PITCHED = ROW_CHUNK * PITCH
OUT_SLOTS = 4


def _moe_kernel(ue_ref, urow_ref, uchunks_ref, used_ref, tok_hbm, x1s_hbm, wg_ref, bg_ref, wu_ref, bu_ref, wd_ref,
                bd_ref, y_hbm, xbuf, yacc, gstage_a, gstage_b, ostage, tok_smem, flags, tok_sem, in_sem, out_sem):
    u = pl.program_id(0)
    f = pl.program_id(1)
    n_u = pl.num_programs(0)
    n_f = pl.num_programs(1)
    n_chunks = uchunks_ref[u]
    row0 = urow_ref[u]
    cur = u % 2
    nxt = 1 - cur
    u_next = jnp.minimum(u + 1, n_u - 1)
    n_next = jnp.where(u + 1 < n_u, uchunks_ref[u_next], 0)
    u_next2 = jnp.minimum(u + 2, n_u - 1)
    n_next2 = jnp.where(u + 2 < n_u, uchunks_ref[u_next2], 0)
    TABLES = 3
    tcur = u % TABLES
    tnxt = (u + 1) % TABLES
    first_step = jnp.logical_and(u == 0, f == 0)
    PENDING, PEND_CHUNK, PEND_SLOT, OUT_BUSY = 0, 1, 2, 3

    def rows(c, size=ROW_CHUNK):
        return pl.ds(pl.multiple_of(c * size, size), size)

    def tok_copy(unit_row, slot):
        src = tok_hbm.at[pl.ds(pl.multiple_of(unit_row, ROW_CHUNK), UNIT_ROWS)]
        dst = tok_smem.at[pl.ds(pl.multiple_of(slot * UNIT_ROWS, UNIT_ROWS), UNIT_ROWS)]
        return pltpu.make_async_copy(src, dst, tok_sem)

    ISSUE_GROUP = 8
    stages = (gstage_a, gstage_b)

    def row_copy(table_row, r, parity):
        src = x1s_hbm.at[pl.ds(tok_smem[table_row], SLAB), :]
        return pltpu.make_async_copy(src, stages[parity].at[pl.ds(r * PITCH, SLAB), :], in_sem.at[parity])

    def gather_start(c, tslot, parity):
        base = tslot * UNIT_ROWS + c * ROW_CHUNK

        def issue(g, carry):
            for i in range(ISSUE_GROUP):
                r = g * ISSUE_GROUP + i
                row_copy(base + r, r, parity).start()
            return carry

        lax.fori_loop(0, ROW_CHUNK // ISSUE_GROUP, issue, 0)

    def gather_wait(parity):
        n = ROW_CHUNK * SLAB
        pltpu.make_async_copy(x1s_hbm.at[pl.ds(0, n), :], stages[parity].at[pl.ds(0, n), :],
                              in_sem.at[parity]).wait()

    def unpack_cols(j, c, xslot, parity):
        lo, hi = _slab_cols(stages[parity], j, ROW_CHUNK)
        xbuf[xslot, rows(c), j * LANES:(j + 1) * LANES] = lo.astype(bf16)
        xbuf[xslot, rows(c), HALF + j * LANES:HALF + (j + 1) * LANES] = hi.astype(bf16)

    def gather_finish(c, xslot, parity):
        gather_wait(parity)
        for j in range(SLAB):
            unpack_cols(j, c, xslot, parity)

    def finish_and_start(c_done, xslot, c_new, tslot, parity_new):
        gather_wait(1 - parity_new)
        base = tslot * UNIT_ROWS + c_new * ROW_CHUNK
        per_piece = ROW_CHUNK // SLAB
        for j in range(SLAB):
            for r in range(j * per_piece, (j + 1) * per_piece):
                row_copy(base + r, r, parity_new).start()
            unpack_cols(j, c_done, xslot, 1 - parity_new)

    def out_copy(c, slot):
        dst = y_hbm.at[pl.ds(pl.multiple_of((row0 + c * ROW_CHUNK) * PITCH, CHUNK_PITCHED), CHUNK_PITCHED), :]
        return pltpu.make_async_copy(ostage.at[slot], dst, out_sem.at[slot])

    def out_wait(slot):
        @pl.when(flags[OUT_BUSY + slot] == 1)
        def _():
            pltpu.make_async_copy(ostage.at[slot], y_hbm.at[pl.ds(0, CHUNK_PITCHED), :], out_sem.at[slot]).wait()
            flags[OUT_BUSY + slot] = 0

    @pl.when(first_step)
    def _():
        for i in range(OUT_BUSY + OUT_SLOTS):
            flags[i] = 0

        @pl.when(n_chunks > 0)
        def _():
            first_table = tok_copy(row0, tcur)
            first_table.start()
            first_table.wait()

            def load(c, carry):
                gather_start(c, tcur, 0)
                gather_finish(c, cur, 0)
                return carry

            lax.fori_loop(0, n_chunks, load, 0)

        @pl.when(n_next > 0)
        def _():
            tok_copy(urow_ref[u_next], tnxt).start()

    @pl.when(jnp.logical_and(f == 0, n_next > 0))
    def _():
        tok_copy(urow_ref[u_next], tnxt).wait()

        @pl.when(n_next2 > 0)
        def _():
            tok_copy(urow_ref[u_next2], (u + 2) % TABLES).start()

    prefetching = f < n_next
    pending = flags[PENDING] == 1
    done_chunk = flags[PEND_CHUNK]
    done_slot = flags[PEND_SLOT]
    fused = jnp.logical_and(jnp.logical_and(pending, prefetching), done_chunk % 2 != f % 2)
    for parity in (0, 1):
        @pl.when(jnp.logical_and(fused, f % 2 == parity))
        def _():
            finish_and_start(done_chunk, done_slot, f, tnxt, parity)

        @pl.when(jnp.logical_and(jnp.logical_and(pending, jnp.logical_not(fused)), done_chunk % 2 == parity))
        def _():
            gather_finish(done_chunk, done_slot, parity)

    for parity in (0, 1):
        @pl.when(jnp.logical_and(jnp.logical_and(prefetching, jnp.logical_not(fused)), f % 2 == parity))
        def _():
            gather_start(f, tnxt, parity)

    flags[PENDING] = prefetching.astype(jnp.int32)

    @pl.when(prefetching)
    def _():
        flags[PEND_CHUNK] = f
        flags[PEND_SLOT] = nxt

    n_quads = n_chunks // 4
    has_pair = (n_chunks % 4) // 2
    has_single = n_chunks % 2

    @pl.when(n_chunks > 0)
    def _():
        bg = bg_ref[...]
        bu = bu_ref[...]
        mm = lambda a, w: lax.dot_general(a, w, (((1,), (0,)), ((), ())), preferred_element_type=f32)

        def mlp(sl, first):
            xc = xbuf[cur, sl, :]
            gt = jnp.minimum(mm(xc, wg_ref[...]) + bg, SWIGLU_LIMIT)
            up = jnp.clip(mm(xc, wu_ref[...]) + bu, -SWIGLU_LIMIT, SWIGLU_LIMIT)
            hid = (up + 1.0) * gt * jax.nn.sigmoid(SWIGLU_ALPHA * gt)
            part = mm(hid.astype(bf16), wd_ref[...])
            if first:
                yacc[sl, :] = part
            else:
                yacc[sl, :] += part

        def blocks(first):
            def quad(c, carry):
                mlp(rows(c, 4 * ROW_CHUNK), first)
                return carry

            lax.fori_loop(0, n_quads, quad, 0)

            @pl.when(has_pair == 1)
            def _():
                mlp(pl.ds(pl.multiple_of(n_quads * 4 * ROW_CHUNK, 2 * ROW_CHUNK), 2 * ROW_CHUNK), first)

            @pl.when(has_single == 1)
            def _():
                mlp(rows(n_chunks - 1), first)

        @pl.when(f == 0)
        def _():
            blocks(True)

        @pl.when(f > 0)
        def _():
            blocks(False)

    @pl.when(jnp.logical_and(f == n_f - 1, n_chunks > 0))
    def _():
        bd = bd_ref[...]

        def store(c, carry):
            slot = c % OUT_SLOTS
            out_wait(slot)
            _store_slabs(ostage.at[slot], yacc[rows(c), :] + bd)
            out_copy(c, slot).start()
            flags[OUT_BUSY + slot] = 1
            return carry

        lax.fori_loop(0, n_chunks, store, 0)

    @pl.when(jnp.logical_and(u == n_u - 1, f == n_f - 1))
    def _():
        for s in range(OUT_SLOTS):
            out_wait(s)
        first = used_ref[0] // ROW_CHUNK
        last = y_hbm.shape[0] // CHUNK_PITCHED
        ostage[0] = jnp.zeros((CHUNK_PITCHED, LANES), u32)

        def tail_copy(c):
            dst = y_hbm.at[pl.ds(pl.multiple_of(c * CHUNK_PITCHED, CHUNK_PITCHED), CHUNK_PITCHED), :]
            return pltpu.make_async_copy(ostage.at[0], dst, out_sem.at[0])

        def start(c, carry):
            tail_copy(c).start()
            return carry

        def wait(c, carry):
            tail_copy(c).wait()
            return carry

        lax.fori_loop(first, last, start, 0)
        lax.fori_loop(first, last, wait, 0)


def _moe_experts(x1s, slot_tok, ue, u_row, u_chunks, n_used_rows, n_pad, max_units, w_gate, b_gate, w_up, b_up,
                 w_down, b_down):
    e, d, dff = w_gate.shape
    n_f = dff // F_TILE
    assert n_f >= UNIT_ROWS // ROW_CHUNK, "one chunk of the next unit is gathered per f-tile step"
    f_idx = lambda u, f, uc: jnp.where(uc[u] > 0, f, n_f - 1)
    col_w = pl.BlockSpec((None, d, F_TILE), lambda u, f, ue, ur, uc, used: (ue[u], 0, f_idx(u, f, uc)))
    col_b = pl.BlockSpec((None, 1, F_TILE), lambda u, f, ue, ur, uc, used: (ue[u], 0, f_idx(u, f, uc)))
    return pl.pallas_call(
        _moe_kernel,
        grid_spec=pltpu.PrefetchScalarGridSpec(
            num_scalar_prefetch=4,
            grid=(max_units, n_f),
            in_specs=[pl.BlockSpec(memory_space=pl.ANY), pl.BlockSpec(memory_space=pl.ANY),
                      col_w, col_b, col_w, col_b,
                      pl.BlockSpec((None, F_TILE, d), lambda u, f, ue, ur, uc, used: (ue[u], f_idx(u, f, uc), 0)),
                      pl.BlockSpec((None, 1, d), lambda u, f, ue, ur, uc, used: (ue[u], 0, 0))],
            out_specs=pl.BlockSpec(memory_space=pl.ANY),
            scratch_shapes=[pltpu.VMEM((2, UNIT_ROWS, d), bf16),
                            pltpu.VMEM((UNIT_ROWS, d), f32),
                            pltpu.VMEM((CHUNK_PITCHED, LANES), u32),
                            pltpu.VMEM((CHUNK_PITCHED, LANES), u32),
                            pltpu.VMEM((OUT_SLOTS, CHUNK_PITCHED, LANES), u32),
                            pltpu.SMEM((3 * UNIT_ROWS,), jnp.int32),
                            pltpu.SMEM((3 + OUT_SLOTS,), jnp.int32),
                            pltpu.SemaphoreType.DMA(()),
                            pltpu.SemaphoreType.DMA((2,)),
                            pltpu.SemaphoreType.DMA((OUT_SLOTS,))],
        ),
        out_shape=jax.ShapeDtypeStruct((n_pad * PITCH, LANES), u32),
        compiler_params=_cparams(("arbitrary", "arbitrary")),
        name="moe_experts",
    )(ue, u_row, u_chunks, n_used_rows.reshape(1), slot_tok, x1s, w_gate, b_gate.reshape(e, 1, dff), w_up,
      b_up.reshape(e, 1, dff), w_down, b_down.reshape(e, 1, d))


COMBINE_ROWS = 256
COMBINE_GROUPS = 2


def _combine_kernel(dest_ref, dest_next_ref, y_hbm, gate_ref, x1_ref, p_ref, lnw_ref, lnb_ref, wp_ref, pnw_ref,
                    wg_ref, o_ref, ybuf_a, ybuf_b, fsum_ref, sem_a, sem_b):
    tm = COMBINE_ROWS
    s = pl.program_id(0)

    def row_copy(table, table_off, t, k, ybuf, sem):
        src = y_hbm.at[pl.ds(table[(table_off + t) * TOP_K + k] * PITCH, SLAB), :]
        return pltpu.make_async_copy(src, ybuf.at[k, pl.ds(t * PITCH, SLAB), :], sem)

    def wait_tile(ybuf, sem):
        for k in range(TOP_K):
            pltpu.make_async_copy(y_hbm.at[pl.ds(0, tm * SLAB), :], ybuf.at[k, pl.ds(0, tm * SLAB), :], sem).wait()

    @pl.when(s == 0)
    def _():
        def issue(t, c):
            for k in range(TOP_K):
                row_copy(dest_ref, 0, t, k, ybuf_a, sem_a).start()
            return c

        lax.fori_loop(0, tm, issue, 0, unroll=2)

    def finish_tile(row0, ybuf, next_table, next_off, next_buf, next_sem):
        tok = pl.ds(row0, tm)
        pieces = SLAB * TOP_K
        per_piece = tm // pieces
        e = jnp.dot(p_ref[tok, :].astype(bf16), wp_ref[...], preferred_element_type=f32)
        e = e * lax.rsqrt(jnp.mean(e * e, axis=-1, keepdims=True) + LN_EPS) * pnw_ref[...]
        gates = gate_ref[tok, :]
        for j in range(SLAB):
            acc_lo, acc_hi = None, None
            for k in range(TOP_K):
                piece = j * TOP_K + k
                for t in range(piece * per_piece, (piece + 1) * per_piece):
                    for kk in range(TOP_K):
                        row_copy(next_table, next_off, t, kk, next_buf, next_sem).start()
                lo, hi = _slab_cols(ybuf.at[k], j, tm)
                g = gates[:, k:k + 1]
                acc_lo = g * lo if k == 0 else acc_lo + g * lo
                acc_hi = g * hi if k == 0 else acc_hi + g * hi
            fsum_ref[:, j * LANES:(j + 1) * LANES] = acc_lo
            fsum_ref[:, HALF + j * LANES:HALF + (j + 1) * LANES] = acc_hi
        x2 = _layer_norm(DN_ALPHA * x1_ref[tok, :] + fsum_ref[...], lnw_ref[...], lnb_ref[...])
        gate = jax.nn.sigmoid(jnp.dot(x2.astype(bf16), wg_ref[...], preferred_element_type=f32))
        o_ref[tok, :] = x2 + gate * e

    wait_tile(ybuf_a, sem_a)
    finish_tile(0, ybuf_a, dest_ref, tm, ybuf_b, sem_b)
    wait_tile(ybuf_b, sem_b)
    finish_tile(tm, ybuf_b, dest_next_ref, 0, ybuf_a, sem_a)

    @pl.when(s == pl.num_programs(0) - 1)
    def _():
        wait_tile(ybuf_a, sem_a)


def _combine(y, dest, gates, x1, p2d, ln_w, ln_b, wp_bf, ple_norm_w, wg_bf):
    t, d = x1.shape
    tm = COMBINE_ROWS
    n_steps = t // (2 * tm)
    row = lambda c: pl.BlockSpec((2 * tm, c), lambda i: (i, 0))
    full = lambda r, c: pl.BlockSpec((r, c), lambda i: (0, 0))
    table = lambda nxt: pl.BlockSpec((2 * tm * TOP_K,), lambda i: (jnp.minimum(i + nxt, n_steps - 1),),
                                     memory_space=pltpu.SMEM)
    dest_flat = dest.reshape(-1)
    return pl.pallas_call(
        _combine_kernel,
        grid=(n_steps,),
        in_specs=[table(0), table(1), pl.BlockSpec(memory_space=pl.ANY),
                  row(TOP_K), row(d), row(PLE_DIM), full(1, d), full(1, d), full(PLE_DIM, d), full(1, d),
                  full(d, d)],
        out_specs=row(d),
        out_shape=jax.ShapeDtypeStruct((t, d), f32),
        scratch_shapes=[pltpu.VMEM((TOP_K, tm * PITCH, LANES), u32), pltpu.VMEM((TOP_K, tm * PITCH, LANES), u32),
                        pltpu.VMEM((tm, d), f32), pltpu.SemaphoreType.DMA(()), pltpu.SemaphoreType.DMA(())],
        compiler_params=_cparams(("arbitrary",)),
        name="combine_ln2_ple",
    )(dest_flat, dest_flat, y, gates, x1, p2d, ln_w.reshape(1, d), ln_b.reshape(1, d), wp_bf,
      ple_norm_w.reshape(1, d), wg_bf)


def _layer(h, p_i, w_in, ret_gn_w, conv_w, conv_b, lru_wa, lru_ba, lru_wx, lru_bx, lru_lam, w_out,
           ln1_w, ln1_b, w_router, b_router, w_gate, b_gate, w_up, b_up, w_down, b_down,
           ln2_w, ln2_b, w_ple_proj, ple_norm_w, w_ple_gate):
    b, s, d = h.shape
    t = b * s
    x2d = h.reshape(t, d)
    proj = _in_proj(x2d, w_in).reshape(b, s, IN_COLS)
    ret_out = _retention(proj, ret_gn_w)
    lru_out = _lru(proj, conv_w, conv_b, lru_wa, lru_ba, lru_wx, lru_bx, lru_lam)
    x1, x1s, top_e, gates = _out_router(ret_out.reshape(t, RET_WIDTH), lru_out.reshape(t, LRU_WIDTH),
                                        w_out.astype(bf16), x2d, ln1_w, ln1_b, w_router, b_router)
    dest, slot_tok, ue, u_row, u_chunks, n_used_rows, n_pad, max_units = _routing_tables(top_e, t)
    y = _moe_experts(x1s, slot_tok, ue, u_row, u_chunks, n_used_rows, n_pad, max_units, w_gate, b_gate, w_up, b_up,
                     w_down, b_down)
    out = _combine(y, dest, gates, x1, p_i.reshape(t, PLE_DIM), ln2_w, ln2_b, w_ple_proj.astype(bf16),
                   ple_norm_w, w_ple_gate.astype(bf16))
    return out.reshape(b, s, d)


def kernel(x, p, w_in, ret_gn_w, conv_w, conv_b, lru_wa, lru_ba, lru_wx, lru_bx, lru_lam, w_out, ln1_w, ln1_b,
           w_router, b_router, w_gate, b_gate, w_up, b_up, w_down, b_down, ln2_w, ln2_b, w_ple_proj, ple_norm_w,
           w_ple_gate):
    h = x.astype(f32)
    for i in range(w_in.shape[0]):
        h = _layer(h, p[i], w_in[i], ret_gn_w[i], conv_w[i], conv_b[i], lru_wa[i], lru_ba[i], lru_wx[i],
                   lru_bx[i], lru_lam[i], w_out[i], ln1_w[i], ln1_b[i], w_router[i], b_router[i], w_gate[i],
                   b_gate[i], w_up[i], b_up[i], w_down[i], b_down[i], ln2_w[i], ln2_b[i], w_ple_proj[i],
                   ple_norm_w[i], w_ple_gate[i])
    return h.astype(x.dtype)
```

```python
import functools
import math

import jax
import jax.numpy as jnp
from jax import lax
from jax.experimental import pallas as pl
from jax.experimental.pallas import tpu as pltpu

D_MODEL = 2048
RET_HEAD_DIM = 128
RET_HEADS = 8
RET_WIDTH = RET_HEADS * RET_HEAD_DIM
LRU_WIDTH = D_MODEL - RET_WIDTH
LRU_BLOCKS = 8
LRU_BLOCK_DIM = LRU_WIDTH // LRU_BLOCKS
IN_COLS = 4 * RET_WIDTH + 2 * LRU_WIDTH
CONV_WIDTH = 4
LRU_C = 8.0
CHUNK = 128
ROPE_BASE = 10000.0
N_EXPERTS = 32
TOP_K = 4
SWIGLU_LIMIT = 7.0
SWIGLU_ALPHA = 1.702
PLE_DIM = 256
LN_EPS = 1e-5
DEPTH = 1
DN_ALPHA = (2.0 * DEPTH) ** 0.25

LANES = 128
SUBLANES = 8
VMEM_LIMIT = 60 * 1024 * 1024

ROW_CHUNK = 256
UNIT_ROWS = 2048
F_TILE = 256

f32 = jnp.float32
bf16 = jnp.bfloat16


def _cparams(sem):
    return pltpu.CompilerParams(dimension_semantics=sem, vmem_limit_bytes=VMEM_LIMIT)


def _in_proj_kernel(x_ref, w_ref, o_ref):
    o_ref[...] = lax.dot_general(x_ref[...].astype(bf16), w_ref[...], (((1,), (0,)), ((), ())),
                                 preferred_element_type=f32)


def _in_proj(x2d, w):
    t, d = x2d.shape
    n = w.shape[1]
    tm, tn = 512, 2048
    return pl.pallas_call(
        _in_proj_kernel,
        grid=(n // tn, t // tm),
        in_specs=[pl.BlockSpec((tm, d), lambda j, i: (i, 0)),
                  pl.BlockSpec((d, tn), lambda j, i: (0, j))],
        out_specs=pl.BlockSpec((tm, tn), lambda j, i: (i, j)),
        out_shape=jax.ShapeDtypeStruct((t, n), f32),
        compiler_params=_cparams(("parallel", "parallel")),
        name="in_proj",
    )(x2d, w)


def _retention_kernel(q_ref, k_ref, v_ref, g_ref, cos_ref, sin_ref, dec_ref, qd_ref, kd_ref, cd_ref, gnw_ref,
                      o_ref):
    s = q_ref.shape[0]
    n_chunks = s // CHUNK
    decay = dec_ref[...]
    q_dec = qd_ref[...]
    k_dec = kd_ref[...]
    c_dec = cd_ref[0:1, :]
    gnw = gnw_ref[...]
    k_scale = RET_HEAD_DIM ** -0.5

    def rope(xv, cos, sin):
        return xv * cos + pltpu.roll(xv, RET_HEAD_DIM // 2, axis=1) * sin

    def body(n, state):
        sl = pl.ds(pl.multiple_of(n * CHUNK, CHUNK), CHUNK)
        cos = cos_ref[sl, :]
        sin = sin_ref[sl, :]
        q = rope(q_ref[sl, :], cos, sin)
        k = rope(k_ref[sl, :], cos, sin) * k_scale
        vb = v_ref[sl, :].astype(bf16)
        scores = lax.dot_general(q.astype(bf16), k.astype(bf16), (((1,), (1,)), ((), ())),
                                 preferred_element_type=f32) * decay
        intra = jnp.dot(scores.astype(bf16), vb, preferred_element_type=f32)
        cross = jnp.dot((q * q_dec).astype(bf16), state.astype(bf16), preferred_element_type=f32)
        kv = lax.dot_general((k * k_dec).astype(bf16), vb, (((0,), (0,)), ((), ())),
                             preferred_element_type=f32)
        ret = intra + cross
        mu = jnp.mean(ret, axis=-1, keepdims=True)
        cen = ret - mu
        var = jnp.mean(cen * cen, axis=-1, keepdims=True)
        ret = cen * lax.rsqrt(var + LN_EPS) * gnw
        g = g_ref[sl, :]
        o_ref[sl, :] = (g * jax.nn.sigmoid(g) * ret).astype(o_ref.dtype)
        return c_dec * state + kv

    lax.fori_loop(0, n_chunks, body, jnp.zeros((RET_HEAD_DIM, RET_HEAD_DIM), f32), unroll=4)


def _retention_tables(s):
    h, d = RET_HEADS, RET_HEAD_DIM
    inv = ROPE_BASE ** (-jnp.arange(0, d, 2, dtype=f32) / d)
    ang = jnp.arange(s, dtype=f32)[:, None] * inv[None, :]
    cos = jnp.cos(ang)
    sin = jnp.sin(ang)
    cos_t = jnp.concatenate([cos, cos], axis=-1)
    sin_t = jnp.concatenate([-sin, sin], axis=-1)
    log_gamma = jnp.log1p(-jnp.exp2(-5.0 - jnp.arange(h, dtype=f32)))
    idx = jnp.arange(CHUNK, dtype=f32)
    diff = idx[:, None] - idx[None, :]
    decay = jnp.where((diff >= 0)[None], jnp.exp(jnp.maximum(diff, 0.0)[None] * log_gamma[:, None, None]), 0.0)
    q_dec = jnp.exp((idx[None, :] + 1.0) * log_gamma[:, None])
    k_dec = jnp.exp((CHUNK - 1.0 - idx)[None, :] * log_gamma[:, None])
    c_dec = jnp.exp(CHUNK * log_gamma)
    q_dec = jnp.broadcast_to(q_dec[:, :, None], (h, CHUNK, d))
    k_dec = jnp.broadcast_to(k_dec[:, :, None], (h, CHUNK, d))
    c_dec = jnp.broadcast_to(c_dec[:, None, None], (h, SUBLANES, d))
    return cos_t, sin_t, decay, q_dec, k_dec, c_dec


def _retention(proj, ret_gn_w):
    b, s, _ = proj.shape
    d = RET_HEAD_DIM
    cos_t, sin_t, decay, q_dec, k_dec, c_dec = _retention_tables(s)
    col = lambda off: pl.BlockSpec((None, s, d), lambda bi, hi: (bi, 0, off + hi))
    per_head = lambda r: pl.BlockSpec((None, r, d), lambda bi, hi: (hi, 0, 0))
    full = lambda shp: pl.BlockSpec(shp, lambda bi, hi: (0,) * len(shp))
    return pl.pallas_call(
        _retention_kernel,
        grid=(b, RET_HEADS),
        in_specs=[col(0), col(RET_HEADS), col(2 * RET_HEADS), col(3 * RET_HEADS),
                  full((s, d)), full((s, d)),
                  per_head(CHUNK), per_head(CHUNK), per_head(CHUNK), per_head(SUBLANES),
                  pl.BlockSpec((1, d), lambda bi, hi: (0, hi))],
        out_specs=pl.BlockSpec((None, s, d), lambda bi, hi: (bi, 0, hi)),
        out_shape=jax.ShapeDtypeStruct((b, s, RET_WIDTH), bf16),
        compiler_params=_cparams(("parallel", "parallel")),
        name="retention",
    )(proj, proj, proj, proj, cos_t, sin_t, decay, q_dec, k_dec, c_dec, ret_gn_w.reshape(1, RET_WIDTH))


def _gelu_tanh(x):
    return 0.5 * x * (1.0 + jnp.tanh(math.sqrt(2.0 / math.pi) * (x + 0.044715 * (x * x * x))))


def _lru_kernel(xr_ref, yg_ref, cw_ref, cb_ref, wa_ref, ba_ref, wx_ref, bx_ref, lam_ref, o_ref, a_ref, b_ref):
    s = xr_ref.shape[0]
    x = xr_ref[...]
    rows = lax.broadcasted_iota(jnp.int32, x.shape, 0)
    xc = cb_ref[...] + cw_ref[CONV_WIDTH - 1:CONV_WIDTH, :] * x
    for back in range(1, CONV_WIDTH):
        shifted = jnp.where(rows >= back, pltpu.roll(x, back, axis=0), 0.0)
        xc = xc + cw_ref[CONV_WIDTH - 1 - back:CONV_WIDTH - back, :] * shifted
    xcb = xc.astype(bf16)
    r = jax.nn.sigmoid(jnp.dot(xcb, wa_ref[...].astype(bf16), preferred_element_type=f32) + ba_ref[...])
    gi = jax.nn.sigmoid(jnp.dot(xcb, wx_ref[...].astype(bf16), preferred_element_type=f32) + bx_ref[...])
    lam = lam_ref[...]
    log_sig = jnp.minimum(lam, 0.0) - jnp.log1p(jnp.exp(-jnp.abs(lam)))
    log_a = LRU_C * r * log_sig
    a = jnp.exp(log_a)
    a_ref[...] = a
    b_ref[...] = jnp.sqrt(-jnp.tanh(log_a) * (a * a + 1.0)) * (gi * xc)

    row8 = lax.broadcasted_iota(jnp.int32, (SUBLANES, LANES), 0)

    def body(i, h_prev):
        sl = pl.ds(pl.multiple_of(i * SUBLANES, SUBLANES), SUBLANES)
        a8 = a_ref[sl, :]
        b8 = b_ref[sl, :]
        for sh in (1, 2, 4):
            a_sh = jnp.where(row8 >= sh, pltpu.roll(a8, sh, axis=0), 1.0)
            b_sh = jnp.where(row8 >= sh, pltpu.roll(b8, sh, axis=0), 0.0)
            b8 = a8 * b_sh + b8
            a8 = a8 * a_sh
        h8 = a8 * h_prev + b8
        o_ref[sl, :] = (_gelu_tanh(yg_ref[sl, :]) * h8).astype(o_ref.dtype)
        return h8[SUBLANES - 1:SUBLANES, :]

    lax.fori_loop(0, s // SUBLANES, body, jnp.zeros((1, LANES), f32), unroll=8)


def _lru(proj, conv_w, conv_b, wa, ba, wx, bx, lam):
    b, s, _ = proj.shape
    d = LRU_BLOCK_DIM
    xr_off = 4 * RET_WIDTH // d
    yg_off = xr_off + LRU_BLOCKS
    col = lambda off: pl.BlockSpec((None, s, d), lambda bi, ji: (bi, 0, off + ji))
    vec = lambda r: pl.BlockSpec((r, d), lambda bi, ji: (0, ji))
    blk = lambda r: pl.BlockSpec((None, r, d), lambda bi, ji: (ji, 0, 0))
    return pl.pallas_call(
        _lru_kernel,
        grid=(b, LRU_BLOCKS),
        in_specs=[col(xr_off), col(yg_off), vec(CONV_WIDTH), vec(1), blk(d), blk(1), blk(d), blk(1), vec(1)],
        out_specs=pl.BlockSpec((None, s, d), lambda bi, ji: (bi, 0, ji)),
        out_shape=jax.ShapeDtypeStruct((b, s, LRU_WIDTH), bf16),
        scratch_shapes=[pltpu.VMEM((s, d), f32), pltpu.VMEM((s, d), f32)],
        compiler_params=_cparams(("parallel", "parallel")),
        name="rg_lru",
    )(proj, proj, conv_w, conv_b.reshape(1, LRU_WIDTH), wa, ba.reshape(LRU_BLOCKS, 1, d), wx,
      bx.reshape(LRU_BLOCKS, 1, d), lam.reshape(1, LRU_WIDTH))


def _layer_norm(y, w, b):
    mu = jnp.mean(y, axis=-1, keepdims=True)
    cen = y - mu
    var = jnp.mean(cen * cen, axis=-1, keepdims=True)
    return cen * lax.rsqrt(var + LN_EPS) * w + b


HALF = D_MODEL // 2
SLAB = HALF // LANES
PITCH = SLAB + 4
u32 = jnp.uint32
HIGH_MASK = 0xFFFF0000


def _bf16_bits(v):
    return lax.bitcast_convert_type(v.astype(bf16).astype(f32), u32)


def _store_slabs(ref, val):
    n = val.shape[0]
    for j in range(SLAB):
        lo = _bf16_bits(val[:, j * LANES:(j + 1) * LANES])
        hi = _bf16_bits(val[:, HALF + j * LANES:HALF + (j + 1) * LANES])
        ref[pl.ds(j, n, stride=PITCH), :] = hi | lax.shift_right_logical(lo, jnp.full_like(lo, 16))
    for j in range(SLAB, PITCH):
        ref[pl.ds(j, n, stride=PITCH), :] = jnp.zeros((n, LANES), u32)


def _slab_cols(ref, j, n):
    w = ref[pl.ds(j, n, stride=PITCH), :]
    lo = lax.bitcast_convert_type(lax.shift_left(w, jnp.full_like(w, 16)), f32)
    hi = lax.bitcast_convert_type(w & jnp.full_like(w, HIGH_MASK), f32)
    return lo, hi


def _split_bf16(v):
    hi = v.astype(bf16)
    lo = (v - hi.astype(f32)).astype(bf16)
    return hi, lo


def _out_router_kernel(ret_ref, lru_ref, wo_ref, x_ref, lnw_ref, lnb_ref, wr_ref, br_ref,
                       x1_ref, x1s_ref, tope_ref, gate_ref):
    m = jnp.dot(ret_ref[...], wo_ref[0:RET_WIDTH, :], preferred_element_type=f32)
    m = m + jnp.dot(lru_ref[...], wo_ref[RET_WIDTH:D_MODEL, :], preferred_element_type=f32)
    x1 = _layer_norm(DN_ALPHA * x_ref[...] + m, lnw_ref[...], lnb_ref[...])
    x1_ref[...] = x1
    _store_slabs(x1s_ref, x1)
    xh, xl = _split_bf16(x1)
    wh, wl = _split_bf16(wr_ref[...])
    logits = (jnp.dot(xh, wh, preferred_element_type=f32) + jnp.dot(xl, wh, preferred_element_type=f32)
              + jnp.dot(xh, wl, preferred_element_type=f32)) + br_ref[...]
    tm = logits.shape[0]
    lane = lax.broadcasted_iota(jnp.int32, logits.shape, 1)
    lane_k = lax.broadcasted_iota(jnp.int32, (tm, TOP_K), 1)
    top_e = jnp.zeros((tm, TOP_K), jnp.int32)
    top_v = jnp.zeros((tm, TOP_K), f32)
    cur = logits
    for kk in range(TOP_K):
        mx = jnp.max(cur, axis=-1, keepdims=True)
        idx = jnp.min(jnp.where(cur == mx, lane, N_EXPERTS), axis=-1, keepdims=True)
        top_e = jnp.where(lane_k == kk, idx, top_e)
        top_v = jnp.where(lane_k == kk, mx, top_v)
        cur = jnp.where(lane == idx, -jnp.inf, cur)
    ex = jnp.exp(top_v - top_v[:, 0:1])
    gate_ref[...] = ex / jnp.sum(ex, axis=-1, keepdims=True)
    tope_ref[...] = top_e


def _out_router(ret_out, lru_out, wo_bf, x2d, ln_w, ln_b, w_router, b_router):
    t, d = x2d.shape
    tm = 512
    row = lambda c: pl.BlockSpec((tm, c), lambda i: (i, 0))
    full = lambda r, c: pl.BlockSpec((r, c), lambda i: (0, 0))
    return pl.pallas_call(
        _out_router_kernel,
        grid=(t // tm,),
        in_specs=[row(RET_WIDTH), row(LRU_WIDTH), full(d, d), row(d), full(1, d), full(1, d),
                  full(d, N_EXPERTS), full(1, N_EXPERTS)],
        out_specs=[row(d), pl.BlockSpec((tm * PITCH, LANES), lambda i: (i, 0)), row(TOP_K), row(TOP_K)],
        out_shape=[jax.ShapeDtypeStruct((t, d), f32), jax.ShapeDtypeStruct((t * PITCH, LANES), u32),
                   jax.ShapeDtypeStruct((t, TOP_K), jnp.int32),
                   jax.ShapeDtypeStruct((t, TOP_K), f32)],
        compiler_params=_cparams(("parallel",)),
        name="out_proj_ln1_router",
    )(ret_out, lru_out, wo_bf, x2d, ln_w.reshape(1, d), ln_b.reshape(1, d), w_router,
      b_router.reshape(1, N_EXPERTS))


def _routing_tables(top_e, t):
    n_pad = t * TOP_K + N_EXPERTS * ROW_CHUNK
    max_units = N_EXPERTS + (t * TOP_K) // UNIT_ROWS
    sel = (top_e[:, :, None] == jnp.arange(N_EXPERTS, dtype=jnp.int32)[None, None, :]).any(axis=1)
    sel = sel.astype(jnp.int32)
    counts = jnp.sum(sel, axis=0)
    rank = jnp.cumsum(sel, axis=0) - sel
    padded = (counts + ROW_CHUNK - 1) // ROW_CHUNK * ROW_CHUNK
    pad_ends = jnp.cumsum(padded)
    pad_starts = pad_ends - padded
    dest_dense = pad_starts[None, :] + rank
    dest = jnp.take_along_axis(dest_dense, top_e, axis=1)
    tok = jnp.broadcast_to(jnp.arange(t, dtype=jnp.int32)[:, None], (t, TOP_K))
    slot_tok = jnp.zeros((n_pad + UNIT_ROWS,), jnp.int32).at[dest.reshape(-1)].set(tok.reshape(-1) * PITCH)
    units_per_e = (padded + UNIT_ROWS - 1) // UNIT_ROWS
    unit_ends = jnp.cumsum(units_per_e)
    unit_starts = unit_ends - units_per_e
    n_units = unit_ends[-1]
    u = jnp.arange(max_units, dtype=jnp.int32)
    u_clamped = jnp.minimum(u, n_units - 1)
    ue = jnp.searchsorted(unit_ends, u_clamped, side='right').astype(jnp.int32)
    ue = jnp.minimum(ue, N_EXPERTS - 1)
    j = u_clamped - unit_starts[ue]
    u_row = pad_starts[ue] + j * UNIT_ROWS
    u_rows = jnp.minimum(UNIT_ROWS, padded[ue] - j * UNIT_ROWS)
    u_chunks = jnp.where(u < n_units, u_rows // ROW_CHUNK, 0).astype(jnp.int32)
    u_real = jnp.clip(counts[ue] - j * UNIT_ROWS, 0, UNIT_ROWS)
    u_granules = jnp.where(u < n_units, (u_real + COMPUTE_GRANULE - 1) // COMPUTE_GRANULE, 0).astype(jnp.int32)
    n_used_rows = jnp.concatenate([pad_ends[-1:].astype(jnp.int32), u_granules])
    return dest.astype(jnp.int32), slot_tok, ue, u_row.astype(jnp.int32), u_chunks, n_used_rows, n_pad, max_units


CHUNK_PITCHED = ROW_CHUNK * PITCH
OUT_SLOTS = 4
COMPUTE_GRANULE = ROW_CHUNK // 2


def _moe_kernel(ue_ref, urow_ref, uchunks_ref, used_ref, tok_hbm, x1s_hbm, wg_ref, bg_ref, wu_ref, bu_ref, wd_ref,
                bd_ref, y_hbm, xbuf, yacc, gstage_a, gstage_b, ostage, tok_smem, flags, tok_sem, in_sem, out_sem):
    u = pl.program_id(0)
    f = pl.program_id(1)
    n_u = pl.num_programs(0)
    n_f = pl.num_programs(1)
    n_chunks = uchunks_ref[u]
    row0 = urow_ref[u]
    cur = u % 2
    nxt = 1 - cur
    u_next = jnp.minimum(u + 1, n_u - 1)
    n_next = jnp.where(u + 1 < n_u, uchunks_ref[u_next], 0)
    u_next2 = jnp.minimum(u + 2, n_u - 1)
    n_next2 = jnp.where(u + 2 < n_u, uchunks_ref[u_next2], 0)
    TABLES = 3
    tcur = u % TABLES
    tnxt = (u + 1) % TABLES
    first_step = jnp.logical_and(u == 0, f == 0)
    PENDING, PEND_CHUNK, PEND_SLOT, OUT_BUSY = 0, 1, 2, 3

    def rows(c, size=ROW_CHUNK):
        return pl.ds(pl.multiple_of(c * size, size), size)

    def tok_copy(unit_row, slot):
        src = tok_hbm.at[pl.ds(pl.multiple_of(unit_row, ROW_CHUNK), UNIT_ROWS)]
        dst = tok_smem.at[pl.ds(pl.multiple_of(slot * UNIT_ROWS, UNIT_ROWS), UNIT_ROWS)]
        return pltpu.make_async_copy(src, dst, tok_sem)

    ISSUE_GROUP = 8
    stages = (gstage_a, gstage_b)

    def row_copy(table_row, r, parity):
        src = x1s_hbm.at[pl.ds(tok_smem[table_row], SLAB), :]
        return pltpu.make_async_copy(src, stages[parity].at[pl.ds(r * PITCH, SLAB), :], in_sem.at[parity])

    def gather_start(c, tslot, parity):
        base = tslot * UNIT_ROWS + c * ROW_CHUNK

        def issue(g, carry):
            for i in range(ISSUE_GROUP):
                r = g * ISSUE_GROUP + i
                row_copy(base + r, r, parity).start()
            return carry

        lax.fori_loop(0, ROW_CHUNK // ISSUE_GROUP, issue, 0)

    def gather_wait(parity):
        n = ROW_CHUNK * SLAB
        pltpu.make_async_copy(x1s_hbm.at[pl.ds(0, n), :], stages[parity].at[pl.ds(0, n), :],
                              in_sem.at[parity]).wait()

    def unpack_cols(j, c, xslot, parity):
        lo, hi = _slab_cols(stages[parity], j, ROW_CHUNK)
        xbuf[xslot, rows(c), j * LANES:(j + 1) * LANES] = lo.astype(bf16)
        xbuf[xslot, rows(c), HALF + j * LANES:HALF + (j + 1) * LANES] = hi.astype(bf16)

    def gather_finish(c, xslot, parity):
        gather_wait(parity)
        for j in range(SLAB):
            unpack_cols(j, c, xslot, parity)

    def finish_and_start(c_done, xslot, c_new, tslot, parity_new):
        gather_wait(1 - parity_new)
        base = tslot * UNIT_ROWS + c_new * ROW_CHUNK
        per_piece = ROW_CHUNK // SLAB
        for j in range(SLAB):
            for r in range(j * per_piece, (j + 1) * per_piece):
                row_copy(base + r, r, parity_new).start()
            unpack_cols(j, c_done, xslot, 1 - parity_new)

    def out_copy(c, slot):
        dst = y_hbm.at[pl.ds(pl.multiple_of((row0 + c * ROW_CHUNK) * PITCH, CHUNK_PITCHED), CHUNK_PITCHED), :]
        return pltpu.make_async_copy(ostage.at[slot], dst, out_sem.at[slot])

    def out_wait(slot):
        @pl.when(flags[OUT_BUSY + slot] == 1)
        def _():
            pltpu.make_async_copy(ostage.at[slot], y_hbm.at[pl.ds(0, CHUNK_PITCHED), :], out_sem.at[slot]).wait()
            flags[OUT_BUSY + slot] = 0

    @pl.when(first_step)
    def _():
        for i in range(OUT_BUSY + OUT_SLOTS):
            flags[i] = 0

        def clear(c, carry):
            yacc[rows(c), :] = jnp.zeros((ROW_CHUNK, D_MODEL), f32)
            return carry

        lax.fori_loop(0, UNIT_ROWS // ROW_CHUNK, clear, 0)

        @pl.when(n_chunks > 0)
        def _():
            first_table = tok_copy(row0, tcur)
            first_table.start()
            first_table.wait()

            def load(c, carry):
                gather_start(c, tcur, 0)
                gather_finish(c, cur, 0)
                return carry

            lax.fori_loop(0, n_chunks, load, 0)

        @pl.when(n_next > 0)
        def _():
            tok_copy(urow_ref[u_next], tnxt).start()

    @pl.when(jnp.logical_and(f == 0, n_next > 0))
    def _():
        tok_copy(urow_ref[u_next], tnxt).wait()

        @pl.when(n_next2 > 0)
        def _():
            tok_copy(urow_ref[u_next2], (u + 2) % TABLES).start()

    prefetching = f < n_next
    pending = flags[PENDING] == 1
    done_chunk = flags[PEND_CHUNK]
    done_slot = flags[PEND_SLOT]
    fused = jnp.logical_and(jnp.logical_and(pending, prefetching), done_chunk % 2 != f % 2)
    for parity in (0, 1):
        @pl.when(jnp.logical_and(fused, f % 2 == parity))
        def _():
            finish_and_start(done_chunk, done_slot, f, tnxt, parity)

        @pl.when(jnp.logical_and(jnp.logical_and(pending, jnp.logical_not(fused)), done_chunk % 2 == parity))
        def _():
            gather_finish(done_chunk, done_slot, parity)

    for parity in (0, 1):
        @pl.when(jnp.logical_and(jnp.logical_and(prefetching, jnp.logical_not(fused)), f % 2 == parity))
        def _():
            gather_start(f, tnxt, parity)

    flags[PENDING] = prefetching.astype(jnp.int32)

    @pl.when(prefetching)
    def _():
        flags[PEND_CHUNK] = f
        flags[PEND_SLOT] = nxt

    n_granules = used_ref[1 + u]
    block_sizes = (8, 4, 2, 1)

    @pl.when(n_chunks > 0)
    def _():
        bg = bg_ref[...]
        bu = bu_ref[...]
        mm = lambda a, w: lax.dot_general(a, w, (((1,), (0,)), ((), ())), preferred_element_type=f32)

        def mlp(sl, first):
            xc = xbuf[cur, sl, :]
            gt = jnp.minimum(mm(xc, wg_ref[...]) + bg, SWIGLU_LIMIT)
            up = jnp.clip(mm(xc, wu_ref[...]) + bu, -SWIGLU_LIMIT, SWIGLU_LIMIT)
            hid = (up + 1.0) * gt * jax.nn.sigmoid(SWIGLU_ALPHA * gt)
            part = mm(hid.astype(bf16), wd_ref[...])
            if first:
                yacc[sl, :] = part
            else:
                yacc[sl, :] += part

        def blocks(first):
            big = block_sizes[0] * COMPUTE_GRANULE
            n_big = n_granules // block_sizes[0]

            def big_block(c, carry):
                mlp(rows(c, big), first)
                return carry

            lax.fori_loop(0, n_big, big_block, 0)
            done = n_big * block_sizes[0]
            for size in block_sizes[1:]:
                present = (n_granules // size) % 2
                m = size * COMPUTE_GRANULE

                @pl.when(present == 1)
                def _(done=done, m=m):
                    mlp(pl.ds(pl.multiple_of(done * COMPUTE_GRANULE, m), m), first)

                done = done + present * size

        @pl.when(f == 0)
        def _():
            blocks(True)

        @pl.when(f > 0)
        def _():
            blocks(False)

    @pl.when(jnp.logical_and(f == n_f - 1, n_chunks > 0))
    def _():
        bd = bd_ref[...]

        def store(c, carry):
            slot = c % OUT_SLOTS
            out_wait(slot)
            _store_slabs(ostage.at[slot], yacc[rows(c), :] + bd)
            out_copy(c, slot).start()
            flags[OUT_BUSY + slot] = 1
            return carry

        lax.fori_loop(0, n_chunks, store, 0)

    @pl.when(jnp.logical_and(u == n_u - 1, f == n_f - 1))
    def _():
        for s in range(OUT_SLOTS):
            out_wait(s)
        first = used_ref[0] // ROW_CHUNK
        last = y_hbm.shape[0] // CHUNK_PITCHED
        ostage[0] = jnp.zeros((CHUNK_PITCHED, LANES), u32)

        def tail_copy(c):
            dst = y_hbm.at[pl.ds(pl.multiple_of(c * CHUNK_PITCHED, CHUNK_PITCHED), CHUNK_PITCHED), :]
            return pltpu.make_async_copy(ostage.at[0], dst, out_sem.at[0])

        def start(c, carry):
            tail_copy(c).start()
            return carry

        def wait(c, carry):
            tail_copy(c).wait()
            return carry

        lax.fori_loop(first, last, start, 0)
        lax.fori_loop(first, last, wait, 0)


def _moe_experts(x1s, slot_tok, ue, u_row, u_chunks, n_used_rows, n_pad, max_units, w_gate, b_gate, w_up, b_up,
                 w_down, b_down):
    e, d, dff = w_gate.shape
    n_f = dff // F_TILE
    assert n_f >= UNIT_ROWS // ROW_CHUNK, "one chunk of the next unit is gathered per f-tile step"
    f_idx = lambda u, f, uc: jnp.where(uc[u] > 0, f, n_f - 1)
    col_w = pl.BlockSpec((None, d, F_TILE), lambda u, f, ue, ur, uc, used: (ue[u], 0, f_idx(u, f, uc)))
    col_b = pl.BlockSpec((None, 1, F_TILE), lambda u, f, ue, ur, uc, used: (ue[u], 0, f_idx(u, f, uc)))
    return pl.pallas_call(
        _moe_kernel,
        grid_spec=pltpu.PrefetchScalarGridSpec(
            num_scalar_prefetch=4,
            grid=(max_units, n_f),
            in_specs=[pl.BlockSpec(memory_space=pl.ANY), pl.BlockSpec(memory_space=pl.ANY),
                      col_w, col_b, col_w, col_b,
                      pl.BlockSpec((None, F_TILE, d), lambda u, f, ue, ur, uc, used: (ue[u], f_idx(u, f, uc), 0)),
                      pl.BlockSpec((None, 1, d), lambda u, f, ue, ur, uc, used: (ue[u], 0, 0))],
            out_specs=pl.BlockSpec(memory_space=pl.ANY),
            scratch_shapes=[pltpu.VMEM((2, UNIT_ROWS, d), bf16),
                            pltpu.VMEM((UNIT_ROWS, d), f32),
                            pltpu.VMEM((CHUNK_PITCHED, LANES), u32),
                            pltpu.VMEM((CHUNK_PITCHED, LANES), u32),
                            pltpu.VMEM((OUT_SLOTS, CHUNK_PITCHED, LANES), u32),
                            pltpu.SMEM((3 * UNIT_ROWS,), jnp.int32),
                            pltpu.SMEM((3 + OUT_SLOTS,), jnp.int32),
                            pltpu.SemaphoreType.DMA(()),
                            pltpu.SemaphoreType.DMA((2,)),
                            pltpu.SemaphoreType.DMA((OUT_SLOTS,))],
        ),
        out_shape=jax.ShapeDtypeStruct((n_pad * PITCH, LANES), u32),
        compiler_params=_cparams(("arbitrary", "arbitrary")),
        name="moe_experts",
    )(ue, u_row, u_chunks, n_used_rows, slot_tok, x1s, w_gate, b_gate.reshape(e, 1, dff), w_up,
      b_up.reshape(e, 1, dff), w_down, b_down.reshape(e, 1, d))


COMBINE_ROWS = 256
COMBINE_GROUPS = 2


def _combine_kernel(dest_ref, dest_next_ref, y_hbm, gate_ref, x1_ref, p_ref, lnw_ref, lnb_ref, wp_ref, pnw_ref,
                    wg_ref, o_ref, ybuf_a, ybuf_b, fsum_ref, sem_a, sem_b):
    tm = COMBINE_ROWS
    s = pl.program_id(0)

    def row_copy(table, table_off, t, k, ybuf, sem):
        src = y_hbm.at[pl.ds(table[(table_off + t) * TOP_K + k] * PITCH, SLAB), :]
        return pltpu.make_async_copy(src, ybuf.at[k, pl.ds(t * PITCH, SLAB), :], sem)

    def wait_tile(ybuf, sem):
        for k in range(TOP_K):
            pltpu.make_async_copy(y_hbm.at[pl.ds(0, tm * SLAB), :], ybuf.at[k, pl.ds(0, tm * SLAB), :], sem).wait()

    @pl.when(s == 0)
    def _():
        def issue(t, c):
            for k in range(TOP_K):
                row_copy(dest_ref, 0, t, k, ybuf_a, sem_a).start()
            return c

        lax.fori_loop(0, tm, issue, 0, unroll=2)

    def finish_tile(row0, ybuf, next_table, next_off, next_buf, next_sem):
        tok = pl.ds(row0, tm)
        pieces = SLAB * TOP_K
        per_piece = tm // pieces
        e = jnp.dot(p_ref[tok, :].astype(bf16), wp_ref[...], preferred_element_type=f32)
        e = e * lax.rsqrt(jnp.mean(e * e, axis=-1, keepdims=True) + LN_EPS) * pnw_ref[...]
        gates = gate_ref[tok, :]
        for j in range(SLAB):
            acc_lo, acc_hi = None, None
            for k in range(TOP_K):
                piece = j * TOP_K + k
                for t in range(piece * per_piece, (piece + 1) * per_piece):
                    for kk in range(TOP_K):
                        row_copy(next_table, next_off, t, kk, next_buf, next_sem).start()
                lo, hi = _slab_cols(ybuf.at[k], j, tm)
                g = gates[:, k:k + 1]
                acc_lo = g * lo if k == 0 else acc_lo + g * lo
                acc_hi = g * hi if k == 0 else acc_hi + g * hi
            fsum_ref[:, j * LANES:(j + 1) * LANES] = acc_lo
            fsum_ref[:, HALF + j * LANES:HALF + (j + 1) * LANES] = acc_hi
        x2 = _layer_norm(DN_ALPHA * x1_ref[tok, :] + fsum_ref[...], lnw_ref[...], lnb_ref[...])
        gate = jax.nn.sigmoid(jnp.dot(x2.astype(bf16), wg_ref[...], preferred_element_type=f32))
        o_ref[tok, :] = x2 + gate * e

    wait_tile(ybuf_a, sem_a)
    finish_tile(0, ybuf_a, dest_ref, tm, ybuf_b, sem_b)
    wait_tile(ybuf_b, sem_b)
    finish_tile(tm, ybuf_b, dest_next_ref, 0, ybuf_a, sem_a)

    @pl.when(s == pl.num_programs(0) - 1)
    def _():
        wait_tile(ybuf_a, sem_a)


def _combine(y, dest, gates, x1, p2d, ln_w, ln_b, wp_bf, ple_norm_w, wg_bf):
    t, d = x1.shape
    tm = COMBINE_ROWS
    n_steps = t // (2 * tm)
    row = lambda c: pl.BlockSpec((2 * tm, c), lambda i: (i, 0))
    full = lambda r, c: pl.BlockSpec((r, c), lambda i: (0, 0))
    table = lambda nxt: pl.BlockSpec((2 * tm * TOP_K,), lambda i: (jnp.minimum(i + nxt, n_steps - 1),),
                                     memory_space=pltpu.SMEM)
    dest_flat = dest.reshape(-1)
    return pl.pallas_call(
        _combine_kernel,
        grid=(n_steps,),
        in_specs=[table(0), table(1), pl.BlockSpec(memory_space=pl.ANY),
                  row(TOP_K), row(d), row(PLE_DIM), full(1, d), full(1, d), full(PLE_DIM, d), full(1, d),
                  full(d, d)],
        out_specs=row(d),
        out_shape=jax.ShapeDtypeStruct((t, d), f32),
        scratch_shapes=[pltpu.VMEM((TOP_K, tm * PITCH, LANES), u32), pltpu.VMEM((TOP_K, tm * PITCH, LANES), u32),
                        pltpu.VMEM((tm, d), f32), pltpu.SemaphoreType.DMA(()), pltpu.SemaphoreType.DMA(())],
        compiler_params=_cparams(("arbitrary",)),
        name="combine_ln2_ple",
    )(dest_flat, dest_flat, y, gates, x1, p2d, ln_w.reshape(1, d), ln_b.reshape(1, d), wp_bf,
      ple_norm_w.reshape(1, d), wg_bf)


def _layer(h, p_i, w_in, ret_gn_w, conv_w, conv_b, lru_wa, lru_ba, lru_wx, lru_bx, lru_lam, w_out,
           ln1_w, ln1_b, w_router, b_router, w_gate, b_gate, w_up, b_up, w_down, b_down,
           ln2_w, ln2_b, w_ple_proj, ple_norm_w, w_ple_gate):
    b, s, d = h.shape
    t = b * s
    x2d = h.reshape(t, d)
    proj = _in_proj(x2d, w_in).reshape(b, s, IN_COLS)
    ret_out = _retention(proj, ret_gn_w)
    lru_out = _lru(proj, conv_w, conv_b, lru_wa, lru_ba, lru_wx, lru_bx, lru_lam)
    x1, x1s, top_e, gates = _out_router(ret_out.reshape(t, RET_WIDTH), lru_out.reshape(t, LRU_WIDTH),
                                        w_out.astype(bf16), x2d, ln1_w, ln1_b, w_router, b_router)
    dest, slot_tok, ue, u_row, u_chunks, n_used_rows, n_pad, max_units = _routing_tables(top_e, t)
    y = _moe_experts(x1s, slot_tok, ue, u_row, u_chunks, n_used_rows, n_pad, max_units, w_gate, b_gate, w_up, b_up,
                     w_down, b_down)
    out = _combine(y, dest, gates, x1, p_i.reshape(t, PLE_DIM), ln2_w, ln2_b, w_ple_proj.astype(bf16),
                   ple_norm_w, w_ple_gate.astype(bf16))
    return out.reshape(b, s, d)


def kernel(x, p, w_in, ret_gn_w, conv_w, conv_b, lru_wa, lru_ba, lru_wx, lru_bx, lru_lam, w_out, ln1_w, ln1_b,
           w_router, b_router, w_gate, b_gate, w_up, b_up, w_down, b_down, ln2_w, ln2_b, w_ple_proj, ple_norm_w,
           w_ple_gate):
    h = x.astype(f32)
    for i in range(w_in.shape[0]):
        h = _layer(h, p[i], w_in[i], ret_gn_w[i], conv_w[i], conv_b[i], lru_wa[i], lru_ba[i], lru_wx[i],
                   lru_bx[i], lru_lam[i], w_out[i], ln1_w[i], ln1_b[i], w_router[i], b_router[i], w_gate[i],
                   b_gate[i], w_up[i], b_up[i], w_down[i], b_down[i], ln2_w[i], ln2_b[i], w_ple_proj[i],
                   ple_norm_w[i], w_ple_gate[i])
    return h.astype(x.dtype)
```

```python
import functools
import math

import jax
import jax.numpy as jnp
from jax import lax
from jax.experimental import pallas as pl
from jax.experimental.pallas import tpu as pltpu

D_MODEL = 2048
RET_HEAD_DIM = 128
RET_HEADS = 8
RET_WIDTH = RET_HEADS * RET_HEAD_DIM
LRU_WIDTH = D_MODEL - RET_WIDTH
LRU_BLOCKS = 8
LRU_BLOCK_DIM = LRU_WIDTH // LRU_BLOCKS
IN_COLS = 4 * RET_WIDTH + 2 * LRU_WIDTH
CONV_WIDTH = 4
LRU_C = 8.0
CHUNK = 128
ROPE_BASE = 10000.0
N_EXPERTS = 32
TOP_K = 4
SWIGLU_LIMIT = 7.0
SWIGLU_ALPHA = 1.702
PLE_DIM = 256
LN_EPS = 1e-5
DEPTH = 1
DN_ALPHA = (2.0 * DEPTH) ** 0.25

LANES = 128
SUBLANES = 8
VMEM_LIMIT = 60 * 1024 * 1024

ROW_CHUNK = 256
UNIT_ROWS = 2048
F_TILE = 256

f32 = jnp.float32
bf16 = jnp.bfloat16


def _cparams(sem):
    return pltpu.CompilerParams(dimension_semantics=sem, vmem_limit_bytes=VMEM_LIMIT)


def _in_proj_kernel(x_ref, w_ref, o_ref):
    o_ref[...] = lax.dot_general(x_ref[...].astype(bf16), w_ref[...], (((1,), (0,)), ((), ())),
                                 preferred_element_type=f32)


def _in_proj(x2d, w):
    t, d = x2d.shape
    n = w.shape[1]
    tm, tn = 512, 2048
    return pl.pallas_call(
        _in_proj_kernel,
        grid=(n // tn, t // tm),
        in_specs=[pl.BlockSpec((tm, d), lambda j, i: (i, 0)),
                  pl.BlockSpec((d, tn), lambda j, i: (0, j))],
        out_specs=pl.BlockSpec((tm, tn), lambda j, i: (i, j)),
        out_shape=jax.ShapeDtypeStruct((t, n), f32),
        compiler_params=_cparams(("parallel", "parallel")),
        name="in_proj",
    )(x2d, w)


def _retention_kernel(q_ref, k_ref, v_ref, g_ref, cos_ref, sin_ref, dec_ref, qd_ref, kd_ref, cd_ref, gnw_ref,
                      o_ref):
    s = q_ref.shape[0]
    n_chunks = s // CHUNK
    decay = dec_ref[...]
    q_dec = qd_ref[...]
    k_dec = kd_ref[...]
    c_dec = cd_ref[0:1, :]
    gnw = gnw_ref[...]
    k_scale = RET_HEAD_DIM ** -0.5

    def rope(xv, cos, sin):
        return xv * cos + pltpu.roll(xv, RET_HEAD_DIM // 2, axis=1) * sin

    def body(n, state):
        sl = pl.ds(pl.multiple_of(n * CHUNK, CHUNK), CHUNK)
        cos = cos_ref[sl, :]
        sin = sin_ref[sl, :]
        q = rope(q_ref[sl, :], cos, sin)
        k = rope(k_ref[sl, :], cos, sin) * k_scale
        vb = v_ref[sl, :].astype(bf16)
        scores = lax.dot_general(q.astype(bf16), k.astype(bf16), (((1,), (1,)), ((), ())),
                                 preferred_element_type=f32) * decay
        intra = jnp.dot(scores.astype(bf16), vb, preferred_element_type=f32)
        cross = jnp.dot((q * q_dec).astype(bf16), state.astype(bf16), preferred_element_type=f32)
        kv = lax.dot_general((k * k_dec).astype(bf16), vb, (((0,), (0,)), ((), ())),
                             preferred_element_type=f32)
        ret = intra + cross
        mu = jnp.mean(ret, axis=-1, keepdims=True)
        cen = ret - mu
        var = jnp.mean(cen * cen, axis=-1, keepdims=True)
        ret = cen * lax.rsqrt(var + LN_EPS) * gnw
        g = g_ref[sl, :]
        o_ref[sl, :] = (g * jax.nn.sigmoid(g) * ret).astype(o_ref.dtype)
        return c_dec * state + kv

    lax.fori_loop(0, n_chunks, body, jnp.zeros((RET_HEAD_DIM, RET_HEAD_DIM), f32), unroll=4)


def _retention_tables(s):
    h, d = RET_HEADS, RET_HEAD_DIM
    inv = ROPE_BASE ** (-jnp.arange(0, d, 2, dtype=f32) / d)
    ang = jnp.arange(s, dtype=f32)[:, None] * inv[None, :]
    cos = jnp.cos(ang)
    sin = jnp.sin(ang)
    cos_t = jnp.concatenate([cos, cos], axis=-1)
    sin_t = jnp.concatenate([-sin, sin], axis=-1)
    log_gamma = jnp.log1p(-jnp.exp2(-5.0 - jnp.arange(h, dtype=f32)))
    idx = jnp.arange(CHUNK, dtype=f32)
    diff = idx[:, None] - idx[None, :]
    decay = jnp.where((diff >= 0)[None], jnp.exp(jnp.maximum(diff, 0.0)[None] * log_gamma[:, None, None]), 0.0)
    q_dec = jnp.exp((idx[None, :] + 1.0) * log_gamma[:, None])
    k_dec = jnp.exp((CHUNK - 1.0 - idx)[None, :] * log_gamma[:, None])
    c_dec = jnp.exp(CHUNK * log_gamma)
    q_dec = jnp.broadcast_to(q_dec[:, :, None], (h, CHUNK, d))
    k_dec = jnp.broadcast_to(k_dec[:, :, None], (h, CHUNK, d))
    c_dec = jnp.broadcast_to(c_dec[:, None, None], (h, SUBLANES, d))
    return cos_t, sin_t, decay, q_dec, k_dec, c_dec


def _retention(proj, ret_gn_w):
    b, s, _ = proj.shape
    d = RET_HEAD_DIM
    cos_t, sin_t, decay, q_dec, k_dec, c_dec = _retention_tables(s)
    col = lambda off: pl.BlockSpec((None, s, d), lambda bi, hi: (bi, 0, off + hi))
    per_head = lambda r: pl.BlockSpec((None, r, d), lambda bi, hi: (hi, 0, 0))
    full = lambda shp: pl.BlockSpec(shp, lambda bi, hi: (0,) * len(shp))
    return pl.pallas_call(
        _retention_kernel,
        grid=(b, RET_HEADS),
        in_specs=[col(0), col(RET_HEADS), col(2 * RET_HEADS), col(3 * RET_HEADS),
                  full((s, d)), full((s, d)),
                  per_head(CHUNK), per_head(CHUNK), per_head(CHUNK), per_head(SUBLANES),
                  pl.BlockSpec((1, d), lambda bi, hi: (0, hi))],
        out_specs=pl.BlockSpec((None, s, d), lambda bi, hi: (bi, 0, hi)),
        out_shape=jax.ShapeDtypeStruct((b, s, RET_WIDTH), bf16),
        compiler_params=_cparams(("parallel", "parallel")),
        name="retention",
    )(proj, proj, proj, proj, cos_t, sin_t, decay, q_dec, k_dec, c_dec, ret_gn_w.reshape(1, RET_WIDTH))


def _gelu_tanh(x):
    return 0.5 * x * (1.0 + jnp.tanh(math.sqrt(2.0 / math.pi) * (x + 0.044715 * (x * x * x))))


def _lru_kernel(xr_ref, yg_ref, cw_ref, cb_ref, wa_ref, ba_ref, wx_ref, bx_ref, lam_ref, o_ref, a_ref, b_ref):
    s = xr_ref.shape[0]
    x = xr_ref[...]
    rows = lax.broadcasted_iota(jnp.int32, x.shape, 0)
    xc = cb_ref[...] + cw_ref[CONV_WIDTH - 1:CONV_WIDTH, :] * x
    for back in range(1, CONV_WIDTH):
        shifted = jnp.where(rows >= back, pltpu.roll(x, back, axis=0), 0.0)
        xc = xc + cw_ref[CONV_WIDTH - 1 - back:CONV_WIDTH - back, :] * shifted
    xcb = xc.astype(bf16)
    r = jax.nn.sigmoid(jnp.dot(xcb, wa_ref[...].astype(bf16), preferred_element_type=f32) + ba_ref[...])
    gi = jax.nn.sigmoid(jnp.dot(xcb, wx_ref[...].astype(bf16), preferred_element_type=f32) + bx_ref[...])
    lam = lam_ref[...]
    log_sig = jnp.minimum(lam, 0.0) - jnp.log1p(jnp.exp(-jnp.abs(lam)))
    log_a = LRU_C * r * log_sig
    a = jnp.exp(log_a)
    a_ref[...] = a
    b_ref[...] = jnp.sqrt(-jnp.tanh(log_a) * (a * a + 1.0)) * (gi * xc)

    row8 = lax.broadcasted_iota(jnp.int32, (SUBLANES, LANES), 0)

    def body(i, h_prev):
        sl = pl.ds(pl.multiple_of(i * SUBLANES, SUBLANES), SUBLANES)
        a8 = a_ref[sl, :]
        b8 = b_ref[sl, :]
        for sh in (1, 2, 4):
            a_sh = jnp.where(row8 >= sh, pltpu.roll(a8, sh, axis=0), 1.0)
            b_sh = jnp.where(row8 >= sh, pltpu.roll(b8, sh, axis=0), 0.0)
            b8 = a8 * b_sh + b8
            a8 = a8 * a_sh
        h8 = a8 * h_prev + b8
        o_ref[sl, :] = (_gelu_tanh(yg_ref[sl, :]) * h8).astype(o_ref.dtype)
        return h8[SUBLANES - 1:SUBLANES, :]

    lax.fori_loop(0, s // SUBLANES, body, jnp.zeros((1, LANES), f32), unroll=8)


def _lru(proj, conv_w, conv_b, wa, ba, wx, bx, lam):
    b, s, _ = proj.shape
    d = LRU_BLOCK_DIM
    xr_off = 4 * RET_WIDTH // d
    yg_off = xr_off + LRU_BLOCKS
    col = lambda off: pl.BlockSpec((None, s, d), lambda bi, ji: (bi, 0, off + ji))
    vec = lambda r: pl.BlockSpec((r, d), lambda bi, ji: (0, ji))
    blk = lambda r: pl.BlockSpec((None, r, d), lambda bi, ji: (ji, 0, 0))
    return pl.pallas_call(
        _lru_kernel,
        grid=(b, LRU_BLOCKS),
        in_specs=[col(xr_off), col(yg_off), vec(CONV_WIDTH), vec(1), blk(d), blk(1), blk(d), blk(1), vec(1)],
        out_specs=pl.BlockSpec((None, s, d), lambda bi, ji: (bi, 0, ji)),
        out_shape=jax.ShapeDtypeStruct((b, s, LRU_WIDTH), bf16),
        scratch_shapes=[pltpu.VMEM((s, d), f32), pltpu.VMEM((s, d), f32)],
        compiler_params=_cparams(("parallel", "parallel")),
        name="rg_lru",
    )(proj, proj, conv_w, conv_b.reshape(1, LRU_WIDTH), wa, ba.reshape(LRU_BLOCKS, 1, d), wx,
      bx.reshape(LRU_BLOCKS, 1, d), lam.reshape(1, LRU_WIDTH))


def _layer_norm(y, w, b):
    mu = jnp.mean(y, axis=-1, keepdims=True)
    cen = y - mu
    var = jnp.mean(cen * cen, axis=-1, keepdims=True)
    return cen * lax.rsqrt(var + LN_EPS) * w + b


HALF = D_MODEL // 2
SLAB = HALF // LANES
PITCH = SLAB + 4
u32 = jnp.uint32
HIGH_MASK = 0xFFFF0000


def _bf16_bits(v):
    return lax.bitcast_convert_type(v.astype(bf16).astype(f32), u32)


def _store_slabs(ref, val, before_piece=None):
    n = val.shape[0]
    for j in range(SLAB):
        if before_piece is not None:
            before_piece(j)
        lo = _bf16_bits(val[:, j * LANES:(j + 1) * LANES])
        hi = _bf16_bits(val[:, HALF + j * LANES:HALF + (j + 1) * LANES])
        ref[pl.ds(j, n, stride=PITCH), :] = hi | lax.shift_right_logical(lo, jnp.full_like(lo, 16))
    for j in range(SLAB, PITCH):
        ref[pl.ds(j, n, stride=PITCH), :] = jnp.zeros((n, LANES), u32)


def _slab_cols(ref, j, n):
    w = ref[pl.ds(j, n, stride=PITCH), :]
    lo = lax.bitcast_convert_type(lax.shift_left(w, jnp.full_like(w, 16)), f32)
    hi = lax.bitcast_convert_type(w & jnp.full_like(w, HIGH_MASK), f32)
    return lo, hi


def _split_bf16(v):
    hi = v.astype(bf16)
    lo = (v - hi.astype(f32)).astype(bf16)
    return hi, lo


def _out_router_kernel(ret_ref, lru_ref, wo_ref, x_ref, lnw_ref, lnb_ref, wr_ref, br_ref,
                       x1_ref, tope_ref, gate_ref):
    m = jnp.dot(ret_ref[...], wo_ref[0:RET_WIDTH, :], preferred_element_type=f32)
    m = m + jnp.dot(lru_ref[...], wo_ref[RET_WIDTH:D_MODEL, :], preferred_element_type=f32)
    x1 = _layer_norm(DN_ALPHA * x_ref[...] + m, lnw_ref[...], lnb_ref[...])
    x1_ref[...] = x1
    xh, xl = _split_bf16(x1)
    wh, wl = _split_bf16(wr_ref[...])
    logits = (jnp.dot(xh, wh, preferred_element_type=f32) + jnp.dot(xl, wh, preferred_element_type=f32)
              + jnp.dot(xh, wl, preferred_element_type=f32)) + br_ref[...]
    tm = logits.shape[0]
    lane = lax.broadcasted_iota(jnp.int32, logits.shape, 1)
    lane_k = lax.broadcasted_iota(jnp.int32, (tm, TOP_K), 1)
    top_e = jnp.zeros((tm, TOP_K), jnp.int32)
    top_v = jnp.zeros((tm, TOP_K), f32)
    cur = logits
    for kk in range(TOP_K):
        mx = jnp.max(cur, axis=-1, keepdims=True)
        idx = jnp.min(jnp.where(cur == mx, lane, N_EXPERTS), axis=-1, keepdims=True)
        top_e = jnp.where(lane_k == kk, idx, top_e)
        top_v = jnp.where(lane_k == kk, mx, top_v)
        cur = jnp.where(lane == idx, -jnp.inf, cur)
    ex = jnp.exp(top_v - top_v[:, 0:1])
    gate_ref[...] = ex / jnp.sum(ex, axis=-1, keepdims=True)
    tope_ref[...] = top_e


def _out_router(ret_out, lru_out, wo_bf, x2d, ln_w, ln_b, w_router, b_router):
    t, d = x2d.shape
    tm = 512
    row = lambda c: pl.BlockSpec((tm, c), lambda i: (i, 0))
    full = lambda r, c: pl.BlockSpec((r, c), lambda i: (0, 0))
    return pl.pallas_call(
        _out_router_kernel,
        grid=(t // tm,),
        in_specs=[row(RET_WIDTH), row(LRU_WIDTH), full(d, d), row(d), full(1, d), full(1, d),
                  full(d, N_EXPERTS), full(1, N_EXPERTS)],
        out_specs=[row(d), row(TOP_K), row(TOP_K)],
        out_shape=[jax.ShapeDtypeStruct((t, d), f32), jax.ShapeDtypeStruct((t, TOP_K), jnp.int32),
                   jax.ShapeDtypeStruct((t, TOP_K), f32)],
        compiler_params=_cparams(("parallel",)),
        name="out_proj_ln1_router",
    )(ret_out, lru_out, wo_bf, x2d, ln_w.reshape(1, d), ln_b.reshape(1, d), w_router,
      b_router.reshape(1, N_EXPERTS))


def _routing_tables(top_e, t):
    n_pad = t * TOP_K + N_EXPERTS * ROW_CHUNK
    max_units = N_EXPERTS + (t * TOP_K) // UNIT_ROWS
    sel = (top_e[:, :, None] == jnp.arange(N_EXPERTS, dtype=jnp.int32)[None, None, :]).any(axis=1)
    sel = sel.astype(jnp.int32)
    counts = jnp.sum(sel, axis=0)
    rank = jnp.cumsum(sel, axis=0) - sel
    padded = (counts + ROW_CHUNK - 1) // ROW_CHUNK * ROW_CHUNK
    pad_ends = jnp.cumsum(padded)
    pad_starts = pad_ends - padded
    dest_dense = pad_starts[None, :] + rank
    dest = jnp.take_along_axis(dest_dense, top_e, axis=1)
    pad_fill = jnp.concatenate([pad_starts + counts, padded - counts, pad_ends[-1:]]).astype(jnp.int32)
    units_per_e = (padded + UNIT_ROWS - 1) // UNIT_ROWS
    unit_ends = jnp.cumsum(units_per_e)
    unit_starts = unit_ends - units_per_e
    n_units = unit_ends[-1]
    u = jnp.arange(max_units, dtype=jnp.int32)
    u_clamped = jnp.minimum(u, n_units - 1)
    ue = jnp.searchsorted(unit_ends, u_clamped, side='right').astype(jnp.int32)
    ue = jnp.minimum(ue, N_EXPERTS - 1)
    j = u_clamped - unit_starts[ue]
    u_row = pad_starts[ue] + j * UNIT_ROWS
    u_rows = jnp.minimum(UNIT_ROWS, padded[ue] - j * UNIT_ROWS)
    u_chunks = jnp.where(u < n_units, u_rows // ROW_CHUNK, 0).astype(jnp.int32)
    u_real = jnp.clip(counts[ue] - j * UNIT_ROWS, 0, UNIT_ROWS)
    u_granules = jnp.where(u < n_units, (u_real + COMPUTE_GRANULE - 1) // COMPUTE_GRANULE, 0).astype(jnp.int32)
    n_used_rows = jnp.concatenate([pad_ends[-1:].astype(jnp.int32), u_granules])
    return dest.astype(jnp.int32), pad_fill, ue, u_row.astype(jnp.int32), u_chunks, n_used_rows, n_pad, max_units


CHUNK_PITCHED = ROW_CHUNK * PITCH
DISPATCH_ROWS = 256
FILL_SLOTS = ROW_CHUNK // 2


def _dispatch_kernel(fill_ref, dest_ref, dest_prev_ref, x1_ref, xs_hbm, stage_a, stage_b, zeros, sem_a, sem_b,
                     sem_z):
    tm = DISPATCH_ROWS
    s = pl.program_id(0)
    last = pl.num_programs(0) - 1

    def slot_copy(table, table_off, t, k, stage, sem):
        dst = xs_hbm.at[pl.ds(table[(table_off + t) * TOP_K + k] * PITCH, PITCH), :]
        return pltpu.make_async_copy(stage.at[pl.ds(t * PITCH, PITCH), :], dst, sem)

    def wait_tile(stage, sem):
        for _ in range(TOP_K):
            pltpu.make_async_copy(stage, xs_hbm.at[pl.ds(0, tm * PITCH), :], sem).wait()

    def start_copies(table, table_off, stage, sem):
        per_piece = tm // SLAB

        def before_piece(j):
            for t in range(j * per_piece, (j + 1) * per_piece):
                for k in range(TOP_K):
                    slot_copy(table, table_off, t, k, stage, sem).start()

        return before_piece

    def pack(row0, stage, before_piece=None):
        _store_slabs(stage, x1_ref[pl.ds(row0, tm), :], before_piece)

    @pl.when(s == 0)
    def _():
        pack(0, stage_a)

    @pl.when(s > 0)
    def _():
        wait_tile(stage_a, sem_a)
        pack(0, stage_a, start_copies(dest_prev_ref, tm, stage_b, sem_b))
        wait_tile(stage_b, sem_b)

    pack(tm, stage_b, start_copies(dest_ref, 0, stage_a, sem_a))

    @pl.when(s == last)
    def _():
        def issue(t, c):
            for k in range(TOP_K):
                slot_copy(dest_ref, tm, t, k, stage_b, sem_b).start()
            return c

        lax.fori_loop(0, tm, issue, 0, unroll=2)
        wait_tile(stage_a, sem_a)
        wait_tile(stage_b, sem_b)

        zeros[...] = jnp.zeros(zeros.shape, u32)

        def fill(slot, n_slots):
            return pltpu.make_async_copy(zeros.at[pl.ds(0, n_slots * PITCH), :],
                                         xs_hbm.at[pl.ds(slot * PITCH, n_slots * PITCH), :], sem_z)

        def expert_pad(e, start):
            slot = fill_ref[e]
            n = fill_ref[N_EXPERTS + e]
            piece = FILL_SLOTS
            while piece >= 1:
                @pl.when((n // piece) % 2 == 1)
                def _(slot=slot, piece=piece):
                    if start:
                        fill(slot, piece).start()
                    else:
                        fill(slot, piece).wait()

                slot = slot + ((n // piece) % 2) * piece
                piece //= 2

        def tail(i, start):
            slot = fill_ref[2 * N_EXPERTS] + i * FILL_SLOTS
            if start:
                fill(slot, FILL_SLOTS).start()
            else:
                fill(slot, FILL_SLOTS).wait()

        n_tail = (xs_hbm.shape[0] // PITCH - fill_ref[2 * N_EXPERTS]) // FILL_SLOTS
        for start in (True, False):
            lax.fori_loop(0, N_EXPERTS, lambda e, c, start=start: (expert_pad(e, start), c)[1], 0)
            lax.fori_loop(0, n_tail, lambda i, c, start=start: (tail(i, start), c)[1], 0)


def _dispatch(x1, dest, pad_fill, n_pad):
    t, d = x1.shape
    tm = DISPATCH_ROWS
    n_steps = t // (2 * tm)
    table = lambda back: pl.BlockSpec((2 * tm * TOP_K,), lambda i, fill: (jnp.maximum(i - back, 0),),
                                      memory_space=pltpu.SMEM)
    dest_flat = dest.reshape(-1)
    stage = pltpu.VMEM((tm * PITCH, LANES), u32)
    return pl.pallas_call(
        _dispatch_kernel,
        grid_spec=pltpu.PrefetchScalarGridSpec(
            num_scalar_prefetch=1,
            grid=(n_steps,),
            in_specs=[table(0), table(1), pl.BlockSpec((2 * tm, d), lambda i, fill: (i, 0))],
            out_specs=pl.BlockSpec(memory_space=pl.ANY),
            scratch_shapes=[stage, stage, pltpu.VMEM((FILL_SLOTS * PITCH, LANES), u32),
                            pltpu.SemaphoreType.DMA(()), pltpu.SemaphoreType.DMA(()), pltpu.SemaphoreType.DMA(())],
        ),
        out_shape=jax.ShapeDtypeStruct((n_pad * PITCH, LANES), u32),
        compiler_params=_cparams(("arbitrary",)),
        name="moe_dispatch",
    )(pad_fill, dest_flat, dest_flat, x1)


OUT_SLOTS = 4
COMPUTE_GRANULE = ROW_CHUNK // 2


def _moe_kernel(ue_ref, urow_ref, uchunks_ref, used_ref, xs_hbm, wg_ref, bg_ref, wu_ref, bu_ref, wd_ref,
                bd_ref, y_hbm, xbuf, yacc, gstage_a, gstage_b, ostage, flags, in_sem, out_sem):
    u = pl.program_id(0)
    f = pl.program_id(1)
    n_u = pl.num_programs(0)
    n_f = pl.num_programs(1)
    n_chunks = uchunks_ref[u]
    row0 = urow_ref[u]
    cur = u % 2
    nxt = 1 - cur
    u_next = jnp.minimum(u + 1, n_u - 1)
    n_next = jnp.where(u + 1 < n_u, uchunks_ref[u_next], 0)
    row_next = urow_ref[u_next]
    first_step = jnp.logical_and(u == 0, f == 0)
    PENDING, PEND_CHUNK, PEND_SLOT, OUT_BUSY = 0, 1, 2, 3

    def rows(c, size=ROW_CHUNK):
        return pl.ds(pl.multiple_of(c * size, size), size)

    stages = (gstage_a, gstage_b)

    def chunk_copy(unit_row, c, parity):
        src = xs_hbm.at[pl.ds(pl.multiple_of((unit_row + c * ROW_CHUNK) * PITCH, CHUNK_PITCHED), CHUNK_PITCHED), :]
        return pltpu.make_async_copy(src, stages[parity], in_sem.at[parity])

    def unpack(c, xslot, parity):
        for j in range(SLAB):
            lo, hi = _slab_cols(stages[parity], j, ROW_CHUNK)
            xbuf[xslot, rows(c), j * LANES:(j + 1) * LANES] = lo.astype(bf16)
            xbuf[xslot, rows(c), HALF + j * LANES:HALF + (j + 1) * LANES] = hi.astype(bf16)

    def out_copy(c, slot):
        dst = y_hbm.at[pl.ds(pl.multiple_of((row0 + c * ROW_CHUNK) * PITCH, CHUNK_PITCHED), CHUNK_PITCHED), :]
        return pltpu.make_async_copy(ostage.at[slot], dst, out_sem.at[slot])

    def out_wait(slot):
        @pl.when(flags[OUT_BUSY + slot] == 1)
        def _():
            pltpu.make_async_copy(ostage.at[slot], y_hbm.at[pl.ds(0, CHUNK_PITCHED), :], out_sem.at[slot]).wait()
            flags[OUT_BUSY + slot] = 0

    @pl.when(first_step)
    def _():
        for i in range(OUT_BUSY + OUT_SLOTS):
            flags[i] = 0

        def clear(c, carry):
            yacc[rows(c), :] = jnp.zeros((ROW_CHUNK, D_MODEL), f32)
            return carry

        lax.fori_loop(0, UNIT_ROWS // ROW_CHUNK, clear, 0)

        def load(c, carry):
            first = chunk_copy(row0, c, 0)
            first.start()
            first.wait()
            unpack(c, cur, 0)
            return carry

        lax.fori_loop(0, n_chunks, load, 0)

    prefetching = f < n_next
    pending = flags[PENDING] == 1
    done_chunk = flags[PEND_CHUNK]
    done_slot = flags[PEND_SLOT]
    same_buffer = jnp.logical_and(pending, done_chunk % 2 == f % 2)
    for late in (False, True):
        for parity in (0, 1):
            starts_now = jnp.logical_and(prefetching, same_buffer if late else jnp.logical_not(same_buffer))

            @pl.when(jnp.logical_and(starts_now, f % 2 == parity))
            def _():
                chunk_copy(row_next, f, parity).start()

        if not late:
            for parity in (0, 1):
                @pl.when(jnp.logical_and(pending, done_chunk % 2 == parity))
                def _():
                    chunk_copy(row_next, done_chunk, parity).wait()
                    unpack(done_chunk, done_slot, parity)

    flags[PENDING] = prefetching.astype(jnp.int32)

    @pl.when(prefetching)
    def _():
        flags[PEND_CHUNK] = f
        flags[PEND_SLOT] = nxt

    n_granules = used_ref[1 + u]
    block_sizes = (8, 4, 2, 1)

    @pl.when(n_chunks > 0)
    def _():
        bg = bg_ref[...]
        bu = bu_ref[...]
        mm = lambda a, w: lax.dot_general(a, w, (((1,), (0,)), ((), ())), preferred_element_type=f32)

        def mlp(sl, first):
            xc = xbuf[cur, sl, :]
            gt = jnp.minimum(mm(xc, wg_ref[...]) + bg, SWIGLU_LIMIT)
            up = jnp.clip(mm(xc, wu_ref[...]) + bu, -SWIGLU_LIMIT, SWIGLU_LIMIT)
            hid = (up + 1.0) * gt * jax.nn.sigmoid(SWIGLU_ALPHA * gt)
            part = mm(hid.astype(bf16), wd_ref[...])
            if first:
                yacc[sl, :] = part
            else:
                yacc[sl, :] += part

        def blocks(first):
            big = block_sizes[0] * COMPUTE_GRANULE
            n_big = n_granules // block_sizes[0]

            def big_block(c, carry):
                mlp(rows(c, big), first)
                return carry

            lax.fori_loop(0, n_big, big_block, 0)
            done = n_big * block_sizes[0]
            for size in block_sizes[1:]:
                present = (n_granules // size) % 2
                m = size * COMPUTE_GRANULE

                @pl.when(present == 1)
                def _(done=done, m=m):
                    mlp(pl.ds(pl.multiple_of(done * COMPUTE_GRANULE, m), m), first)

                done = done + present * size

        @pl.when(f == 0)
        def _():
            blocks(True)

        @pl.when(f > 0)
        def _():
            blocks(False)

    @pl.when(jnp.logical_and(f == n_f - 1, n_chunks > 0))
    def _():
        bd = bd_ref[...]

        def store(c, carry):
            slot = c % OUT_SLOTS
            out_wait(slot)
            _store_slabs(ostage.at[slot], yacc[rows(c), :] + bd)
            out_copy(c, slot).start()
            flags[OUT_BUSY + slot] = 1
            return carry

        lax.fori_loop(0, n_chunks, store, 0)

    @pl.when(jnp.logical_and(u == n_u - 1, f == n_f - 1))
    def _():
        for s in range(OUT_SLOTS):
            out_wait(s)
        first = used_ref[0] // ROW_CHUNK
        last = y_hbm.shape[0] // CHUNK_PITCHED
        ostage[0] = jnp.zeros((CHUNK_PITCHED, LANES), u32)

        def tail_copy(c):
            dst = y_hbm.at[pl.ds(pl.multiple_of(c * CHUNK_PITCHED, CHUNK_PITCHED), CHUNK_PITCHED), :]
            return pltpu.make_async_copy(ostage.at[0], dst, out_sem.at[0])

        def start(c, carry):
            tail_copy(c).start()
            return carry

        def wait(c, carry):
            tail_copy(c).wait()
            return carry

        lax.fori_loop(first, last, start, 0)
        lax.fori_loop(first, last, wait, 0)


def _moe_experts(xs, ue, u_row, u_chunks, n_used_rows, n_pad, max_units, w_gate, b_gate, w_up, b_up, w_down,
                 b_down):
    e, d, dff = w_gate.shape
    n_f = dff // F_TILE
    assert n_f >= UNIT_ROWS // ROW_CHUNK, "one chunk of the next unit is loaded per f-tile step"
    f_idx = lambda u, f, uc: jnp.where(uc[u] > 0, f, n_f - 1)
    col_w = pl.BlockSpec((None, d, F_TILE), lambda u, f, ue, ur, uc, used: (ue[u], 0, f_idx(u, f, uc)))
    col_b = pl.BlockSpec((None, 1, F_TILE), lambda u, f, ue, ur, uc, used: (ue[u], 0, f_idx(u, f, uc)))
    return pl.pallas_call(
        _moe_kernel,
        grid_spec=pltpu.PrefetchScalarGridSpec(
            num_scalar_prefetch=4,
            grid=(max_units, n_f),
            in_specs=[pl.BlockSpec(memory_space=pl.ANY),
                      col_w, col_b, col_w, col_b,
                      pl.BlockSpec((None, F_TILE, d), lambda u, f, ue, ur, uc, used: (ue[u], f_idx(u, f, uc), 0)),
                      pl.BlockSpec((None, 1, d), lambda u, f, ue, ur, uc, used: (ue[u], 0, 0))],
            out_specs=pl.BlockSpec(memory_space=pl.ANY),
            scratch_shapes=[pltpu.VMEM((2, UNIT_ROWS, d), bf16),
                            pltpu.VMEM((UNIT_ROWS, d), f32),
                            pltpu.VMEM((CHUNK_PITCHED, LANES), u32),
                            pltpu.VMEM((CHUNK_PITCHED, LANES), u32),
                            pltpu.VMEM((OUT_SLOTS, CHUNK_PITCHED, LANES), u32),
                            pltpu.SMEM((3 + OUT_SLOTS,), jnp.int32),
                            pltpu.SemaphoreType.DMA((2,)),
                            pltpu.SemaphoreType.DMA((OUT_SLOTS,))],
        ),
        out_shape=jax.ShapeDtypeStruct((n_pad * PITCH, LANES), u32),
        compiler_params=_cparams(("arbitrary", "arbitrary")),
        name="moe_experts",
    )(ue, u_row, u_chunks, n_used_rows, xs, w_gate, b_gate.reshape(e, 1, dff), w_up,
      b_up.reshape(e, 1, dff), w_down, b_down.reshape(e, 1, d))


COMBINE_ROWS = 256
COMBINE_GROUPS = 2


def _combine_kernel(dest_ref, dest_next_ref, y_hbm, gate_ref, x1_ref, p_ref, lnw_ref, lnb_ref, wp_ref, pnw_ref,
                    wg_ref, o_ref, ybuf_a, ybuf_b, fsum_ref, sem_a, sem_b):
    tm = COMBINE_ROWS
    s = pl.program_id(0)

    def row_copy(table, table_off, t, k, ybuf, sem):
        src = y_hbm.at[pl.ds(table[(table_off + t) * TOP_K + k] * PITCH, SLAB), :]
        return pltpu.make_async_copy(src, ybuf.at[k, pl.ds(t * PITCH, SLAB), :], sem)

    def wait_tile(ybuf, sem):
        for k in range(TOP_K):
            pltpu.make_async_copy(y_hbm.at[pl.ds(0, tm * SLAB), :], ybuf.at[k, pl.ds(0, tm * SLAB), :], sem).wait()

    @pl.when(s == 0)
    def _():
        def issue(t, c):
            for k in range(TOP_K):
                row_copy(dest_ref, 0, t, k, ybuf_a, sem_a).start()
            return c

        lax.fori_loop(0, tm, issue, 0, unroll=2)

    def finish_tile(row0, ybuf, next_table, next_off, next_buf, next_sem):
        tok = pl.ds(row0, tm)
        pieces = SLAB * TOP_K
        per_piece = tm // pieces
        e = jnp.dot(p_ref[tok, :].astype(bf16), wp_ref[...], preferred_element_type=f32)
        e = e * lax.rsqrt(jnp.mean(e * e, axis=-1, keepdims=True) + LN_EPS) * pnw_ref[...]
        gates = gate_ref[tok, :]
        for j in range(SLAB):
            acc_lo, acc_hi = None, None
            for k in range(TOP_K):
                piece = j * TOP_K + k
                for t in range(piece * per_piece, (piece + 1) * per_piece):
                    for kk in range(TOP_K):
                        row_copy(next_table, next_off, t, kk, next_buf, next_sem).start()
                lo, hi = _slab_cols(ybuf.at[k], j, tm)
                g = gates[:, k:k + 1]
                acc_lo = g * lo if k == 0 else acc_lo + g * lo
                acc_hi = g * hi if k == 0 else acc_hi + g * hi
            fsum_ref[:, j * LANES:(j + 1) * LANES] = acc_lo
            fsum_ref[:, HALF + j * LANES:HALF + (j + 1) * LANES] = acc_hi
        x2 = _layer_norm(DN_ALPHA * x1_ref[tok, :] + fsum_ref[...], lnw_ref[...], lnb_ref[...])
        gate = jax.nn.sigmoid(jnp.dot(x2.astype(bf16), wg_ref[...], preferred_element_type=f32))
        o_ref[tok, :] = x2 + gate * e

    wait_tile(ybuf_a, sem_a)
    finish_tile(0, ybuf_a, dest_ref, tm, ybuf_b, sem_b)
    wait_tile(ybuf_b, sem_b)
    finish_tile(tm, ybuf_b, dest_next_ref, 0, ybuf_a, sem_a)

    @pl.when(s == pl.num_programs(0) - 1)
    def _():
        wait_tile(ybuf_a, sem_a)


def _combine(y, dest, gates, x1, p2d, ln_w, ln_b, wp_bf, ple_norm_w, wg_bf):
    t, d = x1.shape
    tm = COMBINE_ROWS
    n_steps = t // (2 * tm)
    row = lambda c: pl.BlockSpec((2 * tm, c), lambda i: (i, 0))
    full = lambda r, c: pl.BlockSpec((r, c), lambda i: (0, 0))
    table = lambda nxt: pl.BlockSpec((2 * tm * TOP_K,), lambda i: (jnp.minimum(i + nxt, n_steps - 1),),
                                     memory_space=pltpu.SMEM)
    dest_flat = dest.reshape(-1)
    return pl.pallas_call(
        _combine_kernel,
        grid=(n_steps,),
        in_specs=[table(0), table(1), pl.BlockSpec(memory_space=pl.ANY),
                  row(TOP_K), row(d), row(PLE_DIM), full(1, d), full(1, d), full(PLE_DIM, d), full(1, d),
                  full(d, d)],
        out_specs=row(d),
        out_shape=jax.ShapeDtypeStruct((t, d), f32),
        scratch_shapes=[pltpu.VMEM((TOP_K, tm * PITCH, LANES), u32), pltpu.VMEM((TOP_K, tm * PITCH, LANES), u32),
                        pltpu.VMEM((tm, d), f32), pltpu.SemaphoreType.DMA(()), pltpu.SemaphoreType.DMA(())],
        compiler_params=_cparams(("arbitrary",)),
        name="combine_ln2_ple",
    )(dest_flat, dest_flat, y, gates, x1, p2d, ln_w.reshape(1, d), ln_b.reshape(1, d), wp_bf,
      ple_norm_w.reshape(1, d), wg_bf)


def _layer(h, p_i, w_in, ret_gn_w, conv_w, conv_b, lru_wa, lru_ba, lru_wx, lru_bx, lru_lam, w_out,
           ln1_w, ln1_b, w_router, b_router, w_gate, b_gate, w_up, b_up, w_down, b_down,
           ln2_w, ln2_b, w_ple_proj, ple_norm_w, w_ple_gate):
    b, s, d = h.shape
    t = b * s
    x2d = h.reshape(t, d)
    proj = _in_proj(x2d, w_in).reshape(b, s, IN_COLS)
    ret_out = _retention(proj, ret_gn_w)
    lru_out = _lru(proj, conv_w, conv_b, lru_wa, lru_ba, lru_wx, lru_bx, lru_lam)
    x1, top_e, gates = _out_router(ret_out.reshape(t, RET_WIDTH), lru_out.reshape(t, LRU_WIDTH),
                                   w_out.astype(bf16), x2d, ln1_w, ln1_b, w_router, b_router)
    dest, pad_fill, ue, u_row, u_chunks, n_used_rows, n_pad, max_units = _routing_tables(top_e, t)
    xs = _dispatch(x1, dest, pad_fill, n_pad)
    y = _moe_experts(xs, ue, u_row, u_chunks, n_used_rows, n_pad, max_units, w_gate, b_gate, w_up, b_up, w_down,
                     b_down)
    out = _combine(y, dest, gates, x1, p_i.reshape(t, PLE_DIM), ln2_w, ln2_b, w_ple_proj.astype(bf16),
                   ple_norm_w, w_ple_gate.astype(bf16))
    return out.reshape(b, s, d)


def kernel(x, p, w_in, ret_gn_w, conv_w, conv_b, lru_wa, lru_ba, lru_wx, lru_bx, lru_lam, w_out, ln1_w, ln1_b,
           w_router, b_router, w_gate, b_gate, w_up, b_up, w_down, b_down, ln2_w, ln2_b, w_ple_proj, ple_norm_w,
           w_ple_gate):
    h = x.astype(f32)
    for i in range(w_in.shape[0]):
        h = _layer(h, p[i], w_in[i], ret_gn_w[i], conv_w[i], conv_b[i], lru_wa[i], lru_ba[i], lru_wx[i],
                   lru_bx[i], lru_lam[i], w_out[i], ln1_w[i], ln1_b[i], w_router[i], b_router[i], w_gate[i],
                   b_gate[i], w_up[i], b_up[i], w_down[i], b_down[i], ln2_w[i], ln2_b[i], w_ple_proj[i],
                   ple_norm_w[i], w_ple_gate[i])
    return h.astype(x.dtype)
```

```python
import functools
import math

import jax
import jax.numpy as jnp
from jax import lax
from jax.experimental import pallas as pl
from jax.experimental.pallas import tpu as pltpu

D_MODEL = 2048
RET_HEAD_DIM = 128
RET_HEADS = 8
RET_WIDTH = RET_HEADS * RET_HEAD_DIM
LRU_WIDTH = D_MODEL - RET_WIDTH
LRU_BLOCKS = 8
LRU_BLOCK_DIM = LRU_WIDTH // LRU_BLOCKS
IN_COLS = 4 * RET_WIDTH + 2 * LRU_WIDTH
CONV_WIDTH = 4
LRU_C = 8.0
CHUNK = 128
ROPE_BASE = 10000.0
N_EXPERTS = 32
TOP_K = 4
SWIGLU_LIMIT = 7.0
SWIGLU_ALPHA = 1.702
PLE_DIM = 256
LN_EPS = 1e-5
DEPTH = 1
DN_ALPHA = (2.0 * DEPTH) ** 0.25

LANES = 128
SUBLANES = 8
VMEM_LIMIT = 60 * 1024 * 1024

ROW_CHUNK = 256
UNIT_ROWS = 2048
F_TILE = 256

f32 = jnp.float32
bf16 = jnp.bfloat16


def _cparams(sem):
    return pltpu.CompilerParams(dimension_semantics=sem, vmem_limit_bytes=VMEM_LIMIT)


def _in_proj_kernel(x_ref, w_ref, o_ref):
    o_ref[...] = lax.dot_general(x_ref[...].astype(bf16), w_ref[...], (((1,), (0,)), ((), ())),
                                 preferred_element_type=f32)


def _in_proj(x2d, w):
    t, d = x2d.shape
    n = w.shape[1]
    tm, tn = 512, 2048
    return pl.pallas_call(
        _in_proj_kernel,
        grid=(n // tn, t // tm),
        in_specs=[pl.BlockSpec((tm, d), lambda j, i: (i, 0)),
                  pl.BlockSpec((d, tn), lambda j, i: (0, j))],
        out_specs=pl.BlockSpec((tm, tn), lambda j, i: (i, j)),
        out_shape=jax.ShapeDtypeStruct((t, n), f32),
        compiler_params=_cparams(("parallel", "parallel")),
        name="in_proj",
    )(x2d, w)


def _retention_kernel(q_ref, k_ref, v_ref, g_ref, cos_ref, sin_ref, dec_ref, qd_ref, kd_ref, cd_ref, gnw_ref,
                      o_ref):
    s = q_ref.shape[0]
    n_chunks = s // CHUNK
    decay = dec_ref[...]
    q_dec = qd_ref[...]
    k_dec = kd_ref[...]
    c_dec = cd_ref[0:1, :]
    gnw = gnw_ref[...]
    k_scale = RET_HEAD_DIM ** -0.5

    def rope(xv, cos, sin):
        return xv * cos + pltpu.roll(xv, RET_HEAD_DIM // 2, axis=1) * sin

    def body(n, state):
        sl = pl.ds(pl.multiple_of(n * CHUNK, CHUNK), CHUNK)
        cos = cos_ref[sl, :]
        sin = sin_ref[sl, :]
        q = rope(q_ref[sl, :], cos, sin)
        k = rope(k_ref[sl, :], cos, sin) * k_scale
        vb = v_ref[sl, :].astype(bf16)
        scores = lax.dot_general(q.astype(bf16), k.astype(bf16), (((1,), (1,)), ((), ())),
                                 preferred_element_type=f32) * decay
        intra = jnp.dot(scores.astype(bf16), vb, preferred_element_type=f32)
        cross = jnp.dot((q * q_dec).astype(bf16), state.astype(bf16), preferred_element_type=f32)
        kv = lax.dot_general((k * k_dec).astype(bf16), vb, (((0,), (0,)), ((), ())),
                             preferred_element_type=f32)
        ret = intra + cross
        mu = jnp.mean(ret, axis=-1, keepdims=True)
        cen = ret - mu
        var = jnp.mean(cen * cen, axis=-1, keepdims=True)
        ret = cen * lax.rsqrt(var + LN_EPS) * gnw
        g = g_ref[sl, :]
        o_ref[sl, :] = (g * jax.nn.sigmoid(g) * ret).astype(o_ref.dtype)
        return c_dec * state + kv

    lax.fori_loop(0, n_chunks, body, jnp.zeros((RET_HEAD_DIM, RET_HEAD_DIM), f32), unroll=4)


def _retention_tables(s):
    h, d = RET_HEADS, RET_HEAD_DIM
    inv = ROPE_BASE ** (-jnp.arange(0, d, 2, dtype=f32) / d)
    ang = jnp.arange(s, dtype=f32)[:, None] * inv[None, :]
    cos = jnp.cos(ang)
    sin = jnp.sin(ang)
    cos_t = jnp.concatenate([cos, cos], axis=-1)
    sin_t = jnp.concatenate([-sin, sin], axis=-1)
    log_gamma = jnp.log1p(-jnp.exp2(-5.0 - jnp.arange(h, dtype=f32)))
    idx = jnp.arange(CHUNK, dtype=f32)
    diff = idx[:, None] - idx[None, :]
    decay = jnp.where((diff >= 0)[None], jnp.exp(jnp.maximum(diff, 0.0)[None] * log_gamma[:, None, None]), 0.0)
    q_dec = jnp.exp((idx[None, :] + 1.0) * log_gamma[:, None])
    k_dec = jnp.exp((CHUNK - 1.0 - idx)[None, :] * log_gamma[:, None])
    c_dec = jnp.exp(CHUNK * log_gamma)
    q_dec = jnp.broadcast_to(q_dec[:, :, None], (h, CHUNK, d))
    k_dec = jnp.broadcast_to(k_dec[:, :, None], (h, CHUNK, d))
    c_dec = jnp.broadcast_to(c_dec[:, None, None], (h, SUBLANES, d))
    return cos_t, sin_t, decay, q_dec, k_dec, c_dec


def _retention(proj, ret_gn_w):
    b, s, _ = proj.shape
    d = RET_HEAD_DIM
    cos_t, sin_t, decay, q_dec, k_dec, c_dec = _retention_tables(s)
    col = lambda off: pl.BlockSpec((None, s, d), lambda bi, hi: (bi, 0, off + hi))
    per_head = lambda r: pl.BlockSpec((None, r, d), lambda bi, hi: (hi, 0, 0))
    full = lambda shp: pl.BlockSpec(shp, lambda bi, hi: (0,) * len(shp))
    return pl.pallas_call(
        _retention_kernel,
        grid=(b, RET_HEADS),
        in_specs=[col(0), col(RET_HEADS), col(2 * RET_HEADS), col(3 * RET_HEADS),
                  full((s, d)), full((s, d)),
                  per_head(CHUNK), per_head(CHUNK), per_head(CHUNK), per_head(SUBLANES),
                  pl.BlockSpec((1, d), lambda bi, hi: (0, hi))],
        out_specs=pl.BlockSpec((None, s, d), lambda bi, hi: (bi, 0, hi)),
        out_shape=jax.ShapeDtypeStruct((b, s, RET_WIDTH), bf16),
        compiler_params=_cparams(("parallel", "parallel")),
        name="retention",
    )(proj, proj, proj, proj, cos_t, sin_t, decay, q_dec, k_dec, c_dec, ret_gn_w.reshape(1, RET_WIDTH))


def _gelu_tanh(x):
    return 0.5 * x * (1.0 + jnp.tanh(math.sqrt(2.0 / math.pi) * (x + 0.044715 * (x * x * x))))


def _lru_kernel(xr_ref, yg_ref, cw_ref, cb_ref, wa_ref, ba_ref, wx_ref, bx_ref, lam_ref, o_ref, a_ref, b_ref):
    s = xr_ref.shape[0]
    x = xr_ref[...]
    rows = lax.broadcasted_iota(jnp.int32, x.shape, 0)
    xc = cb_ref[...] + cw_ref[CONV_WIDTH - 1:CONV_WIDTH, :] * x
    for back in range(1, CONV_WIDTH):
        shifted = jnp.where(rows >= back, pltpu.roll(x, back, axis=0), 0.0)
        xc = xc + cw_ref[CONV_WIDTH - 1 - back:CONV_WIDTH - back, :] * shifted
    xcb = xc.astype(bf16)
    r = jax.nn.sigmoid(jnp.dot(xcb, wa_ref[...].astype(bf16), preferred_element_type=f32) + ba_ref[...])
    gi = jax.nn.sigmoid(jnp.dot(xcb, wx_ref[...].astype(bf16), preferred_element_type=f32) + bx_ref[...])
    lam = lam_ref[...]
    log_sig = jnp.minimum(lam, 0.0) - jnp.log1p(jnp.exp(-jnp.abs(lam)))
    log_a = LRU_C * r * log_sig
    a = jnp.exp(log_a)
    a_ref[...] = a
    b_ref[...] = jnp.sqrt(-jnp.tanh(log_a) * (a * a + 1.0)) * (gi * xc)

    row8 = lax.broadcasted_iota(jnp.int32, (SUBLANES, LANES), 0)

    def body(i, h_prev):
        sl = pl.ds(pl.multiple_of(i * SUBLANES, SUBLANES), SUBLANES)
        a8 = a_ref[sl, :]
        b8 = b_ref[sl, :]
        for sh in (1, 2, 4):
            a_sh = jnp.where(row8 >= sh, pltpu.roll(a8, sh, axis=0), 1.0)
            b_sh = jnp.where(row8 >= sh, pltpu.roll(b8, sh, axis=0), 0.0)
            b8 = a8 * b_sh + b8
            a8 = a8 * a_sh
        h8 = a8 * h_prev + b8
        o_ref[sl, :] = (_gelu_tanh(yg_ref[sl, :]) * h8).astype(o_ref.dtype)
        return h8[SUBLANES - 1:SUBLANES, :]

    lax.fori_loop(0, s // SUBLANES, body, jnp.zeros((1, LANES), f32), unroll=8)


def _lru(proj, conv_w, conv_b, wa, ba, wx, bx, lam):
    b, s, _ = proj.shape
    d = LRU_BLOCK_DIM
    xr_off = 4 * RET_WIDTH // d
    yg_off = xr_off + LRU_BLOCKS
    col = lambda off: pl.BlockSpec((None, s, d), lambda bi, ji: (bi, 0, off + ji))
    vec = lambda r: pl.BlockSpec((r, d), lambda bi, ji: (0, ji))
    blk = lambda r: pl.BlockSpec((None, r, d), lambda bi, ji: (ji, 0, 0))
    return pl.pallas_call(
        _lru_kernel,
        grid=(b, LRU_BLOCKS),
        in_specs=[col(xr_off), col(yg_off), vec(CONV_WIDTH), vec(1), blk(d), blk(1), blk(d), blk(1), vec(1)],
        out_specs=pl.BlockSpec((None, s, d), lambda bi, ji: (bi, 0, ji)),
        out_shape=jax.ShapeDtypeStruct((b, s, LRU_WIDTH), bf16),
        scratch_shapes=[pltpu.VMEM((s, d), f32), pltpu.VMEM((s, d), f32)],
        compiler_params=_cparams(("parallel", "parallel")),
        name="rg_lru",
    )(proj, proj, conv_w, conv_b.reshape(1, LRU_WIDTH), wa, ba.reshape(LRU_BLOCKS, 1, d), wx,
      bx.reshape(LRU_BLOCKS, 1, d), lam.reshape(1, LRU_WIDTH))


def _layer_norm(y, w, b):
    mu = jnp.mean(y, axis=-1, keepdims=True)
    cen = y - mu
    var = jnp.mean(cen * cen, axis=-1, keepdims=True)
    return cen * lax.rsqrt(var + LN_EPS) * w + b


HALF = D_MODEL // 2
SLAB = HALF // LANES
PITCH = SLAB + 4
u32 = jnp.uint32
HIGH_MASK = 0xFFFF0000


def _bf16_bits(v):
    return lax.bitcast_convert_type(v.astype(bf16).astype(f32), u32)


def _store_slabs(ref, val, before_piece=None):
    n = val.shape[0]
    for j in range(SLAB):
        if before_piece is not None:
            before_piece(j)
        lo = _bf16_bits(val[:, j * LANES:(j + 1) * LANES])
        hi = _bf16_bits(val[:, HALF + j * LANES:HALF + (j + 1) * LANES])
        ref[pl.ds(j, n, stride=PITCH), :] = hi | lax.shift_right_logical(lo, jnp.full_like(lo, 16))
    for j in range(SLAB, PITCH):
        ref[pl.ds(j, n, stride=PITCH), :] = jnp.zeros((n, LANES), u32)


def _slab_cols(ref, j, n):
    w = ref[pl.ds(j, n, stride=PITCH), :]
    lo = lax.bitcast_convert_type(lax.shift_left(w, jnp.full_like(w, 16)), f32)
    hi = lax.bitcast_convert_type(w & jnp.full_like(w, HIGH_MASK), f32)
    return lo, hi


def _split_bf16(v):
    hi = v.astype(bf16)
    lo = (v - hi.astype(f32)).astype(bf16)
    return hi, lo


def _out_router_kernel(ret_ref, lru_ref, wo_ref, x_ref, lnw_ref, lnb_ref, wr_ref, br_ref,
                       x1_ref, tope_ref, gate_ref):
    m = jnp.dot(ret_ref[...], wo_ref[0:RET_WIDTH, :], preferred_element_type=f32)
    m = m + jnp.dot(lru_ref[...], wo_ref[RET_WIDTH:D_MODEL, :], preferred_element_type=f32)
    x1 = _layer_norm(DN_ALPHA * x_ref[...] + m, lnw_ref[...], lnb_ref[...])
    x1_ref[...] = x1
    xh, xl = _split_bf16(x1)
    wh, wl = _split_bf16(wr_ref[...])
    logits = (jnp.dot(xh, wh, preferred_element_type=f32) + jnp.dot(xl, wh, preferred_element_type=f32)
              + jnp.dot(xh, wl, preferred_element_type=f32)) + br_ref[...]
    tm = logits.shape[0]
    lane = lax.broadcasted_iota(jnp.int32, logits.shape, 1)
    lane_k = lax.broadcasted_iota(jnp.int32, (tm, TOP_K), 1)
    top_e = jnp.zeros((tm, TOP_K), jnp.int32)
    top_v = jnp.zeros((tm, TOP_K), f32)
    cur = logits
    for kk in range(TOP_K):
        mx = jnp.max(cur, axis=-1, keepdims=True)
        idx = jnp.min(jnp.where(cur == mx, lane, N_EXPERTS), axis=-1, keepdims=True)
        top_e = jnp.where(lane_k == kk, idx, top_e)
        top_v = jnp.where(lane_k == kk, mx, top_v)
        cur = jnp.where(lane == idx, -jnp.inf, cur)
    ex = jnp.exp(top_v - top_v[:, 0:1])
    gate_ref[...] = ex / jnp.sum(ex, axis=-1, keepdims=True)
    tope_ref[...] = top_e


def _out_router(ret_out, lru_out, wo_bf, x2d, ln_w, ln_b, w_router, b_router):
    t, d = x2d.shape
    tm = 512
    row = lambda c: pl.BlockSpec((tm, c), lambda i: (i, 0))
    full = lambda r, c: pl.BlockSpec((r, c), lambda i: (0, 0))
    return pl.pallas_call(
        _out_router_kernel,
        grid=(t // tm,),
        in_specs=[row(RET_WIDTH), row(LRU_WIDTH), full(d, d), row(d), full(1, d), full(1, d),
                  full(d, N_EXPERTS), full(1, N_EXPERTS)],
        out_specs=[row(d), row(TOP_K), row(TOP_K)],
        out_shape=[jax.ShapeDtypeStruct((t, d), f32), jax.ShapeDtypeStruct((t, TOP_K), jnp.int32),
                   jax.ShapeDtypeStruct((t, TOP_K), f32)],
        compiler_params=_cparams(("parallel",)),
        name="out_proj_ln1_router",
    )(ret_out, lru_out, wo_bf, x2d, ln_w.reshape(1, d), ln_b.reshape(1, d), w_router,
      b_router.reshape(1, N_EXPERTS))


def _routing_tables(top_e, t):
    n_pad = t * TOP_K + N_EXPERTS * ROW_CHUNK
    max_units = N_EXPERTS + (t * TOP_K) // UNIT_ROWS
    sel = (top_e[:, :, None] == jnp.arange(N_EXPERTS, dtype=jnp.int32)[None, None, :]).any(axis=1)
    sel = sel.astype(jnp.int32)
    counts = jnp.sum(sel, axis=0)
    rank = jnp.cumsum(sel, axis=0) - sel
    padded = (counts + ROW_CHUNK - 1) // ROW_CHUNK * ROW_CHUNK
    pad_ends = jnp.cumsum(padded)
    pad_starts = pad_ends - padded
    dest_dense = pad_starts[None, :] + rank
    dest = jnp.take_along_axis(dest_dense, top_e, axis=1)
    pad_fill = jnp.concatenate([pad_starts + counts, padded - counts, pad_ends[-1:]]).astype(jnp.int32)
    units_per_e = (padded + UNIT_ROWS - 1) // UNIT_ROWS
    unit_ends = jnp.cumsum(units_per_e)
    unit_starts = unit_ends - units_per_e
    n_units = unit_ends[-1]
    u = jnp.arange(max_units, dtype=jnp.int32)
    u_clamped = jnp.minimum(u, n_units - 1)
    ue = jnp.searchsorted(unit_ends, u_clamped, side='right').astype(jnp.int32)
    ue = jnp.minimum(ue, N_EXPERTS - 1)
    j = u_clamped - unit_starts[ue]
    u_row = pad_starts[ue] + j * UNIT_ROWS
    u_rows = jnp.minimum(UNIT_ROWS, padded[ue] - j * UNIT_ROWS)
    u_chunks = jnp.where(u < n_units, u_rows // ROW_CHUNK, 0).astype(jnp.int32)
    u_real = jnp.clip(counts[ue] - j * UNIT_ROWS, 0, UNIT_ROWS)
    u_granules = jnp.where(u < n_units, (u_real + COMPUTE_GRANULE - 1) // COMPUTE_GRANULE, 0).astype(jnp.int32)
    n_used_rows = jnp.concatenate([pad_ends[-1:].astype(jnp.int32), u_granules])
    return dest.astype(jnp.int32), pad_fill, ue, u_row.astype(jnp.int32), u_chunks, n_used_rows, n_pad, max_units


CHUNK_PITCHED = ROW_CHUNK * PITCH
DISPATCH_ROWS = 256
FILL_SLOTS = ROW_CHUNK // 2


def _dispatch_kernel(fill_ref, dest_ref, dest_prev_ref, x1_ref, xs_hbm, stage_a, stage_b, zeros, sem_a, sem_b,
                     sem_z):
    tm = DISPATCH_ROWS
    s = pl.program_id(0)
    last = pl.num_programs(0) - 1

    def slot_copy(table, table_off, t, k, stage, sem):
        dst = xs_hbm.at[pl.ds(table[(table_off + t) * TOP_K + k] * PITCH, PITCH), :]
        return pltpu.make_async_copy(stage.at[pl.ds(t * PITCH, PITCH), :], dst, sem)

    def wait_tile(stage, sem):
        for _ in range(TOP_K):
            pltpu.make_async_copy(stage, xs_hbm.at[pl.ds(0, tm * PITCH), :], sem).wait()

    def start_copies(table, table_off, stage, sem):
        per_piece = tm // SLAB

        def before_piece(j):
            for t in range(j * per_piece, (j + 1) * per_piece):
                for k in range(TOP_K):
                    slot_copy(table, table_off, t, k, stage, sem).start(priority=k % 2)

        return before_piece

    def pack(row0, stage, before_piece=None):
        _store_slabs(stage, x1_ref[pl.ds(row0, tm), :], before_piece)

    @pl.when(s == 0)
    def _():
        pack(0, stage_a)

    @pl.when(s > 0)
    def _():
        wait_tile(stage_a, sem_a)
        pack(0, stage_a, start_copies(dest_prev_ref, tm, stage_b, sem_b))
        wait_tile(stage_b, sem_b)

    pack(tm, stage_b, start_copies(dest_ref, 0, stage_a, sem_a))

    @pl.when(s == last)
    def _():
        def issue(t, c):
            for k in range(TOP_K):
                slot_copy(dest_ref, tm, t, k, stage_b, sem_b).start(priority=k % 2)
            return c

        lax.fori_loop(0, tm, issue, 0, unroll=2)
        wait_tile(stage_a, sem_a)
        wait_tile(stage_b, sem_b)

        zeros[...] = jnp.zeros(zeros.shape, u32)

        def fill(slot, n_slots):
            return pltpu.make_async_copy(zeros.at[pl.ds(0, n_slots * PITCH), :],
                                         xs_hbm.at[pl.ds(slot * PITCH, n_slots * PITCH), :], sem_z)

        def expert_pad(e, start):
            slot = fill_ref[e]
            n = fill_ref[N_EXPERTS + e]
            piece = FILL_SLOTS
            while piece >= 1:
                @pl.when((n // piece) % 2 == 1)
                def _(slot=slot, piece=piece):
                    if start:
                        fill(slot, piece).start()
                    else:
                        fill(slot, piece).wait()

                slot = slot + ((n // piece) % 2) * piece
                piece //= 2

        def tail(i, start):
            slot = fill_ref[2 * N_EXPERTS] + i * FILL_SLOTS
            if start:
                fill(slot, FILL_SLOTS).start()
            else:
                fill(slot, FILL_SLOTS).wait()

        n_tail = (xs_hbm.shape[0] // PITCH - fill_ref[2 * N_EXPERTS]) // FILL_SLOTS
        for start in (True, False):
            lax.fori_loop(0, N_EXPERTS, lambda e, c, start=start: (expert_pad(e, start), c)[1], 0)
            lax.fori_loop(0, n_tail, lambda i, c, start=start: (tail(i, start), c)[1], 0)


def _dispatch(x1, dest, pad_fill, n_pad):
    t, d = x1.shape
    tm = DISPATCH_ROWS
    n_steps = t // (2 * tm)
    table = lambda back: pl.BlockSpec((2 * tm * TOP_K,), lambda i, fill: (jnp.maximum(i - back, 0),),
                                      memory_space=pltpu.SMEM)
    dest_flat = dest.reshape(-1)
    stage = pltpu.VMEM((tm * PITCH, LANES), u32)
    return pl.pallas_call(
        _dispatch_kernel,
        grid_spec=pltpu.PrefetchScalarGridSpec(
            num_scalar_prefetch=1,
            grid=(n_steps,),
            in_specs=[table(0), table(1), pl.BlockSpec((2 * tm, d), lambda i, fill: (i, 0))],
            out_specs=pl.BlockSpec(memory_space=pl.ANY),
            scratch_shapes=[stage, stage, pltpu.VMEM((FILL_SLOTS * PITCH, LANES), u32),
                            pltpu.SemaphoreType.DMA(()), pltpu.SemaphoreType.DMA(()), pltpu.SemaphoreType.DMA(())],
        ),
        out_shape=jax.ShapeDtypeStruct((n_pad * PITCH, LANES), u32),
        compiler_params=_cparams(("arbitrary",)),
        name="moe_dispatch",
    )(pad_fill, dest_flat, dest_flat, x1)


OUT_SLOTS = 4
COMPUTE_GRANULE = ROW_CHUNK // 2


def _moe_kernel(ue_ref, urow_ref, uchunks_ref, used_ref, xs_hbm, wg_ref, bg_ref, wu_ref, bu_ref, wd_ref,
                bd_ref, y_hbm, xbuf, yacc, gstage_a, gstage_b, ostage, flags, in_sem, out_sem):
    u = pl.program_id(0)
    f = pl.program_id(1)
    n_u = pl.num_programs(0)
    n_f = pl.num_programs(1)
    n_chunks = uchunks_ref[u]
    row0 = urow_ref[u]
    cur = u % 2
    nxt = 1 - cur
    u_next = jnp.minimum(u + 1, n_u - 1)
    n_next = jnp.where(u + 1 < n_u, uchunks_ref[u_next], 0)
    row_next = urow_ref[u_next]
    first_step = jnp.logical_and(u == 0, f == 0)
    PENDING, PEND_CHUNK, PEND_SLOT, OUT_BUSY = 0, 1, 2, 3

    def rows(c, size=ROW_CHUNK):
        return pl.ds(pl.multiple_of(c * size, size), size)

    stages = (gstage_a, gstage_b)

    def chunk_copy(unit_row, c, parity):
        src = xs_hbm.at[pl.ds(pl.multiple_of((unit_row + c * ROW_CHUNK) * PITCH, CHUNK_PITCHED), CHUNK_PITCHED), :]
        return pltpu.make_async_copy(src, stages[parity], in_sem.at[parity])

    def unpack(c, xslot, parity):
        for j in range(SLAB):
            lo, hi = _slab_cols(stages[parity], j, ROW_CHUNK)
            xbuf[xslot, rows(c), j * LANES:(j + 1) * LANES] = lo.astype(bf16)
            xbuf[xslot, rows(c), HALF + j * LANES:HALF + (j + 1) * LANES] = hi.astype(bf16)

    def out_copy(c, slot):
        dst = y_hbm.at[pl.ds(pl.multiple_of((row0 + c * ROW_CHUNK) * PITCH, CHUNK_PITCHED), CHUNK_PITCHED), :]
        return pltpu.make_async_copy(ostage.at[slot], dst, out_sem.at[slot])

    def out_wait(slot):
        @pl.when(flags[OUT_BUSY + slot] == 1)
        def _():
            pltpu.make_async_copy(ostage.at[slot], y_hbm.at[pl.ds(0, CHUNK_PITCHED), :], out_sem.at[slot]).wait()
            flags[OUT_BUSY + slot] = 0

    @pl.when(first_step)
    def _():
        for i in range(OUT_BUSY + OUT_SLOTS):
            flags[i] = 0

        def clear(c, carry):
            yacc[rows(c), :] = jnp.zeros((ROW_CHUNK, D_MODEL), f32)
            return carry

        lax.fori_loop(0, UNIT_ROWS // ROW_CHUNK, clear, 0)

        def load(c, carry):
            first = chunk_copy(row0, c, 0)
            first.start()
            first.wait()
            unpack(c, cur, 0)
            return carry

        lax.fori_loop(0, n_chunks, load, 0)

    prefetching = f < n_next
    pending = flags[PENDING] == 1
    done_chunk = flags[PEND_CHUNK]
    done_slot = flags[PEND_SLOT]
    same_buffer = jnp.logical_and(pending, done_chunk % 2 == f % 2)
    for late in (False, True):
        for parity in (0, 1):
            starts_now = jnp.logical_and(prefetching, same_buffer if late else jnp.logical_not(same_buffer))

            @pl.when(jnp.logical_and(starts_now, f % 2 == parity))
            def _():
                chunk_copy(row_next, f, parity).start()

        if not late:
            for parity in (0, 1):
                @pl.when(jnp.logical_and(pending, done_chunk % 2 == parity))
                def _():
                    chunk_copy(row_next, done_chunk, parity).wait()
                    unpack(done_chunk, done_slot, parity)

    flags[PENDING] = prefetching.astype(jnp.int32)

    @pl.when(prefetching)
    def _():
        flags[PEND_CHUNK] = f
        flags[PEND_SLOT] = nxt

    n_granules = used_ref[1 + u]
    block_sizes = (8, 4, 2, 1)

    @pl.when(n_chunks > 0)
    def _():
        bg = bg_ref[...]
        bu = bu_ref[...]
        mm = lambda a, w: lax.dot_general(a, w, (((1,), (0,)), ((), ())), preferred_element_type=f32)

        def mlp(sl, first):
            xc = xbuf[cur, sl, :]
            gt = jnp.minimum(mm(xc, wg_ref[...]) + bg, SWIGLU_LIMIT)
            up = jnp.clip(mm(xc, wu_ref[...]) + bu, -SWIGLU_LIMIT, SWIGLU_LIMIT)
            hid = (up + 1.0) * gt * jax.nn.sigmoid(SWIGLU_ALPHA * gt)
            part = mm(hid.astype(bf16), wd_ref[...])
            if first:
                yacc[sl, :] = part
            else:
                yacc[sl, :] += part

        def blocks(first):
            big = block_sizes[0] * COMPUTE_GRANULE
            n_big = n_granules // block_sizes[0]

            def big_block(c, carry):
                mlp(rows(c, big), first)
                return carry

            lax.fori_loop(0, n_big, big_block, 0)
            done = n_big * block_sizes[0]
            for size in block_sizes[1:]:
                present = (n_granules // size) % 2
                m = size * COMPUTE_GRANULE

                @pl.when(present == 1)
                def _(done=done, m=m):
                    mlp(pl.ds(pl.multiple_of(done * COMPUTE_GRANULE, m), m), first)

                done = done + present * size

        @pl.when(f == 0)
        def _():
            blocks(True)

        @pl.when(f > 0)
        def _():
            blocks(False)

    @pl.when(jnp.logical_and(f == n_f - 1, n_chunks > 0))
    def _():
        bd = bd_ref[...]

        def store(c, carry):
            slot = c % OUT_SLOTS
            out_wait(slot)
            _store_slabs(ostage.at[slot], yacc[rows(c), :] + bd)
            out_copy(c, slot).start()
            flags[OUT_BUSY + slot] = 1
            return carry

        lax.fori_loop(0, n_chunks, store, 0)

    @pl.when(jnp.logical_and(u == n_u - 1, f == n_f - 1))
    def _():
        for s in range(OUT_SLOTS):
            out_wait(s)
        first = used_ref[0] // ROW_CHUNK
        last = y_hbm.shape[0] // CHUNK_PITCHED
        ostage[0] = jnp.zeros((CHUNK_PITCHED, LANES), u32)

        def tail_copy(c):
            dst = y_hbm.at[pl.ds(pl.multiple_of(c * CHUNK_PITCHED, CHUNK_PITCHED), CHUNK_PITCHED), :]
            return pltpu.make_async_copy(ostage.at[0], dst, out_sem.at[0])

        def start(c, carry):
            tail_copy(c).start()
            return carry

        def wait(c, carry):
            tail_copy(c).wait()
            return carry

        lax.fori_loop(first, last, start, 0)
        lax.fori_loop(first, last, wait, 0)


def _moe_experts(xs, ue, u_row, u_chunks, n_used_rows, n_pad, max_units, w_gate, b_gate, w_up, b_up, w_down,
                 b_down):
    e, d, dff = w_gate.shape
    n_f = dff // F_TILE
    assert n_f >= UNIT_ROWS // ROW_CHUNK, "one chunk of the next unit is loaded per f-tile step"
    f_idx = lambda u, f, uc: jnp.where(uc[u] > 0, f, n_f - 1)
    col_w = pl.BlockSpec((None, d, F_TILE), lambda u, f, ue, ur, uc, used: (ue[u], 0, f_idx(u, f, uc)))
    col_b = pl.BlockSpec((None, 1, F_TILE), lambda u, f, ue, ur, uc, used: (ue[u], 0, f_idx(u, f, uc)))
    return pl.pallas_call(
        _moe_kernel,
        grid_spec=pltpu.PrefetchScalarGridSpec(
            num_scalar_prefetch=4,
            grid=(max_units, n_f),
            in_specs=[pl.BlockSpec(memory_space=pl.ANY),
                      col_w, col_b, col_w, col_b,
                      pl.BlockSpec((None, F_TILE, d), lambda u, f, ue, ur, uc, used: (ue[u], f_idx(u, f, uc), 0)),
                      pl.BlockSpec((None, 1, d), lambda u, f, ue, ur, uc, used: (ue[u], 0, 0))],
            out_specs=pl.BlockSpec(memory_space=pl.ANY),
            scratch_shapes=[pltpu.VMEM((2, UNIT_ROWS, d), bf16),
                            pltpu.VMEM((UNIT_ROWS, d), f32),
                            pltpu.VMEM((CHUNK_PITCHED, LANES), u32),
                            pltpu.VMEM((CHUNK_PITCHED, LANES), u32),
                            pltpu.VMEM((OUT_SLOTS, CHUNK_PITCHED, LANES), u32),
                            pltpu.SMEM((3 + OUT_SLOTS,), jnp.int32),
                            pltpu.SemaphoreType.DMA((2,)),
                            pltpu.SemaphoreType.DMA((OUT_SLOTS,))],
        ),
        out_shape=jax.ShapeDtypeStruct((n_pad * PITCH, LANES), u32),
        compiler_params=_cparams(("arbitrary", "arbitrary")),
        name="moe_experts",
    )(ue, u_row, u_chunks, n_used_rows, xs, w_gate, b_gate.reshape(e, 1, dff), w_up,
      b_up.reshape(e, 1, dff), w_down, b_down.reshape(e, 1, d))


COMBINE_ROWS = 256
COMBINE_GROUPS = 2


def _combine_kernel(dest_ref, dest_next_ref, y_hbm, gate_ref, x1_ref, p_ref, lnw_ref, lnb_ref, wp_ref, pnw_ref,
                    wg_ref, o_ref, ybuf_a, ybuf_b, fsum_ref, sem_a, sem_b):
    tm = COMBINE_ROWS
    s = pl.program_id(0)

    def row_copy(table, table_off, t, k, ybuf, sem):
        src = y_hbm.at[pl.ds(table[(table_off + t) * TOP_K + k] * PITCH, SLAB), :]
        return pltpu.make_async_copy(src, ybuf.at[k, pl.ds(t * PITCH, SLAB), :], sem)

    def wait_tile(ybuf, sem):
        for k in range(TOP_K):
            pltpu.make_async_copy(y_hbm.at[pl.ds(0, tm * SLAB), :], ybuf.at[k, pl.ds(0, tm * SLAB), :], sem).wait()

    @pl.when(s == 0)
    def _():
        def issue(t, c):
            for k in range(TOP_K):
                row_copy(dest_ref, 0, t, k, ybuf_a, sem_a).start(priority=k % 2)
            return c

        lax.fori_loop(0, tm, issue, 0, unroll=2)

    def finish_tile(row0, ybuf, next_table, next_off, next_buf, next_sem):
        tok = pl.ds(row0, tm)
        pieces = SLAB * TOP_K
        per_piece = tm // pieces
        e = jnp.dot(p_ref[tok, :].astype(bf16), wp_ref[...], preferred_element_type=f32)
        e = e * lax.rsqrt(jnp.mean(e * e, axis=-1, keepdims=True) + LN_EPS) * pnw_ref[...]
        gates = gate_ref[tok, :]
        for j in range(SLAB):
            acc_lo, acc_hi = None, None
            for k in range(TOP_K):
                piece = j * TOP_K + k
                for t in range(piece * per_piece, (piece + 1) * per_piece):
                    for kk in range(TOP_K):
                        row_copy(next_table, next_off, t, kk, next_buf, next_sem).start(priority=kk % 2)
                lo, hi = _slab_cols(ybuf.at[k], j, tm)
                g = gates[:, k:k + 1]
                acc_lo = g * lo if k == 0 else acc_lo + g * lo
                acc_hi = g * hi if k == 0 else acc_hi + g * hi
            fsum_ref[:, j * LANES:(j + 1) * LANES] = acc_lo
            fsum_ref[:, HALF + j * LANES:HALF + (j + 1) * LANES] = acc_hi
        x2 = _layer_norm(DN_ALPHA * x1_ref[tok, :] + fsum_ref[...], lnw_ref[...], lnb_ref[...])
        gate = jax.nn.sigmoid(jnp.dot(x2.astype(bf16), wg_ref[...], preferred_element_type=f32))
        o_ref[tok, :] = x2 + gate * e

    wait_tile(ybuf_a, sem_a)
    finish_tile(0, ybuf_a, dest_ref, tm, ybuf_b, sem_b)
    wait_tile(ybuf_b, sem_b)
    finish_tile(tm, ybuf_b, dest_next_ref, 0, ybuf_a, sem_a)

    @pl.when(s == pl.num_programs(0) - 1)
    def _():
        wait_tile(ybuf_a, sem_a)


def _combine(y, dest, gates, x1, p2d, ln_w, ln_b, wp_bf, ple_norm_w, wg_bf):
    t, d = x1.shape
    tm = COMBINE_ROWS
    n_steps = t // (2 * tm)
    row = lambda c: pl.BlockSpec((2 * tm, c), lambda i: (i, 0))
    full = lambda r, c: pl.BlockSpec((r, c), lambda i: (0, 0))
    table = lambda nxt: pl.BlockSpec((2 * tm * TOP_K,), lambda i: (jnp.minimum(i + nxt, n_steps - 1),),
                                     memory_space=pltpu.SMEM)
    dest_flat = dest.reshape(-1)
    return pl.pallas_call(
        _combine_kernel,
        grid=(n_steps,),
        in_specs=[table(0), table(1), pl.BlockSpec(memory_space=pl.ANY),
                  row(TOP_K), row(d), row(PLE_DIM), full(1, d), full(1, d), full(PLE_DIM, d), full(1, d),
                  full(d, d)],
        out_specs=row(d),
        out_shape=jax.ShapeDtypeStruct((t, d), f32),
        scratch_shapes=[pltpu.VMEM((TOP_K, tm * PITCH, LANES), u32), pltpu.VMEM((TOP_K, tm * PITCH, LANES), u32),
                        pltpu.VMEM((tm, d), f32), pltpu.SemaphoreType.DMA(()), pltpu.SemaphoreType.DMA(())],
        compiler_params=_cparams(("arbitrary",)),
        name="combine_ln2_ple",
    )(dest_flat, dest_flat, y, gates, x1, p2d, ln_w.reshape(1, d), ln_b.reshape(1, d), wp_bf,
      ple_norm_w.reshape(1, d), wg_bf)


def _layer(h, p_i, w_in, ret_gn_w, conv_w, conv_b, lru_wa, lru_ba, lru_wx, lru_bx, lru_lam, w_out,
           ln1_w, ln1_b, w_router, b_router, w_gate, b_gate, w_up, b_up, w_down, b_down,
           ln2_w, ln2_b, w_ple_proj, ple_norm_w, w_ple_gate):
    b, s, d = h.shape
    t = b * s
    x2d = h.reshape(t, d)
    proj = _in_proj(x2d, w_in).reshape(b, s, IN_COLS)
    ret_out = _retention(proj, ret_gn_w)
    lru_out = _lru(proj, conv_w, conv_b, lru_wa, lru_ba, lru_wx, lru_bx, lru_lam)
    x1, top_e, gates = _out_router(ret_out.reshape(t, RET_WIDTH), lru_out.reshape(t, LRU_WIDTH),
                                   w_out.astype(bf16), x2d, ln1_w, ln1_b, w_router, b_router)
    dest, pad_fill, ue, u_row, u_chunks, n_used_rows, n_pad, max_units = _routing_tables(top_e, t)
    xs = _dispatch(x1, dest, pad_fill, n_pad)
    y = _moe_experts(xs, ue, u_row, u_chunks, n_used_rows, n_pad, max_units, w_gate, b_gate, w_up, b_up, w_down,
                     b_down)
    out = _combine(y, dest, gates, x1, p_i.reshape(t, PLE_DIM), ln2_w, ln2_b, w_ple_proj.astype(bf16),
                   ple_norm_w, w_ple_gate.astype(bf16))
    return out.reshape(b, s, d)


def kernel(x, p, w_in, ret_gn_w, conv_w, conv_b, lru_wa, lru_ba, lru_wx, lru_bx, lru_lam, w_out, ln1_w, ln1_b,
           w_router, b_router, w_gate, b_gate, w_up, b_up, w_down, b_down, ln2_w, ln2_b, w_ple_proj, ple_norm_w,
           w_ple_gate):
    h = x.astype(f32)
    for i in range(w_in.shape[0]):
        h = _layer(h, p[i], w_in[i], ret_gn_w[i], conv_w[i], conv_b[i], lru_wa[i], lru_ba[i], lru_wx[i],
                   lru_bx[i], lru_lam[i], w_out[i], ln1_w[i], ln1_b[i], w_router[i], b_router[i], w_gate[i],
                   b_gate[i], w_up[i], b_up[i], w_down[i], b_down[i], ln2_w[i], ln2_b[i], w_ple_proj[i],
                   ple_norm_w[i], w_ple_gate[i])
    return h.astype(x.dtype)
```

```python
import functools
import math

import jax
import jax.numpy as jnp
from jax import lax
from jax.experimental import pallas as pl
from jax.experimental.pallas import tpu as pltpu

D_MODEL = 2048
RET_HEAD_DIM = 128
RET_HEADS = 8
RET_WIDTH = RET_HEADS * RET_HEAD_DIM
LRU_WIDTH = D_MODEL - RET_WIDTH
LRU_BLOCKS = 8
LRU_BLOCK_DIM = LRU_WIDTH // LRU_BLOCKS
IN_COLS = 4 * RET_WIDTH + 2 * LRU_WIDTH
CONV_WIDTH = 4
LRU_C = 8.0
CHUNK = 128
ROPE_BASE = 10000.0
N_EXPERTS = 32
TOP_K = 4
SWIGLU_LIMIT = 7.0
SWIGLU_ALPHA = 1.702
PLE_DIM = 256
LN_EPS = 1e-5
DEPTH = 1
DN_ALPHA = (2.0 * DEPTH) ** 0.25

LANES = 128
SUBLANES = 8
VMEM_LIMIT = 60 * 1024 * 1024

ROW_CHUNK = 256
UNIT_ROWS = 2048
F_TILE = 256

f32 = jnp.float32
bf16 = jnp.bfloat16


def _cparams(sem):
    return pltpu.CompilerParams(dimension_semantics=sem, vmem_limit_bytes=VMEM_LIMIT)


def _in_proj_kernel(x_ref, w_ref, o_ref):
    o_ref[...] = lax.dot_general(x_ref[...].astype(bf16), w_ref[...], (((1,), (0,)), ((), ())),
                                 preferred_element_type=f32)


def _in_proj(x2d, w):
    t, d = x2d.shape
    n = w.shape[1]
    tm, tn = 512, 2048
    return pl.pallas_call(
        _in_proj_kernel,
        grid=(n // tn, t // tm),
        in_specs=[pl.BlockSpec((tm, d), lambda j, i: (i, 0)),
                  pl.BlockSpec((d, tn), lambda j, i: (0, j))],
        out_specs=pl.BlockSpec((tm, tn), lambda j, i: (i, j)),
        out_shape=jax.ShapeDtypeStruct((t, n), f32),
        compiler_params=_cparams(("parallel", "parallel")),
        name="in_proj",
    )(x2d, w)


def _retention_kernel(q_ref, k_ref, v_ref, g_ref, cos_ref, sin_ref, dec_ref, qd_ref, kd_ref, cd_ref, gnw_ref,
                      o_ref):
    s = q_ref.shape[0]
    n_chunks = s // CHUNK
    decay = dec_ref[...]
    q_dec = qd_ref[...]
    k_dec = kd_ref[...]
    c_dec = cd_ref[0:1, :]
    gnw = gnw_ref[...]
    k_scale = RET_HEAD_DIM ** -0.5

    def rope(xv, cos, sin):
        return xv * cos + pltpu.roll(xv, RET_HEAD_DIM // 2, axis=1) * sin

    def body(n, state):
        sl = pl.ds(pl.multiple_of(n * CHUNK, CHUNK), CHUNK)
        cos = cos_ref[sl, :]
        sin = sin_ref[sl, :]
        q = rope(q_ref[sl, :], cos, sin)
        k = rope(k_ref[sl, :], cos, sin) * k_scale
        vb = v_ref[sl, :].astype(bf16)
        scores = lax.dot_general(q.astype(bf16), k.astype(bf16), (((1,), (1,)), ((), ())),
                                 preferred_element_type=f32) * decay
        intra = jnp.dot(scores.astype(bf16), vb, preferred_element_type=f32)
        cross = jnp.dot((q * q_dec).astype(bf16), state.astype(bf16), preferred_element_type=f32)
        kv = lax.dot_general((k * k_dec).astype(bf16), vb, (((0,), (0,)), ((), ())),
                             preferred_element_type=f32)
        ret = intra + cross
        mu = jnp.mean(ret, axis=-1, keepdims=True)
        cen = ret - mu
        var = jnp.mean(cen * cen, axis=-1, keepdims=True)
        ret = cen * lax.rsqrt(var + LN_EPS) * gnw
        g = g_ref[sl, :]
        o_ref[sl, :] = (g * jax.nn.sigmoid(g) * ret).astype(o_ref.dtype)
        return c_dec * state + kv

    lax.fori_loop(0, n_chunks, body, jnp.zeros((RET_HEAD_DIM, RET_HEAD_DIM), f32), unroll=8)


def _retention_tables(s):
    h, d = RET_HEADS, RET_HEAD_DIM
    inv = ROPE_BASE ** (-jnp.arange(0, d, 2, dtype=f32) / d)
    ang = jnp.arange(s, dtype=f32)[:, None] * inv[None, :]
    cos = jnp.cos(ang)
    sin = jnp.sin(ang)
    cos_t = jnp.concatenate([cos, cos], axis=-1)
    sin_t = jnp.concatenate([-sin, sin], axis=-1)
    log_gamma = jnp.log1p(-jnp.exp2(-5.0 - jnp.arange(h, dtype=f32)))
    idx = jnp.arange(CHUNK, dtype=f32)
    diff = idx[:, None] - idx[None, :]
    decay = jnp.where((diff >= 0)[None], jnp.exp(jnp.maximum(diff, 0.0)[None] * log_gamma[:, None, None]), 0.0)
    q_dec = jnp.exp((idx[None, :] + 1.0) * log_gamma[:, None])
    k_dec = jnp.exp((CHUNK - 1.0 - idx)[None, :] * log_gamma[:, None])
    c_dec = jnp.exp(CHUNK * log_gamma)
    q_dec = jnp.broadcast_to(q_dec[:, :, None], (h, CHUNK, d))
    k_dec = jnp.broadcast_to(k_dec[:, :, None], (h, CHUNK, d))
    c_dec = jnp.broadcast_to(c_dec[:, None, None], (h, SUBLANES, d))
    return cos_t, sin_t, decay, q_dec, k_dec, c_dec


def _retention(proj, ret_gn_w):
    b, s, _ = proj.shape
    d = RET_HEAD_DIM
    cos_t, sin_t, decay, q_dec, k_dec, c_dec = _retention_tables(s)
    col = lambda off: pl.BlockSpec((None, s, d), lambda bi, hi: (bi, 0, off + hi))
    per_head = lambda r: pl.BlockSpec((None, r, d), lambda bi, hi: (hi, 0, 0))
    full = lambda shp: pl.BlockSpec(shp, lambda bi, hi: (0,) * len(shp))
    return pl.pallas_call(
        _retention_kernel,
        grid=(b, RET_HEADS),
        in_specs=[col(0), col(RET_HEADS), col(2 * RET_HEADS), col(3 * RET_HEADS),
                  full((s, d)), full((s, d)),
                  per_head(CHUNK), per_head(CHUNK), per_head(CHUNK), per_head(SUBLANES),
                  pl.BlockSpec((1, d), lambda bi, hi: (0, hi))],
        out_specs=pl.BlockSpec((None, s, d), lambda bi, hi: (bi, 0, hi)),
        out_shape=jax.ShapeDtypeStruct((b, s, RET_WIDTH), bf16),
        compiler_params=_cparams(("parallel", "parallel")),
        name="retention",
    )(proj, proj, proj, proj, cos_t, sin_t, decay, q_dec, k_dec, c_dec, ret_gn_w.reshape(1, RET_WIDTH))


def _gelu_tanh(x):
    return 0.5 * x * (1.0 + jnp.tanh(math.sqrt(2.0 / math.pi) * (x + 0.044715 * (x * x * x))))


def _lru_kernel(xr_ref, yg_ref, cw_ref, cb_ref, wa_ref, ba_ref, wx_ref, bx_ref, lam_ref, o_ref, a_ref, b_ref):
    s = xr_ref.shape[0]
    x = xr_ref[...]
    rows = lax.broadcasted_iota(jnp.int32, x.shape, 0)
    xc = cb_ref[...] + cw_ref[CONV_WIDTH - 1:CONV_WIDTH, :] * x
    for back in range(1, CONV_WIDTH):
        shifted = jnp.where(rows >= back, pltpu.roll(x, back, axis=0), 0.0)
        xc = xc + cw_ref[CONV_WIDTH - 1 - back:CONV_WIDTH - back, :] * shifted
    xcb = xc.astype(bf16)
    r = jax.nn.sigmoid(jnp.dot(xcb, wa_ref[...].astype(bf16), preferred_element_type=f32) + ba_ref[...])
    gi = jax.nn.sigmoid(jnp.dot(xcb, wx_ref[...].astype(bf16), preferred_element_type=f32) + bx_ref[...])
    lam = lam_ref[...]
    log_sig = jnp.minimum(lam, 0.0) - jnp.log1p(jnp.exp(-jnp.abs(lam)))
    log_a = LRU_C * r * log_sig
    a = jnp.exp(log_a)
    a_ref[...] = a
    b_ref[...] = jnp.sqrt(-jnp.tanh(log_a) * (a * a + 1.0)) * (gi * xc)

    row8 = lax.broadcasted_iota(jnp.int32, (SUBLANES, LANES), 0)

    def body(i, h_prev):
        sl = pl.ds(pl.multiple_of(i * SUBLANES, SUBLANES), SUBLANES)
        a8 = a_ref[sl, :]
        b8 = b_ref[sl, :]
        for sh in (1, 2, 4):
            a_sh = jnp.where(row8 >= sh, pltpu.roll(a8, sh, axis=0), 1.0)
            b_sh = jnp.where(row8 >= sh, pltpu.roll(b8, sh, axis=0), 0.0)
            b8 = a8 * b_sh + b8
            a8 = a8 * a_sh
        h8 = a8 * h_prev + b8
        o_ref[sl, :] = (_gelu_tanh(yg_ref[sl, :]) * h8).astype(o_ref.dtype)
        return h8[SUBLANES - 1:SUBLANES, :]

    lax.fori_loop(0, s // SUBLANES, body, jnp.zeros((1, LANES), f32), unroll=16)


def _lru(proj, conv_w, conv_b, wa, ba, wx, bx, lam):
    b, s, _ = proj.shape
    d = LRU_BLOCK_DIM
    xr_off = 4 * RET_WIDTH // d
    yg_off = xr_off + LRU_BLOCKS
    col = lambda off: pl.BlockSpec((None, s, d), lambda bi, ji: (bi, 0, off + ji))
    vec = lambda r: pl.BlockSpec((r, d), lambda bi, ji: (0, ji))
    blk = lambda r: pl.BlockSpec((None, r, d), lambda bi, ji: (ji, 0, 0))
    return pl.pallas_call(
        _lru_kernel,
        grid=(b, LRU_BLOCKS),
        in_specs=[col(xr_off), col(yg_off), vec(CONV_WIDTH), vec(1), blk(d), blk(1), blk(d), blk(1), vec(1)],
        out_specs=pl.BlockSpec((None, s, d), lambda bi, ji: (bi, 0, ji)),
        out_shape=jax.ShapeDtypeStruct((b, s, LRU_WIDTH), bf16),
        scratch_shapes=[pltpu.VMEM((s, d), f32), pltpu.VMEM((s, d), f32)],
        compiler_params=_cparams(("parallel", "parallel")),
        name="rg_lru",
    )(proj, proj, conv_w, conv_b.reshape(1, LRU_WIDTH), wa, ba.reshape(LRU_BLOCKS, 1, d), wx,
      bx.reshape(LRU_BLOCKS, 1, d), lam.reshape(1, LRU_WIDTH))


def _layer_norm(y, w, b):
    mu = jnp.mean(y, axis=-1, keepdims=True)
    cen = y - mu
    var = jnp.mean(cen * cen, axis=-1, keepdims=True)
    return cen * lax.rsqrt(var + LN_EPS) * w + b


HALF = D_MODEL // 2
SLAB = HALF // LANES
PITCH = SLAB + 4
u32 = jnp.uint32
HIGH_MASK = 0xFFFF0000


def _bf16_bits(v):
    return lax.bitcast_convert_type(v.astype(bf16).astype(f32), u32)


def _store_slabs(ref, val, before_piece=None):
    n = val.shape[0]
    for j in range(SLAB):
        if before_piece is not None:
            before_piece(j)
        lo = _bf16_bits(val[:, j * LANES:(j + 1) * LANES])
        hi = _bf16_bits(val[:, HALF + j * LANES:HALF + (j + 1) * LANES])
        ref[pl.ds(j, n, stride=PITCH), :] = hi | lax.shift_right_logical(lo, jnp.full_like(lo, 16))
    for j in range(SLAB, PITCH):
        ref[pl.ds(j, n, stride=PITCH), :] = jnp.zeros((n, LANES), u32)


def _slab_cols(ref, j, n):
    w = ref[pl.ds(j, n, stride=PITCH), :]
    lo = lax.bitcast_convert_type(lax.shift_left(w, jnp.full_like(w, 16)), f32)
    hi = lax.bitcast_convert_type(w & jnp.full_like(w, HIGH_MASK), f32)
    return lo, hi


def _split_bf16(v):
    hi = v.astype(bf16)
    lo = (v - hi.astype(f32)).astype(bf16)
    return hi, lo


def _out_router_kernel(ret_ref, lru_ref, wo_ref, x_ref, lnw_ref, lnb_ref, wr_ref, br_ref,
                       x1_ref, tope_ref, gate_ref):
    m = jnp.dot(ret_ref[...], wo_ref[0:RET_WIDTH, :], preferred_element_type=f32)
    m = m + jnp.dot(lru_ref[...], wo_ref[RET_WIDTH:D_MODEL, :], preferred_element_type=f32)
    x1 = _layer_norm(DN_ALPHA * x_ref[...] + m, lnw_ref[...], lnb_ref[...])
    x1_ref[...] = x1
    xh, xl = _split_bf16(x1)
    wh, wl = _split_bf16(wr_ref[...])
    logits = (jnp.dot(xh, wh, preferred_element_type=f32) + jnp.dot(xl, wh, preferred_element_type=f32)
              + jnp.dot(xh, wl, preferred_element_type=f32)) + br_ref[...]
    tm = logits.shape[0]
    lane = lax.broadcasted_iota(jnp.int32, logits.shape, 1)
    lane_k = lax.broadcasted_iota(jnp.int32, (tm, TOP_K), 1)
    top_e = jnp.zeros((tm, TOP_K), jnp.int32)
    top_v = jnp.zeros((tm, TOP_K), f32)
    cur = logits
    for kk in range(TOP_K):
        mx = jnp.max(cur, axis=-1, keepdims=True)
        idx = jnp.min(jnp.where(cur == mx, lane, N_EXPERTS), axis=-1, keepdims=True)
        top_e = jnp.where(lane_k == kk, idx, top_e)
        top_v = jnp.where(lane_k == kk, mx, top_v)
        cur = jnp.where(lane == idx, -jnp.inf, cur)
    ex = jnp.exp(top_v - top_v[:, 0:1])
    gate_ref[...] = ex / jnp.sum(ex, axis=-1, keepdims=True)
    tope_ref[...] = top_e


def _out_router(ret_out, lru_out, wo_bf, x2d, ln_w, ln_b, w_router, b_router):
    t, d = x2d.shape
    tm = 512
    row = lambda c: pl.BlockSpec((tm, c), lambda i: (i, 0))
    full = lambda r, c: pl.BlockSpec((r, c), lambda i: (0, 0))
    return pl.pallas_call(
        _out_router_kernel,
        grid=(t // tm,),
        in_specs=[row(RET_WIDTH), row(LRU_WIDTH), full(d, d), row(d), full(1, d), full(1, d),
                  full(d, N_EXPERTS), full(1, N_EXPERTS)],
        out_specs=[row(d), row(TOP_K), row(TOP_K)],
        out_shape=[jax.ShapeDtypeStruct((t, d), f32), jax.ShapeDtypeStruct((t, TOP_K), jnp.int32),
                   jax.ShapeDtypeStruct((t, TOP_K), f32)],
        compiler_params=_cparams(("parallel",)),
        name="out_proj_ln1_router",
    )(ret_out, lru_out, wo_bf, x2d, ln_w.reshape(1, d), ln_b.reshape(1, d), w_router,
      b_router.reshape(1, N_EXPERTS))


def _routing_tables(top_e, t):
    n_pad = t * TOP_K + N_EXPERTS * ROW_CHUNK
    max_units = N_EXPERTS + (t * TOP_K) // UNIT_ROWS
    sel = (top_e[:, :, None] == jnp.arange(N_EXPERTS, dtype=jnp.int32)[None, None, :]).any(axis=1)
    sel = sel.astype(jnp.int32)
    counts = jnp.sum(sel, axis=0)
    rank = jnp.cumsum(sel, axis=0) - sel
    padded = (counts + ROW_CHUNK - 1) // ROW_CHUNK * ROW_CHUNK
    pad_ends = jnp.cumsum(padded)
    pad_starts = pad_ends - padded
    dest_dense = pad_starts[None, :] + rank
    dest = jnp.take_along_axis(dest_dense, top_e, axis=1)
    pad_fill = jnp.concatenate([pad_starts + counts, padded - counts, pad_ends[-1:]]).astype(jnp.int32)
    units_per_e = (padded + UNIT_ROWS - 1) // UNIT_ROWS
    unit_ends = jnp.cumsum(units_per_e)
    unit_starts = unit_ends - units_per_e
    n_units = unit_ends[-1]
    u = jnp.arange(max_units, dtype=jnp.int32)
    u_clamped = jnp.minimum(u, n_units - 1)
    ue = jnp.searchsorted(unit_ends, u_clamped, side='right').astype(jnp.int32)
    ue = jnp.minimum(ue, N_EXPERTS - 1)
    j = u_clamped - unit_starts[ue]
    u_row = pad_starts[ue] + j * UNIT_ROWS
    u_rows = jnp.minimum(UNIT_ROWS, padded[ue] - j * UNIT_ROWS)
    u_chunks = jnp.where(u < n_units, u_rows // ROW_CHUNK, 0).astype(jnp.int32)
    u_real = jnp.clip(counts[ue] - j * UNIT_ROWS, 0, UNIT_ROWS)
    u_granules = jnp.where(u < n_units, (u_real + COMPUTE_GRANULE - 1) // COMPUTE_GRANULE, 0).astype(jnp.int32)
    n_used_rows = jnp.concatenate([pad_ends[-1:].astype(jnp.int32), u_granules])
    return dest.astype(jnp.int32), pad_fill, ue, u_row.astype(jnp.int32), u_chunks, n_used_rows, n_pad, max_units


CHUNK_PITCHED = ROW_CHUNK * PITCH
DISPATCH_ROWS = 256
FILL_SLOTS = ROW_CHUNK // 2


def _dispatch_kernel(fill_ref, dest_ref, dest_prev_ref, x1_ref, xs_hbm, stage_a, stage_b, zeros, sem_a, sem_b,
                     sem_z):
    tm = DISPATCH_ROWS
    s = pl.program_id(0)
    last = pl.num_programs(0) - 1

    def slot_copy(table, table_off, t, k, stage, sem):
        dst = xs_hbm.at[pl.ds(table[(table_off + t) * TOP_K + k] * PITCH, PITCH), :]
        return pltpu.make_async_copy(stage.at[pl.ds(t * PITCH, PITCH), :], dst, sem)

    def wait_tile(stage, sem):
        for _ in range(TOP_K):
            pltpu.make_async_copy(stage, xs_hbm.at[pl.ds(0, tm * PITCH), :], sem).wait()

    def start_copies(table, table_off, stage, sem):
        per_piece = tm // SLAB

        def before_piece(j):
            for t in range(j * per_piece, (j + 1) * per_piece):
                for k in range(TOP_K):
                    slot_copy(table, table_off, t, k, stage, sem).start(priority=k % 2)

        return before_piece

    def pack(row0, stage, before_piece=None):
        _store_slabs(stage, x1_ref[pl.ds(row0, tm), :], before_piece)

    @pl.when(s == 0)
    def _():
        pack(0, stage_a)

    @pl.when(s > 0)
    def _():
        wait_tile(stage_a, sem_a)
        pack(0, stage_a, start_copies(dest_prev_ref, tm, stage_b, sem_b))
        wait_tile(stage_b, sem_b)

    pack(tm, stage_b, start_copies(dest_ref, 0, stage_a, sem_a))

    @pl.when(s == last)
    def _():
        def issue(t, c):
            for k in range(TOP_K):
                slot_copy(dest_ref, tm, t, k, stage_b, sem_b).start(priority=k % 2)
            return c

        lax.fori_loop(0, tm, issue, 0, unroll=2)
        wait_tile(stage_a, sem_a)
        wait_tile(stage_b, sem_b)

        zeros[...] = jnp.zeros(zeros.shape, u32)

        def fill(slot, n_slots):
            return pltpu.make_async_copy(zeros.at[pl.ds(0, n_slots * PITCH), :],
                                         xs_hbm.at[pl.ds(slot * PITCH, n_slots * PITCH), :], sem_z)

        def expert_pad(e, start):
            slot = fill_ref[e]
            n = fill_ref[N_EXPERTS + e]
            piece = FILL_SLOTS
            while piece >= 1:
                @pl.when((n // piece) % 2 == 1)
                def _(slot=slot, piece=piece):
                    if start:
                        fill(slot, piece).start()
                    else:
                        fill(slot, piece).wait()

                slot = slot + ((n // piece) % 2) * piece
                piece //= 2

        def tail(i, start):
            slot = fill_ref[2 * N_EXPERTS] + i * FILL_SLOTS
            if start:
                fill(slot, FILL_SLOTS).start()
            else:
                fill(slot, FILL_SLOTS).wait()

        n_tail = (xs_hbm.shape[0] // PITCH - fill_ref[2 * N_EXPERTS]) // FILL_SLOTS
        for start in (True, False):
            lax.fori_loop(0, N_EXPERTS, lambda e, c, start=start: (expert_pad(e, start), c)[1], 0)
            lax.fori_loop(0, n_tail, lambda i, c, start=start: (tail(i, start), c)[1], 0)


def _dispatch(x1, dest, pad_fill, n_pad):
    t, d = x1.shape
    tm = DISPATCH_ROWS
    n_steps = t // (2 * tm)
    table = lambda back: pl.BlockSpec((2 * tm * TOP_K,), lambda i, fill: (jnp.maximum(i - back, 0),),
                                      memory_space=pltpu.SMEM)
    dest_flat = dest.reshape(-1)
    stage = pltpu.VMEM((tm * PITCH, LANES), u32)
    return pl.pallas_call(
        _dispatch_kernel,
        grid_spec=pltpu.PrefetchScalarGridSpec(
            num_scalar_prefetch=1,
            grid=(n_steps,),
            in_specs=[table(0), table(1), pl.BlockSpec((2 * tm, d), lambda i, fill: (i, 0))],
            out_specs=pl.BlockSpec(memory_space=pl.ANY),
            scratch_shapes=[stage, stage, pltpu.VMEM((FILL_SLOTS * PITCH, LANES), u32),
                            pltpu.SemaphoreType.DMA(()), pltpu.SemaphoreType.DMA(()), pltpu.SemaphoreType.DMA(())],
        ),
        out_shape=jax.ShapeDtypeStruct((n_pad * PITCH, LANES), u32),
        compiler_params=_cparams(("arbitrary",)),
        name="moe_dispatch",
    )(pad_fill, dest_flat, dest_flat, x1)


OUT_SLOTS = 4
COMPUTE_GRANULE = ROW_CHUNK // 2


def _moe_kernel(ue_ref, urow_ref, uchunks_ref, used_ref, xs_hbm, wg_ref, bg_ref, wu_ref, bu_ref, wd_ref,
                bd_ref, y_hbm, xbuf, yacc, gstage_a, gstage_b, ostage, flags, in_sem, out_sem):
    u = pl.program_id(0)
    f = pl.program_id(1)
    n_u = pl.num_programs(0)
    n_f = pl.num_programs(1)
    n_chunks = uchunks_ref[u]
    row0 = urow_ref[u]
    cur = u % 2
    nxt = 1 - cur
    u_next = jnp.minimum(u + 1, n_u - 1)
    n_next = jnp.where(u + 1 < n_u, uchunks_ref[u_next], 0)
    row_next = urow_ref[u_next]
    first_step = jnp.logical_and(u == 0, f == 0)
    PENDING, PEND_CHUNK, PEND_SLOT, OUT_BUSY = 0, 1, 2, 3

    def rows(c, size=ROW_CHUNK):
        return pl.ds(pl.multiple_of(c * size, size), size)

    stages = (gstage_a, gstage_b)

    def chunk_copy(unit_row, c, parity):
        src = xs_hbm.at[pl.ds(pl.multiple_of((unit_row + c * ROW_CHUNK) * PITCH, CHUNK_PITCHED), CHUNK_PITCHED), :]
        return pltpu.make_async_copy(src, stages[parity], in_sem.at[parity])

    def unpack(c, xslot, parity):
        for j in range(SLAB):
            lo, hi = _slab_cols(stages[parity], j, ROW_CHUNK)
            xbuf[xslot, rows(c), j * LANES:(j + 1) * LANES] = lo.astype(bf16)
            xbuf[xslot, rows(c), HALF + j * LANES:HALF + (j + 1) * LANES] = hi.astype(bf16)

    def out_copy(c, slot):
        dst = y_hbm.at[pl.ds(pl.multiple_of((row0 + c * ROW_CHUNK) * PITCH, CHUNK_PITCHED), CHUNK_PITCHED), :]
        return pltpu.make_async_copy(ostage.at[slot], dst, out_sem.at[slot])

    def out_wait(slot):
        @pl.when(flags[OUT_BUSY + slot] == 1)
        def _():
            pltpu.make_async_copy(ostage.at[slot], y_hbm.at[pl.ds(0, CHUNK_PITCHED), :], out_sem.at[slot]).wait()
            flags[OUT_BUSY + slot] = 0

    @pl.when(first_step)
    def _():
        for i in range(OUT_BUSY + OUT_SLOTS):
            flags[i] = 0

        def clear(c, carry):
            yacc[rows(c), :] = jnp.zeros((ROW_CHUNK, D_MODEL), f32)
            return carry

        lax.fori_loop(0, UNIT_ROWS // ROW_CHUNK, clear, 0)

        def load(c, carry):
            first = chunk_copy(row0, c, 0)
            first.start()
            first.wait()
            unpack(c, cur, 0)
            return carry

        lax.fori_loop(0, n_chunks, load, 0)

    prefetching = f < n_next
    pending = flags[PENDING] == 1
    done_chunk = flags[PEND_CHUNK]
    done_slot = flags[PEND_SLOT]
    same_buffer = jnp.logical_and(pending, done_chunk % 2 == f % 2)
    for late in (False, True):
        for parity in (0, 1):
            starts_now = jnp.logical_and(prefetching, same_buffer if late else jnp.logical_not(same_buffer))

            @pl.when(jnp.logical_and(starts_now, f % 2 == parity))
            def _():
                chunk_copy(row_next, f, parity).start()

        if not late:
            for parity in (0, 1):
                @pl.when(jnp.logical_and(pending, done_chunk % 2 == parity))
                def _():
                    chunk_copy(row_next, done_chunk, parity).wait()
                    unpack(done_chunk, done_slot, parity)

    flags[PENDING] = prefetching.astype(jnp.int32)

    @pl.when(prefetching)
    def _():
        flags[PEND_CHUNK] = f
        flags[PEND_SLOT] = nxt

    n_granules = used_ref[1 + u]
    block_sizes = (8, 4, 2, 1)

    @pl.when(n_chunks > 0)
    def _():
        bg = bg_ref[...]
        bu = bu_ref[...]
        mm = lambda a, w: lax.dot_general(a, w, (((1,), (0,)), ((), ())), preferred_element_type=f32)

        def mlp(sl, first):
            xc = xbuf[cur, sl, :]
            gt = jnp.minimum(mm(xc, wg_ref[...]) + bg, SWIGLU_LIMIT)
            up = jnp.clip(mm(xc, wu_ref[...]) + bu, -SWIGLU_LIMIT, SWIGLU_LIMIT)
            hid = (up + 1.0) * gt * jax.nn.sigmoid(SWIGLU_ALPHA * gt)
            part = mm(hid.astype(bf16), wd_ref[...])
            if first:
                yacc[sl, :] = part
            else:
                yacc[sl, :] += part

        def blocks(first):
            big = block_sizes[0] * COMPUTE_GRANULE
            n_big = n_granules // block_sizes[0]

            def big_block(c, carry):
                mlp(rows(c, big), first)
                return carry

            lax.fori_loop(0, n_big, big_block, 0)
            done = n_big * block_sizes[0]
            for size in block_sizes[1:]:
                present = (n_granules // size) % 2
                m = size * COMPUTE_GRANULE

                @pl.when(present == 1)
                def _(done=done, m=m):
                    mlp(pl.ds(pl.multiple_of(done * COMPUTE_GRANULE, m), m), first)

                done = done + present * size

        @pl.when(f == 0)
        def _():
            blocks(True)

        @pl.when(f > 0)
        def _():
            blocks(False)

    @pl.when(jnp.logical_and(f == n_f - 1, n_chunks > 0))
    def _():
        bd = bd_ref[...]

        def store(c, carry):
            slot = c % OUT_SLOTS
            out_wait(slot)
            _store_slabs(ostage.at[slot], yacc[rows(c), :] + bd)
            out_copy(c, slot).start()
            flags[OUT_BUSY + slot] = 1
            return carry

        lax.fori_loop(0, n_chunks, store, 0)

    @pl.when(jnp.logical_and(u == n_u - 1, f == n_f - 1))
    def _():
        for s in range(OUT_SLOTS):
            out_wait(s)
        first = used_ref[0] // ROW_CHUNK
        last = y_hbm.shape[0] // CHUNK_PITCHED
        ostage[0] = jnp.zeros((CHUNK_PITCHED, LANES), u32)

        def tail_copy(c):
            dst = y_hbm.at[pl.ds(pl.multiple_of(c * CHUNK_PITCHED, CHUNK_PITCHED), CHUNK_PITCHED), :]
            return pltpu.make_async_copy(ostage.at[0], dst, out_sem.at[0])

        def start(c, carry):
            tail_copy(c).start()
            return carry

        def wait(c, carry):
            tail_copy(c).wait()
            return carry

        lax.fori_loop(first, last, start, 0)
        lax.fori_loop(first, last, wait, 0)


def _moe_experts(xs, ue, u_row, u_chunks, n_used_rows, n_pad, max_units, w_gate, b_gate, w_up, b_up, w_down,
                 b_down):
    e, d, dff = w_gate.shape
    n_f = dff // F_TILE
    assert n_f >= UNIT_ROWS // ROW_CHUNK, "one chunk of the next unit is loaded per f-tile step"
    f_idx = lambda u, f, uc: jnp.where(uc[u] > 0, f, n_f - 1)
    col_w = pl.BlockSpec((None, d, F_TILE), lambda u, f, ue, ur, uc, used: (ue[u], 0, f_idx(u, f, uc)))
    col_b = pl.BlockSpec((None, 1, F_TILE), lambda u, f, ue, ur, uc, used: (ue[u], 0, f_idx(u, f, uc)))
    return pl.pallas_call(
        _moe_kernel,
        grid_spec=pltpu.PrefetchScalarGridSpec(
            num_scalar_prefetch=4,
            grid=(max_units, n_f),
            in_specs=[pl.BlockSpec(memory_space=pl.ANY),
                      col_w, col_b, col_w, col_b,
                      pl.BlockSpec((None, F_TILE, d), lambda u, f, ue, ur, uc, used: (ue[u], f_idx(u, f, uc), 0)),
                      pl.BlockSpec((None, 1, d), lambda u, f, ue, ur, uc, used: (ue[u], 0, 0))],
            out_specs=pl.BlockSpec(memory_space=pl.ANY),
            scratch_shapes=[pltpu.VMEM((2, UNIT_ROWS, d), bf16),
                            pltpu.VMEM((UNIT_ROWS, d), f32),
                            pltpu.VMEM((CHUNK_PITCHED, LANES), u32),
                            pltpu.VMEM((CHUNK_PITCHED, LANES), u32),
                            pltpu.VMEM((OUT_SLOTS, CHUNK_PITCHED, LANES), u32),
                            pltpu.SMEM((3 + OUT_SLOTS,), jnp.int32),
                            pltpu.SemaphoreType.DMA((2,)),
                            pltpu.SemaphoreType.DMA((OUT_SLOTS,))],
        ),
        out_shape=jax.ShapeDtypeStruct((n_pad * PITCH, LANES), u32),
        compiler_params=_cparams(("arbitrary", "arbitrary")),
        name="moe_experts",
    )(ue, u_row, u_chunks, n_used_rows, xs, w_gate, b_gate.reshape(e, 1, dff), w_up,
      b_up.reshape(e, 1, dff), w_down, b_down.reshape(e, 1, d))


COMBINE_ROWS = 256
COMBINE_GROUPS = 2


def _combine_kernel(dest_ref, dest_next_ref, y_hbm, gate_ref, x1_ref, p_ref, lnw_ref, lnb_ref, wp_ref, pnw_ref,
                    wg_ref, o_ref, ybuf_a, ybuf_b, fsum_ref, sem_a, sem_b):
    tm = COMBINE_ROWS
    s = pl.program_id(0)

    def row_copy(table, table_off, t, k, ybuf, sem):
        src = y_hbm.at[pl.ds(table[(table_off + t) * TOP_K + k] * PITCH, SLAB), :]
        return pltpu.make_async_copy(src, ybuf.at[k, pl.ds(t * PITCH, SLAB), :], sem)

    def wait_tile(ybuf, sem):
        for k in range(TOP_K):
            pltpu.make_async_copy(y_hbm.at[pl.ds(0, tm * SLAB), :], ybuf.at[k, pl.ds(0, tm * SLAB), :], sem).wait()

    @pl.when(s == 0)
    def _():
        def issue(t, c):
            for k in range(TOP_K):
                row_copy(dest_ref, 0, t, k, ybuf_a, sem_a).start(priority=k % 2)
            return c

        lax.fori_loop(0, tm, issue, 0, unroll=2)

    def finish_tile(row0, ybuf, next_table, next_off, next_buf, next_sem):
        tok = pl.ds(row0, tm)
        pieces = SLAB * TOP_K
        per_piece = tm // pieces
        e = jnp.dot(p_ref[tok, :].astype(bf16), wp_ref[...], preferred_element_type=f32)
        e = e * lax.rsqrt(jnp.mean(e * e, axis=-1, keepdims=True) + LN_EPS) * pnw_ref[...]
        gates = gate_ref[tok, :]
        for j in range(SLAB):
            acc_lo, acc_hi = None, None
            for k in range(TOP_K):
                piece = j * TOP_K + k
                for t in range(piece * per_piece, (piece + 1) * per_piece):
                    for kk in range(TOP_K):
                        row_copy(next_table, next_off, t, kk, next_buf, next_sem).start(priority=kk % 2)
                lo, hi = _slab_cols(ybuf.at[k], j, tm)
                g = gates[:, k:k + 1]
                acc_lo = g * lo if k == 0 else acc_lo + g * lo
                acc_hi = g * hi if k == 0 else acc_hi + g * hi
            fsum_ref[:, j * LANES:(j + 1) * LANES] = acc_lo
            fsum_ref[:, HALF + j * LANES:HALF + (j + 1) * LANES] = acc_hi
        x2 = _layer_norm(DN_ALPHA * x1_ref[tok, :] + fsum_ref[...], lnw_ref[...], lnb_ref[...])
        gate = jax.nn.sigmoid(jnp.dot(x2.astype(bf16), wg_ref[...], preferred_element_type=f32))
        o_ref[tok, :] = x2 + gate * e

    wait_tile(ybuf_a, sem_a)
    finish_tile(0, ybuf_a, dest_ref, tm, ybuf_b, sem_b)
    wait_tile(ybuf_b, sem_b)
    finish_tile(tm, ybuf_b, dest_next_ref, 0, ybuf_a, sem_a)

    @pl.when(s == pl.num_programs(0) - 1)
    def _():
        wait_tile(ybuf_a, sem_a)


def _combine(y, dest, gates, x1, p2d, ln_w, ln_b, wp_bf, ple_norm_w, wg_bf):
    t, d = x1.shape
    tm = COMBINE_ROWS
    n_steps = t // (2 * tm)
    row = lambda c: pl.BlockSpec((2 * tm, c), lambda i: (i, 0))
    full = lambda r, c: pl.BlockSpec((r, c), lambda i: (0, 0))
    table = lambda nxt: pl.BlockSpec((2 * tm * TOP_K,), lambda i: (jnp.minimum(i + nxt, n_steps - 1),),
                                     memory_space=pltpu.SMEM)
    dest_flat = dest.reshape(-1)
    return pl.pallas_call(
        _combine_kernel,
        grid=(n_steps,),
        in_specs=[table(0), table(1), pl.BlockSpec(memory_space=pl.ANY),
                  row(TOP_K), row(d), row(PLE_DIM), full(1, d), full(1, d), full(PLE_DIM, d), full(1, d),
                  full(d, d)],
        out_specs=row(d),
        out_shape=jax.ShapeDtypeStruct((t, d), f32),
        scratch_shapes=[pltpu.VMEM((TOP_K, tm * PITCH, LANES), u32), pltpu.VMEM((TOP_K, tm * PITCH, LANES), u32),
                        pltpu.VMEM((tm, d), f32), pltpu.SemaphoreType.DMA(()), pltpu.SemaphoreType.DMA(())],
        compiler_params=_cparams(("arbitrary",)),
        name="combine_ln2_ple",
    )(dest_flat, dest_flat, y, gates, x1, p2d, ln_w.reshape(1, d), ln_b.reshape(1, d), wp_bf,
      ple_norm_w.reshape(1, d), wg_bf)


def _layer(h, p_i, w_in, ret_gn_w, conv_w, conv_b, lru_wa, lru_ba, lru_wx, lru_bx, lru_lam, w_out,
           ln1_w, ln1_b, w_router, b_router, w_gate, b_gate, w_up, b_up, w_down, b_down,
           ln2_w, ln2_b, w_ple_proj, ple_norm_w, w_ple_gate):
    b, s, d = h.shape
    t = b * s
    x2d = h.reshape(t, d)
    proj = _in_proj(x2d, w_in).reshape(b, s, IN_COLS)
    ret_out = _retention(proj, ret_gn_w)
    lru_out = _lru(proj, conv_w, conv_b, lru_wa, lru_ba, lru_wx, lru_bx, lru_lam)
    x1, top_e, gates = _out_router(ret_out.reshape(t, RET_WIDTH), lru_out.reshape(t, LRU_WIDTH),
                                   w_out.astype(bf16), x2d, ln1_w, ln1_b, w_router, b_router)
    dest, pad_fill, ue, u_row, u_chunks, n_used_rows, n_pad, max_units = _routing_tables(top_e, t)
    xs = _dispatch(x1, dest, pad_fill, n_pad)
    y = _moe_experts(xs, ue, u_row, u_chunks, n_used_rows, n_pad, max_units, w_gate, b_gate, w_up, b_up, w_down,
                     b_down)
    out = _combine(y, dest, gates, x1, p_i.reshape(t, PLE_DIM), ln2_w, ln2_b, w_ple_proj.astype(bf16),
                   ple_norm_w, w_ple_gate.astype(bf16))
    return out.reshape(b, s, d)


def kernel(x, p, w_in, ret_gn_w, conv_w, conv_b, lru_wa, lru_ba, lru_wx, lru_bx, lru_lam, w_out, ln1_w, ln1_b,
           w_router, b_router, w_gate, b_gate, w_up, b_up, w_down, b_down, ln2_w, ln2_b, w_ple_proj, ple_norm_w,
           w_ple_gate):
    h = x.astype(f32)
    for i in range(w_in.shape[0]):
        h = _layer(h, p[i], w_in[i], ret_gn_w[i], conv_w[i], conv_b[i], lru_wa[i], lru_ba[i], lru_wx[i],
                   lru_bx[i], lru_lam[i], w_out[i], ln1_w[i], ln1_b[i], w_router[i], b_router[i], w_gate[i],
                   b_gate[i], w_up[i], b_up[i], w_down[i], b_down[i], ln2_w[i], ln2_b[i], w_ple_proj[i],
                   ple_norm_w[i], w_ple_gate[i])
    return h.astype(x.dtype)
```

```python
import functools
import math

import jax
import jax.numpy as jnp
from jax import lax
from jax.experimental import pallas as pl
from jax.experimental.pallas import tpu as pltpu

D_MODEL = 2048
RET_HEAD_DIM = 128
RET_HEADS = 8
RET_WIDTH = RET_HEADS * RET_HEAD_DIM
LRU_WIDTH = D_MODEL - RET_WIDTH
LRU_BLOCKS = 8
LRU_BLOCK_DIM = LRU_WIDTH // LRU_BLOCKS
IN_COLS = 4 * RET_WIDTH + 2 * LRU_WIDTH
CONV_WIDTH = 4
LRU_C = 8.0
CHUNK = 128
ROPE_BASE = 10000.0
N_EXPERTS = 32
TOP_K = 4
SWIGLU_LIMIT = 7.0
SWIGLU_ALPHA = 1.702
PLE_DIM = 256
LN_EPS = 1e-5
DEPTH = 1
DN_ALPHA = (2.0 * DEPTH) ** 0.25

LANES = 128
SUBLANES = 8
VMEM_LIMIT = 60 * 1024 * 1024

ROW_CHUNK = 256
UNIT_ROWS = 2048
F_TILE = 256

f32 = jnp.float32
bf16 = jnp.bfloat16


def _cparams(sem):
    return pltpu.CompilerParams(dimension_semantics=sem, vmem_limit_bytes=VMEM_LIMIT)


def _in_proj_kernel(x_ref, w_ref, o_ref):
    o_ref[...] = lax.dot_general(x_ref[...].astype(bf16), w_ref[...], (((1,), (0,)), ((), ())),
                                 preferred_element_type=f32)


def _in_proj(x2d, w):
    t, d = x2d.shape
    n = w.shape[1]
    tm, tn = 512, 2048
    return pl.pallas_call(
        _in_proj_kernel,
        grid=(n // tn, t // tm),
        in_specs=[pl.BlockSpec((tm, d), lambda j, i: (i, 0)),
                  pl.BlockSpec((d, tn), lambda j, i: (0, j))],
        out_specs=pl.BlockSpec((tm, tn), lambda j, i: (i, j)),
        out_shape=jax.ShapeDtypeStruct((t, n), f32),
        compiler_params=_cparams(("parallel", "parallel")),
        name="in_proj",
    )(x2d, w)


def _retention_kernel(q_ref, k_ref, v_ref, g_ref, cos_ref, sin_ref, dec_ref, qd_ref, kd_ref, cd_ref, gnw_ref,
                      o_ref):
    s = q_ref.shape[0]
    n_chunks = s // CHUNK
    decay = dec_ref[...]
    q_dec = qd_ref[...]
    k_dec = kd_ref[...]
    c_dec = cd_ref[0:1, :]
    gnw = gnw_ref[...]
    k_scale = RET_HEAD_DIM ** -0.5

    def rope(xv, cos, sin):
        return xv * cos + pltpu.roll(xv, RET_HEAD_DIM // 2, axis=1) * sin

    def body(n, state):
        sl = pl.ds(pl.multiple_of(n * CHUNK, CHUNK), CHUNK)
        cos = cos_ref[sl, :]
        sin = sin_ref[sl, :]
        q = rope(q_ref[sl, :], cos, sin)
        k = rope(k_ref[sl, :], cos, sin) * k_scale
        vb = v_ref[sl, :].astype(bf16)
        scores = lax.dot_general(q.astype(bf16), k.astype(bf16), (((1,), (1,)), ((), ())),
                                 preferred_element_type=f32) * decay
        intra = jnp.dot(scores.astype(bf16), vb, preferred_element_type=f32)
        cross = jnp.dot((q * q_dec).astype(bf16), state.astype(bf16), preferred_element_type=f32)
        kv = lax.dot_general((k * k_dec).astype(bf16), vb, (((0,), (0,)), ((), ())),
                             preferred_element_type=f32)
        ret = intra + cross
        mu = jnp.mean(ret, axis=-1, keepdims=True)
        cen = ret - mu
        var = jnp.mean(cen * cen, axis=-1, keepdims=True)
        ret = cen * lax.rsqrt(var + LN_EPS) * gnw
        g = g_ref[sl, :]
        o_ref[sl, :] = (g * jax.nn.sigmoid(g) * ret).astype(o_ref.dtype)
        return c_dec * state + kv

    lax.fori_loop(0, n_chunks, body, jnp.zeros((RET_HEAD_DIM, RET_HEAD_DIM), f32), unroll=8)


def _retention_tables(s):
    h, d = RET_HEADS, RET_HEAD_DIM
    inv = ROPE_BASE ** (-jnp.arange(0, d, 2, dtype=f32) / d)
    ang = jnp.arange(s, dtype=f32)[:, None] * inv[None, :]
    cos = jnp.cos(ang)
    sin = jnp.sin(ang)
    cos_t = jnp.concatenate([cos, cos], axis=-1)
    sin_t = jnp.concatenate([-sin, sin], axis=-1)
    log_gamma = jnp.log1p(-jnp.exp2(-5.0 - jnp.arange(h, dtype=f32)))
    idx = jnp.arange(CHUNK, dtype=f32)
    diff = idx[:, None] - idx[None, :]
    decay = jnp.where((diff >= 0)[None], jnp.exp(jnp.maximum(diff, 0.0)[None] * log_gamma[:, None, None]), 0.0)
    q_dec = jnp.exp((idx[None, :] + 1.0) * log_gamma[:, None])
    k_dec = jnp.exp((CHUNK - 1.0 - idx)[None, :] * log_gamma[:, None])
    c_dec = jnp.exp(CHUNK * log_gamma)
    q_dec = jnp.broadcast_to(q_dec[:, :, None], (h, CHUNK, d))
    k_dec = jnp.broadcast_to(k_dec[:, :, None], (h, CHUNK, d))
    c_dec = jnp.broadcast_to(c_dec[:, None, None], (h, SUBLANES, d))
    return cos_t, sin_t, decay, q_dec, k_dec, c_dec


def _retention(proj, ret_gn_w):
    b, s, _ = proj.shape
    d = RET_HEAD_DIM
    cos_t, sin_t, decay, q_dec, k_dec, c_dec = _retention_tables(s)
    col = lambda off: pl.BlockSpec((None, s, d), lambda bi, hi: (bi, 0, off + hi))
    per_head = lambda r: pl.BlockSpec((None, r, d), lambda bi, hi: (hi, 0, 0))
    full = lambda shp: pl.BlockSpec(shp, lambda bi, hi: (0,) * len(shp))
    return pl.pallas_call(
        _retention_kernel,
        grid=(b, RET_HEADS),
        in_specs=[col(0), col(RET_HEADS), col(2 * RET_HEADS), col(3 * RET_HEADS),
                  full((s, d)), full((s, d)),
                  per_head(CHUNK), per_head(CHUNK), per_head(CHUNK), per_head(SUBLANES),
                  pl.BlockSpec((1, d), lambda bi, hi: (0, hi))],
        out_specs=pl.BlockSpec((None, s, d), lambda bi, hi: (bi, 0, hi)),
        out_shape=jax.ShapeDtypeStruct((b, s, RET_WIDTH), bf16),
        compiler_params=_cparams(("parallel", "parallel")),
        name="retention",
    )(proj, proj, proj, proj, cos_t, sin_t, decay, q_dec, k_dec, c_dec, ret_gn_w.reshape(1, RET_WIDTH))


def _gelu_tanh(x):
    return 0.5 * x * (1.0 + jnp.tanh(math.sqrt(2.0 / math.pi) * (x + 0.044715 * (x * x * x))))


def _lru_kernel(xr_ref, yg_ref, cw_ref, cb_ref, wa_ref, ba_ref, wx_ref, bx_ref, lam_ref, o_ref, a_ref, b_ref):
    s = xr_ref.shape[0]
    x = xr_ref[...]
    rows = lax.broadcasted_iota(jnp.int32, x.shape, 0)
    xc = cb_ref[...] + cw_ref[CONV_WIDTH - 1:CONV_WIDTH, :] * x
    for back in range(1, CONV_WIDTH):
        shifted = jnp.where(rows >= back, pltpu.roll(x, back, axis=0), 0.0)
        xc = xc + cw_ref[CONV_WIDTH - 1 - back:CONV_WIDTH - back, :] * shifted
    xcb = xc.astype(bf16)
    r = jax.nn.sigmoid(jnp.dot(xcb, wa_ref[...].astype(bf16), preferred_element_type=f32) + ba_ref[...])
    gi = jax.nn.sigmoid(jnp.dot(xcb, wx_ref[...].astype(bf16), preferred_element_type=f32) + bx_ref[...])
    lam = lam_ref[...]
    log_sig = jnp.minimum(lam, 0.0) - jnp.log1p(jnp.exp(-jnp.abs(lam)))
    log_a = LRU_C * r * log_sig
    a = jnp.exp(log_a)
    a_ref[...] = a
    b_ref[...] = jnp.sqrt(-jnp.tanh(log_a) * (a * a + 1.0)) * (gi * xc)

    row8 = lax.broadcasted_iota(jnp.int32, (SUBLANES, LANES), 0)

    def body(i, h_prev):
        sl = pl.ds(pl.multiple_of(i * SUBLANES, SUBLANES), SUBLANES)
        a8 = a_ref[sl, :]
        b8 = b_ref[sl, :]
        for sh in (1, 2, 4):
            a_sh = jnp.where(row8 >= sh, pltpu.roll(a8, sh, axis=0), 1.0)
            b_sh = jnp.where(row8 >= sh, pltpu.roll(b8, sh, axis=0), 0.0)
            b8 = a8 * b_sh + b8
            a8 = a8 * a_sh
        h8 = a8 * h_prev + b8
        o_ref[sl, :] = (_gelu_tanh(yg_ref[sl, :]) * h8).astype(o_ref.dtype)
        return h8[SUBLANES - 1:SUBLANES, :]

    lax.fori_loop(0, s // SUBLANES, body, jnp.zeros((1, LANES), f32), unroll=16)


def _lru(proj, conv_w, conv_b, wa, ba, wx, bx, lam):
    b, s, _ = proj.shape
    d = LRU_BLOCK_DIM
    xr_off = 4 * RET_WIDTH // d
    yg_off = xr_off + LRU_BLOCKS
    col = lambda off: pl.BlockSpec((None, s, d), lambda bi, ji: (bi, 0, off + ji))
    vec = lambda r: pl.BlockSpec((r, d), lambda bi, ji: (0, ji))
    blk = lambda r: pl.BlockSpec((None, r, d), lambda bi, ji: (ji, 0, 0))
    return pl.pallas_call(
        _lru_kernel,
        grid=(b, LRU_BLOCKS),
        in_specs=[col(xr_off), col(yg_off), vec(CONV_WIDTH), vec(1), blk(d), blk(1), blk(d), blk(1), vec(1)],
        out_specs=pl.BlockSpec((None, s, d), lambda bi, ji: (bi, 0, ji)),
        out_shape=jax.ShapeDtypeStruct((b, s, LRU_WIDTH), bf16),
        scratch_shapes=[pltpu.VMEM((s, d), f32), pltpu.VMEM((s, d), f32)],
        compiler_params=_cparams(("parallel", "parallel")),
        name="rg_lru",
    )(proj, proj, conv_w, conv_b.reshape(1, LRU_WIDTH), wa, ba.reshape(LRU_BLOCKS, 1, d), wx,
      bx.reshape(LRU_BLOCKS, 1, d), lam.reshape(1, LRU_WIDTH))


def _layer_norm(y, w, b):
    mu = jnp.mean(y, axis=-1, keepdims=True)
    cen = y - mu
    var = jnp.mean(cen * cen, axis=-1, keepdims=True)
    return cen * lax.rsqrt(var + LN_EPS) * w + b


HALF = D_MODEL // 2
SLAB = HALF // LANES
PITCH = SLAB + 4
u32 = jnp.uint32
HIGH_MASK = 0xFFFF0000


def _bf16_bits(v):
    return lax.bitcast_convert_type(v.astype(bf16).astype(f32), u32)


def _store_slabs(ref, val, before_piece=None):
    n = val.shape[0]
    for j in range(SLAB):
        if before_piece is not None:
            before_piece(j)
        lo = _bf16_bits(val[:, j * LANES:(j + 1) * LANES])
        hi = _bf16_bits(val[:, HALF + j * LANES:HALF + (j + 1) * LANES])
        ref[pl.ds(j, n, stride=PITCH), :] = hi | lax.shift_right_logical(lo, jnp.full_like(lo, 16))
    for j in range(SLAB, PITCH):
        ref[pl.ds(j, n, stride=PITCH), :] = jnp.zeros((n, LANES), u32)


def _slab_cols(ref, j, n):
    w = ref[pl.ds(j, n, stride=PITCH), :]
    lo = lax.bitcast_convert_type(lax.shift_left(w, jnp.full_like(w, 16)), f32)
    hi = lax.bitcast_convert_type(w & jnp.full_like(w, HIGH_MASK), f32)
    return lo, hi


def _split_bf16(v):
    hi = v.astype(bf16)
    lo = (v - hi.astype(f32)).astype(bf16)
    return hi, lo


def _out_router_kernel(ret_ref, lru_ref, wo_ref, x_ref, lnw_ref, lnb_ref, wr_ref, br_ref,
                       x1_ref, tope_ref, gate_ref):
    m = jnp.dot(ret_ref[...], wo_ref[0:RET_WIDTH, :], preferred_element_type=f32)
    m = m + jnp.dot(lru_ref[...], wo_ref[RET_WIDTH:D_MODEL, :], preferred_element_type=f32)
    x1 = _layer_norm(DN_ALPHA * x_ref[...] + m, lnw_ref[...], lnb_ref[...])
    x1_ref[...] = x1
    xh, xl = _split_bf16(x1)
    wh, wl = _split_bf16(wr_ref[...])
    nt = lambda a, b: lax.dot_general(a, b, (((1,), (1,)), ((), ())), preferred_element_type=f32)
    logits = nt(wh, xh) + nt(wh, xl) + nt(wl, xh) + br_ref[...]
    tm = logits.shape[1]
    expert = lax.broadcasted_iota(jnp.int32, logits.shape, 0)
    choice = lax.broadcasted_iota(jnp.int32, (TOP_K, tm), 0)
    top_e = jnp.zeros((TOP_K, tm), jnp.int32)
    top_v = jnp.zeros((TOP_K, tm), f32)
    cur = logits
    for kk in range(TOP_K):
        mx = jnp.max(cur, axis=0, keepdims=True)
        idx = jnp.min(jnp.where(cur == mx, expert, N_EXPERTS), axis=0, keepdims=True)
        top_e = jnp.where(choice == kk, idx, top_e)
        top_v = jnp.where(choice == kk, mx, top_v)
        cur = jnp.where(expert == idx, -jnp.inf, cur)
    ex = jnp.exp(top_v - top_v[0:1, :])
    gate_ref[...] = ex / jnp.sum(ex, axis=0, keepdims=True)
    tope_ref[...] = top_e


def _out_router(ret_out, lru_out, wo_bf, x2d, ln_w, ln_b, w_router, b_router):
    t, d = x2d.shape
    tm = 512
    row = lambda c: pl.BlockSpec((tm, c), lambda i: (i, 0))
    full = lambda r, c: pl.BlockSpec((r, c), lambda i: (0, 0))
    per_choice = pl.BlockSpec((TOP_K, tm), lambda i: (0, i))
    return pl.pallas_call(
        _out_router_kernel,
        grid=(t // tm,),
        in_specs=[row(RET_WIDTH), row(LRU_WIDTH), full(d, d), row(d), full(1, d), full(1, d),
                  full(N_EXPERTS, d), full(N_EXPERTS, 1)],
        out_specs=[row(d), per_choice, per_choice],
        out_shape=[jax.ShapeDtypeStruct((t, d), f32), jax.ShapeDtypeStruct((TOP_K, t), jnp.int32),
                   jax.ShapeDtypeStruct((TOP_K, t), f32)],
        compiler_params=_cparams(("parallel",)),
        name="out_proj_ln1_router",
    )(ret_out, lru_out, wo_bf, x2d, ln_w.reshape(1, d), ln_b.reshape(1, d), w_router.T,
      b_router.reshape(N_EXPERTS, 1))


def _routing_tables(top_e, t):
    n_pad = t * TOP_K + N_EXPERTS * ROW_CHUNK
    max_units = N_EXPERTS + (t * TOP_K) // UNIT_ROWS
    sel = (top_e[:, None, :] == jnp.arange(N_EXPERTS, dtype=jnp.int32)[None, :, None]).any(axis=0)
    sel = sel.astype(jnp.int32)
    counts = jnp.sum(sel, axis=1)
    rank = jnp.cumsum(sel, axis=1) - sel
    padded = (counts + ROW_CHUNK - 1) // ROW_CHUNK * ROW_CHUNK
    pad_ends = jnp.cumsum(padded)
    pad_starts = pad_ends - padded
    dest_dense = pad_starts[:, None] + rank
    dest = jnp.take_along_axis(dest_dense, top_e, axis=0).T
    pad_fill = jnp.concatenate([pad_starts + counts, padded - counts, pad_ends[-1:]]).astype(jnp.int32)
    units_per_e = (padded + UNIT_ROWS - 1) // UNIT_ROWS
    unit_ends = jnp.cumsum(units_per_e)
    unit_starts = unit_ends - units_per_e
    n_units = unit_ends[-1]
    u = jnp.arange(max_units, dtype=jnp.int32)
    u_clamped = jnp.minimum(u, n_units - 1)
    ue = jnp.searchsorted(unit_ends, u_clamped, side='right').astype(jnp.int32)
    ue = jnp.minimum(ue, N_EXPERTS - 1)
    j = u_clamped - unit_starts[ue]
    u_row = pad_starts[ue] + j * UNIT_ROWS
    u_rows = jnp.minimum(UNIT_ROWS, padded[ue] - j * UNIT_ROWS)
    u_chunks = jnp.where(u < n_units, u_rows // ROW_CHUNK, 0).astype(jnp.int32)
    u_real = jnp.clip(counts[ue] - j * UNIT_ROWS, 0, UNIT_ROWS)
    u_granules = jnp.where(u < n_units, (u_real + COMPUTE_GRANULE - 1) // COMPUTE_GRANULE, 0).astype(jnp.int32)
    n_used_rows = jnp.concatenate([pad_ends[-1:].astype(jnp.int32), u_granules])
    return dest.astype(jnp.int32), pad_fill, ue, u_row.astype(jnp.int32), u_chunks, n_used_rows, n_pad, max_units


CHUNK_PITCHED = ROW_CHUNK * PITCH
DISPATCH_ROWS = 256
FILL_SLOTS = ROW_CHUNK // 2


def _dispatch_kernel(fill_ref, dest_ref, dest_prev_ref, x1_ref, xs_hbm, stage_a, stage_b, zeros, sem_a, sem_b,
                     sem_z):
    tm = DISPATCH_ROWS
    s = pl.program_id(0)
    last = pl.num_programs(0) - 1

    def slot_copy(table, table_off, t, k, stage, sem):
        dst = xs_hbm.at[pl.ds(table[(table_off + t) * TOP_K + k] * PITCH, PITCH), :]
        return pltpu.make_async_copy(stage.at[pl.ds(t * PITCH, PITCH), :], dst, sem)

    def wait_tile(stage, sem):
        for _ in range(TOP_K):
            pltpu.make_async_copy(stage, xs_hbm.at[pl.ds(0, tm * PITCH), :], sem).wait()

    def start_copies(table, table_off, stage, sem):
        per_piece = tm // SLAB

        def before_piece(j):
            for t in range(j * per_piece, (j + 1) * per_piece):
                for k in range(TOP_K):
                    slot_copy(table, table_off, t, k, stage, sem).start(priority=k % 2)

        return before_piece

    def pack(row0, stage, before_piece=None):
        _store_slabs(stage, x1_ref[pl.ds(row0, tm), :], before_piece)

    @pl.when(s == 0)
    def _():
        pack(0, stage_a)

    @pl.when(s > 0)
    def _():
        wait_tile(stage_a, sem_a)
        pack(0, stage_a, start_copies(dest_prev_ref, tm, stage_b, sem_b))
        wait_tile(stage_b, sem_b)

    pack(tm, stage_b, start_copies(dest_ref, 0, stage_a, sem_a))

    @pl.when(s == last)
    def _():
        def issue(t, c):
            for k in range(TOP_K):
                slot_copy(dest_ref, tm, t, k, stage_b, sem_b).start(priority=k % 2)
            return c

        lax.fori_loop(0, tm, issue, 0, unroll=2)
        wait_tile(stage_a, sem_a)
        wait_tile(stage_b, sem_b)

        zeros[...] = jnp.zeros(zeros.shape, u32)

        def fill(slot, n_slots):
            return pltpu.make_async_copy(zeros.at[pl.ds(0, n_slots * PITCH), :],
                                         xs_hbm.at[pl.ds(slot * PITCH, n_slots * PITCH), :], sem_z)

        def expert_pad(e, start):
            slot = fill_ref[e]
            n = fill_ref[N_EXPERTS + e]
            piece = FILL_SLOTS
            while piece >= 1:
                @pl.when((n // piece) % 2 == 1)
                def _(slot=slot, piece=piece):
                    if start:
                        fill(slot, piece).start()
                    else:
                        fill(slot, piece).wait()

                slot = slot + ((n // piece) % 2) * piece
                piece //= 2

        def tail(i, start):
            slot = fill_ref[2 * N_EXPERTS] + i * FILL_SLOTS
            if start:
                fill(slot, FILL_SLOTS).start()
            else:
                fill(slot, FILL_SLOTS).wait()

        n_tail = (xs_hbm.shape[0] // PITCH - fill_ref[2 * N_EXPERTS]) // FILL_SLOTS
        for start in (True, False):
            lax.fori_loop(0, N_EXPERTS, lambda e, c, start=start: (expert_pad(e, start), c)[1], 0)
            lax.fori_loop(0, n_tail, lambda i, c, start=start: (tail(i, start), c)[1], 0)


def _dispatch(x1, dest, pad_fill, n_pad):
    t, d = x1.shape
    tm = DISPATCH_ROWS
    n_steps = t // (2 * tm)
    table = lambda back: pl.BlockSpec((2 * tm * TOP_K,), lambda i, fill: (jnp.maximum(i - back, 0),),
                                      memory_space=pltpu.SMEM)
    dest_flat = dest.reshape(-1)
    stage = pltpu.VMEM((tm * PITCH, LANES), u32)
    return pl.pallas_call(
        _dispatch_kernel,
        grid_spec=pltpu.PrefetchScalarGridSpec(
            num_scalar_prefetch=1,
            grid=(n_steps,),
            in_specs=[table(0), table(1), pl.BlockSpec((2 * tm, d), lambda i, fill: (i, 0))],
            out_specs=pl.BlockSpec(memory_space=pl.ANY),
            scratch_shapes=[stage, stage, pltpu.VMEM((FILL_SLOTS * PITCH, LANES), u32),
                            pltpu.SemaphoreType.DMA(()), pltpu.SemaphoreType.DMA(()), pltpu.SemaphoreType.DMA(())],
        ),
        out_shape=jax.ShapeDtypeStruct((n_pad * PITCH, LANES), u32),
        compiler_params=_cparams(("arbitrary",)),
        name="moe_dispatch",
    )(pad_fill, dest_flat, dest_flat, x1)


OUT_SLOTS = 4
COMPUTE_GRANULE = ROW_CHUNK // 2


def _moe_kernel(ue_ref, urow_ref, uchunks_ref, used_ref, xs_hbm, wg_ref, bg_ref, wu_ref, bu_ref, wd_ref,
                bd_ref, y_hbm, xbuf, yacc, gstage_a, gstage_b, ostage, flags, in_sem, out_sem):
    u = pl.program_id(0)
    f = pl.program_id(1)
    n_u = pl.num_programs(0)
    n_f = pl.num_programs(1)
    n_chunks = uchunks_ref[u]
    row0 = urow_ref[u]
    cur = u % 2
    nxt = 1 - cur
    u_next = jnp.minimum(u + 1, n_u - 1)
    n_next = jnp.where(u + 1 < n_u, uchunks_ref[u_next], 0)
    row_next = urow_ref[u_next]
    first_step = jnp.logical_and(u == 0, f == 0)
    PENDING, PEND_CHUNK, PEND_SLOT, OUT_BUSY = 0, 1, 2, 3

    def rows(c, size=ROW_CHUNK):
        return pl.ds(pl.multiple_of(c * size, size), size)

    stages = (gstage_a, gstage_b)

    def chunk_copy(unit_row, c, parity):
        src = xs_hbm.at[pl.ds(pl.multiple_of((unit_row + c * ROW_CHUNK) * PITCH, CHUNK_PITCHED), CHUNK_PITCHED), :]
        return pltpu.make_async_copy(src, stages[parity], in_sem.at[parity])

    def unpack(c, xslot, parity):
        for j in range(SLAB):
            lo, hi = _slab_cols(stages[parity], j, ROW_CHUNK)
            xbuf[xslot, rows(c), j * LANES:(j + 1) * LANES] = lo.astype(bf16)
            xbuf[xslot, rows(c), HALF + j * LANES:HALF + (j + 1) * LANES] = hi.astype(bf16)

    def out_copy(c, slot):
        dst = y_hbm.at[pl.ds(pl.multiple_of((row0 + c * ROW_CHUNK) * PITCH, CHUNK_PITCHED), CHUNK_PITCHED), :]
        return pltpu.make_async_copy(ostage.at[slot], dst, out_sem.at[slot])

    def out_wait(slot):
        @pl.when(flags[OUT_BUSY + slot] == 1)
        def _():
            pltpu.make_async_copy(ostage.at[slot], y_hbm.at[pl.ds(0, CHUNK_PITCHED), :], out_sem.at[slot]).wait()
            flags[OUT_BUSY + slot] = 0

    @pl.when(first_step)
    def _():
        for i in range(OUT_BUSY + OUT_SLOTS):
            flags[i] = 0

        def clear(c, carry):
            yacc[rows(c), :] = jnp.zeros((ROW_CHUNK, D_MODEL), f32)
            return carry

        lax.fori_loop(0, UNIT_ROWS // ROW_CHUNK, clear, 0)

        def load(c, carry):
            first = chunk_copy(row0, c, 0)
            first.start()
            first.wait()
            unpack(c, cur, 0)
            return carry

        lax.fori_loop(0, n_chunks, load, 0)

    prefetching = f < n_next
    pending = flags[PENDING] == 1
    done_chunk = flags[PEND_CHUNK]
    done_slot = flags[PEND_SLOT]
    same_buffer = jnp.logical_and(pending, done_chunk % 2 == f % 2)
    for late in (False, True):
        for parity in (0, 1):
            starts_now = jnp.logical_and(prefetching, same_buffer if late else jnp.logical_not(same_buffer))

            @pl.when(jnp.logical_and(starts_now, f % 2 == parity))
            def _():
                chunk_copy(row_next, f, parity).start()

        if not late:
            for parity in (0, 1):
                @pl.when(jnp.logical_and(pending, done_chunk % 2 == parity))
                def _():
                    chunk_copy(row_next, done_chunk, parity).wait()
                    unpack(done_chunk, done_slot, parity)

    flags[PENDING] = prefetching.astype(jnp.int32)

    @pl.when(prefetching)
    def _():
        flags[PEND_CHUNK] = f
        flags[PEND_SLOT] = nxt

    n_granules = used_ref[1 + u]
    block_sizes = (8, 4, 2, 1)

    @pl.when(n_chunks > 0)
    def _():
        bg = bg_ref[...]
        bu = bu_ref[...]
        mm = lambda a, w: lax.dot_general(a, w, (((1,), (0,)), ((), ())), preferred_element_type=f32)

        def mlp(sl, first):
            xc = xbuf[cur, sl, :]
            gt = jnp.minimum(mm(xc, wg_ref[...]) + bg, SWIGLU_LIMIT)
            up = jnp.clip(mm(xc, wu_ref[...]) + bu, -SWIGLU_LIMIT, SWIGLU_LIMIT)
            hid = (up + 1.0) * gt * jax.nn.sigmoid(SWIGLU_ALPHA * gt)
            part = mm(hid.astype(bf16), wd_ref[...])
            if first:
                yacc[sl, :] = part
            else:
                yacc[sl, :] += part

        def blocks(first):
            big = block_sizes[0] * COMPUTE_GRANULE
            n_big = n_granules // block_sizes[0]

            def big_block(c, carry):
                mlp(rows(c, big), first)
                return carry

            lax.fori_loop(0, n_big, big_block, 0)
            done = n_big * block_sizes[0]
            for size in block_sizes[1:]:
                present = (n_granules // size) % 2
                m = size * COMPUTE_GRANULE

                @pl.when(present == 1)
                def _(done=done, m=m):
                    mlp(pl.ds(pl.multiple_of(done * COMPUTE_GRANULE, m), m), first)

                done = done + present * size

        @pl.when(f == 0)
        def _():
            blocks(True)

        @pl.when(f > 0)
        def _():
            blocks(False)

    @pl.when(jnp.logical_and(f == n_f - 1, n_chunks > 0))
    def _():
        bd = bd_ref[...]

        def store(c, carry):
            slot = c % OUT_SLOTS
            out_wait(slot)
            _store_slabs(ostage.at[slot], yacc[rows(c), :] + bd)
            out_copy(c, slot).start()
            flags[OUT_BUSY + slot] = 1
            return carry

        lax.fori_loop(0, n_chunks, store, 0)

    @pl.when(jnp.logical_and(u == n_u - 1, f == n_f - 1))
    def _():
        for s in range(OUT_SLOTS):
            out_wait(s)
        first = used_ref[0] // ROW_CHUNK
        last = y_hbm.shape[0] // CHUNK_PITCHED
        ostage[0] = jnp.zeros((CHUNK_PITCHED, LANES), u32)

        def tail_copy(c):
            dst = y_hbm.at[pl.ds(pl.multiple_of(c * CHUNK_PITCHED, CHUNK_PITCHED), CHUNK_PITCHED), :]
            return pltpu.make_async_copy(ostage.at[0], dst, out_sem.at[0])

        def start(c, carry):
            tail_copy(c).start()
            return carry

        def wait(c, carry):
            tail_copy(c).wait()
            return carry

        lax.fori_loop(first, last, start, 0)
        lax.fori_loop(first, last, wait, 0)


def _moe_experts(xs, ue, u_row, u_chunks, n_used_rows, n_pad, max_units, w_gate, b_gate, w_up, b_up, w_down,
                 b_down):
    e, d, dff = w_gate.shape
    n_f = dff // F_TILE
    assert n_f >= UNIT_ROWS // ROW_CHUNK, "one chunk of the next unit is loaded per f-tile step"
    f_idx = lambda u, f, uc: jnp.where(uc[u] > 0, f, n_f - 1)
    col_w = pl.BlockSpec((None, d, F_TILE), lambda u, f, ue, ur, uc, used: (ue[u], 0, f_idx(u, f, uc)))
    col_b = pl.BlockSpec((None, 1, F_TILE), lambda u, f, ue, ur, uc, used: (ue[u], 0, f_idx(u, f, uc)))
    return pl.pallas_call(
        _moe_kernel,
        grid_spec=pltpu.PrefetchScalarGridSpec(
            num_scalar_prefetch=4,
            grid=(max_units, n_f),
            in_specs=[pl.BlockSpec(memory_space=pl.ANY),
                      col_w, col_b, col_w, col_b,
                      pl.BlockSpec((None, F_TILE, d), lambda u, f, ue, ur, uc, used: (ue[u], f_idx(u, f, uc), 0)),
                      pl.BlockSpec((None, 1, d), lambda u, f, ue, ur, uc, used: (ue[u], 0, 0))],
            out_specs=pl.BlockSpec(memory_space=pl.ANY),
            scratch_shapes=[pltpu.VMEM((2, UNIT_ROWS, d), bf16),
                            pltpu.VMEM((UNIT_ROWS, d), f32),
                            pltpu.VMEM((CHUNK_PITCHED, LANES), u32),
                            pltpu.VMEM((CHUNK_PITCHED, LANES), u32),
                            pltpu.VMEM((OUT_SLOTS, CHUNK_PITCHED, LANES), u32),
                            pltpu.SMEM((3 + OUT_SLOTS,), jnp.int32),
                            pltpu.SemaphoreType.DMA((2,)),
                            pltpu.SemaphoreType.DMA((OUT_SLOTS,))],
        ),
        out_shape=jax.ShapeDtypeStruct((n_pad * PITCH, LANES), u32),
        compiler_params=_cparams(("arbitrary", "arbitrary")),
        name="moe_experts",
    )(ue, u_row, u_chunks, n_used_rows, xs, w_gate, b_gate.reshape(e, 1, dff), w_up,
      b_up.reshape(e, 1, dff), w_down, b_down.reshape(e, 1, d))


COMBINE_ROWS = 256
COMBINE_GROUPS = 2


def _combine_kernel(dest_ref, dest_next_ref, y_hbm, gate_ref, x1_ref, p_ref, lnw_ref, lnb_ref, wp_ref, pnw_ref,
                    wg_ref, o_ref, ybuf_a, ybuf_b, fsum_ref, sem_a, sem_b):
    tm = COMBINE_ROWS
    s = pl.program_id(0)

    def row_copy(table, table_off, t, k, ybuf, sem):
        src = y_hbm.at[pl.ds(table[(table_off + t) * TOP_K + k] * PITCH, SLAB), :]
        return pltpu.make_async_copy(src, ybuf.at[k, pl.ds(t * PITCH, SLAB), :], sem)

    def wait_tile(ybuf, sem):
        for k in range(TOP_K):
            pltpu.make_async_copy(y_hbm.at[pl.ds(0, tm * SLAB), :], ybuf.at[k, pl.ds(0, tm * SLAB), :], sem).wait()

    @pl.when(s == 0)
    def _():
        def issue(t, c):
            for k in range(TOP_K):
                row_copy(dest_ref, 0, t, k, ybuf_a, sem_a).start(priority=k % 2)
            return c

        lax.fori_loop(0, tm, issue, 0, unroll=2)

    def finish_tile(row0, ybuf, next_table, next_off, next_buf, next_sem):
        tok = pl.ds(row0, tm)
        pieces = SLAB * TOP_K
        per_piece = tm // pieces
        e = jnp.dot(p_ref[tok, :].astype(bf16), wp_ref[...], preferred_element_type=f32)
        e = e * lax.rsqrt(jnp.mean(e * e, axis=-1, keepdims=True) + LN_EPS) * pnw_ref[...]
        gates = gate_ref[tok, :]
        for j in range(SLAB):
            acc_lo, acc_hi = None, None
            for k in range(TOP_K):
                piece = j * TOP_K + k
                for t in range(piece * per_piece, (piece + 1) * per_piece):
                    for kk in range(TOP_K):
                        row_copy(next_table, next_off, t, kk, next_buf, next_sem).start(priority=kk % 2)
                lo, hi = _slab_cols(ybuf.at[k], j, tm)
                g = gates[:, k:k + 1]
                acc_lo = g * lo if k == 0 else acc_lo + g * lo
                acc_hi = g * hi if k == 0 else acc_hi + g * hi
            fsum_ref[:, j * LANES:(j + 1) * LANES] = acc_lo
            fsum_ref[:, HALF + j * LANES:HALF + (j + 1) * LANES] = acc_hi
        x2 = _layer_norm(DN_ALPHA * x1_ref[tok, :] + fsum_ref[...], lnw_ref[...], lnb_ref[...])
        gate = jax.nn.sigmoid(jnp.dot(x2.astype(bf16), wg_ref[...], preferred_element_type=f32))
        o_ref[tok, :] = x2 + gate * e

    wait_tile(ybuf_a, sem_a)
    finish_tile(0, ybuf_a, dest_ref, tm, ybuf_b, sem_b)
    wait_tile(ybuf_b, sem_b)
    finish_tile(tm, ybuf_b, dest_next_ref, 0, ybuf_a, sem_a)

    @pl.when(s == pl.num_programs(0) - 1)
    def _():
        wait_tile(ybuf_a, sem_a)


def _combine(y, dest, gates, x1, p2d, ln_w, ln_b, wp_bf, ple_norm_w, wg_bf):
    t, d = x1.shape
    tm = COMBINE_ROWS
    n_steps = t // (2 * tm)
    row = lambda c: pl.BlockSpec((2 * tm, c), lambda i: (i, 0))
    full = lambda r, c: pl.BlockSpec((r, c), lambda i: (0, 0))
    table = lambda nxt: pl.BlockSpec((2 * tm * TOP_K,), lambda i: (jnp.minimum(i + nxt, n_steps - 1),),
                                     memory_space=pltpu.SMEM)
    dest_flat = dest.reshape(-1)
    return pl.pallas_call(
        _combine_kernel,
        grid=(n_steps,),
        in_specs=[table(0), table(1), pl.BlockSpec(memory_space=pl.ANY),
                  row(TOP_K), row(d), row(PLE_DIM), full(1, d), full(1, d), full(PLE_DIM, d), full(1, d),
                  full(d, d)],
        out_specs=row(d),
        out_shape=jax.ShapeDtypeStruct((t, d), f32),
        scratch_shapes=[pltpu.VMEM((TOP_K, tm * PITCH, LANES), u32), pltpu.VMEM((TOP_K, tm * PITCH, LANES), u32),
                        pltpu.VMEM((tm, d), f32), pltpu.SemaphoreType.DMA(()), pltpu.SemaphoreType.DMA(())],
        compiler_params=_cparams(("arbitrary",)),
        name="combine_ln2_ple",
    )(dest_flat, dest_flat, y, gates, x1, p2d, ln_w.reshape(1, d), ln_b.reshape(1, d), wp_bf,
      ple_norm_w.reshape(1, d), wg_bf)


def _layer(h, p_i, w_in, ret_gn_w, conv_w, conv_b, lru_wa, lru_ba, lru_wx, lru_bx, lru_lam, w_out,
           ln1_w, ln1_b, w_router, b_router, w_gate, b_gate, w_up, b_up, w_down, b_down,
           ln2_w, ln2_b, w_ple_proj, ple_norm_w, w_ple_gate):
    b, s, d = h.shape
    t = b * s
    x2d = h.reshape(t, d)
    proj = _in_proj(x2d, w_in).reshape(b, s, IN_COLS)
    ret_out = _retention(proj, ret_gn_w)
    lru_out = _lru(proj, conv_w, conv_b, lru_wa, lru_ba, lru_wx, lru_bx, lru_lam)
    x1, top_e, gates = _out_router(ret_out.reshape(t, RET_WIDTH), lru_out.reshape(t, LRU_WIDTH),
                                   w_out.astype(bf16), x2d, ln1_w, ln1_b, w_router, b_router)
    dest, pad_fill, ue, u_row, u_chunks, n_used_rows, n_pad, max_units = _routing_tables(top_e, t)
    xs = _dispatch(x1, dest, pad_fill, n_pad)
    y = _moe_experts(xs, ue, u_row, u_chunks, n_used_rows, n_pad, max_units, w_gate, b_gate, w_up, b_up, w_down,
                     b_down)
    out = _combine(y, dest, gates.T, x1, p_i.reshape(t, PLE_DIM), ln2_w, ln2_b, w_ple_proj.astype(bf16),
                   ple_norm_w, w_ple_gate.astype(bf16))
    return out.reshape(b, s, d)


def kernel(x, p, w_in, ret_gn_w, conv_w, conv_b, lru_wa, lru_ba, lru_wx, lru_bx, lru_lam, w_out, ln1_w, ln1_b,
           w_router, b_router, w_gate, b_gate, w_up, b_up, w_down, b_down, ln2_w, ln2_b, w_ple_proj, ple_norm_w,
           w_ple_gate):
    h = x.astype(f32)
    for i in range(w_in.shape[0]):
        h = _layer(h, p[i], w_in[i], ret_gn_w[i], conv_w[i], conv_b[i], lru_wa[i], lru_ba[i], lru_wx[i],
                   lru_bx[i], lru_lam[i], w_out[i], ln1_w[i], ln1_b[i], w_router[i], b_router[i], w_gate[i],
                   b_gate[i], w_up[i], b_up[i], w_down[i], b_down[i], ln2_w[i], ln2_b[i], w_ple_proj[i],
                   ple_norm_w[i], w_ple_gate[i])
    return h.astype(x.dtype)
```

```python
import math

import jax
import jax.numpy as jnp
from jax import lax
from jax.experimental import pallas as pl
from jax.experimental.pallas import tpu as pltpu

D_MODEL = 2048
RET_HEAD_DIM = 128
RET_HEADS = 8
RET_WIDTH = RET_HEADS * RET_HEAD_DIM
LRU_WIDTH = D_MODEL - RET_WIDTH
LRU_BLOCKS = 8
LRU_BLOCK_DIM = LRU_WIDTH // LRU_BLOCKS
IN_COLS = 4 * RET_WIDTH + 2 * LRU_WIDTH
CONV_WIDTH = 4
LRU_C = 8.0
CHUNK = 128
ROPE_BASE = 10000.0
N_EXPERTS = 32
TOP_K = 4
SWIGLU_LIMIT = 7.0
SWIGLU_ALPHA = 1.702
PLE_DIM = 256
LN_EPS = 1e-5
DEPTH = 1
DN_ALPHA = (2.0 * DEPTH) ** 0.25

LANES = 128
SUBLANES = 8
VMEM_LIMIT = 60 * 1024 * 1024

ROW_CHUNK = 256
UNIT_ROWS = 2048
F_TILE = 256

f32 = jnp.float32
bf16 = jnp.bfloat16


def _cparams(sem):
    return pltpu.CompilerParams(dimension_semantics=sem, vmem_limit_bytes=VMEM_LIMIT)


def _in_proj_kernel(x_ref, w_ref, o_ref):
    o_ref[...] = lax.dot_general(x_ref[...].astype(bf16), w_ref[...], (((1,), (0,)), ((), ())),
                                 preferred_element_type=f32)


def _in_proj(x2d, w):
    t, d = x2d.shape
    n = w.shape[1]
    tm, tn = 512, 2048
    return pl.pallas_call(
        _in_proj_kernel,
        grid=(n // tn, t // tm),
        in_specs=[pl.BlockSpec((tm, d), lambda j, i: (i, 0)),
                  pl.BlockSpec((d, tn), lambda j, i: (0, j))],
        out_specs=pl.BlockSpec((tm, tn), lambda j, i: (i, j)),
        out_shape=jax.ShapeDtypeStruct((t, n), f32),
        compiler_params=_cparams(("parallel", "parallel")),
        name="in_proj",
    )(x2d, w)


def _retention_kernel(q_ref, k_ref, v_ref, g_ref, cos_ref, sin_ref, dec_ref, qd_ref, kd_ref, cd_ref, gnw_ref,
                      o_ref):
    s = q_ref.shape[0]
    n_chunks = s // CHUNK
    decay = dec_ref[...]
    q_dec = qd_ref[...]
    k_dec = kd_ref[...]
    c_dec = cd_ref[0:1, :]
    gnw = gnw_ref[...]
    k_scale = RET_HEAD_DIM ** -0.5

    def rope(xv, cos, sin):
        return xv * cos + pltpu.roll(xv, RET_HEAD_DIM // 2, axis=1) * sin

    def body(n, state):
        sl = pl.ds(pl.multiple_of(n * CHUNK, CHUNK), CHUNK)
        cos = cos_ref[sl, :]
        sin = sin_ref[sl, :]
        q = rope(q_ref[sl, :], cos, sin)
        k = rope(k_ref[sl, :], cos, sin) * k_scale
        vb = v_ref[sl, :].astype(bf16)
        scores = lax.dot_general(q.astype(bf16), k.astype(bf16), (((1,), (1,)), ((), ())),
                                 preferred_element_type=f32) * decay
        intra = jnp.dot(scores.astype(bf16), vb, preferred_element_type=f32)
        cross = jnp.dot((q * q_dec).astype(bf16), state.astype(bf16), preferred_element_type=f32)
        kv = lax.dot_general((k * k_dec).astype(bf16), vb, (((0,), (0,)), ((), ())),
                             preferred_element_type=f32)
        ret = intra + cross
        mu = jnp.mean(ret, axis=-1, keepdims=True)
        cen = ret - mu
        var = jnp.mean(cen * cen, axis=-1, keepdims=True)
        ret = cen * lax.rsqrt(var + LN_EPS) * gnw
        g = g_ref[sl, :]
        o_ref[sl, :] = (g * jax.nn.sigmoid(g) * ret).astype(o_ref.dtype)
        return c_dec * state + kv

    lax.fori_loop(0, n_chunks, body, jnp.zeros((RET_HEAD_DIM, RET_HEAD_DIM), f32), unroll=8)


def _retention_tables(s):
    h, d = RET_HEADS, RET_HEAD_DIM
    inv = ROPE_BASE ** (-jnp.arange(0, d, 2, dtype=f32) / d)
    ang = jnp.arange(s, dtype=f32)[:, None] * inv[None, :]
    cos = jnp.cos(ang)
    sin = jnp.sin(ang)
    cos_t = jnp.concatenate([cos, cos], axis=-1)
    sin_t = jnp.concatenate([-sin, sin], axis=-1)
    log_gamma = jnp.log1p(-jnp.exp2(-5.0 - jnp.arange(h, dtype=f32)))
    idx = jnp.arange(CHUNK, dtype=f32)
    diff = idx[:, None] - idx[None, :]
    decay = jnp.where((diff >= 0)[None], jnp.exp(jnp.maximum(diff, 0.0)[None] * log_gamma[:, None, None]), 0.0)
    q_dec = jnp.exp((idx[None, :] + 1.0) * log_gamma[:, None])
    k_dec = jnp.exp((CHUNK - 1.0 - idx)[None, :] * log_gamma[:, None])
    c_dec = jnp.exp(CHUNK * log_gamma)
    q_dec = jnp.broadcast_to(q_dec[:, :, None], (h, CHUNK, d))
    k_dec = jnp.broadcast_to(k_dec[:, :, None], (h, CHUNK, d))
    c_dec = jnp.broadcast_to(c_dec[:, None, None], (h, SUBLANES, d))
    return cos_t, sin_t, decay, q_dec, k_dec, c_dec


def _retention(proj, ret_gn_w):
    b, s, _ = proj.shape
    d = RET_HEAD_DIM
    cos_t, sin_t, decay, q_dec, k_dec, c_dec = _retention_tables(s)
    col = lambda off: pl.BlockSpec((None, s, d), lambda bi, hi: (bi, 0, off + hi))
    per_head = lambda r: pl.BlockSpec((None, r, d), lambda bi, hi: (hi, 0, 0))
    full = lambda shp: pl.BlockSpec(shp, lambda bi, hi: (0,) * len(shp))
    return pl.pallas_call(
        _retention_kernel,
        grid=(b, RET_HEADS),
        in_specs=[col(0), col(RET_HEADS), col(2 * RET_HEADS), col(3 * RET_HEADS),
                  full((s, d)), full((s, d)),
                  per_head(CHUNK), per_head(CHUNK), per_head(CHUNK), per_head(SUBLANES),
                  pl.BlockSpec((1, d), lambda bi, hi: (0, hi))],
        out_specs=pl.BlockSpec((None, s, d), lambda bi, hi: (bi, 0, hi)),
        out_shape=jax.ShapeDtypeStruct((b, s, RET_WIDTH), bf16),
        compiler_params=_cparams(("parallel", "parallel")),
        name="retention",
    )(proj, proj, proj, proj, cos_t, sin_t, decay, q_dec, k_dec, c_dec, ret_gn_w.reshape(1, RET_WIDTH))


def _gelu_tanh(x):
    return 0.5 * x * (1.0 + jnp.tanh(math.sqrt(2.0 / math.pi) * (x + 0.044715 * (x * x * x))))


def _lru_kernel(xr_ref, yg_ref, cw_ref, cb_ref, wa_ref, ba_ref, wx_ref, bx_ref, lam_ref, o_ref, a_ref, b_ref):
    s = xr_ref.shape[0]
    x = xr_ref[...]
    rows = lax.broadcasted_iota(jnp.int32, x.shape, 0)
    xc = cb_ref[...] + cw_ref[CONV_WIDTH - 1:CONV_WIDTH, :] * x
    for back in range(1, CONV_WIDTH):
        shifted = jnp.where(rows >= back, pltpu.roll(x, back, axis=0), 0.0)
        xc = xc + cw_ref[CONV_WIDTH - 1 - back:CONV_WIDTH - back, :] * shifted
    xcb = xc.astype(bf16)
    r = jax.nn.sigmoid(jnp.dot(xcb, wa_ref[...].astype(bf16), preferred_element_type=f32) + ba_ref[...])
    gi = jax.nn.sigmoid(jnp.dot(xcb, wx_ref[...].astype(bf16), preferred_element_type=f32) + bx_ref[...])
    lam = lam_ref[...]
    log_sig = jnp.minimum(lam, 0.0) - jnp.log1p(jnp.exp(-jnp.abs(lam)))
    log_a = LRU_C * r * log_sig
    a = jnp.exp(log_a)
    a_ref[...] = a
    b_ref[...] = jnp.sqrt(-jnp.tanh(log_a) * (a * a + 1.0)) * (gi * xc)

    row8 = lax.broadcasted_iota(jnp.int32, (SUBLANES, LANES), 0)

    def body(i, h_prev):
        sl = pl.ds(pl.multiple_of(i * SUBLANES, SUBLANES), SUBLANES)
        a8 = a_ref[sl, :]
        b8 = b_ref[sl, :]
        for sh in (1, 2, 4):
            a_sh = jnp.where(row8 >= sh, pltpu.roll(a8, sh, axis=0), 1.0)
            b_sh = jnp.where(row8 >= sh, pltpu.roll(b8, sh, axis=0), 0.0)
            b8 = a8 * b_sh + b8
            a8 = a8 * a_sh
        h8 = a8 * h_prev + b8
        o_ref[sl, :] = (_gelu_tanh(yg_ref[sl, :]) * h8).astype(o_ref.dtype)
        return h8[SUBLANES - 1:SUBLANES, :]

    lax.fori_loop(0, s // SUBLANES, body, jnp.zeros((1, LANES), f32), unroll=16)


def _lru(proj, conv_w, conv_b, wa, ba, wx, bx, lam):
    b, s, _ = proj.shape
    d = LRU_BLOCK_DIM
    xr_off = 4 * RET_WIDTH // d
    yg_off = xr_off + LRU_BLOCKS
    col = lambda off: pl.BlockSpec((None, s, d), lambda bi, ji: (bi, 0, off + ji))
    vec = lambda r: pl.BlockSpec((r, d), lambda bi, ji: (0, ji))
    blk = lambda r: pl.BlockSpec((None, r, d), lambda bi, ji: (ji, 0, 0))
    return pl.pallas_call(
        _lru_kernel,
        grid=(b, LRU_BLOCKS),
        in_specs=[col(xr_off), col(yg_off), vec(CONV_WIDTH), vec(1), blk(d), blk(1), blk(d), blk(1), vec(1)],
        out_specs=pl.BlockSpec((None, s, d), lambda bi, ji: (bi, 0, ji)),
        out_shape=jax.ShapeDtypeStruct((b, s, LRU_WIDTH), bf16),
        scratch_shapes=[pltpu.VMEM((s, d), f32), pltpu.VMEM((s, d), f32)],
        compiler_params=_cparams(("parallel", "parallel")),
        name="rg_lru",
    )(proj, proj, conv_w, conv_b.reshape(1, LRU_WIDTH), wa, ba.reshape(LRU_BLOCKS, 1, d), wx,
      bx.reshape(LRU_BLOCKS, 1, d), lam.reshape(1, LRU_WIDTH))


def _layer_norm(y, w, b):
    mu = jnp.mean(y, axis=-1, keepdims=True)
    cen = y - mu
    var = jnp.mean(cen * cen, axis=-1, keepdims=True)
    return cen * lax.rsqrt(var + LN_EPS) * w + b


HALF = D_MODEL // 2
SLAB = HALF // LANES
PITCH = SLAB + 4
u32 = jnp.uint32
HIGH_MASK = 0xFFFF0000


def _bf16_bits(v):
    return lax.bitcast_convert_type(v.astype(bf16).astype(f32), u32)


def _store_slabs(ref, val, before_piece=None):
    n = val.shape[0]
    for j in range(SLAB):
        if before_piece is not None:
            before_piece(j)
        lo = _bf16_bits(val[:, j * LANES:(j + 1) * LANES])
        hi = _bf16_bits(val[:, HALF + j * LANES:HALF + (j + 1) * LANES])
        ref[pl.ds(j, n, stride=PITCH), :] = hi | lax.shift_right_logical(lo, jnp.full_like(lo, 16))
    for j in range(SLAB, PITCH):
        ref[pl.ds(j, n, stride=PITCH), :] = jnp.zeros((n, LANES), u32)


def _slab_cols(ref, j, n):
    w = ref[pl.ds(j, n, stride=PITCH), :]
    lo = lax.bitcast_convert_type(lax.shift_left(w, jnp.full_like(w, 16)), f32)
    hi = lax.bitcast_convert_type(w & jnp.full_like(w, HIGH_MASK), f32)
    return lo, hi


def _split_bf16(v):
    hi = v.astype(bf16)
    lo = (v - hi.astype(f32)).astype(bf16)
    return hi, lo


def _out_router_kernel(ret_ref, lru_ref, wo_ref, x_ref, lnw_ref, lnb_ref, wr_ref, br_ref,
                       x1_ref, tope_ref, gate_ref):
    m = jnp.dot(ret_ref[...], wo_ref[0:RET_WIDTH, :], preferred_element_type=f32)
    m = m + jnp.dot(lru_ref[...], wo_ref[RET_WIDTH:D_MODEL, :], preferred_element_type=f32)
    x1 = _layer_norm(DN_ALPHA * x_ref[...] + m, lnw_ref[...], lnb_ref[...])
    x1_ref[...] = x1
    xh, xl = _split_bf16(x1)
    wh, wl = _split_bf16(wr_ref[...])
    nt = lambda a, b: lax.dot_general(a, b, (((1,), (1,)), ((), ())), preferred_element_type=f32)
    logits = nt(wh, xh) + nt(wh, xl) + nt(wl, xh) + br_ref[...]
    tm = logits.shape[1]
    expert = lax.broadcasted_iota(jnp.int32, logits.shape, 0)
    choice = lax.broadcasted_iota(jnp.int32, (TOP_K, tm), 0)
    top_e = jnp.zeros((TOP_K, tm), jnp.int32)
    top_v = jnp.zeros((TOP_K, tm), f32)
    cur = logits
    for kk in range(TOP_K):
        mx = jnp.max(cur, axis=0, keepdims=True)
        idx = jnp.min(jnp.where(cur == mx, expert, N_EXPERTS), axis=0, keepdims=True)
        top_e = jnp.where(choice == kk, idx, top_e)
        top_v = jnp.where(choice == kk, mx, top_v)
        cur = jnp.where(expert == idx, -jnp.inf, cur)
    ex = jnp.exp(top_v - top_v[0:1, :])
    gate_ref[...] = ex / jnp.sum(ex, axis=0, keepdims=True)
    tope_ref[...] = top_e


def _out_router(ret_out, lru_out, wo_bf, x2d, ln_w, ln_b, w_router, b_router):
    t, d = x2d.shape
    tm = 512
    row = lambda c: pl.BlockSpec((tm, c), lambda i: (i, 0))
    full = lambda r, c: pl.BlockSpec((r, c), lambda i: (0, 0))
    per_choice = pl.BlockSpec((TOP_K, tm), lambda i: (0, i))
    return pl.pallas_call(
        _out_router_kernel,
        grid=(t // tm,),
        in_specs=[row(RET_WIDTH), row(LRU_WIDTH), full(d, d), row(d), full(1, d), full(1, d),
                  full(N_EXPERTS, d), full(N_EXPERTS, 1)],
        out_specs=[row(d), per_choice, per_choice],
        out_shape=[jax.ShapeDtypeStruct((t, d), f32), jax.ShapeDtypeStruct((TOP_K, t), jnp.int32),
                   jax.ShapeDtypeStruct((TOP_K, t), f32)],
        compiler_params=_cparams(("parallel",)),
        name="out_proj_ln1_router",
    )(ret_out, lru_out, wo_bf, x2d, ln_w.reshape(1, d), ln_b.reshape(1, d), w_router.T,
      b_router.reshape(N_EXPERTS, 1))


def _routing_tables(top_e, t):
    n_pad = t * TOP_K + N_EXPERTS * ROW_CHUNK
    max_units = N_EXPERTS + (t * TOP_K) // UNIT_ROWS
    sel = (top_e[:, None, :] == jnp.arange(N_EXPERTS, dtype=jnp.int32)[None, :, None]).any(axis=0)
    sel = sel.astype(jnp.int32)
    counts = jnp.sum(sel, axis=1)
    rank = jnp.cumsum(sel, axis=1) - sel
    padded = (counts + ROW_CHUNK - 1) // ROW_CHUNK * ROW_CHUNK
    pad_ends = jnp.cumsum(padded)
    pad_starts = pad_ends - padded
    dest_dense = pad_starts[:, None] + rank
    dest = jnp.take_along_axis(dest_dense, top_e, axis=0).T
    pad_fill = jnp.concatenate([pad_starts + counts, padded - counts, pad_ends[-1:]]).astype(jnp.int32)
    units_per_e = (padded + UNIT_ROWS - 1) // UNIT_ROWS
    unit_ends = jnp.cumsum(units_per_e)
    unit_starts = unit_ends - units_per_e
    n_units = unit_ends[-1]
    u = jnp.arange(max_units, dtype=jnp.int32)
    u_clamped = jnp.minimum(u, n_units - 1)
    ue = jnp.searchsorted(unit_ends, u_clamped, side='right').astype(jnp.int32)
    ue = jnp.minimum(ue, N_EXPERTS - 1)
    j = u_clamped - unit_starts[ue]
    u_row = pad_starts[ue] + j * UNIT_ROWS
    u_rows = jnp.minimum(UNIT_ROWS, padded[ue] - j * UNIT_ROWS)
    u_chunks = jnp.where(u < n_units, u_rows // ROW_CHUNK, 0).astype(jnp.int32)
    u_real = jnp.clip(counts[ue] - j * UNIT_ROWS, 0, UNIT_ROWS)
    u_granules = jnp.where(u < n_units, (u_real + COMPUTE_GRANULE - 1) // COMPUTE_GRANULE, 0).astype(jnp.int32)
    n_used_rows = jnp.concatenate([pad_ends[-1:].astype(jnp.int32), u_granules])
    return dest.astype(jnp.int32), pad_fill, ue, u_row.astype(jnp.int32), u_chunks, n_used_rows, n_pad, max_units


CHUNK_PITCHED = ROW_CHUNK * PITCH
DISPATCH_ROWS = 256
FILL_SLOTS = ROW_CHUNK // 2


def _dispatch_kernel(fill_ref, dest_ref, dest_prev_ref, x1_ref, xs_hbm, stage_a, stage_b, zeros, sem_a, sem_b,
                     sem_z):
    tm = DISPATCH_ROWS
    s = pl.program_id(0)
    last = pl.num_programs(0) - 1

    def slot_copy(table, table_off, t, k, stage, sem):
        dst = xs_hbm.at[pl.ds(table[(table_off + t) * TOP_K + k] * PITCH, PITCH), :]
        return pltpu.make_async_copy(stage.at[pl.ds(t * PITCH, PITCH), :], dst, sem)

    def wait_tile(stage, sem):
        for _ in range(TOP_K):
            pltpu.make_async_copy(stage, xs_hbm.at[pl.ds(0, tm * PITCH), :], sem).wait()

    def start_copies(table, table_off, stage, sem):
        per_piece = tm // SLAB

        def before_piece(j):
            for t in range(j * per_piece, (j + 1) * per_piece):
                for k in range(TOP_K):
                    slot_copy(table, table_off, t, k, stage, sem).start(priority=k % 2)

        return before_piece

    def pack(row0, stage, before_piece=None):
        _store_slabs(stage, x1_ref[pl.ds(row0, tm), :], before_piece)

    @pl.when(s == 0)
    def _():
        pack(0, stage_a)

    @pl.when(s > 0)
    def _():
        wait_tile(stage_a, sem_a)
        pack(0, stage_a, start_copies(dest_prev_ref, tm, stage_b, sem_b))
        wait_tile(stage_b, sem_b)

    pack(tm, stage_b, start_copies(dest_ref, 0, stage_a, sem_a))

    @pl.when(s == last)
    def _():
        def issue(t, c):
            for k in range(TOP_K):
                slot_copy(dest_ref, tm, t, k, stage_b, sem_b).start(priority=k % 2)
            return c

        lax.fori_loop(0, tm, issue, 0, unroll=2)
        wait_tile(stage_a, sem_a)
        wait_tile(stage_b, sem_b)

        zeros[...] = jnp.zeros(zeros.shape, u32)

        def fill(slot, n_slots):
            return pltpu.make_async_copy(zeros.at[pl.ds(0, n_slots * PITCH), :],
                                         xs_hbm.at[pl.ds(slot * PITCH, n_slots * PITCH), :], sem_z)

        def expert_pad(e, start):
            slot = fill_ref[e]
            n = fill_ref[N_EXPERTS + e]
            piece = FILL_SLOTS
            while piece >= 1:
                @pl.when((n // piece) % 2 == 1)
                def _(slot=slot, piece=piece):
                    if start:
                        fill(slot, piece).start()
                    else:
                        fill(slot, piece).wait()

                slot = slot + ((n // piece) % 2) * piece
                piece //= 2

        def tail(i, start):
            slot = fill_ref[2 * N_EXPERTS] + i * FILL_SLOTS
            if start:
                fill(slot, FILL_SLOTS).start()
            else:
                fill(slot, FILL_SLOTS).wait()

        n_tail = (xs_hbm.shape[0] // PITCH - fill_ref[2 * N_EXPERTS]) // FILL_SLOTS
        for start in (True, False):
            lax.fori_loop(0, N_EXPERTS, lambda e, c, start=start: (expert_pad(e, start), c)[1], 0)
            lax.fori_loop(0, n_tail, lambda i, c, start=start: (tail(i, start), c)[1], 0)


def _dispatch(x1, dest, pad_fill, n_pad):
    t, d = x1.shape
    tm = DISPATCH_ROWS
    n_steps = t // (2 * tm)
    table = lambda back: pl.BlockSpec((2 * tm * TOP_K,), lambda i, fill: (jnp.maximum(i - back, 0),),
                                      memory_space=pltpu.SMEM)
    dest_flat = dest.reshape(-1)
    stage = pltpu.VMEM((tm * PITCH, LANES), u32)
    return pl.pallas_call(
        _dispatch_kernel,
        grid_spec=pltpu.PrefetchScalarGridSpec(
            num_scalar_prefetch=1,
            grid=(n_steps,),
            in_specs=[table(0), table(1), pl.BlockSpec((2 * tm, d), lambda i, fill: (i, 0))],
            out_specs=pl.BlockSpec(memory_space=pl.ANY),
            scratch_shapes=[stage, stage, pltpu.VMEM((FILL_SLOTS * PITCH, LANES), u32),
                            pltpu.SemaphoreType.DMA(()), pltpu.SemaphoreType.DMA(()), pltpu.SemaphoreType.DMA(())],
        ),
        out_shape=jax.ShapeDtypeStruct((n_pad * PITCH, LANES), u32),
        compiler_params=_cparams(("arbitrary",)),
        name="moe_dispatch",
    )(pad_fill, dest_flat, dest_flat, x1)


OUT_SLOTS = 4
COMPUTE_GRANULE = ROW_CHUNK // 2


def _moe_kernel(ue_ref, urow_ref, uchunks_ref, used_ref, xs_hbm, wg_ref, bg_ref, wu_ref, bu_ref, wd_ref,
                bd_ref, y_hbm, xbuf, yacc, gstage_a, gstage_b, ostage, flags, in_sem, out_sem):
    u = pl.program_id(0)
    f = pl.program_id(1)
    n_u = pl.num_programs(0)
    n_f = pl.num_programs(1)
    n_chunks = uchunks_ref[u]
    row0 = urow_ref[u]
    cur = u % 2
    nxt = 1 - cur
    u_next = jnp.minimum(u + 1, n_u - 1)
    n_next = jnp.where(u + 1 < n_u, uchunks_ref[u_next], 0)
    row_next = urow_ref[u_next]
    first_step = jnp.logical_and(u == 0, f == 0)
    PENDING, PEND_CHUNK, PEND_SLOT, OUT_BUSY = 0, 1, 2, 3

    def rows(c, size=ROW_CHUNK):
        return pl.ds(pl.multiple_of(c * size, size), size)

    stages = (gstage_a, gstage_b)

    def chunk_copy(unit_row, c, parity):
        src = xs_hbm.at[pl.ds(pl.multiple_of((unit_row + c * ROW_CHUNK) * PITCH, CHUNK_PITCHED), CHUNK_PITCHED), :]
        return pltpu.make_async_copy(src, stages[parity], in_sem.at[parity])

    def unpack(c, xslot, parity):
        for j in range(SLAB):
            lo, hi = _slab_cols(stages[parity], j, ROW_CHUNK)
            xbuf[xslot, rows(c), j * LANES:(j + 1) * LANES] = lo.astype(bf16)
            xbuf[xslot, rows(c), HALF + j * LANES:HALF + (j + 1) * LANES] = hi.astype(bf16)

    def out_copy(c, slot):
        dst = y_hbm.at[pl.ds(pl.multiple_of((row0 + c * ROW_CHUNK) * PITCH, CHUNK_PITCHED), CHUNK_PITCHED), :]
        return pltpu.make_async_copy(ostage.at[slot], dst, out_sem.at[slot])

    def out_wait(slot):
        @pl.when(flags[OUT_BUSY + slot] == 1)
        def _():
            pltpu.make_async_copy(ostage.at[slot], y_hbm.at[pl.ds(0, CHUNK_PITCHED), :], out_sem.at[slot]).wait()
            flags[OUT_BUSY + slot] = 0

    @pl.when(first_step)
    def _():
        for i in range(OUT_BUSY + OUT_SLOTS):
            flags[i] = 0

        def clear(c, carry):
            yacc[rows(c), :] = jnp.zeros((ROW_CHUNK, D_MODEL), f32)
            return carry

        lax.fori_loop(0, UNIT_ROWS // ROW_CHUNK, clear, 0)

        def load(c, carry):
            first = chunk_copy(row0, c, 0)
            first.start()
            first.wait()
            unpack(c, cur, 0)
            return carry

        lax.fori_loop(0, n_chunks, load, 0)

    prefetching = f < n_next
    pending = flags[PENDING] == 1
    done_chunk = flags[PEND_CHUNK]
    done_slot = flags[PEND_SLOT]
    same_buffer = jnp.logical_and(pending, done_chunk % 2 == f % 2)
    for late in (False, True):
        for parity in (0, 1):
            starts_now = jnp.logical_and(prefetching, same_buffer if late else jnp.logical_not(same_buffer))

            @pl.when(jnp.logical_and(starts_now, f % 2 == parity))
            def _():
                chunk_copy(row_next, f, parity).start()

        if not late:
            for parity in (0, 1):
                @pl.when(jnp.logical_and(pending, done_chunk % 2 == parity))
                def _():
                    chunk_copy(row_next, done_chunk, parity).wait()
                    unpack(done_chunk, done_slot, parity)

    flags[PENDING] = prefetching.astype(jnp.int32)

    @pl.when(prefetching)
    def _():
        flags[PEND_CHUNK] = f
        flags[PEND_SLOT] = nxt

    n_granules = used_ref[1 + u]
    block_sizes = (8, 4, 2, 1)

    @pl.when(n_chunks > 0)
    def _():
        bias_row = pl.ds(ue_ref[u] * n_f + f, 1)
        bg = bg_ref[bias_row, :]
        bu = bu_ref[bias_row, :]
        mm = lambda a, w: lax.dot_general(a, w, (((1,), (0,)), ((), ())), preferred_element_type=f32)

        def mlp(sl, first):
            xc = xbuf[cur, sl, :]
            gt = jnp.minimum(mm(xc, wg_ref[...]) + bg, SWIGLU_LIMIT)
            up = jnp.clip(mm(xc, wu_ref[...]) + bu, -SWIGLU_LIMIT, SWIGLU_LIMIT)
            hid = (up + 1.0) * gt * jax.nn.sigmoid(SWIGLU_ALPHA * gt)
            part = mm(hid.astype(bf16), wd_ref[...])
            if first:
                yacc[sl, :] = part
            else:
                yacc[sl, :] += part

        def blocks(first):
            big = block_sizes[0] * COMPUTE_GRANULE
            n_big = n_granules // block_sizes[0]

            def big_block(c, carry):
                mlp(rows(c, big), first)
                return carry

            lax.fori_loop(0, n_big, big_block, 0)
            done = n_big * block_sizes[0]
            for size in block_sizes[1:]:
                present = (n_granules // size) % 2
                m = size * COMPUTE_GRANULE

                @pl.when(present == 1)
                def _(done=done, m=m):
                    mlp(pl.ds(pl.multiple_of(done * COMPUTE_GRANULE, m), m), first)

                done = done + present * size

        @pl.when(f == 0)
        def _():
            blocks(True)

        @pl.when(f > 0)
        def _():
            blocks(False)

    @pl.when(jnp.logical_and(f == n_f - 1, n_chunks > 0))
    def _():
        bd = bd_ref[pl.ds(ue_ref[u], 1), :]

        def store(c, carry):
            slot = c % OUT_SLOTS
            out_wait(slot)
            _store_slabs(ostage.at[slot], yacc[rows(c), :] + bd)
            out_copy(c, slot).start()
            flags[OUT_BUSY + slot] = 1
            return carry

        lax.fori_loop(0, n_chunks, store, 0)

    @pl.when(jnp.logical_and(u == n_u - 1, f == n_f - 1))
    def _():
        for s in range(OUT_SLOTS):
            out_wait(s)
        first = used_ref[0] // ROW_CHUNK
        last = y_hbm.shape[0] // CHUNK_PITCHED
        ostage[0] = jnp.zeros((CHUNK_PITCHED, LANES), u32)

        def tail_copy(c):
            dst = y_hbm.at[pl.ds(pl.multiple_of(c * CHUNK_PITCHED, CHUNK_PITCHED), CHUNK_PITCHED), :]
            return pltpu.make_async_copy(ostage.at[0], dst, out_sem.at[0])

        def start(c, carry):
            tail_copy(c).start()
            return carry

        def wait(c, carry):
            tail_copy(c).wait()
            return carry

        lax.fori_loop(first, last, start, 0)
        lax.fori_loop(first, last, wait, 0)


def _moe_experts(xs, ue, u_row, u_chunks, n_used_rows, n_pad, max_units, w_gate, b_gate, w_up, b_up, w_down,
                 b_down):
    e, d, dff = w_gate.shape
    n_f = dff // F_TILE
    assert n_f >= UNIT_ROWS // ROW_CHUNK, "one chunk of the next unit is loaded per f-tile step"
    f_idx = lambda u, f, uc: jnp.where(uc[u] > 0, f, n_f - 1)
    col_w = pl.BlockSpec((None, d, F_TILE), lambda u, f, ue, ur, uc, used: (ue[u], 0, f_idx(u, f, uc)))
    whole = lambda shape: pl.BlockSpec(shape, lambda u, f, ue, ur, uc, used: (0,) * len(shape))
    return pl.pallas_call(
        _moe_kernel,
        grid_spec=pltpu.PrefetchScalarGridSpec(
            num_scalar_prefetch=4,
            grid=(max_units, n_f),
            in_specs=[pl.BlockSpec(memory_space=pl.ANY),
                      col_w, whole((e * n_f, F_TILE)), col_w, whole((e * n_f, F_TILE)),
                      pl.BlockSpec((None, F_TILE, d), lambda u, f, ue, ur, uc, used: (ue[u], f_idx(u, f, uc), 0)),
                      whole((e, d))],
            out_specs=pl.BlockSpec(memory_space=pl.ANY),
            scratch_shapes=[pltpu.VMEM((2, UNIT_ROWS, d), bf16),
                            pltpu.VMEM((UNIT_ROWS, d), f32),
                            pltpu.VMEM((CHUNK_PITCHED, LANES), u32),
                            pltpu.VMEM((CHUNK_PITCHED, LANES), u32),
                            pltpu.VMEM((OUT_SLOTS, CHUNK_PITCHED, LANES), u32),
                            pltpu.SMEM((3 + OUT_SLOTS,), jnp.int32),
                            pltpu.SemaphoreType.DMA((2,)),
                            pltpu.SemaphoreType.DMA((OUT_SLOTS,))],
        ),
        out_shape=jax.ShapeDtypeStruct((n_pad * PITCH, LANES), u32),
        compiler_params=_cparams(("arbitrary", "arbitrary")),
        name="moe_experts",
    )(ue, u_row, u_chunks, n_used_rows, xs, w_gate, b_gate.reshape(e * n_f, F_TILE), w_up,
      b_up.reshape(e * n_f, F_TILE), w_down, b_down)


COMBINE_ROWS = 256


def _combine_kernel(dest_ref, dest_next_ref, y_hbm, gate_ref, x1_ref, p_ref, lnw_ref, lnb_ref, wp_ref, pnw_ref,
                    wg_ref, o_ref, ybuf_a, ybuf_b, fsum_ref, sem_a, sem_b):
    tm = COMBINE_ROWS
    s = pl.program_id(0)

    def row_copy(table, table_off, t, k, ybuf, sem):
        src = y_hbm.at[pl.ds(table[(table_off + t) * TOP_K + k] * PITCH, SLAB), :]
        return pltpu.make_async_copy(src, ybuf.at[k, pl.ds(t * PITCH, SLAB), :], sem)

    def wait_tile(ybuf, sem):
        for k in range(TOP_K):
            pltpu.make_async_copy(y_hbm.at[pl.ds(0, tm * SLAB), :], ybuf.at[k, pl.ds(0, tm * SLAB), :], sem).wait()

    @pl.when(s == 0)
    def _():
        def issue(t, c):
            for k in range(TOP_K):
                row_copy(dest_ref, 0, t, k, ybuf_a, sem_a).start(priority=k % 2)
            return c

        lax.fori_loop(0, tm, issue, 0, unroll=2)

    def finish_tile(row0, ybuf, next_table, next_off, next_buf, next_sem):
        tok = pl.ds(row0, tm)
        pieces = SLAB * TOP_K
        per_piece = tm // pieces
        e = jnp.dot(p_ref[tok, :].astype(bf16), wp_ref[...], preferred_element_type=f32)
        e = e * lax.rsqrt(jnp.mean(e * e, axis=-1, keepdims=True) + LN_EPS) * pnw_ref[...]
        gates = gate_ref[tok, :]
        for j in range(SLAB):
            acc_lo, acc_hi = None, None
            for k in range(TOP_K):
                piece = j * TOP_K + k
                for t in range(piece * per_piece, (piece + 1) * per_piece):
                    for kk in range(TOP_K):
                        row_copy(next_table, next_off, t, kk, next_buf, next_sem).start(priority=kk % 2)
                lo, hi = _slab_cols(ybuf.at[k], j, tm)
                g = gates[:, k:k + 1]
                acc_lo = g * lo if k == 0 else acc_lo + g * lo
                acc_hi = g * hi if k == 0 else acc_hi + g * hi
            fsum_ref[:, j * LANES:(j + 1) * LANES] = acc_lo
            fsum_ref[:, HALF + j * LANES:HALF + (j + 1) * LANES] = acc_hi
        x2 = _layer_norm(DN_ALPHA * x1_ref[tok, :] + fsum_ref[...], lnw_ref[...], lnb_ref[...])
        gate = jax.nn.sigmoid(jnp.dot(x2.astype(bf16), wg_ref[...], preferred_element_type=f32))
        o_ref[tok, :] = x2 + gate * e

    wait_tile(ybuf_a, sem_a)
    finish_tile(0, ybuf_a, dest_ref, tm, ybuf_b, sem_b)
    wait_tile(ybuf_b, sem_b)
    finish_tile(tm, ybuf_b, dest_next_ref, 0, ybuf_a, sem_a)

    @pl.when(s == pl.num_programs(0) - 1)
    def _():
        wait_tile(ybuf_a, sem_a)


def _combine(y, dest, gates, x1, p2d, ln_w, ln_b, wp_bf, ple_norm_w, wg_bf):
    t, d = x1.shape
    tm = COMBINE_ROWS
    n_steps = t // (2 * tm)
    row = lambda c: pl.BlockSpec((2 * tm, c), lambda i: (i, 0))
    full = lambda r, c: pl.BlockSpec((r, c), lambda i: (0, 0))
    table = lambda nxt: pl.BlockSpec((2 * tm * TOP_K,), lambda i: (jnp.minimum(i + nxt, n_steps - 1),),
                                     memory_space=pltpu.SMEM)
    dest_flat = dest.reshape(-1)
    return pl.pallas_call(
        _combine_kernel,
        grid=(n_steps,),
        in_specs=[table(0), table(1), pl.BlockSpec(memory_space=pl.ANY),
                  row(TOP_K), row(d), row(PLE_DIM), full(1, d), full(1, d), full(PLE_DIM, d), full(1, d),
                  full(d, d)],
        out_specs=row(d),
        out_shape=jax.ShapeDtypeStruct((t, d), f32),
        scratch_shapes=[pltpu.VMEM((TOP_K, tm * PITCH, LANES), u32), pltpu.VMEM((TOP_K, tm * PITCH, LANES), u32),
                        pltpu.VMEM((tm, d), f32), pltpu.SemaphoreType.DMA(()), pltpu.SemaphoreType.DMA(())],
        compiler_params=_cparams(("arbitrary",)),
        name="combine_ln2_ple",
    )(dest_flat, dest_flat, y, gates, x1, p2d, ln_w.reshape(1, d), ln_b.reshape(1, d), wp_bf,
      ple_norm_w.reshape(1, d), wg_bf)


def _layer(h, p_i, w_in, ret_gn_w, conv_w, conv_b, lru_wa, lru_ba, lru_wx, lru_bx, lru_lam, w_out,
           ln1_w, ln1_b, w_router, b_router, w_gate, b_gate, w_up, b_up, w_down, b_down,
           ln2_w, ln2_b, w_ple_proj, ple_norm_w, w_ple_gate):
    b, s, d = h.shape
    t = b * s
    x2d = h.reshape(t, d)
    proj = _in_proj(x2d, w_in).reshape(b, s, IN_COLS)
    ret_out = _retention(proj, ret_gn_w)
    lru_out = _lru(proj, conv_w, conv_b, lru_wa, lru_ba, lru_wx, lru_bx, lru_lam)
    x1, top_e, gates = _out_router(ret_out.reshape(t, RET_WIDTH), lru_out.reshape(t, LRU_WIDTH),
                                   w_out.astype(bf16), x2d, ln1_w, ln1_b, w_router, b_router)
    dest, pad_fill, ue, u_row, u_chunks, n_used_rows, n_pad, max_units = _routing_tables(top_e, t)
    xs = _dispatch(x1, dest, pad_fill, n_pad)
    y = _moe_experts(xs, ue, u_row, u_chunks, n_used_rows, n_pad, max_units, w_gate, b_gate, w_up, b_up, w_down,
                     b_down)
    out = _combine(y, dest, gates.T, x1, p_i.reshape(t, PLE_DIM), ln2_w, ln2_b, w_ple_proj.astype(bf16),
                   ple_norm_w, w_ple_gate.astype(bf16))
    return out.reshape(b, s, d)


def kernel(x, p, w_in, ret_gn_w, conv_w, conv_b, lru_wa, lru_ba, lru_wx, lru_bx, lru_lam, w_out, ln1_w, ln1_b,
           w_router, b_router, w_gate, b_gate, w_up, b_up, w_down, b_down, ln2_w, ln2_b, w_ple_proj, ple_norm_w,
           w_ple_gate):
    h = x.astype(f32)
    for i in range(w_in.shape[0]):
        h = _layer(h, p[i], w_in[i], ret_gn_w[i], conv_w[i], conv_b[i], lru_wa[i], lru_ba[i], lru_wx[i],
                   lru_bx[i], lru_lam[i], w_out[i], ln1_w[i], ln1_b[i], w_router[i], b_router[i], w_gate[i],
                   b_gate[i], w_up[i], b_up[i], w_down[i], b_down[i], ln2_w[i], ln2_b[i], w_ple_proj[i],
                   ple_norm_w[i], w_ple_gate[i])
    return h.astype(x.dtype)
```

```python
import math

import jax
import jax.numpy as jnp
from jax import lax
from jax.experimental import pallas as pl
from jax.experimental.pallas import tpu as pltpu

D_MODEL = 2048
RET_HEAD_DIM = 128
RET_HEADS = 8
RET_WIDTH = RET_HEADS * RET_HEAD_DIM
LRU_WIDTH = D_MODEL - RET_WIDTH
LRU_BLOCKS = 8
LRU_BLOCK_DIM = LRU_WIDTH // LRU_BLOCKS
IN_COLS = 4 * RET_WIDTH + 2 * LRU_WIDTH
CONV_WIDTH = 4
LRU_C = 8.0
CHUNK = 128
ROPE_BASE = 10000.0
N_EXPERTS = 32
TOP_K = 4
SWIGLU_LIMIT = 7.0
SWIGLU_ALPHA = 1.702
PLE_DIM = 256
LN_EPS = 1e-5
DEPTH = 1
DN_ALPHA = (2.0 * DEPTH) ** 0.25

LANES = 128
SUBLANES = 8
VMEM_LIMIT = 60 * 1024 * 1024

ROW_CHUNK = 256
UNIT_ROWS = 2048
F_TILE = 256

f32 = jnp.float32
bf16 = jnp.bfloat16


def _cparams(sem):
    return pltpu.CompilerParams(dimension_semantics=sem, vmem_limit_bytes=VMEM_LIMIT)


def _in_proj_kernel(x_ref, w_ref, o_ref):
    o_ref[...] = lax.dot_general(x_ref[...].astype(bf16), w_ref[...], (((1,), (0,)), ((), ())),
                                 preferred_element_type=f32)


def _in_proj(x2d, w):
    t, d = x2d.shape
    n = w.shape[1]
    tm, tn = 512, 2048
    return pl.pallas_call(
        _in_proj_kernel,
        grid=(n // tn, t // tm),
        in_specs=[pl.BlockSpec((tm, d), lambda j, i: (i, 0)),
                  pl.BlockSpec((d, tn), lambda j, i: (0, j))],
        out_specs=pl.BlockSpec((tm, tn), lambda j, i: (i, j)),
        out_shape=jax.ShapeDtypeStruct((t, n), f32),
        compiler_params=_cparams(("parallel", "parallel")),
        name="in_proj",
    )(x2d, w)


def _retention_kernel(q_ref, k_ref, v_ref, g_ref, cos_ref, sin_ref, dec_ref, qd_ref, kd_ref, cd_ref, gnw_ref,
                      o_ref):
    s = q_ref.shape[0]
    n_chunks = s // CHUNK
    head = pl.program_id(1)
    decay = dec_ref[head]
    q_dec = qd_ref[head]
    k_dec = kd_ref[head]
    c_dec = cd_ref[head, 0:1, :]
    gnw = gnw_ref[head]
    k_scale = RET_HEAD_DIM ** -0.5

    def rope(xv, cos, sin):
        return xv * cos + pltpu.roll(xv, RET_HEAD_DIM // 2, axis=1) * sin

    def body(n, state):
        sl = pl.ds(pl.multiple_of(n * CHUNK, CHUNK), CHUNK)
        cos = cos_ref[sl, :]
        sin = sin_ref[sl, :]
        q = rope(q_ref[sl, :], cos, sin)
        k = rope(k_ref[sl, :], cos, sin) * k_scale
        vb = v_ref[sl, :].astype(bf16)
        scores = lax.dot_general(q.astype(bf16), k.astype(bf16), (((1,), (1,)), ((), ())),
                                 preferred_element_type=f32) * decay
        intra = jnp.dot(scores.astype(bf16), vb, preferred_element_type=f32)
        cross = jnp.dot((q * q_dec).astype(bf16), state.astype(bf16), preferred_element_type=f32)
        kv = lax.dot_general((k * k_dec).astype(bf16), vb, (((0,), (0,)), ((), ())),
                             preferred_element_type=f32)
        ret = intra + cross
        mu = jnp.mean(ret, axis=-1, keepdims=True)
        cen = ret - mu
        var = jnp.mean(cen * cen, axis=-1, keepdims=True)
        ret = cen * lax.rsqrt(var + LN_EPS) * gnw
        g = g_ref[sl, :]
        o_ref[sl, :] = (g * jax.nn.sigmoid(g) * ret).astype(o_ref.dtype)
        return c_dec * state + kv

    lax.fori_loop(0, n_chunks, body, jnp.zeros((RET_HEAD_DIM, RET_HEAD_DIM), f32), unroll=8)


def _retention_tables(s):
    h, d = RET_HEADS, RET_HEAD_DIM
    inv = ROPE_BASE ** (-jnp.arange(0, d, 2, dtype=f32) / d)
    ang = jnp.arange(s, dtype=f32)[:, None] * inv[None, :]
    cos = jnp.cos(ang)
    sin = jnp.sin(ang)
    cos_t = jnp.concatenate([cos, cos], axis=-1)
    sin_t = jnp.concatenate([-sin, sin], axis=-1)
    log_gamma = jnp.log1p(-jnp.exp2(-5.0 - jnp.arange(h, dtype=f32)))
    idx = jnp.arange(CHUNK, dtype=f32)
    diff = idx[:, None] - idx[None, :]
    decay = jnp.where((diff >= 0)[None], jnp.exp(jnp.maximum(diff, 0.0)[None] * log_gamma[:, None, None]), 0.0)
    q_dec = jnp.exp((idx[None, :] + 1.0) * log_gamma[:, None])
    k_dec = jnp.exp((CHUNK - 1.0 - idx)[None, :] * log_gamma[:, None])
    c_dec = jnp.exp(CHUNK * log_gamma)
    q_dec = jnp.broadcast_to(q_dec[:, :, None], (h, CHUNK, d))
    k_dec = jnp.broadcast_to(k_dec[:, :, None], (h, CHUNK, d))
    c_dec = jnp.broadcast_to(c_dec[:, None, None], (h, SUBLANES, d))
    return cos_t, sin_t, decay, q_dec, k_dec, c_dec


def _retention(proj, ret_gn_w):
    b, s, _ = proj.shape
    d = RET_HEAD_DIM
    cos_t, sin_t, decay, q_dec, k_dec, c_dec = _retention_tables(s)
    col = lambda off: pl.BlockSpec((None, s, d), lambda bi, hi: (bi, 0, off + hi))
    per_head = lambda r: pl.BlockSpec((RET_HEADS, r, d), lambda bi, hi: (0, 0, 0))
    full = lambda shp: pl.BlockSpec(shp, lambda bi, hi: (0,) * len(shp))
    return pl.pallas_call(
        _retention_kernel,
        grid=(b, RET_HEADS),
        in_specs=[col(0), col(RET_HEADS), col(2 * RET_HEADS), col(3 * RET_HEADS),
                  full((s, d)), full((s, d)),
                  per_head(CHUNK), per_head(CHUNK), per_head(CHUNK), per_head(SUBLANES), per_head(1)],
        out_specs=pl.BlockSpec((None, s, d), lambda bi, hi: (bi, 0, hi)),
        out_shape=jax.ShapeDtypeStruct((b, s, RET_WIDTH), bf16),
        compiler_params=_cparams(("parallel", "parallel")),
        name="retention",
    )(proj, proj, proj, proj, cos_t, sin_t, decay, q_dec, k_dec, c_dec, ret_gn_w.reshape(RET_HEADS, 1, d))


def _gelu_tanh(x):
    return 0.5 * x * (1.0 + jnp.tanh(math.sqrt(2.0 / math.pi) * (x + 0.044715 * (x * x * x))))


def _lru_kernel(xr_ref, yg_ref, cw_ref, cb_ref, wa_ref, ba_ref, wx_ref, bx_ref, lam_ref, o_ref, a_ref, b_ref):
    s = xr_ref.shape[0]
    blk = pl.program_id(1)
    x = xr_ref[...]
    rows = lax.broadcasted_iota(jnp.int32, x.shape, 0)
    xc = cb_ref[blk] + cw_ref[blk, CONV_WIDTH - 1:CONV_WIDTH, :] * x
    for back in range(1, CONV_WIDTH):
        shifted = jnp.where(rows >= back, pltpu.roll(x, back, axis=0), 0.0)
        xc = xc + cw_ref[blk, CONV_WIDTH - 1 - back:CONV_WIDTH - back, :] * shifted
    xcb = xc.astype(bf16)
    r = jax.nn.sigmoid(jnp.dot(xcb, wa_ref[blk].astype(bf16), preferred_element_type=f32) + ba_ref[blk])
    gi = jax.nn.sigmoid(jnp.dot(xcb, wx_ref[blk].astype(bf16), preferred_element_type=f32) + bx_ref[blk])
    lam = lam_ref[blk]
    log_sig = jnp.minimum(lam, 0.0) - jnp.log1p(jnp.exp(-jnp.abs(lam)))
    log_a = LRU_C * r * log_sig
    a = jnp.exp(log_a)
    a_ref[...] = a
    b_ref[...] = jnp.sqrt(-jnp.tanh(log_a) * (a * a + 1.0)) * (gi * xc)

    row8 = lax.broadcasted_iota(jnp.int32, (SUBLANES, LANES), 0)

    def body(i, h_prev):
        sl = pl.ds(pl.multiple_of(i * SUBLANES, SUBLANES), SUBLANES)
        a8 = a_ref[sl, :]
        b8 = b_ref[sl, :]
        for sh in (1, 2, 4):
            a_sh = jnp.where(row8 >= sh, pltpu.roll(a8, sh, axis=0), 1.0)
            b_sh = jnp.where(row8 >= sh, pltpu.roll(b8, sh, axis=0), 0.0)
            b8 = a8 * b_sh + b8
            a8 = a8 * a_sh
        h8 = a8 * h_prev + b8
        o_ref[sl, :] = (_gelu_tanh(yg_ref[sl, :]) * h8).astype(o_ref.dtype)
        return h8[SUBLANES - 1:SUBLANES, :]

    lax.fori_loop(0, s // SUBLANES, body, jnp.zeros((1, LANES), f32), unroll=16)


def _lru(proj, conv_w, conv_b, wa, ba, wx, bx, lam):
    b, s, _ = proj.shape
    d = LRU_BLOCK_DIM
    xr_off = 4 * RET_WIDTH // d
    yg_off = xr_off + LRU_BLOCKS
    col = lambda off: pl.BlockSpec((None, s, d), lambda bi, ji: (bi, 0, off + ji))
    blk = lambda r: pl.BlockSpec((LRU_BLOCKS, r, d), lambda bi, ji: (0, 0, 0))
    per_block = lambda v: v.reshape(-1, LRU_BLOCKS, d).transpose(1, 0, 2)
    return pl.pallas_call(
        _lru_kernel,
        grid=(b, LRU_BLOCKS),
        in_specs=[col(xr_off), col(yg_off), blk(CONV_WIDTH), blk(1), blk(d), blk(1), blk(d), blk(1), blk(1)],
        out_specs=pl.BlockSpec((None, s, d), lambda bi, ji: (bi, 0, ji)),
        out_shape=jax.ShapeDtypeStruct((b, s, LRU_WIDTH), bf16),
        scratch_shapes=[pltpu.VMEM((s, d), f32), pltpu.VMEM((s, d), f32)],
        compiler_params=_cparams(("parallel", "parallel")),
        name="rg_lru",
    )(proj, proj, per_block(conv_w), per_block(conv_b), wa, ba.reshape(LRU_BLOCKS, 1, d), wx,
      bx.reshape(LRU_BLOCKS, 1, d), per_block(lam))


def _layer_norm(y, w, b):
    mu = jnp.mean(y, axis=-1, keepdims=True)
    cen = y - mu
    var = jnp.mean(cen * cen, axis=-1, keepdims=True)
    return cen * lax.rsqrt(var + LN_EPS) * w + b


HALF = D_MODEL // 2
SLAB = HALF // LANES
PITCH = SLAB + 4
u32 = jnp.uint32
HIGH_MASK = 0xFFFF0000


def _bf16_bits(v):
    return lax.bitcast_convert_type(v.astype(bf16).astype(f32), u32)


def _store_slabs(ref, val, before_piece=None):
    n = val.shape[0]
    for j in range(SLAB):
        if before_piece is not None:
            before_piece(j)
        lo = _bf16_bits(val[:, j * LANES:(j + 1) * LANES])
        hi = _bf16_bits(val[:, HALF + j * LANES:HALF + (j + 1) * LANES])
        ref[pl.ds(j, n, stride=PITCH), :] = hi | lax.shift_right_logical(lo, jnp.full_like(lo, 16))
    for j in range(SLAB, PITCH):
        ref[pl.ds(j, n, stride=PITCH), :] = jnp.zeros((n, LANES), u32)


def _slab_cols(ref, j, n):
    w = ref[pl.ds(j, n, stride=PITCH), :]
    lo = lax.bitcast_convert_type(lax.shift_left(w, jnp.full_like(w, 16)), f32)
    hi = lax.bitcast_convert_type(w & jnp.full_like(w, HIGH_MASK), f32)
    return lo, hi


def _split_bf16(v):
    hi = v.astype(bf16)
    lo = (v - hi.astype(f32)).astype(bf16)
    return hi, lo


def _out_router_kernel(ret_ref, lru_ref, wo_ref, x_ref, lnw_ref, lnb_ref, wr_ref, br_ref,
                       x1_ref, tope_ref, gate_ref):
    m = jnp.dot(ret_ref[...], wo_ref[0:RET_WIDTH, :], preferred_element_type=f32)
    m = m + jnp.dot(lru_ref[...], wo_ref[RET_WIDTH:D_MODEL, :], preferred_element_type=f32)
    x1 = _layer_norm(DN_ALPHA * x_ref[...] + m, lnw_ref[...], lnb_ref[...])
    x1_ref[...] = x1
    xh, xl = _split_bf16(x1)
    wh, wl = _split_bf16(wr_ref[...])
    nt = lambda a, b: lax.dot_general(a, b, (((1,), (1,)), ((), ())), preferred_element_type=f32)
    logits = nt(wh, xh) + nt(wh, xl) + nt(wl, xh) + br_ref[...]
    tm = logits.shape[1]
    expert = lax.broadcasted_iota(jnp.int32, logits.shape, 0)
    choice = lax.broadcasted_iota(jnp.int32, (TOP_K, tm), 0)
    top_e = jnp.zeros((TOP_K, tm), jnp.int32)
    top_v = jnp.zeros((TOP_K, tm), f32)
    cur = logits
    for kk in range(TOP_K):
        mx = jnp.max(cur, axis=0, keepdims=True)
        idx = jnp.min(jnp.where(cur == mx, expert, N_EXPERTS), axis=0, keepdims=True)
        top_e = jnp.where(choice == kk, idx, top_e)
        top_v = jnp.where(choice == kk, mx, top_v)
        cur = jnp.where(expert == idx, -jnp.inf, cur)
    ex = jnp.exp(top_v - top_v[0:1, :])
    gate_ref[...] = ex / jnp.sum(ex, axis=0, keepdims=True)
    tope_ref[...] = top_e


def _out_router(ret_out, lru_out, wo_bf, x2d, ln_w, ln_b, w_router, b_router):
    t, d = x2d.shape
    tm = 512
    row = lambda c: pl.BlockSpec((tm, c), lambda i: (i, 0))
    full = lambda r, c: pl.BlockSpec((r, c), lambda i: (0, 0))
    per_choice = pl.BlockSpec((TOP_K, tm), lambda i: (0, i))
    return pl.pallas_call(
        _out_router_kernel,
        grid=(t // tm,),
        in_specs=[row(RET_WIDTH), row(LRU_WIDTH), full(d, d), row(d), full(1, d), full(1, d),
                  full(N_EXPERTS, d), full(N_EXPERTS, 1)],
        out_specs=[row(d), per_choice, per_choice],
        out_shape=[jax.ShapeDtypeStruct((t, d), f32), jax.ShapeDtypeStruct((TOP_K, t), jnp.int32),
                   jax.ShapeDtypeStruct((TOP_K, t), f32)],
        compiler_params=_cparams(("parallel",)),
        name="out_proj_ln1_router",
    )(ret_out, lru_out, wo_bf, x2d, ln_w.reshape(1, d), ln_b.reshape(1, d), w_router.T,
      b_router.reshape(N_EXPERTS, 1))


def _routing_tables(top_e, t):
    n_pad = t * TOP_K + N_EXPERTS * ROW_CHUNK
    max_units = N_EXPERTS + (t * TOP_K) // UNIT_ROWS
    sel = (top_e[:, None, :] == jnp.arange(N_EXPERTS, dtype=jnp.int32)[None, :, None]).any(axis=0)
    sel = sel.astype(jnp.int32)
    counts = jnp.sum(sel, axis=1)
    rank = jnp.cumsum(sel, axis=1) - sel
    padded = (counts + ROW_CHUNK - 1) // ROW_CHUNK * ROW_CHUNK
    pad_ends = jnp.cumsum(padded)
    pad_starts = pad_ends - padded
    dest_dense = pad_starts[:, None] + rank
    dest = jnp.take_along_axis(dest_dense, top_e, axis=0).T
    pad_fill = jnp.concatenate([pad_starts + counts, padded - counts, pad_ends[-1:]]).astype(jnp.int32)
    units_per_e = (padded + UNIT_ROWS - 1) // UNIT_ROWS
    unit_ends = jnp.cumsum(units_per_e)
    unit_starts = unit_ends - units_per_e
    n_units = unit_ends[-1]
    u = jnp.arange(max_units, dtype=jnp.int32)
    u_clamped = jnp.minimum(u, n_units - 1)
    ue = jnp.searchsorted(unit_ends, u_clamped, side='right').astype(jnp.int32)
    ue = jnp.minimum(ue, N_EXPERTS - 1)
    j = u_clamped - unit_starts[ue]
    u_row = pad_starts[ue] + j * UNIT_ROWS
    u_rows = jnp.minimum(UNIT_ROWS, padded[ue] - j * UNIT_ROWS)
    u_chunks = jnp.where(u < n_units, u_rows // ROW_CHUNK, 0).astype(jnp.int32)
    u_real = jnp.clip(counts[ue] - j * UNIT_ROWS, 0, UNIT_ROWS)
    u_granules = jnp.where(u < n_units, (u_real + COMPUTE_GRANULE - 1) // COMPUTE_GRANULE, 0).astype(jnp.int32)
    n_used_rows = jnp.concatenate([pad_ends[-1:].astype(jnp.int32), u_granules])
    return dest.astype(jnp.int32), pad_fill, ue, u_row.astype(jnp.int32), u_chunks, n_used_rows, n_pad, max_units


CHUNK_PITCHED = ROW_CHUNK * PITCH
DISPATCH_ROWS = 256
FILL_SLOTS = ROW_CHUNK // 2


def _dispatch_kernel(fill_ref, dest_ref, dest_prev_ref, x1_ref, xs_hbm, stage_a, stage_b, zeros, sem_a, sem_b,
                     sem_z):
    tm = DISPATCH_ROWS
    s = pl.program_id(0)
    last = pl.num_programs(0) - 1

    def slot_copy(table, table_off, t, k, stage, sem):
        dst = xs_hbm.at[pl.ds(table[(table_off + t) * TOP_K + k] * PITCH, PITCH), :]
        return pltpu.make_async_copy(stage.at[pl.ds(t * PITCH, PITCH), :], dst, sem)

    def wait_tile(stage, sem):
        for _ in range(TOP_K):
            pltpu.make_async_copy(stage, xs_hbm.at[pl.ds(0, tm * PITCH), :], sem).wait()

    def start_copies(table, table_off, stage, sem):
        per_piece = tm // SLAB

        def before_piece(j):
            for t in range(j * per_piece, (j + 1) * per_piece):
                for k in range(TOP_K):
                    slot_copy(table, table_off, t, k, stage, sem).start(priority=k % 2)

        return before_piece

    def pack(row0, stage, before_piece=None):
        _store_slabs(stage, x1_ref[pl.ds(row0, tm), :], before_piece)

    @pl.when(s == 0)
    def _():
        pack(0, stage_a)

    @pl.when(s > 0)
    def _():
        wait_tile(stage_a, sem_a)
        pack(0, stage_a, start_copies(dest_prev_ref, tm, stage_b, sem_b))
        wait_tile(stage_b, sem_b)

    pack(tm, stage_b, start_copies(dest_ref, 0, stage_a, sem_a))

    @pl.when(s == last)
    def _():
        def issue(t, c):
            for k in range(TOP_K):
                slot_copy(dest_ref, tm, t, k, stage_b, sem_b).start(priority=k % 2)
            return c

        lax.fori_loop(0, tm, issue, 0, unroll=2)
        wait_tile(stage_a, sem_a)
        wait_tile(stage_b, sem_b)

        zeros[...] = jnp.zeros(zeros.shape, u32)

        def fill(slot, n_slots):
            return pltpu.make_async_copy(zeros.at[pl.ds(0, n_slots * PITCH), :],
                                         xs_hbm.at[pl.ds(slot * PITCH, n_slots * PITCH), :], sem_z)

        def expert_pad(e, start):
            slot = fill_ref[e]
            n = fill_ref[N_EXPERTS + e]
            piece = FILL_SLOTS
            while piece >= 1:
                @pl.when((n // piece) % 2 == 1)
                def _(slot=slot, piece=piece):
                    if start:
                        fill(slot, piece).start()
                    else:
                        fill(slot, piece).wait()

                slot = slot + ((n // piece) % 2) * piece
                piece //= 2

        def tail(i, start):
            slot = fill_ref[2 * N_EXPERTS] + i * FILL_SLOTS
            if start:
                fill(slot, FILL_SLOTS).start()
            else:
                fill(slot, FILL_SLOTS).wait()

        n_tail = (xs_hbm.shape[0] // PITCH - fill_ref[2 * N_EXPERTS]) // FILL_SLOTS
        for start in (True, False):
            lax.fori_loop(0, N_EXPERTS, lambda e, c, start=start: (expert_pad(e, start), c)[1], 0)
            lax.fori_loop(0, n_tail, lambda i, c, start=start: (tail(i, start), c)[1], 0)


def _dispatch(x1, dest, pad_fill, n_pad):
    t, d = x1.shape
    tm = DISPATCH_ROWS
    n_steps = t // (2 * tm)
    table = lambda back: pl.BlockSpec((2 * tm * TOP_K,), lambda i, fill: (jnp.maximum(i - back, 0),),
                                      memory_space=pltpu.SMEM)
    dest_flat = dest.reshape(-1)
    stage = pltpu.VMEM((tm * PITCH, LANES), u32)
    return pl.pallas_call(
        _dispatch_kernel,
        grid_spec=pltpu.PrefetchScalarGridSpec(
            num_scalar_prefetch=1,
            grid=(n_steps,),
            in_specs=[table(0), table(1), pl.BlockSpec((2 * tm, d), lambda i, fill: (i, 0))],
            out_specs=pl.BlockSpec(memory_space=pl.ANY),
            scratch_shapes=[stage, stage, pltpu.VMEM((FILL_SLOTS * PITCH, LANES), u32),
                            pltpu.SemaphoreType.DMA(()), pltpu.SemaphoreType.DMA(()), pltpu.SemaphoreType.DMA(())],
        ),
        out_shape=jax.ShapeDtypeStruct((n_pad * PITCH, LANES), u32),
        compiler_params=_cparams(("arbitrary",)),
        name="moe_dispatch",
    )(pad_fill, dest_flat, dest_flat, x1)


OUT_SLOTS = 4
COMPUTE_GRANULE = ROW_CHUNK // 2


def _moe_kernel(ue_ref, urow_ref, uchunks_ref, used_ref, xs_hbm, wg_ref, bg_ref, wu_ref, bu_ref, wd_ref,
                bd_ref, y_hbm, xbuf, yacc, gstage_a, gstage_b, ostage, flags, in_sem, out_sem):
    u = pl.program_id(0)
    f = pl.program_id(1)
    n_u = pl.num_programs(0)
    n_f = pl.num_programs(1)
    n_chunks = uchunks_ref[u]
    row0 = urow_ref[u]
    cur = u % 2
    nxt = 1 - cur
    u_next = jnp.minimum(u + 1, n_u - 1)
    n_next = jnp.where(u + 1 < n_u, uchunks_ref[u_next], 0)
    row_next = urow_ref[u_next]
    first_step = jnp.logical_and(u == 0, f == 0)
    PENDING, PEND_CHUNK, PEND_SLOT, OUT_BUSY = 0, 1, 2, 3

    def rows(c, size=ROW_CHUNK):
        return pl.ds(pl.multiple_of(c * size, size), size)

    stages = (gstage_a, gstage_b)

    def chunk_copy(unit_row, c, parity):
        src = xs_hbm.at[pl.ds(pl.multiple_of((unit_row + c * ROW_CHUNK) * PITCH, CHUNK_PITCHED), CHUNK_PITCHED), :]
        return pltpu.make_async_copy(src, stages[parity], in_sem.at[parity])

    def unpack(c, xslot, parity):
        for j in range(SLAB):
            lo, hi = _slab_cols(stages[parity], j, ROW_CHUNK)
            xbuf[xslot, rows(c), j * LANES:(j + 1) * LANES] = lo.astype(bf16)
            xbuf[xslot, rows(c), HALF + j * LANES:HALF + (j + 1) * LANES] = hi.astype(bf16)

    def out_copy(c, slot):
        dst = y_hbm.at[pl.ds(pl.multiple_of((row0 + c * ROW_CHUNK) * PITCH, CHUNK_PITCHED), CHUNK_PITCHED), :]
        return pltpu.make_async_copy(ostage.at[slot], dst, out_sem.at[slot])

    def out_wait(slot):
        @pl.when(flags[OUT_BUSY + slot] == 1)
        def _():
            pltpu.make_async_copy(ostage.at[slot], y_hbm.at[pl.ds(0, CHUNK_PITCHED), :], out_sem.at[slot]).wait()
            flags[OUT_BUSY + slot] = 0

    @pl.when(first_step)
    def _():
        for i in range(OUT_BUSY + OUT_SLOTS):
            flags[i] = 0

        def clear(c, carry):
            yacc[rows(c), :] = jnp.zeros((ROW_CHUNK, D_MODEL), f32)
            return carry

        lax.fori_loop(0, UNIT_ROWS // ROW_CHUNK, clear, 0)

        def load(c, carry):
            first = chunk_copy(row0, c, 0)
            first.start()
            first.wait()
            unpack(c, cur, 0)
            return carry

        lax.fori_loop(0, n_chunks, load, 0)

    prefetching = f < n_next
    pending = flags[PENDING] == 1
    done_chunk = flags[PEND_CHUNK]
    done_slot = flags[PEND_SLOT]
    same_buffer = jnp.logical_and(pending, done_chunk % 2 == f % 2)
    for late in (False, True):
        for parity in (0, 1):
            starts_now = jnp.logical_and(prefetching, same_buffer if late else jnp.logical_not(same_buffer))

            @pl.when(jnp.logical_and(starts_now, f % 2 == parity))
            def _():
                chunk_copy(row_next, f, parity).start()

        if not late:
            for parity in (0, 1):
                @pl.when(jnp.logical_and(pending, done_chunk % 2 == parity))
                def _():
                    chunk_copy(row_next, done_chunk, parity).wait()
                    unpack(done_chunk, done_slot, parity)

    flags[PENDING] = prefetching.astype(jnp.int32)

    @pl.when(prefetching)
    def _():
        flags[PEND_CHUNK] = f
        flags[PEND_SLOT] = nxt

    n_granules = used_ref[1 + u]
    block_sizes = (8, 4, 2, 1)

    @pl.when(n_chunks > 0)
    def _():
        bias_row = pl.ds(ue_ref[u] * n_f + f, 1)
        bg = bg_ref[bias_row, :]
        bu = bu_ref[bias_row, :]
        mm = lambda a, w: lax.dot_general(a, w, (((1,), (0,)), ((), ())), preferred_element_type=f32)

        def mlp(sl, first):
            xc = xbuf[cur, sl, :]
            gt = jnp.minimum(mm(xc, wg_ref[...]) + bg, SWIGLU_LIMIT)
            up = jnp.clip(mm(xc, wu_ref[...]) + bu, -SWIGLU_LIMIT, SWIGLU_LIMIT)
            hid = (up + 1.0) * gt * jax.nn.sigmoid(SWIGLU_ALPHA * gt)
            part = mm(hid.astype(bf16), wd_ref[...])
            if first:
                yacc[sl, :] = part
            else:
                yacc[sl, :] += part

        def blocks(first):
            big = block_sizes[0] * COMPUTE_GRANULE
            n_big = n_granules // block_sizes[0]

            def big_block(c, carry):
                mlp(rows(c, big), first)
                return carry

            lax.fori_loop(0, n_big, big_block, 0)
            done = n_big * block_sizes[0]
            for size in block_sizes[1:]:
                present = (n_granules // size) % 2
                m = size * COMPUTE_GRANULE

                @pl.when(present == 1)
                def _(done=done, m=m):
                    mlp(pl.ds(pl.multiple_of(done * COMPUTE_GRANULE, m), m), first)

                done = done + present * size

        @pl.when(f == 0)
        def _():
            blocks(True)

        @pl.when(f > 0)
        def _():
            blocks(False)

    @pl.when(jnp.logical_and(f == n_f - 1, n_chunks > 0))
    def _():
        bd = bd_ref[pl.ds(ue_ref[u], 1), :]

        def store(c, carry):
            slot = c % OUT_SLOTS
            out_wait(slot)
            _store_slabs(ostage.at[slot], yacc[rows(c), :] + bd)
            out_copy(c, slot).start()
            flags[OUT_BUSY + slot] = 1
            return carry

        lax.fori_loop(0, n_chunks, store, 0)

    @pl.when(jnp.logical_and(u == n_u - 1, f == n_f - 1))
    def _():
        for s in range(OUT_SLOTS):
            out_wait(s)
        first = used_ref[0] // ROW_CHUNK
        last = y_hbm.shape[0] // CHUNK_PITCHED
        ostage[0] = jnp.zeros((CHUNK_PITCHED, LANES), u32)

        def tail_copy(c):
            dst = y_hbm.at[pl.ds(pl.multiple_of(c * CHUNK_PITCHED, CHUNK_PITCHED), CHUNK_PITCHED), :]
            return pltpu.make_async_copy(ostage.at[0], dst, out_sem.at[0])

        def start(c, carry):
            tail_copy(c).start()
            return carry

        def wait(c, carry):
            tail_copy(c).wait()
            return carry

        lax.fori_loop(first, last, start, 0)
        lax.fori_loop(first, last, wait, 0)


def _moe_experts(xs, ue, u_row, u_chunks, n_used_rows, n_pad, max_units, w_gate, b_gate, w_up, b_up, w_down,
                 b_down):
    e, d, dff = w_gate.shape
    n_f = dff // F_TILE
    assert n_f >= UNIT_ROWS // ROW_CHUNK, "one chunk of the next unit is loaded per f-tile step"
    f_idx = lambda u, f, uc: jnp.where(uc[u] > 0, f, n_f - 1)
    col_w = pl.BlockSpec((None, d, F_TILE), lambda u, f, ue, ur, uc, used: (ue[u], 0, f_idx(u, f, uc)))
    whole = lambda shape: pl.BlockSpec(shape, lambda u, f, ue, ur, uc, used: (0,) * len(shape))
    return pl.pallas_call(
        _moe_kernel,
        grid_spec=pltpu.PrefetchScalarGridSpec(
            num_scalar_prefetch=4,
            grid=(max_units, n_f),
            in_specs=[pl.BlockSpec(memory_space=pl.ANY),
                      col_w, whole((e * n_f, F_TILE)), col_w, whole((e * n_f, F_TILE)),
                      pl.BlockSpec((None, F_TILE, d), lambda u, f, ue, ur, uc, used: (ue[u], f_idx(u, f, uc), 0)),
                      whole((e, d))],
            out_specs=pl.BlockSpec(memory_space=pl.ANY),
            scratch_shapes=[pltpu.VMEM((2, UNIT_ROWS, d), bf16),
                            pltpu.VMEM((UNIT_ROWS, d), f32),
                            pltpu.VMEM((CHUNK_PITCHED, LANES), u32),
                            pltpu.VMEM((CHUNK_PITCHED, LANES), u32),
                            pltpu.VMEM((OUT_SLOTS, CHUNK_PITCHED, LANES), u32),
                            pltpu.SMEM((3 + OUT_SLOTS,), jnp.int32),
                            pltpu.SemaphoreType.DMA((2,)),
                            pltpu.SemaphoreType.DMA((OUT_SLOTS,))],
        ),
        out_shape=jax.ShapeDtypeStruct((n_pad * PITCH, LANES), u32),
        compiler_params=_cparams(("arbitrary", "arbitrary")),
        name="moe_experts",
    )(ue, u_row, u_chunks, n_used_rows, xs, w_gate, b_gate.reshape(e * n_f, F_TILE), w_up,
      b_up.reshape(e * n_f, F_TILE), w_down, b_down)


COMBINE_ROWS = 256


def _combine_kernel(dest_ref, dest_next_ref, y_hbm, gate_ref, x1_ref, p_ref, lnw_ref, lnb_ref, wp_ref, pnw_ref,
                    wg_ref, o_ref, ybuf_a, ybuf_b, fsum_ref, sem_a, sem_b):
    tm = COMBINE_ROWS
    s = pl.program_id(0)

    def row_copy(table, table_off, t, k, ybuf, sem):
        src = y_hbm.at[pl.ds(table[(table_off + t) * TOP_K + k] * PITCH, SLAB), :]
        return pltpu.make_async_copy(src, ybuf.at[k, pl.ds(t * PITCH, SLAB), :], sem)

    def wait_tile(ybuf, sem):
        for k in range(TOP_K):
            pltpu.make_async_copy(y_hbm.at[pl.ds(0, tm * SLAB), :], ybuf.at[k, pl.ds(0, tm * SLAB), :], sem).wait()

    @pl.when(s == 0)
    def _():
        def issue(t, c):
            for k in range(TOP_K):
                row_copy(dest_ref, 0, t, k, ybuf_a, sem_a).start(priority=k % 2)
            return c

        lax.fori_loop(0, tm, issue, 0, unroll=2)

    def finish_tile(row0, ybuf, next_table, next_off, next_buf, next_sem):
        tok = pl.ds(row0, tm)
        pieces = SLAB * TOP_K
        per_piece = tm // pieces
        e = jnp.dot(p_ref[tok, :].astype(bf16), wp_ref[...], preferred_element_type=f32)
        e = e * lax.rsqrt(jnp.mean(e * e, axis=-1, keepdims=True) + LN_EPS) * pnw_ref[...]
        gates = gate_ref[tok, :]
        for j in range(SLAB):
            acc_lo, acc_hi = None, None
            for k in range(TOP_K):
                piece = j * TOP_K + k
                for t in range(piece * per_piece, (piece + 1) * per_piece):
                    for kk in range(TOP_K):
                        row_copy(next_table, next_off, t, kk, next_buf, next_sem).start(priority=kk % 2)
                lo, hi = _slab_cols(ybuf.at[k], j, tm)
                g = gates[:, k:k + 1]
                acc_lo = g * lo if k == 0 else acc_lo + g * lo
                acc_hi = g * hi if k == 0 else acc_hi + g * hi
            fsum_ref[:, j * LANES:(j + 1) * LANES] = acc_lo
            fsum_ref[:, HALF + j * LANES:HALF + (j + 1) * LANES] = acc_hi
        x2 = _layer_norm(DN_ALPHA * x1_ref[tok, :] + fsum_ref[...], lnw_ref[...], lnb_ref[...])
        gate = jax.nn.sigmoid(jnp.dot(x2.astype(bf16), wg_ref[...], preferred_element_type=f32))
        o_ref[tok, :] = x2 + gate * e

    wait_tile(ybuf_a, sem_a)
    finish_tile(0, ybuf_a, dest_ref, tm, ybuf_b, sem_b)
    wait_tile(ybuf_b, sem_b)
    finish_tile(tm, ybuf_b, dest_next_ref, 0, ybuf_a, sem_a)

    @pl.when(s == pl.num_programs(0) - 1)
    def _():
        wait_tile(ybuf_a, sem_a)


def _combine(y, dest, gates, x1, p2d, ln_w, ln_b, wp_bf, ple_norm_w, wg_bf):
    t, d = x1.shape
    tm = COMBINE_ROWS
    n_steps = t // (2 * tm)
    row = lambda c: pl.BlockSpec((2 * tm, c), lambda i: (i, 0))
    full = lambda r, c: pl.BlockSpec((r, c), lambda i: (0, 0))
    table = lambda nxt: pl.BlockSpec((2 * tm * TOP_K,), lambda i: (jnp.minimum(i + nxt, n_steps - 1),),
                                     memory_space=pltpu.SMEM)
    dest_flat = dest.reshape(-1)
    return pl.pallas_call(
        _combine_kernel,
        grid=(n_steps,),
        in_specs=[table(0), table(1), pl.BlockSpec(memory_space=pl.ANY),
                  row(TOP_K), row(d), row(PLE_DIM), full(1, d), full(1, d), full(PLE_DIM, d), full(1, d),
                  full(d, d)],
        out_specs=row(d),
        out_shape=jax.ShapeDtypeStruct((t, d), f32),
        scratch_shapes=[pltpu.VMEM((TOP_K, tm * PITCH, LANES), u32), pltpu.VMEM((TOP_K, tm * PITCH, LANES), u32),
                        pltpu.VMEM((tm, d), f32), pltpu.SemaphoreType.DMA(()), pltpu.SemaphoreType.DMA(())],
        compiler_params=_cparams(("arbitrary",)),
        name="combine_ln2_ple",
    )(dest_flat, dest_flat, y, gates, x1, p2d, ln_w.reshape(1, d), ln_b.reshape(1, d), wp_bf,
      ple_norm_w.reshape(1, d), wg_bf)


def _layer(h, p_i, w_in, ret_gn_w, conv_w, conv_b, lru_wa, lru_ba, lru_wx, lru_bx, lru_lam, w_out,
           ln1_w, ln1_b, w_router, b_router, w_gate, b_gate, w_up, b_up, w_down, b_down,
           ln2_w, ln2_b, w_ple_proj, ple_norm_w, w_ple_gate):
    b, s, d = h.shape
    t = b * s
    x2d = h.reshape(t, d)
    proj = _in_proj(x2d, w_in).reshape(b, s, IN_COLS)
    ret_out = _retention(proj, ret_gn_w)
    lru_out = _lru(proj, conv_w, conv_b, lru_wa, lru_ba, lru_wx, lru_bx, lru_lam)
    x1, top_e, gates = _out_router(ret_out.reshape(t, RET_WIDTH), lru_out.reshape(t, LRU_WIDTH),
                                   w_out.astype(bf16), x2d, ln1_w, ln1_b, w_router, b_router)
    dest, pad_fill, ue, u_row, u_chunks, n_used_rows, n_pad, max_units = _routing_tables(top_e, t)
    xs = _dispatch(x1, dest, pad_fill, n_pad)
    y = _moe_experts(xs, ue, u_row, u_chunks, n_used_rows, n_pad, max_units, w_gate, b_gate, w_up, b_up, w_down,
                     b_down)
    out = _combine(y, dest, gates.T, x1, p_i.reshape(t, PLE_DIM), ln2_w, ln2_b, w_ple_proj.astype(bf16),
                   ple_norm_w, w_ple_gate.astype(bf16))
    return out.reshape(b, s, d)


def kernel(x, p, w_in, ret_gn_w, conv_w, conv_b, lru_wa, lru_ba, lru_wx, lru_bx, lru_lam, w_out, ln1_w, ln1_b,
           w_router, b_router, w_gate, b_gate, w_up, b_up, w_down, b_down, ln2_w, ln2_b, w_ple_proj, ple_norm_w,
           w_ple_gate):
    h = x.astype(f32)
    for i in range(w_in.shape[0]):
        h = _layer(h, p[i], w_in[i], ret_gn_w[i], conv_w[i], conv_b[i], lru_wa[i], lru_ba[i], lru_wx[i],
                   lru_bx[i], lru_lam[i], w_out[i], ln1_w[i], ln1_b[i], w_router[i], b_router[i], w_gate[i],
                   b_gate[i], w_up[i], b_up[i], w_down[i], b_down[i], ln2_w[i], ln2_b[i], w_ple_proj[i],
                   ple_norm_w[i], w_ple_gate[i])
    return h.astype(x.dtype)
```
